```python
import math
import jax
import jax.numpy as jnp
from jax import lax
import numpy as np

D_MODEL = 2048
BATCH = 4
SEQ = 4096
DEPTH = 2

F32 = jnp.float32
GRID_W = 64
CTX_LEN = 256
EPS = 1e-6
ROPE_THETA = 10000.0
Q_BLOCK = 128
CHUNK = 128
CONV_W = 5
MLA_HEADS = 4
MLA_NOPE = 128
MLA_ROPE = 64
MLA_V = 128
MLA_Q_LORA = 448
MLA_KV_LORA = 128
SSD_HEADS = 8
SSD_HEAD_DIM = 64
SSD_D_INNER = SSD_HEADS * SSD_HEAD_DIM
SSD_STATE = 128
SSD_GROUPS = 2
SSD_CONV_DIM = SSD_D_INNER + 2 * SSD_GROUPS * SSD_STATE
NA_HEADS = 4
NA_HEAD_DIM = 128
NA_DIM = NA_HEADS * NA_HEAD_DIM
NA_WIN_ROWS = 8
NA_WIN_COLS = 16
GDN_HEADS = 4
GDN_HEAD_DIM = 128
GDN_DIM = GDN_HEADS * GDN_HEAD_DIM
MLA_COLS = MLA_Q_LORA + MLA_KV_LORA + MLA_ROPE
SSD_COLS = SSD_D_INNER + SSD_CONV_DIM + 2 * SSD_HEADS
NA_COLS = 3 * NA_DIM
GDN_COLS = 4 * GDN_DIM + 4 * GDN_HEADS
IN_COLS = MLA_COLS + SSD_COLS + NA_COLS + GDN_COLS
D_MIX = MLA_HEADS * MLA_V + SSD_D_INNER + NA_DIM + GDN_DIM
D_FF = 5632
N_EXPERTS = 8
TOP_K = 2
D_EXPERT = 2816

kernel_name = 'hybrid_parallel_heads_dit_block'


def rmsnorm(x, g):
    xf = x.astype(F32)
    y = xf * lax.rsqrt(jnp.mean(xf * xf, axis=-1, keepdims=True) + EPS)
    return (y * g.astype(F32)).astype(x.dtype)


def l2norm(x):
    xf = x.astype(F32)
    return (xf * lax.rsqrt(jnp.sum(xf * xf, axis=-1, keepdims=True) + EPS)).astype(x.dtype)


def split_cols(t, sizes):
    return jnp.split(t, np.cumsum(sizes)[:-1].tolist(), axis=-1)


def modulate(h, g, shift, scale):
    return rmsnorm(h, g) * (1.0 + scale) + shift


def sandwich(h, y, g_post, gate):
    return h + gate * rmsnorm(y, g_post)


def dwconv(x, w):
    k = w.shape[0]
    return lax.conv_general_dilated(x, w[:, None, :].astype(x.dtype), window_strides=(1,),
                                    padding=[(k // 2, k // 2)], dimension_numbers=('NWC', 'WIO', 'NWC'),
                                    feature_group_count=x.shape[-1])


def flip(t):
    return t[:, ::-1]


def rope_2d(t, n_tokens):
    half = t.shape[-1] // 2
    n_axis = half // 2
    inv_freq = ROPE_THETA ** (-jnp.arange(n_axis, dtype=F32) / n_axis)
    pos = jnp.arange(n_tokens)
    rows = (pos // GRID_W).astype(F32)
    cols = (pos % GRID_W).astype(F32)
    ang = jnp.concatenate([rows[:, None] * inv_freq, cols[:, None] * inv_freq], axis=-1)[None, :, None, :]
    cos, sin = jnp.cos(ang), jnp.sin(ang)
    tf = t.astype(F32)
    t1, t2 = tf[..., :half], tf[..., half:]
    return jnp.concatenate([t1 * cos - t2 * sin, t1 * sin + t2 * cos], axis=-1).astype(t.dtype)


def block_attention(q, k, v, scale):
    b_, lq, h, dk = q.shape
    nb = lq // Q_BLOCK
    qb = jnp.moveaxis(q.reshape(b_, nb, Q_BLOCK, h, dk), 1, 0)

    def one(qi):
        s = jnp.einsum('bqhd,bkhd->bhqk', qi, k).astype(F32) * scale
        p = jax.nn.softmax(s, axis=-1).astype(v.dtype)
        return jnp.einsum('bhqk,bkhd->bqhd', p, v)

    o = lax.map(one, qb)
    return jnp.moveaxis(o, 0, 1).reshape(b_, lq, h * v.shape[-1])


def mla_qkv(proj, q_norm, w_uq, kv_norm, w_ukv, rotate):
    b_, n, _ = proj.shape
    cq, ckv, kr = split_cols(proj, [MLA_Q_LORA, MLA_KV_LORA, MLA_ROPE])
    q = (rmsnorm(cq, q_norm) @ w_uq).reshape(b_, n, MLA_HEADS, MLA_NOPE + MLA_ROPE)
    kv = (rmsnorm(ckv, kv_norm) @ w_ukv).reshape(b_, n, MLA_HEADS, MLA_NOPE + MLA_V)
    q_nope, q_rope = q[..., :MLA_NOPE], q[..., MLA_NOPE:]
    k_nope, v = kv[..., :MLA_NOPE], kv[..., MLA_NOPE:]
    kr = kr[:, :, None, :]
    if rotate:
        q_rope = rope_2d(q_rope, n)
        kr = rope_2d(kr, n)
    q = jnp.concatenate([q_nope, q_rope], axis=-1)
    k = jnp.concatenate([k_nope, jnp.broadcast_to(kr, (b_, n, MLA_HEADS, MLA_ROPE))], axis=-1)
    return q, k, v


def mla_mixer(pc, px, q_norm, w_uq, kv_norm, w_ukv, need_ctx):
    qc, kc, vc = mla_qkv(pc, q_norm, w_uq, kv_norm, w_ukv, rotate=False)
    qx, kx, vx = mla_qkv(px, q_norm, w_uq, kv_norm, w_ukv, rotate=True)
    scale = (MLA_NOPE + MLA_ROPE) ** -0.5
    yc = block_attention(qc, kc, vc, scale) if need_ctx else None
    yx = block_attention(qx, jnp.concatenate([kx, kc], axis=1), jnp.concatenate([vx, vc], axis=1), scale)
    return yc, yx


def ssd_chunked(x, dt, a, bm, cm, h0):
    in_dtype = x.dtype
    b_, n, h, p = x.shape
    nc = n // CHUNK
    x, dt, bm, cm = [t.astype(F32).reshape(b_, nc, CHUNK, *t.shape[2:]) for t in (x, dt, bm, cm)]
    acum = jnp.cumsum(dt * a, axis=2)
    ah = jnp.moveaxis(acum, 3, 2)
    incl = jnp.tril(jnp.ones((CHUNK, CHUNK), bool))
    seg = ah[..., :, None] - ah[..., None, :]
    decay = jnp.where(incl, jnp.exp(jnp.where(incl, seg, 0.0)), 0.0)
    scores = jnp.einsum('bcthn,bcshn->bchts', cm, bm) * decay
    y_intra = jnp.einsum('bchts,bcsh,bcshp->bcthp', scores, dt, x)
    to_end = jnp.exp(acum[:, :, -1:, :] - acum)
    states = jnp.einsum('bcsh,bcshn,bcshp->bchpn', to_end * dt, bm, x)
    chunk_decay = jnp.exp(acum[:, :, -1, :])

    def step(hs, inp):
        st, dec = inp
        return hs * dec[:, :, None, None] + st, hs

    h_last, h_start = lax.scan(step, h0.astype(F32), (jnp.moveaxis(states, 1, 0), jnp.moveaxis(chunk_decay, 1, 0)))
    h_start = jnp.moveaxis(h_start, 0, 1)
    y_inter = jnp.einsum('bcthn,bchpn->bcthp', cm, h_start) * jnp.exp(acum)[..., None]
    y = (y_intra + y_inter).reshape(b_, n, h, p)
    return y.astype(in_dtype), h_last


def ssd_prep(proj, conv_w, conv_b, dt_bias):
    b_, n, _ = proj.shape
    z, xbc, dt = split_cols(proj, [SSD_D_INNER, SSD_CONV_DIM, 2 * SSD_HEADS])
    xbc = jax.nn.silu(dwconv(xbc, conv_w) + conv_b)
    xs, bm, cm = split_cols(xbc, [SSD_D_INNER, SSD_GROUPS * SSD_STATE, SSD_GROUPS * SSD_STATE])
    rep = SSD_HEADS // SSD_GROUPS
    xs = xs.reshape(b_, n, SSD_HEADS, SSD_HEAD_DIM)
    bm = jnp.repeat(bm.reshape(b_, n, SSD_GROUPS, SSD_STATE), rep, axis=2)
    cm = jnp.repeat(cm.reshape(b_, n, SSD_GROUPS, SSD_STATE), rep, axis=2)
    dt = jax.nn.softplus(dt.astype(F32).reshape(b_, n, 2, SSD_HEADS) + dt_bias.astype(F32))
    return z, xs, bm, cm, dt


def ssd_mixer(pc, px, conv_w, conv_b, a_log, dt_bias, d_skip, norm_g, need_ctx):
    zc, xc, bc, cc, dtc = ssd_prep(pc, conv_w, conv_b, dt_bias)
    zx, xx, bx, cx, dtx = ssd_prep(px, conv_w, conv_b, dt_bias)
    a = -jnp.exp(a_log.astype(F32))
    b_ = px.shape[0]
    yc = d_skip[:, None] * xc
    yx = d_skip[:, None] * xx
    for d in range(2):
        fl = flip if d == 1 else (lambda t: t)
        h0 = jnp.zeros((b_, SSD_HEADS, SSD_HEAD_DIM, SSD_STATE), F32)
        yc_d, hc = ssd_chunked(fl(xc), fl(dtc[:, :, d]), a[d], fl(bc), fl(cc), h0)
        yx_d, _ = ssd_chunked(fl(xx), fl(dtx[:, :, d]), a[d], fl(bx), fl(cx), hc)
        yc = yc + fl(yc_d)
        yx = yx + fl(yx_d)

    def out(y, z):
        return rmsnorm(y.reshape(z.shape) * jax.nn.silu(z), norm_g)

    return (out(yc, zc) if need_ctx else None), out(yx, zx)


def na_mixer(pc, px, rpb, need_ctx):
    b_, lx, _ = px.shape
    qc, kc, vc = [t.reshape(b_, pc.shape[1], NA_HEADS, NA_HEAD_DIM) for t in split_cols(pc, [NA_DIM] * 3)]
    qx, kx, vx = [t.reshape(b_, lx, NA_HEADS, NA_HEAD_DIM) for t in split_cols(px, [NA_DIM] * 3)]
    scale = NA_HEAD_DIM ** -0.5
    yc = block_attention(qc, kc, vc, scale) if need_ctx else None
    rows = lx // GRID_W
    wr = min(NA_WIN_ROWS, rows)
    n_loc = wr * NA_WIN_COLS
    col_start = np.clip(np.arange(GRID_W) - NA_WIN_COLS // 2, 0, GRID_W - NA_WIN_COLS)
    col_idx = col_start[:, None] + np.arange(NA_WIN_COLS)[None, :]
    dc_idx = col_idx - np.arange(GRID_W)[:, None] + NA_WIN_COLS - 1
    kg = kx.reshape(b_, rows, GRID_W, NA_HEADS, NA_HEAD_DIM)
    vg = vx.reshape(b_, rows, GRID_W, NA_HEADS, NA_HEAD_DIM)
    qg = jnp.moveaxis(qx.reshape(b_, rows, GRID_W, NA_HEADS, NA_HEAD_DIM), 1, 0)

    def one_row(args):
        r, q = args
        rs = jnp.clip(r - wr // 2, 0, rows - wr)
        k_win = lax.dynamic_slice_in_dim(kg, rs, wr, axis=1)[:, :, col_idx]
        v_win = lax.dynamic_slice_in_dim(vg, rs, wr, axis=1)[:, :, col_idx]
        dr_idx = rs + jnp.arange(wr) - r + NA_WIN_ROWS - 1
        bias = rpb[:, dr_idx[None, :, None], dc_idx[:, None, :]]
        s_loc = jnp.einsum('bqhd,bwqchd->bhqwc', q, k_win).astype(F32) * scale + bias.astype(F32)
        s_ctx = jnp.einsum('bqhd,bkhd->bhqk', q, kc).astype(F32) * scale
        s = jnp.concatenate([s_loc.reshape(b_, NA_HEADS, GRID_W, n_loc), s_ctx], axis=-1)
        p = jax.nn.softmax(s, axis=-1).astype(vx.dtype)
        p_loc = p[..., :n_loc].reshape(b_, NA_HEADS, GRID_W, wr, NA_WIN_COLS)
        return (jnp.einsum('bhqwc,bwqchd->bqhd', p_loc, v_win)
                + jnp.einsum('bhqk,bkhd->bqhd', p[..., n_loc:], vc))

    o = lax.map(one_row, (jnp.arange(rows), qg))
    yx = jnp.moveaxis(o, 0, 1).reshape(b_, lx, NA_DIM)
    return yc, yx


def gated_delta_chunked(q, k, v, g, beta, s0):
    in_dtype = v.dtype
    b_, n, h, dk = q.shape
    dv = v.shape[-1]
    nc = n // CHUNK

    def to_chunks(t):
        return jnp.moveaxis(t.astype(F32).reshape(b_, nc, CHUNK, h, *t.shape[3:]), 3, 2)

    q = to_chunks(q) * dk ** -0.5
    k, v, g, beta = to_chunks(k), to_chunks(v), to_chunks(g), to_chunks(beta)
    gc = jnp.cumsum(g, axis=-1)
    incl = jnp.tril(jnp.ones((CHUNK, CHUNK), bool))
    strict = jnp.tril(jnp.ones((CHUNK, CHUNK), bool), -1)
    seg = gc[..., :, None] - gc[..., None, :]
    decay = jnp.where(incl, jnp.exp(jnp.where(incl, seg, 0.0)), 0.0)
    kb = k * beta[..., None]
    a_mat = jnp.eye(CHUNK, dtype=F32) + jnp.where(strict, jnp.einsum('bchtk,bchsk->bchts', kb, k) * decay, 0.0)
    rhs = jnp.concatenate([v * beta[..., None], kb * jnp.exp(gc)[..., None]], axis=-1)
    sol = lax.linalg.triangular_solve(a_mat, rhs, left_side=True, lower=True, unit_diagonal=True)
    u, w = sol[..., :dv], sol[..., dv:]
    qk = jnp.einsum('bchtk,bchsk->bchts', q, k) * decay
    q_dec = q * jnp.exp(gc)[..., None]
    k_dec = k * jnp.exp(gc[..., -1:] - gc)[..., None]
    g_end = jnp.exp(gc[..., -1])

    def step(s, xs):
        qk_i, qd_i, w_i, u_i, kd_i, ge_i = xs
        v_new = u_i - jnp.einsum('bhtk,bhkv->bhtv', w_i, s)
        o = jnp.einsum('bhtk,bhkv->bhtv', qd_i, s) + jnp.einsum('bhts,bhsv->bhtv', qk_i, v_new)
        s = s * ge_i[..., None, None] + jnp.einsum('bhsk,bhsv->bhkv', kd_i, v_new)
        return s, o

    xs = (jnp.moveaxis(qk, 1, 0), jnp.moveaxis(q_dec, 1, 0), jnp.moveaxis(w, 1, 0),
          jnp.moveaxis(u, 1, 0), jnp.moveaxis(k_dec, 1, 0), jnp.moveaxis(g_end, 1, 0))
    s_last, o = lax.scan(step, s0.astype(F32), xs)
    o = jnp.moveaxis(jnp.moveaxis(o, 0, 1), 2, 3).reshape(b_, n, h, dv)
    return o.astype(in_dtype), s_last


def gdn_prep(proj, conv_w, a_log, dt_bias):
    b_, n, _ = proj.shape
    qkv, z, a, bt = split_cols(proj, [3 * GDN_DIM, GDN_DIM, 2 * GDN_HEADS, 2 * GDN_HEADS])
    qkv = jax.nn.silu(dwconv(qkv, conv_w))
    q, k, v = [t.reshape(b_, n, GDN_HEADS, GDN_HEAD_DIM) for t in split_cols(qkv, [GDN_DIM] * 3)]
    g = -jnp.exp(a_log.astype(F32)) * jax.nn.softplus(a.astype(F32).reshape(b_, n, 2, GDN_HEADS) + dt_bias.astype(F32))
    beta = jax.nn.sigmoid(bt.astype(F32).reshape(b_, n, 2, GDN_HEADS))
    return l2norm(q), l2norm(k), v, z.reshape(b_, n, GDN_HEADS, GDN_HEAD_DIM), g, beta


def gdn_mixer(pc, px, conv_w, a_log, dt_bias, norm_g, need_ctx):
    qc, kc, vc, zc, gcx, bc = gdn_prep(pc, conv_w, a_log, dt_bias)
    qx, kx, vx, zx, gx, bx = gdn_prep(px, conv_w, a_log, dt_bias)
    b_ = px.shape[0]
    oc = jnp.zeros_like(vc)
    ox = jnp.zeros_like(vx)
    for d in range(2):
        fl = flip if d == 1 else (lambda t: t)
        s0 = jnp.zeros((b_, GDN_HEADS, GDN_HEAD_DIM, GDN_HEAD_DIM), F32)
        oc_d, sc = gated_delta_chunked(fl(qc), fl(kc), fl(vc), fl(gcx[:, :, d]), fl(bc[:, :, d]), s0)
        ox_d, _ = gated_delta_chunked(fl(qx), fl(kx), fl(vx), fl(gx[:, :, d]), fl(bx[:, :, d]), sc)
        oc = oc + fl(oc_d)
        ox = ox + fl(ox_d)

    def out(o, z):
        return (rmsnorm(o, norm_g) * jax.nn.silu(z)).reshape(o.shape[0], o.shape[1], GDN_DIM)

    return (out(oc, zc) if need_ctx else None), out(ox, zx)


def swiglu(u, w_gate, w_up, w_down):
    return (jax.nn.silu(u @ w_gate) * (u @ w_up)) @ w_down


def moe_swiglu(u, router, e_gate, e_up, e_down):
    logits = (u @ router).astype(F32)
    top_v, top_i = lax.top_k(logits, TOP_K)
    top_w = jax.nn.softmax(top_v, axis=-1)
    gates = jnp.sum(jax.nn.one_hot(top_i, N_EXPERTS, dtype=F32) * top_w[..., None], axis=-2).astype(u.dtype)
    y = jnp.zeros_like(u)
    for e in range(N_EXPERTS):
        y = y + gates[..., e:e + 1] * swiglu(u, e_gate[e], e_up[e], e_down[e])
    return y


def setup_inputs(seed: int = 0) -> dict:
    key = jax.random.key(seed)
    ks = iter(jax.random.split(key, 64))

    def nrm(shape, scale):
        return jax.random.normal(next(ks), shape, F32) * scale

    def gain(shape):
        return 1.0 + nrm(shape, 0.05)

    def dt_bias(shape):
        dt = jnp.exp(jax.random.uniform(next(ks), shape, F32, math.log(1e-3), math.log(1e-1)))
        return jnp.log(jnp.expm1(dt))

    def a_log(shape):
        return jnp.log(jax.random.uniform(next(ks), shape, F32, 1.0, 16.0))

    n_dense = (DEPTH + 1) // 2
    n_moe = DEPTH // 2
    dm = D_MODEL
    return {
        'x': nrm((BATCH, SEQ, dm), 1.0),
        'c': nrm((BATCH, dm), 1.0),
        'ctx': nrm((BATCH, CTX_LEN, dm), 1.0),
        'c_ctx': nrm((dm,), 1.0),
        'w_ada': nrm((DEPTH, dm, 6 * dm), 0.5 * dm ** -0.5),
        'b_ada': nrm((DEPTH, 6 * dm), 0.02),
        'g_pre_mix': gain((DEPTH, dm)),
        'g_post_mix': gain((DEPTH, dm)),
        'g_pre_ffn': gain((DEPTH, dm)),
        'g_post_ffn': gain((DEPTH, dm)),
        'w_in': nrm((DEPTH, dm, IN_COLS), dm ** -0.5),
        'w_out': nrm((DEPTH, D_MIX, dm), D_MIX ** -0.5),
        'mla_q_norm': gain((DEPTH, MLA_Q_LORA)),
        'mla_w_uq': nrm((DEPTH, MLA_Q_LORA, MLA_HEADS * (MLA_NOPE + MLA_ROPE)), MLA_Q_LORA ** -0.5),
        'mla_kv_norm': gain((DEPTH, MLA_KV_LORA)),
        'mla_w_ukv': nrm((DEPTH, MLA_KV_LORA, MLA_HEADS * (MLA_NOPE + MLA_V)), MLA_KV_LORA ** -0.5),
        'ssd_conv_w': nrm((DEPTH, CONV_W, SSD_CONV_DIM), CONV_W ** -0.5),
        'ssd_conv_b': nrm((DEPTH, SSD_CONV_DIM), 0.02),
        'ssd_a_log': a_log((DEPTH, 2, SSD_HEADS)),
        'ssd_dt_bias': dt_bias((DEPTH, 2, SSD_HEADS)),
        'ssd_d': gain((DEPTH, SSD_HEADS)),
        'ssd_norm': gain((DEPTH, SSD_D_INNER)),
        'na_rpb': nrm((DEPTH, NA_HEADS, 2 * NA_WIN_ROWS - 1, 2 * NA_WIN_COLS - 1), 0.1),
        'gdn_conv_w': nrm((DEPTH, CONV_W, 3 * GDN_DIM), CONV_W ** -0.5),
        'gdn_a_log': a_log((DEPTH, 2, GDN_HEADS)),
        'gdn_dt_bias': dt_bias((DEPTH, 2, GDN_HEADS)),
        'gdn_norm': gain((DEPTH, GDN_HEAD_DIM)),
        'ffn_w_gate': nrm((n_dense, dm, D_FF), dm ** -0.5),
        'ffn_w_up': nrm((n_dense, dm, D_FF), dm ** -0.5),
        'ffn_w_down': nrm((n_dense, D_FF, dm), D_FF ** -0.5),
        'moe_router': nrm((n_moe, dm, N_EXPERTS), dm ** -0.5),
        'moe_w_gate': nrm((n_moe, N_EXPERTS, dm, D_EXPERT), dm ** -0.5),
        'moe_w_up': nrm((n_moe, N_EXPERTS, dm, D_EXPERT), dm ** -0.5),
        'moe_w_down': nrm((n_moe, N_EXPERTS, D_EXPERT, dm), D_EXPERT ** -0.5),
    }


def reference(x, c, ctx, c_ctx, w_ada, b_ada, g_pre_mix, g_post_mix, g_pre_ffn, g_post_ffn, w_in, w_out,
              mla_q_norm, mla_w_uq, mla_kv_norm, mla_w_ukv, ssd_conv_w, ssd_conv_b, ssd_a_log, ssd_dt_bias,
              ssd_d, ssd_norm, na_rpb, gdn_conv_w, gdn_a_log, gdn_dt_bias, gdn_norm,
              ffn_w_gate, ffn_w_up, ffn_w_down, moe_router, moe_w_gate, moe_w_up, moe_w_down):
    hx, hc = x, ctx
    sc_x = jax.nn.silu(c)[:, None, :]
    sc_c = jax.nn.silu(c_ctx)[None, None, :]
    col_sizes = [MLA_COLS, SSD_COLS, NA_COLS, GDN_COLS]
    for i in range(DEPTH):
        need_ctx = i < DEPTH - 1
        sh1x, sc1x, gt1x, sh2x, sc2x, gt2x = jnp.split(sc_x @ w_ada[i] + b_ada[i], 6, axis=-1)
        sh1c, sc1c, gt1c, sh2c, sc2c, gt2c = jnp.split(sc_c @ w_ada[i] + b_ada[i], 6, axis=-1)
        pa_x, pb_x, pn_x, pd_x = split_cols(modulate(hx, g_pre_mix[i], sh1x, sc1x) @ w_in[i], col_sizes)
        pa_c, pb_c, pn_c, pd_c = split_cols(modulate(hc, g_pre_mix[i], sh1c, sc1c) @ w_in[i], col_sizes)
        ya_c, ya_x = mla_mixer(pa_c, pa_x, mla_q_norm[i], mla_w_uq[i], mla_kv_norm[i], mla_w_ukv[i], need_ctx)
        yb_c, yb_x = ssd_mixer(pb_c, pb_x, ssd_conv_w[i], ssd_conv_b[i], ssd_a_log[i], ssd_dt_bias[i],
                               ssd_d[i], ssd_norm[i], need_ctx)
        yn_c, yn_x = na_mixer(pn_c, pn_x, na_rpb[i], need_ctx)
        yd_c, yd_x = gdn_mixer(pd_c, pd_x, gdn_conv_w[i], gdn_a_log[i], gdn_dt_bias[i], gdn_norm[i], need_ctx)
        hx = sandwich(hx, jnp.concatenate([ya_x, yb_x, yn_x, yd_x], axis=-1) @ w_out[i], g_post_mix[i], gt1x)
        if need_ctx:
            hc = sandwich(hc, jnp.concatenate([ya_c, yb_c, yn_c, yd_c], axis=-1) @ w_out[i], g_post_mix[i], gt1c)
        j = i // 2
        if i % 2 == 0:
            def ffn(u):
                return swiglu(u, ffn_w_gate[j], ffn_w_up[j], ffn_w_down[j])
        else:
            def ffn(u):
                return moe_swiglu(u, moe_router[j], moe_w_gate[j], moe_w_up[j], moe_w_down[j])
        hx = sandwich(hx, ffn(modulate(hx, g_pre_ffn[i], sh2x, sc2x)), g_post_ffn[i], gt2x)
        if need_ctx:
            hc = sandwich(hc, ffn(modulate(hc, g_pre_ffn[i], sh2c, sc2c)), g_post_ffn[i], gt2c)
    return hx
```

```python
import functools
import math

import numpy as np
import jax
import jax.numpy as jnp
from jax import lax
from jax.experimental import pallas as pl
from jax.experimental.pallas import tpu as pltpu

F32 = jnp.float32
BF16 = jnp.bfloat16
HIGHEST = lax.Precision.HIGHEST

GRID_W = 64
EPS = 1e-6
ROPE_THETA = 10000.0
CHUNK = 128
CONV_W = 5
MLA_HEADS, MLA_NOPE, MLA_ROPE, MLA_V = 4, 128, 64, 128
MLA_Q_LORA, MLA_KV_LORA = 448, 128
SSD_HEADS, SSD_HEAD_DIM, SSD_STATE, SSD_GROUPS = 8, 64, 128, 2
SSD_D_INNER = SSD_HEADS * SSD_HEAD_DIM
SSD_CONV_DIM = SSD_D_INNER + 2 * SSD_GROUPS * SSD_STATE
NA_HEADS, NA_HEAD_DIM = 4, 128
NA_DIM = NA_HEADS * NA_HEAD_DIM
NA_WIN_ROWS, NA_WIN_COLS = 8, 16
GDN_HEADS, GDN_HEAD_DIM = 4, 128
GDN_DIM = GDN_HEADS * GDN_HEAD_DIM
N_EXPERTS, TOP_K = 8, 2
MLA_COLS = MLA_Q_LORA + MLA_KV_LORA + MLA_ROPE
SSD_COLS = SSD_D_INNER + SSD_CONV_DIM + 2 * SSD_HEADS
NA_COLS = 3 * NA_DIM
GDN_COLS = 4 * GDN_DIM + 4 * GDN_HEADS

P_XBC, P_ZSSD, P_NAQ, P_NAK, P_NAV = 0, 1024, 1536, 2048, 2560
P_GQKV, P_GZ, P_MLA = 3072, 4608, 5120
P_MAIN = 5760
P_SMALL = 128
NEG = -1e30
VMEM_MB = 1024 * 1024


def _cp(sem, mb):
    return pltpu.CompilerParams(dimension_semantics=sem, vmem_limit_bytes=mb * VMEM_MB)


def _dot(a, b):
    return jnp.dot(a, b, preferred_element_type=F32)


def _dot_nt(a, b, precision=None):
    return lax.dot_general(a, b, (((1,), (1,)), ((), ())), preferred_element_type=F32, precision=precision)


def _dot_tn(a, b):
    return lax.dot_general(a, b, (((0,), (0,)), ((), ())), preferred_element_type=F32)


def _sigmoid(x):
    return 1.0 / (1.0 + jnp.exp(-x))


def _silu(x):
    return x * _sigmoid(x)


def _softplus(x):
    return jnp.maximum(x, 0.0) + jnp.log(1.0 + jnp.exp(-jnp.abs(x)))


def _rms_scale(x):
    return lax.rsqrt(jnp.mean(x * x, axis=-1, keepdims=True) + EPS)


def _in_proj_layout():
    o_mla, o_ssd = 0, MLA_COLS
    o_na, o_gdn = o_ssd + SSD_COLS, o_ssd + SSD_COLS + NA_COLS
    r = lambda a, n: np.arange(a, a + n)
    main = np.concatenate([
        r(o_ssd + SSD_D_INNER, SSD_CONV_DIM),
        r(o_ssd, SSD_D_INNER),
        r(o_na, NA_COLS),
        r(o_gdn, 3 * GDN_DIM),
        r(o_gdn + 3 * GDN_DIM, GDN_DIM),
        r(o_mla + MLA_Q_LORA, MLA_KV_LORA),
        r(o_mla, MLA_Q_LORA),
        r(o_mla + MLA_Q_LORA + MLA_KV_LORA, MLA_ROPE),
    ])
    small = np.concatenate([
        r(o_ssd + SSD_D_INNER + SSD_CONV_DIM, 2 * SSD_HEADS),
        r(o_gdn + 4 * GDN_DIM, 4 * GDN_HEADS),
    ])
    assert main.shape[0] == P_MAIN
    return main, small


def _ada_kernel(c_ref, w_ref, b_ref, o_ref):
    s = _silu(c_ref[...]).astype(BF16)
    o_ref[...] = _dot(s, w_ref[...].astype(BF16)) + b_ref[...]


def _ada_call(cvec, w_ada, b_ada):
    depth, d, n = w_ada.shape
    tn = 1024
    return pl.pallas_call(
        _ada_kernel,
        grid=(depth, n // tn),
        in_specs=[pl.BlockSpec((8, d), lambda l, j: (0, 0)),
                  pl.BlockSpec((None, d, tn), lambda l, j: (l, 0, j)),
                  pl.BlockSpec((None, 1, tn), lambda l, j: (l, 0, j))],
        out_specs=pl.BlockSpec((None, 8, tn), lambda l, j: (l, 0, j)),
        out_shape=jax.ShapeDtypeStruct((depth, 8, n), F32),
        compiler_params=_cp(("parallel", "parallel"), 40),
        name="adaln",
    )(cvec, w_ada, b_ada.reshape(depth, 1, n))


def _inproj_kernel(h_ref, mod_ref, g_ref, w_ref, ws_ref, o_ref, os_ref, u_scr):
    @pl.when(pl.program_id(1) == 0)
    def _():
        x = h_ref[...]
        y = x * _rms_scale(x) * g_ref[...]
        u = (y * (1.0 + mod_ref[1:2, :]) + mod_ref[0:1, :]).astype(BF16)
        u_scr[...] = u
        os_ref[...] = _dot(u, ws_ref[...])

    o_ref[...] = _dot(u_scr[...], w_ref[...]).astype(BF16)


def _inproj_call(h, mod_l, g_pre, w_main, w_small, *, n_b, lx, tm):
    rows, d = h.shape
    tn = 640
    nxt, per_b = n_b * lx // tm, lx // tm

    def mod_idx(i, j):
        return (jnp.where(i < nxt, i // per_b, n_b), 0, 0)

    return pl.pallas_call(
        _inproj_kernel,
        grid=(rows // tm, P_MAIN // tn),
        in_specs=[pl.BlockSpec((tm, d), lambda i, j: (i, 0)),
                  pl.BlockSpec((None, 6, d), mod_idx),
                  pl.BlockSpec((1, d), lambda i, j: (0, 0)),
                  pl.BlockSpec((d, tn), lambda i, j: (0, j)),
                  pl.BlockSpec((d, P_SMALL), lambda i, j: (0, 0))],
        out_specs=[pl.BlockSpec((tm, tn), lambda i, j: (i, j)),
                   pl.BlockSpec((tm, P_SMALL), lambda i, j: (i, 0))],
        out_shape=[jax.ShapeDtypeStruct((rows, P_MAIN), BF16),
                   jax.ShapeDtypeStruct((rows, P_SMALL), F32)],
        scratch_shapes=[pltpu.VMEM((tm, d), BF16)],
        compiler_params=_cp(("parallel", "arbitrary"), 40),
        name="in_proj",
    )(h, mod_l, g_pre.reshape(1, d), w_main, w_small)


HALO = 16


def _conv_kernel(prev_ref, cur_ref, next_ref, w_ref, b_ref, o_ref, ext_scr, *, tr, blocks_x, seq_x, seq_c):
    i = pl.program_id(0)
    in_x = i < blocks_x
    pos = jnp.where(in_x, i % seq_x, (i - blocks_x) % seq_c)
    last_pos = jnp.where(in_x, seq_x - 1, seq_c - 1)
    ext_scr[0:HALO, :] = jnp.where(pos == 0, 0.0, prev_ref[...].astype(F32))
    ext_scr[HALO:HALO + tr, :] = cur_ref[...].astype(F32)
    ext_scr[HALO + tr:2 * HALO + tr, :] = jnp.where(pos == last_pos, 0.0, next_ref[...].astype(F32))
    acc = b_ref[...] + w_ref[0:1, :] * ext_scr[HALO - 2:HALO - 2 + tr, :]
    for k in range(1, CONV_W):
        acc = acc + w_ref[k:k + 1, :] * ext_scr[HALO - 2 + k:HALO - 2 + k + tr, :]
    o_ref[...] = _silu(acc).astype(BF16)


def _conv_call(proj, w, b, *, col_off, n_b, lx, lc, tr):
    rows = proj.shape[0]
    c = w.shape[1]
    cb = col_off // c
    assert cb * c == col_off
    hb = tr // HALO
    n_halo = rows // HALO
    kern = functools.partial(_conv_kernel, tr=tr, blocks_x=n_b * lx // tr, seq_x=lx // tr, seq_c=lc // tr)
    return pl.pallas_call(
        kern,
        grid=(rows // tr,),
        in_specs=[pl.BlockSpec((HALO, c), lambda i: (jnp.maximum(i * hb - 1, 0), cb)),
                  pl.BlockSpec((tr, c), lambda i: (i, cb)),
                  pl.BlockSpec((HALO, c), lambda i: (jnp.minimum((i + 1) * hb, n_halo - 1), cb)),
                  pl.BlockSpec((CONV_W, c), lambda i: (0, 0)),
                  pl.BlockSpec((1, c), lambda i: (0, 0))],
        out_specs=pl.BlockSpec((tr, c), lambda i: (i, 0)),
        out_shape=jax.ShapeDtypeStruct((rows, c), BF16),
        scratch_shapes=[pltpu.VMEM((tr + 2 * HALO, c), F32)],
        compiler_params=_cp(("parallel",), 40),
        name="dwconv_silu",
    )(proj, proj, proj, w, b.reshape(1, c))


def _mla_prep_kernel(p_ref, qn_ref, wq_ref, wqs_ref, kvn_ref, wkv_ref, cos_ref, sin_ref, perm_ref,
                     q_ref, k_ref, v_ref):
    p = p_ref[...].astype(F32)
    ckv = p[:, 0:MLA_KV_LORA]
    ce = p[:, MLA_KV_LORA:]
    lane = lax.broadcasted_iota(jnp.int32, ce.shape, 1)
    ssq = jnp.sum(jnp.where(lane < MLA_Q_LORA, ce * ce, 0.0), axis=-1, keepdims=True)
    cqn = (ce * lax.rsqrt(ssq / MLA_Q_LORA + EPS) * qn_ref[...]).astype(BF16)
    ckvn = (ckv * _rms_scale(ckv) * kvn_ref[...]).astype(BF16)
    q = _dot(cqn, wq_ref[...])
    qs = _dot(cqn, wqs_ref[...])
    kv = _dot(ckvn, wkv_ref[...])
    cos, sin = cos_ref[...], sin_ref[...]
    kr = p_ref[:, MLA_KV_LORA + MLA_Q_LORA:]
    kr_rot = kr.astype(F32) * cos + _dot(kr, perm_ref[...]) * sin
    nr = MLA_HEADS * MLA_NOPE
    for h in range(MLA_HEADS):
        q_ref[h, :, 0:MLA_NOPE] = q[:, h * MLA_NOPE:(h + 1) * MLA_NOPE].astype(BF16)
        qr = q[:, nr + h * MLA_ROPE:nr + (h + 1) * MLA_ROPE] * cos + qs[:, h * MLA_ROPE:(h + 1) * MLA_ROPE] * sin
        q_ref[h, :, MLA_NOPE:] = qr.astype(BF16)
        hv = h * (MLA_NOPE + MLA_V)
        k_ref[h, :, 0:MLA_NOPE] = kv[:, hv:hv + MLA_NOPE].astype(BF16)
        k_ref[h, :, MLA_NOPE:] = kr_rot.astype(BF16)
        v_ref[h] = kv[:, hv + MLA_NOPE:hv + MLA_NOPE + MLA_V].astype(BF16)


def _mla_prep_call(proj, qn_ext, wq, wqs, kvn, wkv, cos_t, sin_t, perm, *, n_b, lx, tm):
    rows = proj.shape[0]
    nxt, per_b = n_b * lx // tm, lx // tm
    dk = MLA_NOPE + MLA_ROPE
    rope_idx = lambda i: (jnp.where(i < nxt, i % per_b, per_b), 0)
    full = lambda a: pl.BlockSpec(a.shape, lambda i: (0,) * a.ndim)
    return pl.pallas_call(
        _mla_prep_kernel,
        grid=(rows // tm,),
        in_specs=[pl.BlockSpec((tm, MLA_COLS), lambda i: (i, P_MLA // MLA_COLS)),
                  full(qn_ext), full(wq), full(wqs), full(kvn), full(wkv),
                  pl.BlockSpec((tm, MLA_ROPE), rope_idx), pl.BlockSpec((tm, MLA_ROPE), rope_idx),
                  full(perm)],
        out_specs=[pl.BlockSpec((MLA_HEADS, tm, dk), lambda i: (0, i, 0)),
                   pl.BlockSpec((MLA_HEADS, tm, dk), lambda i: (0, i, 0)),
                   pl.BlockSpec((MLA_HEADS, tm, MLA_V), lambda i: (0, i, 0))],
        out_shape=[jax.ShapeDtypeStruct((MLA_HEADS, rows, dk), BF16),
                   jax.ShapeDtypeStruct((MLA_HEADS, rows, dk), BF16),
                   jax.ShapeDtypeStruct((MLA_HEADS, rows, MLA_V), BF16)],
        compiler_params=_cp(("parallel",), 40),
        name="mla_prep",
    )(proj, qn_ext, wq, wqs, kvn, wkv, cos_t, sin_t, perm)


def _softmax_pv(scores, values):
    m = functools.reduce(jnp.maximum, [jnp.max(s, axis=-1, keepdims=True) for s in scores])
    ps = [jnp.exp(s - m) for s in scores]
    den = functools.reduce(lambda a, b: a + b, [jnp.sum(p, axis=-1, keepdims=True) for p in ps])
    num = functools.reduce(lambda a, b: a + b, [_dot(p.astype(BF16), v) for p, v in zip(ps, values)])
    return num / den


def _mla_attn_kernel(q_ref, kx_ref, vx_ref, kc_ref, vc_ref, o_ref, *, nqx, scale):
    qi = pl.program_id(2)
    q = q_ref[...]
    sc = _dot_nt(q, kc_ref[...]) * scale

    @pl.when(qi < nqx)
    def _():
        sx = _dot_nt(q, kx_ref[...]) * scale
        o_ref[...] = _softmax_pv([sx, sc], [vx_ref[...], vc_ref[...]]).astype(BF16)

    @pl.when(qi >= nqx)
    def _():
        o_ref[...] = _softmax_pv([sc], [vc_ref[...]]).astype(BF16)


def _mla_attn_call(q, k, v, *, n_b, lx, lc, tq, need_ctx):
    rows = q.shape[1]
    dk = q.shape[2]
    nqx, nqc = lx // tq, lc // tq
    nq = nqx + (nqc if need_ctx else 0)
    nbx = n_b * lx // lc

    def q_row(b, qi):
        return jnp.where(qi < nqx, b * nqx + qi, n_b * nqx + b * nqc + (qi - nqx))

    kern = functools.partial(_mla_attn_kernel, nqx=nqx, scale=dk ** -0.5)
    return pl.pallas_call(
        kern,
        grid=(n_b, MLA_HEADS, nq),
        in_specs=[pl.BlockSpec((None, tq, dk), lambda b, h, qi: (h, q_row(b, qi), 0)),
                  pl.BlockSpec((None, lx, dk), lambda b, h, qi: (h, b, 0)),
                  pl.BlockSpec((None, lx, MLA_V), lambda b, h, qi: (h, b, 0)),
                  pl.BlockSpec((None, lc, dk), lambda b, h, qi: (h, nbx + b, 0)),
                  pl.BlockSpec((None, lc, MLA_V), lambda b, h, qi: (h, nbx + b, 0))],
        out_specs=pl.BlockSpec((tq, MLA_V), lambda b, h, qi: (q_row(b, qi), h)),
        out_shape=jax.ShapeDtypeStruct((rows, MLA_HEADS * MLA_V), BF16),
        compiler_params=_cp(("parallel", "parallel", "arbitrary"), 48),
        name="mla_attn",
    )(q, k, v, k, v)


def _na_kernel(q_ref, kx_ref, vx_ref, kc_ref, vc_ref, bias_ref, o_ref, *, g_rows, wr, scale):
    r = pl.program_id(2)
    q = q_ref[...]
    sc = _dot_nt(q, kc_ref[...]) * scale

    @pl.when(r < g_rows)
    def _():
        rs = jnp.clip(r - wr // 2, 0, g_rows - wr)
        start = pl.multiple_of(rs * GRID_W, GRID_W)
        kw = kx_ref[pl.ds(start, wr * GRID_W), :]
        vw = vx_ref[pl.ds(start, wr * GRID_W), :]
        sl = _dot_nt(q, kw) * scale + bias_ref[...]
        o_ref[...] = _softmax_pv([sl, sc], [vw, vc_ref[...]]).astype(BF16)

    @pl.when(r >= g_rows)
    def _():
        o_ref[...] = _softmax_pv([sc], [vc_ref[...]]).astype(BF16)


def _na_call(proj, bias_tab, *, n_b, lx, lc, need_ctx):
    rows = proj.shape[0]
    g_rows = lx // GRID_W
    wr = min(NA_WIN_ROWS, g_rows)
    nqc = lc // GRID_W
    nq = g_rows + (nqc if need_ctx else 0)
    nbx = n_b * lx // lc
    hd = NA_HEAD_DIM
    cq, ck, cv = P_NAQ // hd, P_NAK // hd, P_NAV // hd

    def q_row(b, r):
        return jnp.where(r < g_rows, b * g_rows + r, n_b * g_rows + b * nqc + (r - g_rows))

    def variant(r):
        rr = jnp.minimum(r, g_rows - 1)
        return rr - jnp.clip(rr - wr // 2, 0, g_rows - wr)

    kern = functools.partial(_na_kernel, g_rows=g_rows, wr=wr, scale=hd ** -0.5)
    return pl.pallas_call(
        kern,
        grid=(n_b, NA_HEADS, nq),
        in_specs=[pl.BlockSpec((GRID_W, hd), lambda b, h, r: (q_row(b, r), cq + h)),
                  pl.BlockSpec((lx, hd), lambda b, h, r: (b, ck + h)),
                  pl.BlockSpec((lx, hd), lambda b, h, r: (b, cv + h)),
                  pl.BlockSpec((lc, hd), lambda b, h, r: (nbx + b, ck + h)),
                  pl.BlockSpec((lc, hd), lambda b, h, r: (nbx + b, cv + h)),
                  pl.BlockSpec((None, None, GRID_W, wr * GRID_W), lambda b, h, r: (variant(r), h, 0, 0))],
        out_specs=pl.BlockSpec((GRID_W, hd), lambda b, h, r: (q_row(b, r), h)),
        out_shape=jax.ShapeDtypeStruct((rows, NA_DIM), BF16),
        compiler_params=_cp(("parallel", "parallel", "arbitrary"), 40),
        name="na_attn",
    )(proj, proj, proj, proj, proj, bias_tab)


def _na_bias_table(rpb, g_rows):
    wr = min(NA_WIN_ROWS, g_rows)
    col_start = np.clip(np.arange(GRID_W) - NA_WIN_COLS // 2, 0, GRID_W - NA_WIN_COLS)
    cc = np.arange(GRID_W)
    inside = (cc[None, :] >= col_start[:, None]) & (cc[None, :] < col_start[:, None] + NA_WIN_COLS)
    dc = np.clip(cc[None, :] - cc[:, None] + NA_WIN_COLS - 1, 0, 2 * NA_WIN_COLS - 2)
    dr = np.arange(wr)[None, :] - np.arange(wr)[:, None] + NA_WIN_ROWS - 1
    t = rpb[:, dr[:, :, None, None], dc[None, None, :, :]]
    t = jnp.where(inside[None, None, None], t.astype(F32), NEG)
    return jnp.transpose(t, (1, 0, 3, 2, 4)).reshape(wr, rpb.shape[0], GRID_W, wr * GRID_W)


def _chunk_block(n_b, nxc, ncc):
    def f(b, d, c):
        cc = jnp.where(d == 0, c, ncc - 1 - c)
        cx = jnp.where(d == 0, c - ncc, nxc - 1 - (c - ncc))
        return jnp.where(c < ncc, n_b * nxc + b * ncc + cc, b * nxc + cx)
    return f


def _dir_masks(d):
    row = lax.broadcasted_iota(jnp.int32, (CHUNK, CHUNK), 0)
    col = lax.broadcasted_iota(jnp.int32, (CHUNK, CHUNK), 1)
    diff = (row - col) * jnp.where(d == 0, 1, -1)
    return diff >= 0, diff > 0


def _cumsum_lanes(x, incl):
    cs = jnp.dot(incl.astype(F32), x, preferred_element_type=F32, precision=HIGHEST)
    return cs, cs.T


def _lane_vec(vals, offset):
    flat = vals.reshape(-1).astype(F32)
    return jnp.pad(flat, (offset, P_SMALL - offset - flat.shape[0])).reshape(1, P_SMALL)


def _ssd_kernel(xbc_ref, sm_ref, alog_ref, dtb_ref, dsk_ref, y_ref, s_scr):
    d, c = pl.program_id(1), pl.program_id(2)
    nh, hp, ns = SSD_HEADS, SSD_HEAD_DIM, SSD_STATE
    gh = nh // SSD_GROUPS

    @pl.when(c == 0)
    def _():
        s_scr[...] = jnp.zeros_like(s_scr)

    dt2 = _softplus(sm_ref[...] + dtb_ref[...])
    dta2 = dt2 * (-jnp.exp(alog_ref[...]))
    incl, _ = _dir_masks(d)
    acum2, acum2_t = _cumsum_lanes(dta2, incl)
    tot2 = jnp.sum(dta2, axis=0, keepdims=True)
    pick = lambda a: jnp.where(d == 0, a[:, 0:nh], a[:, nh:2 * nh])
    dt, acum, tot = pick(dt2), pick(acum2), pick(tot2)
    acum_t = jnp.where(d == 0, acum2_t[0:nh, :], acum2_t[nh:2 * nh, :])
    e_acum = jnp.exp(acum)
    w_end = jnp.exp(tot - acum) * dt
    c_dec = jnp.exp(tot)
    dsk = dsk_ref[...]
    bo, co = SSD_D_INNER, SSD_D_INNER + SSD_GROUPS * ns
    for g in range(SSD_GROUPS):
        bg = xbc_ref[:, bo + g * ns:bo + (g + 1) * ns]
        cg = xbc_ref[:, co + g * ns:co + (g + 1) * ns]
        scores = _dot_nt(cg, bg)
        s_g = s_scr[:, g * gh * hp:(g + 1) * gh * hp]
        y_int = _dot(cg, s_g.astype(BF16))
        xw = []
        for hh in range(gh):
            h = g * gh + hh
            xh = xbc_ref[:, h * hp:(h + 1) * hp].astype(F32)
            seg = acum[:, h:h + 1] - acum_t[h:h + 1, :]
            dec = jnp.where(incl, jnp.exp(jnp.where(incl, seg, 0.0)), 0.0)
            m = (scores * dec).astype(BF16)
            y = _dot(m, (xh * dt[:, h:h + 1]).astype(BF16))
            y = y + y_int[:, hh * hp:(hh + 1) * hp] * e_acum[:, h:h + 1]
            y = y + jnp.where(d == 0, dsk[:, h:h + 1], 0.0) * xh
            y_ref[:, h * hp:(h + 1) * hp] = y
            xw.append((xh * w_end[:, h:h + 1]).astype(BF16))
        upd = _dot_tn(bg, jnp.concatenate(xw, axis=1))
        for hh in range(gh):
            h = g * gh + hh
            sl = slice(h * hp, (h + 1) * hp)
            s_scr[:, sl] = s_scr[:, sl] * c_dec[:, h:h + 1] + upd[:, hh * hp:(hh + 1) * hp]


def _ssd_call(xbc, small, a_log, dt_bias, d_skip, *, n_b, lx, lc):
    rows = xbc.shape[0]
    nxc, ncc = lx // CHUNK, lc // CHUNK
    blk = _chunk_block(n_b, nxc, ncc)
    nh = SSD_HEADS
    return pl.pallas_call(
        _ssd_kernel,
        grid=(n_b, 2, ncc + nxc),
        in_specs=[pl.BlockSpec((CHUNK, SSD_CONV_DIM), lambda b, d, c: (blk(b, d, c), 0)),
                  pl.BlockSpec((CHUNK, P_SMALL), lambda b, d, c: (blk(b, d, c), 0)),
                  pl.BlockSpec((1, P_SMALL), lambda b, d, c: (0, 0)),
                  pl.BlockSpec((1, P_SMALL), lambda b, d, c: (0, 0)),
                  pl.BlockSpec((1, nh), lambda b, d, c: (0, 0))],
        out_specs=pl.BlockSpec((None, CHUNK, SSD_D_INNER), lambda b, d, c: (d, blk(b, d, c), 0)),
        out_shape=jax.ShapeDtypeStruct((2, rows, SSD_D_INNER), F32),
        scratch_shapes=[pltpu.VMEM((SSD_STATE, SSD_D_INNER), F32)],
        compiler_params=_cp(("parallel", "arbitrary", "arbitrary"), 40),
        name="ssd_scan",
    )(xbc, small, _lane_vec(a_log, 0), _lane_vec(dt_bias, 0), d_skip.reshape(1, nh))


SOLVE_BLOCK = 16


def _unit_tri_solve(n_mat, rhs):
    ln = n_mat.shape[0]
    row = lax.broadcasted_iota(jnp.int32, (ln, ln), 0)
    col = lax.broadcasted_iota(jnp.int32, (ln, ln), 1)
    on_diag_block = (row // SOLVE_BLOCK) == (col // SOLVE_BLOCK)
    mm = lambda a, b: _dot(a.astype(BF16), b.astype(BF16))
    m = jnp.where(on_diag_block, -n_mat, 0.0)
    e = jnp.where(on_diag_block, 0.0, n_mat)
    p = jnp.where(row == col, 1.0, 0.0) + m
    mp = m
    k = 1
    while 2 * k < SOLVE_BLOCK:
        mp = mm(mp, mp)
        p = p + mm(p, mp)
        k *= 2
    f = -mm(p, e)
    y = mm(p, rhs)
    fs = [f]
    k = 1
    while 2 * k < ln // SOLVE_BLOCK:
        fs.append(mm(fs[-1], fs[-1]))
        k *= 2
    for fp in reversed(fs):
        y = y + mm(fp, y)
    return y


def _gdn_kernel(qkv_ref, sm_ref, alog_ref, dtb_ref, o_ref, s_scr):
    d, c = pl.program_id(1), pl.program_id(2)
    nh, hd = GDN_HEADS, GDN_HEAD_DIM

    @pl.when(c == 0)
    def _():
        s_scr[...] = jnp.zeros_like(s_scr)

    sm = sm_ref[...]
    g2 = -jnp.exp(alog_ref[...]) * _softplus(sm + dtb_ref[...])
    beta2 = _sigmoid(sm)
    incl, strict = _dir_masks(d)
    gc2, gc2_t = _cumsum_lanes(g2, incl)
    gtot2 = jnp.sum(g2, axis=0, keepdims=True)
    o_g, o_b = 2 * SSD_HEADS, 2 * SSD_HEADS + 2 * nh
    for h in range(nh):
        pick_c = lambda a, o: jnp.where(d == 0, a[:, o + h:o + h + 1], a[:, o + nh + h:o + nh + h + 1])
        gcc, beta, gtot = pick_c(gc2, o_g), pick_c(beta2, o_b), pick_c(gtot2, o_g)
        gcr = jnp.where(d == 0, gc2_t[o_g + h:o_g + h + 1, :], gc2_t[o_g + nh + h:o_g + nh + h + 1, :])
        qh = qkv_ref[:, h * hd:(h + 1) * hd].astype(F32)
        kh = qkv_ref[:, GDN_DIM + h * hd:GDN_DIM + (h + 1) * hd].astype(F32)
        vh = qkv_ref[:, 2 * GDN_DIM + h * hd:2 * GDN_DIM + (h + 1) * hd].astype(F32)
        qn = qh * (lax.rsqrt(jnp.sum(qh * qh, axis=-1, keepdims=True) + EPS) * hd ** -0.5)
        kn = kh * lax.rsqrt(jnp.sum(kh * kh, axis=-1, keepdims=True) + EPS)
        dec = jnp.where(incl, jnp.exp(jnp.where(incl, gcc - gcr, 0.0)), 0.0)
        kb = kn * beta
        kn_b = kn.astype(BF16)
        n_mat = jnp.where(strict, _dot_nt(kb.astype(BF16), kn_b) * dec, 0.0)
        e_gc = jnp.exp(gcc)
        sol = _unit_tri_solve(n_mat, jnp.concatenate([vh * beta, kb * e_gc], axis=1))
        u, w = sol[:, 0:hd], sol[:, hd:2 * hd]
        qk = _dot_nt(qn.astype(BF16), kn_b) * dec
        s_b = s_scr[h].astype(BF16)
        v_new = u - _dot(w.astype(BF16), s_b)
        v_new_b = v_new.astype(BF16)
        o = _dot((qn * e_gc).astype(BF16), s_b) + _dot(qk.astype(BF16), v_new_b)
        o_ref[:, h * hd:(h + 1) * hd] = o
        k_dec = (kn * jnp.exp(gtot - gcc)).astype(BF16)
        s_scr[h] = s_scr[h] * jnp.exp(gtot) + _dot_tn(k_dec, v_new_b)


def _gdn_call(qkv, small, a_log, dt_bias, *, n_b, lx, lc):
    rows = qkv.shape[0]
    nxc, ncc = lx // CHUNK, lc // CHUNK
    blk = _chunk_block(n_b, nxc, ncc)
    nh = GDN_HEADS
    return pl.pallas_call(
        _gdn_kernel,
        grid=(n_b, 2, ncc + nxc),
        in_specs=[pl.BlockSpec((CHUNK, 3 * GDN_DIM), lambda b, d, c: (blk(b, d, c), 0)),
                  pl.BlockSpec((CHUNK, P_SMALL), lambda b, d, c: (blk(b, d, c), 0)),
                  pl.BlockSpec((1, P_SMALL), lambda b, d, c: (0, 0)),
                  pl.BlockSpec((1, P_SMALL), lambda b, d, c: (0, 0))],
        out_specs=pl.BlockSpec((None, CHUNK, GDN_DIM), lambda b, d, c: (d, blk(b, d, c), 0)),
        out_shape=jax.ShapeDtypeStruct((2, rows, GDN_DIM), F32),
        scratch_shapes=[pltpu.VMEM((nh, GDN_HEAD_DIM, GDN_HEAD_DIM), F32)],
        compiler_params=_cp(("parallel", "arbitrary", "arbitrary"), 40),
        name="gdn_scan",
    )(qkv, small, _lane_vec(a_log, 2 * SSD_HEADS), _lane_vec(dt_bias, 2 * SSD_HEADS))


def _mixout_kernel(h_ref, mod_ref, gpost_ref, ya_ref, ys_ref, zs_ref, sn_ref, yn_ref, og_ref, zg_ref, gn_ref,
                   w_ref, o_ref):
    ssd = (ys_ref[0] + ys_ref[1]) * _silu(zs_ref[...].astype(F32))
    yb = (ssd * _rms_scale(ssd) * sn_ref[...]).astype(BF16)
    gd = og_ref[0] + og_ref[1]
    zg = _silu(zg_ref[...].astype(F32))
    hd = GDN_HEAD_DIM
    yd = []
    for h in range(GDN_HEADS):
        oh = gd[:, h * hd:(h + 1) * hd]
        yd.append((oh * _rms_scale(oh) * gn_ref[...] * zg[:, h * hd:(h + 1) * hd]).astype(BF16))
    parts = [ya_ref[...], yb, yn_ref[...]] + yd
    widths = [512, 512, 512] + [hd] * GDN_HEADS
    y = None
    off = 0
    for part, wd in zip(parts, widths):
        t = _dot(part, w_ref[off:off + wd, :])
        y = t if y is None else y + t
        off += wd
    o_ref[...] = h_ref[...] + mod_ref[2:3, :] * (y * _rms_scale(y) * gpost_ref[...])


def _mixout_call(h, mod_l, g_post, ya, ys, proj, ssd_norm, yn, og, gdn_norm, w_out, *, n_b, lx, n_rows, tm):
    d = h.shape[1]
    nxt, per_b = n_b * lx // tm, lx // tm
    mod_idx = lambda i: (jnp.where(i < nxt, i // per_b, n_b), 0, 0)
    row = lambda i: (i, 0)
    const = lambda i: (0, 0)
    return pl.pallas_call(
        _mixout_kernel,
        grid=(n_rows // tm,),
        in_specs=[pl.BlockSpec((tm, d), row),
                  pl.BlockSpec((None, 6, d), mod_idx),
                  pl.BlockSpec((1, d), const),
                  pl.BlockSpec((tm, 512), row),
                  pl.BlockSpec((2, tm, 512), lambda i: (0, i, 0)),
                  pl.BlockSpec((tm, 512), lambda i: (i, P_ZSSD // 512)),
                  pl.BlockSpec((1, 512), const),
                  pl.BlockSpec((tm, 512), row),
                  pl.BlockSpec((2, tm, 512), lambda i: (0, i, 0)),
                  pl.BlockSpec((tm, 512), lambda i: (i, P_GZ // 512)),
                  pl.BlockSpec((1, GDN_HEAD_DIM), const),
                  pl.BlockSpec((d, d), const)],
        out_specs=pl.BlockSpec((tm, d), row),
        out_shape=jax.ShapeDtypeStruct((n_rows, d), F32),
        compiler_params=_cp(("parallel",), 56),
        name="mix_out",
    )(h, mod_l, g_post.reshape(1, d), ya, ys, proj, ssd_norm.reshape(1, 512), yn, og, proj,
      gdn_norm.reshape(1, GDN_HEAD_DIM), w_out)


def _ffn_kernel(h_ref, mod_ref, gpre_ref, gpost_ref, wg_ref, wu_ref, wd_ref, o_ref, u_scr, acc_scr):
    j = pl.program_id(1)

    @pl.when(j == 0)
    def _():
        x = h_ref[...]
        y = x * _rms_scale(x) * gpre_ref[...]
        u_scr[...] = (y * (1.0 + mod_ref[4:5, :]) + mod_ref[3:4, :]).astype(BF16)
        acc_scr[...] = jnp.zeros_like(acc_scr)

    u = u_scr[...]
    mid = (_silu(_dot(u, wg_ref[...])) * _dot(u, wu_ref[...])).astype(BF16)
    acc_scr[...] += _dot(mid, wd_ref[...])

    @pl.when(j == pl.num_programs(1) - 1)
    def _():
        y = acc_scr[...]
        o_ref[...] = h_ref[...] + mod_ref[5:6, :] * (y * _rms_scale(y) * gpost_ref[...])


def _ffn_call(h, mod_l, g_pre, g_post, wg, wu, wd, *, n_b, lx, tm, tf):
    rows, d = h.shape
    ff = wg.shape[1]
    nxt, per_b = n_b * lx // tm, lx // tm
    mod_idx = lambda i, j: (jnp.where(i < nxt, i // per_b, n_b), 0, 0)
    return pl.pallas_call(
        _ffn_kernel,
        grid=(rows // tm, ff // tf),
        in_specs=[pl.BlockSpec((tm, d), lambda i, j: (i, 0)),
                  pl.BlockSpec((None, 6, d), mod_idx),
                  pl.BlockSpec((1, d), lambda i, j: (0, 0)),
                  pl.BlockSpec((1, d), lambda i, j: (0, 0)),
                  pl.BlockSpec((d, tf), lambda i, j: (0, j)),
                  pl.BlockSpec((d, tf), lambda i, j: (0, j)),
                  pl.BlockSpec((tf, d), lambda i, j: (j, 0))],
        out_specs=pl.BlockSpec((tm, d), lambda i, j: (i, 0)),
        out_shape=jax.ShapeDtypeStruct((rows, d), F32),
        scratch_shapes=[pltpu.VMEM((tm, d), BF16), pltpu.VMEM((tm, d), F32)],
        compiler_params=_cp(("parallel", "arbitrary"), 56),
        name="ffn_swiglu",
    )(h, mod_l, g_pre.reshape(1, d), g_post.reshape(1, d), wg, wu, wd)


def _moe_kernel(h_ref, mod_ref, gpre_ref, gpost_ref, wr_ref, wg_ref, wu_ref, wd_ref, o_ref,
                u_scr, gate_scr, acc_scr):
    e, j = pl.program_id(1), pl.program_id(2)
    lanes = gate_scr.shape[1]

    @pl.when((e == 0) & (j == 0))
    def _():
        x = h_ref[...]
        y = x * _rms_scale(x) * gpre_ref[...]
        u = y * (1.0 + mod_ref[4:5, :]) + mod_ref[3:4, :]
        u_scr[...] = u.astype(BF16)
        acc_scr[...] = jnp.zeros_like(acc_scr)
        logits = jnp.dot(u, wr_ref[...], preferred_element_type=F32, precision=HIGHEST)
        lane = lax.broadcasted_iota(jnp.int32, logits.shape, 1).astype(F32)
        lg = jnp.where(lane < N_EXPERTS, logits, NEG)
        m1 = jnp.max(lg, axis=-1, keepdims=True)
        i1 = jnp.min(jnp.where(lg == m1, lane, float(lanes)), axis=-1, keepdims=True)
        lg2 = jnp.where(lane == i1, NEG, lg)
        m2 = jnp.max(lg2, axis=-1, keepdims=True)
        i2 = jnp.min(jnp.where(lg2 == m2, lane, float(lanes)), axis=-1, keepdims=True)
        e2 = jnp.exp(m2 - m1)
        gate_scr[...] = jnp.where(lane == i1, 1.0 / (1.0 + e2), 0.0) + jnp.where(lane == i2, e2 / (1.0 + e2), 0.0)

    lane = lax.broadcasted_iota(jnp.int32, gate_scr.shape, 1)
    ge = jnp.sum(jnp.where(lane == e, gate_scr[...], 0.0), axis=-1, keepdims=True)
    u = u_scr[...]
    mid = (_silu(_dot(u, wg_ref[...])) * _dot(u, wu_ref[...]) * ge).astype(BF16)
    acc_scr[...] += _dot(mid, wd_ref[...])

    @pl.when((e == pl.num_programs(1) - 1) & (j == pl.num_programs(2) - 1))
    def _():
        y = acc_scr[...]
        o_ref[...] = h_ref[...] + mod_ref[5:6, :] * (y * _rms_scale(y) * gpost_ref[...])


def _moe_call(h, mod_l, g_pre, g_post, router_pad, wg, wu, wd, *, n_b, lx, tm, tf):
    rows, d = h.shape
    ne, _, fe = wg.shape
    nxt, per_b = n_b * lx // tm, lx // tm
    mod_idx = lambda i, e, j: (jnp.where(i < nxt, i // per_b, n_b), 0, 0)
    c2 = lambda i, e, j: (0, 0)
    return pl.pallas_call(
        _moe_kernel,
        grid=(rows // tm, ne, fe // tf),
        in_specs=[pl.BlockSpec((tm, d), lambda i, e, j: (i, 0)),
                  pl.BlockSpec((None, 6, d), mod_idx),
                  pl.BlockSpec((1, d), c2),
                  pl.BlockSpec((1, d), c2),
                  pl.BlockSpec((d, 128), c2),
                  pl.BlockSpec((None, d, tf), lambda i, e, j: (e, 0, j)),
                  pl.BlockSpec((None, d, tf), lambda i, e, j: (e, 0, j)),
                  pl.BlockSpec((None, tf, d), lambda i, e, j: (e, j, 0))],
        out_specs=pl.BlockSpec((tm, d), lambda i, e, j: (i, 0)),
        out_shape=jax.ShapeDtypeStruct((rows, d), F32),
        scratch_shapes=[pltpu.VMEM((tm, d), BF16), pltpu.VMEM((tm, 128), F32), pltpu.VMEM((tm, d), F32)],
        compiler_params=_cp(("parallel", "arbitrary", "arbitrary"), 56),
        name="moe_swiglu",
    )(h, mod_l, g_pre.reshape(1, d), g_post.reshape(1, d), router_pad, wg, wu, wd)


def _rope_tables(lx, tm):
    half = MLA_ROPE // 2
    n_axis = half // 2
    inv_freq = ROPE_THETA ** (-jnp.arange(n_axis, dtype=F32) / n_axis)
    pos = jnp.arange(lx)
    rows = (pos // GRID_W).astype(F32)
    cols = (pos % GRID_W).astype(F32)
    ang = jnp.concatenate([rows[:, None] * inv_freq, cols[:, None] * inv_freq], axis=-1)
    cos, sin = jnp.cos(ang), jnp.sin(ang)
    cos_t = jnp.concatenate([cos, cos], axis=-1)
    sin_t = jnp.concatenate([-sin, sin], axis=-1)
    cos_t = jnp.concatenate([cos_t, jnp.ones((tm, MLA_ROPE), F32)], axis=0)
    sin_t = jnp.concatenate([sin_t, jnp.zeros((tm, MLA_ROPE), F32)], axis=0)
    return cos_t, sin_t


def _mla_weights(q_norm, w_uq, kv_norm, w_ukv):
    dq = MLA_NOPE + MLA_ROPE
    half = MLA_ROPE // 2
    nope = np.concatenate([np.arange(h * dq, h * dq + MLA_NOPE) for h in range(MLA_HEADS)])
    rope = np.concatenate([np.arange(h * dq + MLA_NOPE, (h + 1) * dq) for h in range(MLA_HEADS)])
    rope_sw = np.concatenate([np.concatenate([np.arange(h * dq + MLA_NOPE + half, (h + 1) * dq),
                                              np.arange(h * dq + MLA_NOPE, h * dq + MLA_NOPE + half)])
                              for h in range(MLA_HEADS)])
    pad = ((0, MLA_ROPE), (0, 0))
    wq = jnp.pad(w_uq[:, np.concatenate([nope, rope])], pad).astype(BF16)
    wqs = jnp.pad(w_uq[:, rope_sw], pad).astype(BF16)
    qn_ext = jnp.pad(q_norm, (0, MLA_ROPE)).reshape(1, -1)
    perm = np.zeros((MLA_ROPE, MLA_ROPE), np.float32)
    perm[(np.arange(MLA_ROPE) + half) % MLA_ROPE, np.arange(MLA_ROPE)] = 1.0
    return qn_ext, wq, wqs, kv_norm.reshape(1, -1), w_ukv.astype(BF16), jnp.asarray(perm, BF16)


def _pick_tile(n, cands):
    for t in cands:
        if n % t == 0:
            return t
    raise ValueError(f"no tile for {n}")


def kernel(x, c, ctx, c_ctx, w_ada, b_ada, g_pre_mix, g_post_mix, g_pre_ffn, g_post_ffn, w_in, w_out, mla_q_norm, mla_w_uq, mla_kv_norm, mla_w_ukv, ssd_conv_w, ssd_conv_b, ssd_a_log, ssd_dt_bias, ssd_d, ssd_norm, na_rpb, gdn_conv_w, gdn_a_log, gdn_dt_bias, gdn_norm, ffn_w_gate, ffn_w_up, ffn_w_down, moe_router, moe_w_gate, moe_w_up, moe_w_down):
    n_b, lx, d = x.shape
    lc = ctx.shape[1]
    depth = w_ada.shape[0]
    rows_x, rows_c = n_b * lx, n_b * lc
    assert n_b + 1 <= 8 and lx % GRID_W == 0 and lx % lc == 0 and lc % CHUNK == 0
    tm = _pick_tile(math.gcd(lx, rows_c), (512, 256, 128))
    tq = _pick_tile(lc, (256, 128))
    tr = _pick_tile(lc, (256, 128))

    cvec = jnp.concatenate([c, c_ctx[None, :], jnp.zeros((8 - n_b - 1, d), F32)], axis=0)
    mod = _ada_call(cvec, w_ada, b_ada).reshape(depth, 8, 6, d)
    main_cols, small_cols = _in_proj_layout()
    cos_t, sin_t = _rope_tables(lx, tm)
    h_all = jnp.concatenate([x.reshape(rows_x, d), ctx.reshape(rows_c, d)], axis=0)

    for i in range(depth):
        need_ctx = i < depth - 1
        w_main = w_in[i][:, main_cols].astype(BF16)
        w_small = jnp.pad(w_in[i][:, small_cols], ((0, 0), (0, P_SMALL - small_cols.shape[0]))).astype(BF16)
        proj, small = _inproj_call(h_all, mod[i], g_pre_mix[i], w_main, w_small, n_b=n_b, lx=lx, tm=tm)

        mla_w = _mla_weights(mla_q_norm[i], mla_w_uq[i], mla_kv_norm[i], mla_w_ukv[i])
        q_a, k_a, v_a = _mla_prep_call(proj, *mla_w[:5], cos_t, sin_t, mla_w[5], n_b=n_b, lx=lx, tm=tm)
        ya = _mla_attn_call(q_a, k_a, v_a, n_b=n_b, lx=lx, lc=lc, tq=tq, need_ctx=need_ctx)

        xbc = _conv_call(proj, ssd_conv_w[i], ssd_conv_b[i], col_off=P_XBC, n_b=n_b, lx=lx, lc=lc, tr=tr)
        ys = _ssd_call(xbc, small, ssd_a_log[i], ssd_dt_bias[i], ssd_d[i], n_b=n_b, lx=lx, lc=lc)

        yn = _na_call(proj, _na_bias_table(na_rpb[i], lx // GRID_W), n_b=n_b, lx=lx, lc=lc, need_ctx=need_ctx)

        qkv = _conv_call(proj, gdn_conv_w[i], jnp.zeros((3 * GDN_DIM,), F32), col_off=P_GQKV,
                         n_b=n_b, lx=lx, lc=lc, tr=tr)
        og = _gdn_call(qkv, small, gdn_a_log[i], gdn_dt_bias[i], n_b=n_b, lx=lx, lc=lc)

        n_rows = rows_x + rows_c if need_ctx else rows_x
        h_mid = _mixout_call(h_all, mod[i], g_post_mix[i], ya, ys, proj, ssd_norm[i], yn, og, gdn_norm[i],
                             w_out[i].astype(BF16), n_b=n_b, lx=lx, n_rows=n_rows, tm=min(tm, 256))
        j = i // 2
        if i % 2 == 0:
            h_all = _ffn_call(h_mid, mod[i], g_pre_ffn[i], g_post_ffn[i], ffn_w_gate[j].astype(BF16),
                              ffn_w_up[j].astype(BF16), ffn_w_down[j].astype(BF16), n_b=n_b, lx=lx, tm=tm, tf=512)
        else:
            router_pad = jnp.pad(moe_router[j], ((0, 0), (0, 128 - N_EXPERTS)))
            h_all = _moe_call(h_mid, mod[i], g_pre_ffn[i], g_post_ffn[i], router_pad, moe_w_gate[j].astype(BF16),
                              moe_w_up[j].astype(BF16), moe_w_down[j].astype(BF16), n_b=n_b, lx=lx, tm=tm, tf=256)
    return h_all[:rows_x].reshape(n_b, lx, d)
```

```python
import functools
import math

import numpy as np
import jax
import jax.numpy as jnp
from jax import lax
from jax.experimental import pallas as pl
from jax.experimental.pallas import tpu as pltpu

F32 = jnp.float32
BF16 = jnp.bfloat16
HIGHEST = lax.Precision.HIGHEST

GRID_W = 64
EPS = 1e-6
ROPE_THETA = 10000.0
CHUNK = 128
CONV_W = 5
MLA_HEADS, MLA_NOPE, MLA_ROPE, MLA_V = 4, 128, 64, 128
MLA_Q_LORA, MLA_KV_LORA = 448, 128
SSD_HEADS, SSD_HEAD_DIM, SSD_STATE, SSD_GROUPS = 8, 64, 128, 2
SSD_D_INNER = SSD_HEADS * SSD_HEAD_DIM
SSD_CONV_DIM = SSD_D_INNER + 2 * SSD_GROUPS * SSD_STATE
NA_HEADS, NA_HEAD_DIM = 4, 128
NA_DIM = NA_HEADS * NA_HEAD_DIM
NA_WIN_ROWS, NA_WIN_COLS = 8, 16
GDN_HEADS, GDN_HEAD_DIM = 4, 128
GDN_DIM = GDN_HEADS * GDN_HEAD_DIM
N_EXPERTS, TOP_K = 8, 2
MLA_COLS = MLA_Q_LORA + MLA_KV_LORA + MLA_ROPE
SSD_COLS = SSD_D_INNER + SSD_CONV_DIM + 2 * SSD_HEADS
NA_COLS = 3 * NA_DIM
GDN_COLS = 4 * GDN_DIM + 4 * GDN_HEADS

P_XBC, P_ZSSD, P_NAQ, P_NAK, P_NAV = 0, 1024, 1536, 2048, 2560
P_GQKV, P_GZ, P_MLA = 3072, 4608, 5120
P_MAIN = 5760
P_SMALL = 128
NEG = -1e30
VMEM_MB = 1024 * 1024


def _cp(sem, mb):
    return pltpu.CompilerParams(dimension_semantics=sem, vmem_limit_bytes=mb * VMEM_MB)


def _dot(a, b):
    return jnp.dot(a, b, preferred_element_type=F32)


def _dot_nt(a, b, precision=None):
    return lax.dot_general(a, b, (((1,), (1,)), ((), ())), preferred_element_type=F32, precision=precision)


def _dot_tn(a, b):
    return lax.dot_general(a, b, (((0,), (0,)), ((), ())), preferred_element_type=F32)


def _sigmoid(x):
    return 1.0 / (1.0 + jnp.exp(-x))


def _silu(x):
    return x * _sigmoid(x)


def _softplus(x):
    return jnp.maximum(x, 0.0) + jnp.log(1.0 + jnp.exp(-jnp.abs(x)))


def _rms_scale(x):
    return lax.rsqrt(jnp.mean(x * x, axis=-1, keepdims=True) + EPS)


def _regroup_w_in(w):
    o_mla, o_ssd = 0, MLA_COLS
    o_na, o_gdn = o_ssd + SSD_COLS, o_ssd + SSD_COLS + NA_COLS
    main_segs = [
        (o_ssd + SSD_D_INNER, SSD_CONV_DIM),
        (o_ssd, SSD_D_INNER),
        (o_na, NA_COLS),
        (o_gdn, 4 * GDN_DIM),
        (o_mla + MLA_Q_LORA, MLA_KV_LORA),
        (o_mla, MLA_Q_LORA),
        (o_mla + MLA_Q_LORA + MLA_KV_LORA, MLA_ROPE),
    ]
    small_segs = [(o_ssd + SSD_D_INNER + SSD_CONV_DIM, 2 * SSD_HEADS), (o_gdn + 4 * GDN_DIM, 4 * GDN_HEADS)]
    assert sum(n for _, n in main_segs) == P_MAIN
    main = jnp.concatenate([w[:, a:a + n] for a, n in main_segs], axis=1).astype(BF16)
    n_small = sum(n for _, n in small_segs)
    small = jnp.concatenate([w[:, a:a + n] for a, n in small_segs]
                            + [jnp.zeros((w.shape[0], P_SMALL - n_small), w.dtype)], axis=1).astype(BF16)
    return main, small


def _ada_kernel(c_ref, w_ref, b_ref, o_ref):
    s = _silu(c_ref[...]).astype(BF16)
    o_ref[...] = _dot(s, w_ref[...].astype(BF16)) + b_ref[...]


def _ada_call(cvec, w_ada, b_ada):
    depth, d, n = w_ada.shape
    tn = 1024
    return pl.pallas_call(
        _ada_kernel,
        grid=(depth, n // tn),
        in_specs=[pl.BlockSpec((8, d), lambda l, j: (0, 0)),
                  pl.BlockSpec((None, d, tn), lambda l, j: (l, 0, j)),
                  pl.BlockSpec((None, 1, tn), lambda l, j: (l, 0, j))],
        out_specs=pl.BlockSpec((None, 8, tn), lambda l, j: (l, 0, j)),
        out_shape=jax.ShapeDtypeStruct((depth, 8, n), F32),
        compiler_params=_cp(("parallel", "parallel"), 40),
        name="adaln",
    )(cvec, w_ada, b_ada.reshape(depth, 1, n))


def _inproj_kernel(h_ref, mod_ref, g_ref, w_ref, ws_ref, o_ref, os_ref, u_scr):
    @pl.when(pl.program_id(1) == 0)
    def _():
        x = h_ref[...]
        y = x * _rms_scale(x) * g_ref[...]
        u = (y * (1.0 + mod_ref[1:2, :]) + mod_ref[0:1, :]).astype(BF16)
        u_scr[...] = u
        os_ref[...] = _dot(u, ws_ref[...])

    o_ref[...] = _dot(u_scr[...], w_ref[...]).astype(BF16)


def _inproj_call(h, mod_l, g_pre, w_main, w_small, *, n_b, lx, tm):
    rows, d = h.shape
    tn = 640
    nxt, per_b = n_b * lx // tm, lx // tm

    def mod_idx(i, j):
        return (jnp.where(i < nxt, i // per_b, n_b), 0, 0)

    return pl.pallas_call(
        _inproj_kernel,
        grid=(rows // tm, P_MAIN // tn),
        in_specs=[pl.BlockSpec((tm, d), lambda i, j: (i, 0)),
                  pl.BlockSpec((None, 6, d), mod_idx),
                  pl.BlockSpec((1, d), lambda i, j: (0, 0)),
                  pl.BlockSpec((d, tn), lambda i, j: (0, j)),
                  pl.BlockSpec((d, P_SMALL), lambda i, j: (0, 0))],
        out_specs=[pl.BlockSpec((tm, tn), lambda i, j: (i, j)),
                   pl.BlockSpec((tm, P_SMALL), lambda i, j: (i, 0))],
        out_shape=[jax.ShapeDtypeStruct((rows, P_MAIN), BF16),
                   jax.ShapeDtypeStruct((rows, P_SMALL), F32)],
        scratch_shapes=[pltpu.VMEM((tm, d), BF16)],
        compiler_params=_cp(("parallel", "arbitrary"), 40),
        name="in_proj",
    )(h, mod_l, g_pre.reshape(1, d), w_main, w_small)


HALO = 16


def _conv_kernel(prev_ref, cur_ref, next_ref, w_ref, b_ref, o_ref, ext_scr, *, tr, blocks_x, seq_x, seq_c):
    i = pl.program_id(0)
    in_x = i < blocks_x
    pos = jnp.where(in_x, i % seq_x, (i - blocks_x) % seq_c)
    last_pos = jnp.where(in_x, seq_x - 1, seq_c - 1)
    ext_scr[0:HALO, :] = jnp.where(pos == 0, 0.0, prev_ref[...].astype(F32))
    ext_scr[HALO:HALO + tr, :] = cur_ref[...].astype(F32)
    ext_scr[HALO + tr:2 * HALO + tr, :] = jnp.where(pos == last_pos, 0.0, next_ref[...].astype(F32))
    acc = b_ref[...] + w_ref[0:1, :] * ext_scr[HALO - 2:HALO - 2 + tr, :]
    for k in range(1, CONV_W):
        acc = acc + w_ref[k:k + 1, :] * ext_scr[HALO - 2 + k:HALO - 2 + k + tr, :]
    o_ref[...] = _silu(acc).astype(BF16)


def _conv_call(proj, w, b, *, col_off, n_b, lx, lc, tr):
    rows = proj.shape[0]
    c = w.shape[1]
    cb = col_off // c
    assert cb * c == col_off
    hb = tr // HALO
    n_halo = rows // HALO
    kern = functools.partial(_conv_kernel, tr=tr, blocks_x=n_b * lx // tr, seq_x=lx // tr, seq_c=lc // tr)
    return pl.pallas_call(
        kern,
        grid=(rows // tr,),
        in_specs=[pl.BlockSpec((HALO, c), lambda i: (jnp.maximum(i * hb - 1, 0), cb)),
                  pl.BlockSpec((tr, c), lambda i: (i, cb)),
                  pl.BlockSpec((HALO, c), lambda i: (jnp.minimum((i + 1) * hb, n_halo - 1), cb)),
                  pl.BlockSpec((CONV_W, c), lambda i: (0, 0)),
                  pl.BlockSpec((1, c), lambda i: (0, 0))],
        out_specs=pl.BlockSpec((tr, c), lambda i: (i, 0)),
        out_shape=jax.ShapeDtypeStruct((rows, c), BF16),
        scratch_shapes=[pltpu.VMEM((tr + 2 * HALO, c), F32)],
        compiler_params=_cp(("parallel",), 40),
        name="dwconv_silu",
    )(proj, proj, proj, w, b.reshape(1, c))


def _mla_prep_kernel(p_ref, qn_ref, wq_ref, wqs_ref, kvn_ref, wkv_ref, cos_ref, sin_ref, perm_ref,
                     q_ref, k_ref, v_ref):
    p = p_ref[...].astype(F32)
    ckv = p[:, 0:MLA_KV_LORA]
    ce = p[:, MLA_KV_LORA:]
    lane = lax.broadcasted_iota(jnp.int32, ce.shape, 1)
    ssq = jnp.sum(jnp.where(lane < MLA_Q_LORA, ce * ce, 0.0), axis=-1, keepdims=True)
    cqn = (ce * lax.rsqrt(ssq / MLA_Q_LORA + EPS) * qn_ref[...]).astype(BF16)
    ckvn = (ckv * _rms_scale(ckv) * kvn_ref[...]).astype(BF16)
    q = _dot(cqn, wq_ref[...])
    qs = _dot(cqn, wqs_ref[...])
    kv = _dot(ckvn, wkv_ref[...])
    cos, sin = cos_ref[...], sin_ref[...]
    kr = p_ref[:, MLA_KV_LORA + MLA_Q_LORA:]
    kr_rot = kr.astype(F32) * cos + _dot(kr, perm_ref[...]) * sin
    nr = MLA_HEADS * MLA_NOPE
    for h in range(MLA_HEADS):
        q_ref[h, :, 0:MLA_NOPE] = q[:, h * MLA_NOPE:(h + 1) * MLA_NOPE].astype(BF16)
        qr = q[:, nr + h * MLA_ROPE:nr + (h + 1) * MLA_ROPE] * cos + qs[:, h * MLA_ROPE:(h + 1) * MLA_ROPE] * sin
        q_ref[h, :, MLA_NOPE:] = qr.astype(BF16)
        hv = h * (MLA_NOPE + MLA_V)
        k_ref[h, :, 0:MLA_NOPE] = kv[:, hv:hv + MLA_NOPE].astype(BF16)
        k_ref[h, :, MLA_NOPE:] = kr_rot.astype(BF16)
        v_ref[h] = kv[:, hv + MLA_NOPE:hv + MLA_NOPE + MLA_V].astype(BF16)


def _mla_prep_call(proj, qn_ext, wq, wqs, kvn, wkv, cos_t, sin_t, perm, *, n_b, lx, tm):
    rows = proj.shape[0]
    nxt, per_b = n_b * lx // tm, lx // tm
    dk = MLA_NOPE + MLA_ROPE
    rope_idx = lambda i: (jnp.where(i < nxt, i % per_b, per_b), 0)
    full = lambda a: pl.BlockSpec(a.shape, lambda i: (0,) * a.ndim)
    return pl.pallas_call(
        _mla_prep_kernel,
        grid=(rows // tm,),
        in_specs=[pl.BlockSpec((tm, MLA_COLS), lambda i: (i, P_MLA // MLA_COLS)),
                  full(qn_ext), full(wq), full(wqs), full(kvn), full(wkv),
                  pl.BlockSpec((tm, MLA_ROPE), rope_idx), pl.BlockSpec((tm, MLA_ROPE), rope_idx),
                  full(perm)],
        out_specs=[pl.BlockSpec((MLA_HEADS, tm, dk), lambda i: (0, i, 0)),
                   pl.BlockSpec((MLA_HEADS, tm, dk), lambda i: (0, i, 0)),
                   pl.BlockSpec((MLA_HEADS, tm, MLA_V), lambda i: (0, i, 0))],
        out_shape=[jax.ShapeDtypeStruct((MLA_HEADS, rows, dk), BF16),
                   jax.ShapeDtypeStruct((MLA_HEADS, rows, dk), BF16),
                   jax.ShapeDtypeStruct((MLA_HEADS, rows, MLA_V), BF16)],
        compiler_params=_cp(("parallel",), 40),
        name="mla_prep",
    )(proj, qn_ext, wq, wqs, kvn, wkv, cos_t, sin_t, perm)


def _softmax_pv(scores, values):
    m = functools.reduce(jnp.maximum, [jnp.max(s, axis=-1, keepdims=True) for s in scores])
    ps = [jnp.exp(s - m) for s in scores]
    den = functools.reduce(lambda a, b: a + b, [jnp.sum(p, axis=-1, keepdims=True) for p in ps])
    num = functools.reduce(lambda a, b: a + b, [_dot(p.astype(BF16), v) for p, v in zip(ps, values)])
    return num / den


def _mla_attn_kernel(q_ref, kx_ref, vx_ref, kc_ref, vc_ref, o_ref, *, nqx, scale):
    qi = pl.program_id(2)
    q = q_ref[...]
    sc = _dot_nt(q, kc_ref[...]) * scale

    @pl.when(qi < nqx)
    def _():
        sx = _dot_nt(q, kx_ref[...]) * scale
        o_ref[...] = _softmax_pv([sx, sc], [vx_ref[...], vc_ref[...]]).astype(BF16)

    @pl.when(qi >= nqx)
    def _():
        o_ref[...] = _softmax_pv([sc], [vc_ref[...]]).astype(BF16)


def _mla_attn_call(q, k, v, *, n_b, lx, lc, tq, need_ctx):
    rows = q.shape[1]
    dk = q.shape[2]
    nqx, nqc = lx // tq, lc // tq
    nq = nqx + (nqc if need_ctx else 0)
    nbx = n_b * lx // lc

    def q_row(b, qi):
        return jnp.where(qi < nqx, b * nqx + qi, n_b * nqx + b * nqc + (qi - nqx))

    kern = functools.partial(_mla_attn_kernel, nqx=nqx, scale=dk ** -0.5)
    return pl.pallas_call(
        kern,
        grid=(n_b, MLA_HEADS, nq),
        in_specs=[pl.BlockSpec((None, tq, dk), lambda b, h, qi: (h, q_row(b, qi), 0)),
                  pl.BlockSpec((None, lx, dk), lambda b, h, qi: (h, b, 0)),
                  pl.BlockSpec((None, lx, MLA_V), lambda b, h, qi: (h, b, 0)),
                  pl.BlockSpec((None, lc, dk), lambda b, h, qi: (h, nbx + b, 0)),
                  pl.BlockSpec((None, lc, MLA_V), lambda b, h, qi: (h, nbx + b, 0))],
        out_specs=pl.BlockSpec((tq, MLA_V), lambda b, h, qi: (q_row(b, qi), h)),
        out_shape=jax.ShapeDtypeStruct((rows, MLA_HEADS * MLA_V), BF16),
        compiler_params=_cp(("parallel", "parallel", "arbitrary"), 48),
        name="mla_attn",
    )(q, k, v, k, v)


def _na_plan(g_rows, lc):
    wr = min(NA_WIN_ROWS, g_rows)
    rg = next(r for r in (4, 2, 1) if g_rows % r == 0 and lc % (r * GRID_W) == 0)
    wk = min(rg + wr - 1, g_rows)
    n_groups = g_rows // rg
    ks = np.clip(np.arange(n_groups) * rg - wr // 2, 0, g_rows - wk)
    r = np.arange(g_rows)
    rs = np.clip(r - wr // 2, 0, g_rows - wr)
    q_off = (r - np.repeat(ks, rg)).reshape(n_groups, rg)
    rel = (rs - np.repeat(ks, rg)).reshape(n_groups, rg)
    assert (rel >= 0).all() and (rel + wr <= wk).all()
    pats = [tuple(q_off[g]) + tuple(rel[g]) for g in range(n_groups)]
    uniq = sorted(set(pats))
    var = np.array([uniq.index(p) for p in pats], np.int32)
    q_off_v = np.array([p[:rg] for p in uniq])
    rel_v = np.array([p[rg:] for p in uniq])
    return wr, rg, wk, ks.astype(np.int32), var, q_off_v, rel_v


def _na_bias_table(rpb, g_rows, lc):
    wr, rg, wk, _, _, q_off_v, rel_v = _na_plan(g_rows, lc)
    col_start = np.clip(np.arange(GRID_W) - NA_WIN_COLS // 2, 0, GRID_W - NA_WIN_COLS)
    cc = np.arange(GRID_W)
    col_ok = (cc[None, :] >= col_start[:, None]) & (cc[None, :] < col_start[:, None] + NA_WIN_COLS)
    dc = np.clip(cc[None, :] - cc[:, None] + NA_WIN_COLS - 1, 0, 2 * NA_WIN_COLS - 2)
    w = np.arange(wk)
    row_ok = (w[None, None, :] >= rel_v[:, :, None]) & (w[None, None, :] < rel_v[:, :, None] + wr)
    dr = np.clip(w[None, None, :] - q_off_v[:, :, None] + NA_WIN_ROWS - 1, 0, 2 * NA_WIN_ROWS - 2)
    t = rpb[:, dr[:, :, None, :, None], dc[None, None, :, None, :]]
    ok = row_ok[:, :, None, :, None] & col_ok[None, None, :, None, :]
    t = jnp.where(ok[None], t.astype(F32), NEG)
    nv = q_off_v.shape[0]
    return jnp.transpose(t, (1, 0, 2, 3, 4, 5)).reshape(nv, rpb.shape[0], rg * GRID_W, wk * GRID_W)


def _na_kernel(var_ref, ks_ref, q_ref, kx_ref, vx_ref, kc_ref, vc_ref, bias_ref, o_ref, *, n_groups, wk, scale):
    g = pl.program_id(2)
    q = q_ref[...]
    sc = _dot_nt(q, kc_ref[...]) * scale

    @pl.when(g < n_groups)
    def _():
        start = pl.multiple_of(ks_ref[g] * GRID_W, GRID_W)
        kw = kx_ref[pl.ds(start, wk * GRID_W), :]
        vw = vx_ref[pl.ds(start, wk * GRID_W), :]
        sl = _dot_nt(q, kw) * scale + bias_ref[...]
        o_ref[...] = _softmax_pv([sl, sc], [vw, vc_ref[...]]).astype(BF16)

    @pl.when(g >= n_groups)
    def _():
        o_ref[...] = _softmax_pv([sc], [vc_ref[...]]).astype(BF16)


def _na_call(proj, bias_tab, *, n_b, lx, lc, need_ctx):
    rows = proj.shape[0]
    g_rows = lx // GRID_W
    _, rg, wk, ks, var, _, _ = _na_plan(g_rows, lc)
    n_groups = g_rows // rg
    tq = rg * GRID_W
    nqc = lc // tq
    nq = n_groups + (nqc if need_ctx else 0)
    nbx = n_b * lx // lc
    hd = NA_HEAD_DIM
    cq, ck, cv = P_NAQ // hd, P_NAK // hd, P_NAV // hd

    def q_row(b, g):
        return jnp.where(g < n_groups, b * n_groups + g, n_b * n_groups + b * nqc + (g - n_groups))

    kern = functools.partial(_na_kernel, n_groups=n_groups, wk=wk, scale=hd ** -0.5)
    grid_spec = pltpu.PrefetchScalarGridSpec(
        num_scalar_prefetch=2,
        grid=(n_b, NA_HEADS, nq),
        in_specs=[pl.BlockSpec((tq, hd), lambda b, h, g, var_r, ks_r: (q_row(b, g), cq + h)),
                  pl.BlockSpec((lx, hd), lambda b, h, g, var_r, ks_r: (b, ck + h)),
                  pl.BlockSpec((lx, hd), lambda b, h, g, var_r, ks_r: (b, cv + h)),
                  pl.BlockSpec((lc, hd), lambda b, h, g, var_r, ks_r: (nbx + b, ck + h)),
                  pl.BlockSpec((lc, hd), lambda b, h, g, var_r, ks_r: (nbx + b, cv + h)),
                  pl.BlockSpec((None, None, tq, wk * GRID_W),
                               lambda b, h, g, var_r, ks_r: (var_r[jnp.minimum(g, n_groups - 1)], h, 0, 0))],
        out_specs=pl.BlockSpec((tq, hd), lambda b, h, g, var_r, ks_r: (q_row(b, g), h)),
    )
    return pl.pallas_call(
        kern,
        grid_spec=grid_spec,
        out_shape=jax.ShapeDtypeStruct((rows, NA_DIM), BF16),
        compiler_params=_cp(("parallel", "parallel", "arbitrary"), 40),
        name="na_attn",
    )(jnp.asarray(var), jnp.asarray(ks), proj, proj, proj, proj, proj, bias_tab)


def _chunk_block(n_b, nxc, ncc):
    def f(b, d, c):
        cc = jnp.where(d == 0, c, ncc - 1 - c)
        cx = jnp.where(d == 0, c - ncc, nxc - 1 - (c - ncc))
        return jnp.where(c < ncc, n_b * nxc + b * ncc + cc, b * nxc + cx)
    return f


def _dir_masks(d):
    row = lax.broadcasted_iota(jnp.int32, (CHUNK, CHUNK), 0)
    col = lax.broadcasted_iota(jnp.int32, (CHUNK, CHUNK), 1)
    diff = (row - col) * jnp.where(d == 0, 1, -1)
    return diff >= 0, diff > 0


def _cumsum_lanes(x, incl):
    cs = jnp.dot(incl.astype(F32), x, preferred_element_type=F32, precision=HIGHEST)
    return cs, cs.T


def _lane_vec(vals, offset):
    flat = vals.reshape(-1).astype(F32)
    return jnp.pad(flat, (offset, P_SMALL - offset - flat.shape[0])).reshape(1, P_SMALL)


def _ssd_kernel(xbc_ref, sm_ref, alog_ref, dtb_ref, dsk_ref, y_ref, s_scr):
    d, c = pl.program_id(1), pl.program_id(2)
    nh, hp, ns = SSD_HEADS, SSD_HEAD_DIM, SSD_STATE
    gh = nh // SSD_GROUPS

    @pl.when(c == 0)
    def _():
        s_scr[...] = jnp.zeros_like(s_scr)

    dt2 = _softplus(sm_ref[...] + dtb_ref[...])
    dta2 = dt2 * (-jnp.exp(alog_ref[...]))
    incl, _ = _dir_masks(d)
    acum2, acum2_t = _cumsum_lanes(dta2, incl)
    tot2 = jnp.sum(dta2, axis=0, keepdims=True)
    pick = lambda a: jnp.where(d == 0, a[:, 0:nh], a[:, nh:2 * nh])
    dt, acum, tot = pick(dt2), pick(acum2), pick(tot2)
    acum_t = jnp.where(d == 0, acum2_t[0:nh, :], acum2_t[nh:2 * nh, :])
    e_acum = jnp.exp(acum)
    w_end = jnp.exp(tot - acum) * dt
    c_dec = jnp.exp(tot)
    dsk = dsk_ref[...]
    bo, co = SSD_D_INNER, SSD_D_INNER + SSD_GROUPS * ns
    ys, new_states = [], []
    for g in range(SSD_GROUPS):
        bg = xbc_ref[:, bo + g * ns:bo + (g + 1) * ns]
        cg = xbc_ref[:, co + g * ns:co + (g + 1) * ns]
        scores = _dot_nt(cg, bg)
        s_g = s_scr[:, g * gh * hp:(g + 1) * gh * hp]
        y_int = _dot(cg, s_g.astype(BF16))
        xw = []
        for hh in range(gh):
            h = g * gh + hh
            xh = xbc_ref[:, h * hp:(h + 1) * hp].astype(F32)
            seg = acum[:, h:h + 1] - acum_t[h:h + 1, :]
            dec = jnp.where(incl, jnp.exp(jnp.where(incl, seg, 0.0)), 0.0)
            m = (scores * dec).astype(BF16)
            y = _dot(m, (xh * dt[:, h:h + 1]).astype(BF16))
            y = y + y_int[:, hh * hp:(hh + 1) * hp] * e_acum[:, h:h + 1]
            ys.append(y + jnp.where(d == 0, dsk[:, h:h + 1], 0.0) * xh)
            xw.append((xh * w_end[:, h:h + 1]).astype(BF16))
        upd = _dot_tn(bg, jnp.concatenate(xw, axis=1))
        for hh in range(gh):
            h = g * gh + hh
            new_states.append(s_g[:, hh * hp:(hh + 1) * hp] * c_dec[:, h:h + 1] + upd[:, hh * hp:(hh + 1) * hp])
    y_ref[...] = jnp.concatenate(ys, axis=1)
    s_scr[...] = jnp.concatenate(new_states, axis=1)


def _ssd_call(xbc, small, a_log, dt_bias, d_skip, *, n_b, lx, lc):
    rows = xbc.shape[0]
    nxc, ncc = lx // CHUNK, lc // CHUNK
    blk = _chunk_block(n_b, nxc, ncc)
    nh = SSD_HEADS
    return pl.pallas_call(
        _ssd_kernel,
        grid=(n_b, 2, ncc + nxc),
        in_specs=[pl.BlockSpec((CHUNK, SSD_CONV_DIM), lambda b, d, c: (blk(b, d, c), 0)),
                  pl.BlockSpec((CHUNK, P_SMALL), lambda b, d, c: (blk(b, d, c), 0)),
                  pl.BlockSpec((1, P_SMALL), lambda b, d, c: (0, 0)),
                  pl.BlockSpec((1, P_SMALL), lambda b, d, c: (0, 0)),
                  pl.BlockSpec((1, nh), lambda b, d, c: (0, 0))],
        out_specs=pl.BlockSpec((None, CHUNK, SSD_D_INNER), lambda b, d, c: (d, blk(b, d, c), 0)),
        out_shape=jax.ShapeDtypeStruct((2, rows, SSD_D_INNER), F32),
        scratch_shapes=[pltpu.VMEM((SSD_STATE, SSD_D_INNER), F32)],
        compiler_params=_cp(("parallel", "arbitrary", "arbitrary"), 40),
        name="ssd_scan",
    )(xbc, small, _lane_vec(a_log, 0), _lane_vec(dt_bias, 0), d_skip.reshape(1, nh))


SOLVE_BLOCK = 16


def _unit_tri_solve_many(n_mats, rhss):
    ln = n_mats[0].shape[0]
    row = lax.broadcasted_iota(jnp.int32, (ln, ln), 0)
    col = lax.broadcasted_iota(jnp.int32, (ln, ln), 1)
    on_diag_block = (row // SOLVE_BLOCK) == (col // SOLVE_BLOCK)
    eye = jnp.where(row == col, 1.0, 0.0)
    mm = lambda a, b: _dot(a.astype(BF16), b.astype(BF16))
    ms = [jnp.where(on_diag_block, -n, 0.0) for n in n_mats]
    es = [jnp.where(on_diag_block, 0.0, n) for n in n_mats]
    ps = [eye + m for m in ms]
    mps = ms
    k = 1
    while 2 * k < SOLVE_BLOCK:
        mps = [mm(x, x) for x in mps]
        ps = [p + mm(p, x) for p, x in zip(ps, mps)]
        k *= 2
    f_pows = [[-mm(p, e) for p, e in zip(ps, es)]]
    ys = [mm(p, r) for p, r in zip(ps, rhss)]
    k = 1
    while 2 * k < ln // SOLVE_BLOCK:
        f_pows.append([mm(f, f) for f in f_pows[-1]])
        k *= 2
    for fl in reversed(f_pows):
        ys = [y + mm(f, y) for f, y in zip(fl, ys)]
    return ys


GDN_PACK = 5 * GDN_DIM
GDN_G_LANE = 2 * SSD_HEADS
GDN_B_LANE = 2 * SSD_HEADS + 2 * GDN_HEADS


def _gdn_prep_kernel(qkv_ref, sm_ref, alog_ref, dtb_ref, o_ref):
    nh, hd = GDN_HEADS, GDN_HEAD_DIM
    sm = sm_ref[...]
    g2 = -jnp.exp(alog_ref[...]) * _softplus(sm + dtb_ref[...])
    beta2 = _sigmoid(sm)
    row = lax.broadcasted_iota(jnp.int32, (CHUNK, CHUNK), 0)
    col = lax.broadcasted_iota(jnp.int32, (CHUNK, CHUNK), 1)
    incl = [row >= col, row <= col]
    strict = [row > col, row < col]
    cs = [jnp.dot(m.astype(F32), g2, preferred_element_type=F32, precision=HIGHEST) for m in incl]
    cs_t = [x.T for x in cs]
    gtot2 = jnp.sum(g2, axis=0, keepdims=True)
    qn, kn, kn_b, vv, qk_raw = [], [], [], [], []
    for h in range(nh):
        qh = qkv_ref[:, h * hd:(h + 1) * hd].astype(F32)
        kh = qkv_ref[:, GDN_DIM + h * hd:GDN_DIM + (h + 1) * hd].astype(F32)
        vv.append(qkv_ref[:, 2 * GDN_DIM + h * hd:2 * GDN_DIM + (h + 1) * hd].astype(F32))
        qn.append(qh * (lax.rsqrt(jnp.sum(qh * qh, axis=-1, keepdims=True) + EPS) * hd ** -0.5))
        kn.append(kh * lax.rsqrt(jnp.sum(kh * kh, axis=-1, keepdims=True) + EPS))
        kn_b.append(kn[h].astype(BF16))
        qk_raw.append(_dot_nt(qn[h].astype(BF16), kn_b[h]))
    n_mats, rhss, qks, qds, kds = [], [], [], [], []
    for d in range(2):
        for h in range(nh):
            lg, lb = GDN_G_LANE + d * nh + h, GDN_B_LANE + d * nh + h
            gcc, gcr = cs[d][:, lg:lg + 1], cs_t[d][lg:lg + 1, :]
            beta, gtot = beta2[:, lb:lb + 1], gtot2[:, lg:lg + 1]
            dec = jnp.where(incl[d], jnp.exp(jnp.where(incl[d], gcc - gcr, 0.0)), 0.0)
            kb = kn[h] * beta
            n_mats.append(jnp.where(strict[d], _dot_nt(kb.astype(BF16), kn_b[h]) * dec, 0.0))
            e_gc = jnp.exp(gcc)
            rhss.append(jnp.concatenate([vv[h] * beta, kb * e_gc], axis=1))
            qks.append(qk_raw[h] * dec)
            qds.append(qn[h] * e_gc)
            kds.append(kn[h] * jnp.exp(gtot - gcc))
    sols = _unit_tri_solve_many(n_mats, rhss)
    pieces = []
    for d in range(2):
        js = range(d * nh, (d + 1) * nh)
        pieces += [sols[j][:, 0:hd] for j in js] + [sols[j][:, hd:2 * hd] for j in js]
        pieces += [qks[j] for j in js] + [qds[j] for j in js] + [kds[j] for j in js]
    o_ref[...] = jnp.concatenate([p.astype(BF16) for p in pieces], axis=1)


def _gdn_scan_kernel(pk_ref, sm_ref, alog_ref, dtb_ref, o_ref, s_scr):
    d, c = pl.program_id(1), pl.program_id(2)
    nh, hd = GDN_HEADS, GDN_HEAD_DIM

    @pl.when(c == 0)
    def _():
        s_scr[...] = jnp.zeros_like(s_scr)

    g2 = -jnp.exp(alog_ref[...]) * _softplus(sm_ref[...] + dtb_ref[...])
    g_end2 = jnp.exp(jnp.sum(g2, axis=0, keepdims=True))
    outs, states = [], []
    for h in range(nh):
        lg = GDN_G_LANE + h
        g_end = jnp.where(d == 0, g_end2[:, lg:lg + 1], g_end2[:, lg + nh:lg + nh + 1])
        part = lambda j: pk_ref[:, (j * nh + h) * hd:(j * nh + h + 1) * hd]
        u, w, qk, qd, kd = part(0), part(1), part(2), part(3), part(4)
        s_h = s_scr[:, h * hd:(h + 1) * hd]
        s_b = s_h.astype(BF16)
        v_new = (u.astype(F32) - _dot(w, s_b)).astype(BF16)
        outs.append(_dot(qd, s_b) + _dot(qk, v_new))
        states.append(s_h * g_end + _dot_tn(kd, v_new))
    o_ref[...] = jnp.concatenate(outs, axis=1)
    s_scr[...] = jnp.concatenate(states, axis=1)


def _gdn_call(qkv, small, a_log, dt_bias, *, n_b, lx, lc):
    rows = qkv.shape[0]
    nxc, ncc = lx // CHUNK, lc // CHUNK
    blk = _chunk_block(n_b, nxc, ncc)
    alog_v, dtb_v = _lane_vec(a_log, GDN_G_LANE), _lane_vec(dt_bias, GDN_G_LANE)
    packed = pl.pallas_call(
        _gdn_prep_kernel,
        grid=(rows // CHUNK,),
        in_specs=[pl.BlockSpec((CHUNK, 3 * GDN_DIM), lambda i: (i, 0)),
                  pl.BlockSpec((CHUNK, P_SMALL), lambda i: (i, 0)),
                  pl.BlockSpec((1, P_SMALL), lambda i: (0, 0)),
                  pl.BlockSpec((1, P_SMALL), lambda i: (0, 0))],
        out_specs=pl.BlockSpec((CHUNK, 2 * GDN_PACK), lambda i: (i, 0)),
        out_shape=jax.ShapeDtypeStruct((rows, 2 * GDN_PACK), BF16),
        compiler_params=_cp(("parallel",), 40),
        name="gdn_prep",
    )(qkv, small, alog_v, dtb_v)
    return pl.pallas_call(
        _gdn_scan_kernel,
        grid=(n_b, 2, ncc + nxc),
        in_specs=[pl.BlockSpec((CHUNK, GDN_PACK), lambda b, d, c: (blk(b, d, c), d)),
                  pl.BlockSpec((CHUNK, P_SMALL), lambda b, d, c: (blk(b, d, c), 0)),
                  pl.BlockSpec((1, P_SMALL), lambda b, d, c: (0, 0)),
                  pl.BlockSpec((1, P_SMALL), lambda b, d, c: (0, 0))],
        out_specs=pl.BlockSpec((None, CHUNK, GDN_DIM), lambda b, d, c: (d, blk(b, d, c), 0)),
        out_shape=jax.ShapeDtypeStruct((2, rows, GDN_DIM), F32),
        scratch_shapes=[pltpu.VMEM((GDN_HEAD_DIM, GDN_DIM), F32)],
        compiler_params=_cp(("parallel", "arbitrary", "arbitrary"), 40),
        name="gdn_scan",
    )(packed, small, alog_v, dtb_v)


def _mixout_kernel(h_ref, mod_ref, gpost_ref, ya_ref, ys_ref, zs_ref, sn_ref, yn_ref, og_ref, zg_ref, gn_ref,
                   w_ref, o_ref):
    ssd = (ys_ref[0] + ys_ref[1]) * _silu(zs_ref[...].astype(F32))
    yb = (ssd * _rms_scale(ssd) * sn_ref[...]).astype(BF16)
    gd = og_ref[0] + og_ref[1]
    zg = _silu(zg_ref[...].astype(F32))
    hd = GDN_HEAD_DIM
    yd = []
    for h in range(GDN_HEADS):
        oh = gd[:, h * hd:(h + 1) * hd]
        yd.append((oh * _rms_scale(oh) * gn_ref[...] * zg[:, h * hd:(h + 1) * hd]).astype(BF16))
    parts = [ya_ref[...], yb, yn_ref[...]] + yd
    widths = [512, 512, 512] + [hd] * GDN_HEADS
    y = None
    off = 0
    for part, wd in zip(parts, widths):
        t = _dot(part, w_ref[off:off + wd, :])
        y = t if y is None else y + t
        off += wd
    o_ref[...] = h_ref[...] + mod_ref[2:3, :] * (y * _rms_scale(y) * gpost_ref[...])


def _mixout_call(h, mod_l, g_post, ya, ys, proj, ssd_norm, yn, og, gdn_norm, w_out, *, n_b, lx, n_rows, tm):
    d = h.shape[1]
    nxt, per_b = n_b * lx // tm, lx // tm
    mod_idx = lambda i: (jnp.where(i < nxt, i // per_b, n_b), 0, 0)
    row = lambda i: (i, 0)
    const = lambda i: (0, 0)
    return pl.pallas_call(
        _mixout_kernel,
        grid=(n_rows // tm,),
        in_specs=[pl.BlockSpec((tm, d), row),
                  pl.BlockSpec((None, 6, d), mod_idx),
                  pl.BlockSpec((1, d), const),
                  pl.BlockSpec((tm, 512), row),
                  pl.BlockSpec((2, tm, 512), lambda i: (0, i, 0)),
                  pl.BlockSpec((tm, 512), lambda i: (i, P_ZSSD // 512)),
                  pl.BlockSpec((1, 512), const),
                  pl.BlockSpec((tm, 512), row),
                  pl.BlockSpec((2, tm, 512), lambda i: (0, i, 0)),
                  pl.BlockSpec((tm, 512), lambda i: (i, P_GZ // 512)),
                  pl.BlockSpec((1, GDN_HEAD_DIM), const),
                  pl.BlockSpec((d, d), const)],
        out_specs=pl.BlockSpec((tm, d), row),
        out_shape=jax.ShapeDtypeStruct((n_rows, d), F32),
        compiler_params=_cp(("parallel",), 56),
        name="mix_out",
    )(h, mod_l, g_post.reshape(1, d), ya, ys, proj, ssd_norm.reshape(1, 512), yn, og, proj,
      gdn_norm.reshape(1, GDN_HEAD_DIM), w_out)


def _ffn_kernel(h_ref, mod_ref, gpre_ref, gpost_ref, wg_ref, wu_ref, wd_ref, o_ref, u_scr, acc_scr):
    j = pl.program_id(1)

    @pl.when(j == 0)
    def _():
        x = h_ref[...]
        y = x * _rms_scale(x) * gpre_ref[...]
        u_scr[...] = (y * (1.0 + mod_ref[4:5, :]) + mod_ref[3:4, :]).astype(BF16)
        acc_scr[...] = jnp.zeros_like(acc_scr)

    u = u_scr[...]
    mid = (_silu(_dot(u, wg_ref[...])) * _dot(u, wu_ref[...])).astype(BF16)
    acc_scr[...] += _dot(mid, wd_ref[...])

    @pl.when(j == pl.num_programs(1) - 1)
    def _():
        y = acc_scr[...]
        o_ref[...] = h_ref[...] + mod_ref[5:6, :] * (y * _rms_scale(y) * gpost_ref[...])


def _ffn_call(h, mod_l, g_pre, g_post, wg, wu, wd, *, n_b, lx, tm, tf):
    rows, d = h.shape
    ff = wg.shape[1]
    nxt, per_b = n_b * lx // tm, lx // tm
    mod_idx = lambda i, j: (jnp.where(i < nxt, i // per_b, n_b), 0, 0)
    return pl.pallas_call(
        _ffn_kernel,
        grid=(rows // tm, ff // tf),
        in_specs=[pl.BlockSpec((tm, d), lambda i, j: (i, 0)),
                  pl.BlockSpec((None, 6, d), mod_idx),
                  pl.BlockSpec((1, d), lambda i, j: (0, 0)),
                  pl.BlockSpec((1, d), lambda i, j: (0, 0)),
                  pl.BlockSpec((d, tf), lambda i, j: (0, j)),
                  pl.BlockSpec((d, tf), lambda i, j: (0, j)),
                  pl.BlockSpec((tf, d), lambda i, j: (j, 0))],
        out_specs=pl.BlockSpec((tm, d), lambda i, j: (i, 0)),
        out_shape=jax.ShapeDtypeStruct((rows, d), F32),
        scratch_shapes=[pltpu.VMEM((tm, d), BF16), pltpu.VMEM((tm, d), F32)],
        compiler_params=_cp(("parallel", "arbitrary"), 56),
        name="ffn_swiglu",
    )(h, mod_l, g_pre.reshape(1, d), g_post.reshape(1, d), wg, wu, wd)


def _moe_kernel(h_ref, mod_ref, gpre_ref, gpost_ref, wr_ref, wg_ref, wu_ref, wd_ref, o_ref,
                u_scr, gate_scr, acc_scr):
    e, j = pl.program_id(1), pl.program_id(2)
    lanes = gate_scr.shape[1]

    @pl.when((e == 0) & (j == 0))
    def _():
        x = h_ref[...]
        y = x * _rms_scale(x) * gpre_ref[...]
        u = y * (1.0 + mod_ref[4:5, :]) + mod_ref[3:4, :]
        u_scr[...] = u.astype(BF16)
        acc_scr[...] = jnp.zeros_like(acc_scr)
        logits = jnp.dot(u, wr_ref[...], preferred_element_type=F32, precision=HIGHEST)
        lane = lax.broadcasted_iota(jnp.int32, logits.shape, 1).astype(F32)
        lg = jnp.where(lane < N_EXPERTS, logits, NEG)
        m1 = jnp.max(lg, axis=-1, keepdims=True)
        i1 = jnp.min(jnp.where(lg == m1, lane, float(lanes)), axis=-1, keepdims=True)
        lg2 = jnp.where(lane == i1, NEG, lg)
        m2 = jnp.max(lg2, axis=-1, keepdims=True)
        i2 = jnp.min(jnp.where(lg2 == m2, lane, float(lanes)), axis=-1, keepdims=True)
        e2 = jnp.exp(m2 - m1)
        gate_scr[...] = jnp.where(lane == i1, 1.0 / (1.0 + e2), 0.0) + jnp.where(lane == i2, e2 / (1.0 + e2), 0.0)

    lane = lax.broadcasted_iota(jnp.int32, gate_scr.shape, 1)
    ge = jnp.sum(jnp.where(lane == e, gate_scr[...], 0.0), axis=-1, keepdims=True)
    u = u_scr[...]
    mid = (_silu(_dot(u, wg_ref[...])) * _dot(u, wu_ref[...]) * ge).astype(BF16)
    acc_scr[...] += _dot(mid, wd_ref[...])

    @pl.when((e == pl.num_programs(1) - 1) & (j == pl.num_programs(2) - 1))
    def _():
        y = acc_scr[...]
        o_ref[...] = h_ref[...] + mod_ref[5:6, :] * (y * _rms_scale(y) * gpost_ref[...])


def _moe_call(h, mod_l, g_pre, g_post, router_pad, wg, wu, wd, *, n_b, lx, tm, tf):
    rows, d = h.shape
    ne, _, fe = wg.shape
    nxt, per_b = n_b * lx // tm, lx // tm
    mod_idx = lambda i, e, j: (jnp.where(i < nxt, i // per_b, n_b), 0, 0)
    c2 = lambda i, e, j: (0, 0)
    return pl.pallas_call(
        _moe_kernel,
        grid=(rows // tm, ne, fe // tf),
        in_specs=[pl.BlockSpec((tm, d), lambda i, e, j: (i, 0)),
                  pl.BlockSpec((None, 6, d), mod_idx),
                  pl.BlockSpec((1, d), c2),
                  pl.BlockSpec((1, d), c2),
                  pl.BlockSpec((d, 128), c2),
                  pl.BlockSpec((None, d, tf), lambda i, e, j: (e, 0, j)),
                  pl.BlockSpec((None, d, tf), lambda i, e, j: (e, 0, j)),
                  pl.BlockSpec((None, tf, d), lambda i, e, j: (e, j, 0))],
        out_specs=pl.BlockSpec((tm, d), lambda i, e, j: (i, 0)),
        out_shape=jax.ShapeDtypeStruct((rows, d), F32),
        scratch_shapes=[pltpu.VMEM((tm, d), BF16), pltpu.VMEM((tm, 128), F32), pltpu.VMEM((tm, d), F32)],
        compiler_params=_cp(("parallel", "arbitrary", "arbitrary"), 56),
        name="moe_swiglu",
    )(h, mod_l, g_pre.reshape(1, d), g_post.reshape(1, d), router_pad, wg, wu, wd)


def _rope_tables(lx, tm):
    half = MLA_ROPE // 2
    n_axis = half // 2
    inv_freq = ROPE_THETA ** (-jnp.arange(n_axis, dtype=F32) / n_axis)
    pos = jnp.arange(lx)
    rows = (pos // GRID_W).astype(F32)
    cols = (pos % GRID_W).astype(F32)
    ang = jnp.concatenate([rows[:, None] * inv_freq, cols[:, None] * inv_freq], axis=-1)
    cos, sin = jnp.cos(ang), jnp.sin(ang)
    cos_t = jnp.concatenate([cos, cos], axis=-1)
    sin_t = jnp.concatenate([-sin, sin], axis=-1)
    cos_t = jnp.concatenate([cos_t, jnp.ones((tm, MLA_ROPE), F32)], axis=0)
    sin_t = jnp.concatenate([sin_t, jnp.zeros((tm, MLA_ROPE), F32)], axis=0)
    return cos_t, sin_t


def _mla_weights(q_norm, w_uq, kv_norm, w_ukv):
    dq = MLA_NOPE + MLA_ROPE
    half = MLA_ROPE // 2
    cols = lambda a, b: w_uq[:, a:b]
    nope = [cols(h * dq, h * dq + MLA_NOPE) for h in range(MLA_HEADS)]
    rope = [cols(h * dq + MLA_NOPE, (h + 1) * dq) for h in range(MLA_HEADS)]
    rope_sw = [cols(h * dq + MLA_NOPE + s * half, h * dq + MLA_NOPE + (s + 1) * half)
               for h in range(MLA_HEADS) for s in (1, 0)]
    pad = ((0, MLA_ROPE), (0, 0))
    wq = jnp.pad(jnp.concatenate(nope + rope, axis=1), pad).astype(BF16)
    wqs = jnp.pad(jnp.concatenate(rope_sw, axis=1), pad).astype(BF16)
    qn_ext = jnp.pad(q_norm, (0, MLA_ROPE)).reshape(1, -1)
    perm = np.zeros((MLA_ROPE, MLA_ROPE), np.float32)
    perm[(np.arange(MLA_ROPE) + half) % MLA_ROPE, np.arange(MLA_ROPE)] = 1.0
    return qn_ext, wq, wqs, kv_norm.reshape(1, -1), w_ukv.astype(BF16), jnp.asarray(perm, BF16)


def _pick_tile(n, cands):
    for t in cands:
        if n % t == 0:
            return t
    raise ValueError(f"no tile for {n}")


def kernel(x, c, ctx, c_ctx, w_ada, b_ada, g_pre_mix, g_post_mix, g_pre_ffn, g_post_ffn, w_in, w_out, mla_q_norm, mla_w_uq, mla_kv_norm, mla_w_ukv, ssd_conv_w, ssd_conv_b, ssd_a_log, ssd_dt_bias, ssd_d, ssd_norm, na_rpb, gdn_conv_w, gdn_a_log, gdn_dt_bias, gdn_norm, ffn_w_gate, ffn_w_up, ffn_w_down, moe_router, moe_w_gate, moe_w_up, moe_w_down):
    n_b, lx, d = x.shape
    lc = ctx.shape[1]
    depth = w_ada.shape[0]
    rows_x, rows_c = n_b * lx, n_b * lc
    assert n_b + 1 <= 8 and lx % GRID_W == 0 and lx % lc == 0 and lc % CHUNK == 0
    tm = _pick_tile(math.gcd(lx, rows_c), (512, 256, 128))
    tq = _pick_tile(lc, (256, 128))
    tr = _pick_tile(lc, (256, 128))

    cvec = jnp.concatenate([c, c_ctx[None, :], jnp.zeros((8 - n_b - 1, d), F32)], axis=0)
    mod = _ada_call(cvec, w_ada, b_ada).reshape(depth, 8, 6, d)
    cos_t, sin_t = _rope_tables(lx, tm)
    h_all = jnp.concatenate([x.reshape(rows_x, d), ctx.reshape(rows_c, d)], axis=0)

    for i in range(depth):
        need_ctx = i < depth - 1
        w_main, w_small = _regroup_w_in(w_in[i])
        proj, small = _inproj_call(h_all, mod[i], g_pre_mix[i], w_main, w_small, n_b=n_b, lx=lx, tm=tm)

        mla_w = _mla_weights(mla_q_norm[i], mla_w_uq[i], mla_kv_norm[i], mla_w_ukv[i])
        q_a, k_a, v_a = _mla_prep_call(proj, *mla_w[:5], cos_t, sin_t, mla_w[5], n_b=n_b, lx=lx, tm=tm)
        ya = _mla_attn_call(q_a, k_a, v_a, n_b=n_b, lx=lx, lc=lc, tq=tq, need_ctx=need_ctx)

        xbc = _conv_call(proj, ssd_conv_w[i], ssd_conv_b[i], col_off=P_XBC, n_b=n_b, lx=lx, lc=lc, tr=tr)
        ys = _ssd_call(xbc, small, ssd_a_log[i], ssd_dt_bias[i], ssd_d[i], n_b=n_b, lx=lx, lc=lc)

        yn = _na_call(proj, _na_bias_table(na_rpb[i], lx // GRID_W, lc), n_b=n_b, lx=lx, lc=lc, need_ctx=need_ctx)

        qkv = _conv_call(proj, gdn_conv_w[i], jnp.zeros((3 * GDN_DIM,), F32), col_off=P_GQKV,
                         n_b=n_b, lx=lx, lc=lc, tr=tr)
        og = _gdn_call(qkv, small, gdn_a_log[i], gdn_dt_bias[i], n_b=n_b, lx=lx, lc=lc)

        n_rows = rows_x + rows_c if need_ctx else rows_x
        h_mid = _mixout_call(h_all, mod[i], g_post_mix[i], ya, ys, proj, ssd_norm[i], yn, og, gdn_norm[i],
                             w_out[i].astype(BF16), n_b=n_b, lx=lx, n_rows=n_rows, tm=min(tm, 256))
        j = i // 2
        if i % 2 == 0:
            h_all = _ffn_call(h_mid, mod[i], g_pre_ffn[i], g_post_ffn[i], ffn_w_gate[j].astype(BF16),
                              ffn_w_up[j].astype(BF16), ffn_w_down[j].astype(BF16), n_b=n_b, lx=lx, tm=tm, tf=512)
        else:
            router_pad = jnp.pad(moe_router[j], ((0, 0), (0, 128 - N_EXPERTS)))
            h_all = _moe_call(h_mid, mod[i], g_pre_ffn[i], g_post_ffn[i], router_pad, moe_w_gate[j].astype(BF16),
                              moe_w_up[j].astype(BF16), moe_w_down[j].astype(BF16), n_b=n_b, lx=lx, tm=tm, tf=256)
    return h_all[:rows_x].reshape(n_b, lx, d)
```

```python
import functools
import math

import numpy as np
import jax
import jax.numpy as jnp
from jax import lax
from jax.experimental import pallas as pl
from jax.experimental.pallas import tpu as pltpu

F32 = jnp.float32
BF16 = jnp.bfloat16
HIGHEST = lax.Precision.HIGHEST

GRID_W = 64
EPS = 1e-6
ROPE_THETA = 10000.0
CHUNK = 128
CONV_W = 5
MLA_HEADS, MLA_NOPE, MLA_ROPE, MLA_V = 4, 128, 64, 128
MLA_Q_LORA, MLA_KV_LORA = 448, 128
SSD_HEADS, SSD_HEAD_DIM, SSD_STATE, SSD_GROUPS = 8, 64, 128, 2
SSD_D_INNER = SSD_HEADS * SSD_HEAD_DIM
SSD_CONV_DIM = SSD_D_INNER + 2 * SSD_GROUPS * SSD_STATE
NA_HEADS, NA_HEAD_DIM = 4, 128
NA_DIM = NA_HEADS * NA_HEAD_DIM
NA_WIN_ROWS, NA_WIN_COLS = 8, 16
GDN_HEADS, GDN_HEAD_DIM = 4, 128
GDN_DIM = GDN_HEADS * GDN_HEAD_DIM
N_EXPERTS, TOP_K = 8, 2
MLA_COLS = MLA_Q_LORA + MLA_KV_LORA + MLA_ROPE
SSD_COLS = SSD_D_INNER + SSD_CONV_DIM + 2 * SSD_HEADS
NA_COLS = 3 * NA_DIM
GDN_COLS = 4 * GDN_DIM + 4 * GDN_HEADS

P_XBC, P_ZSSD, P_NAQ, P_NAK, P_NAV = 0, 1024, 1536, 2048, 2560
P_GQKV, P_GZ, P_MLA = 3072, 4608, 5120
P_MAIN = 5760
P_SMALL = 128
NEG = -1e30
VMEM_MB = 1024 * 1024


def _cp(sem, mb):
    return pltpu.CompilerParams(dimension_semantics=sem, vmem_limit_bytes=mb * VMEM_MB)


def _dot(a, b):
    return jnp.dot(a, b, preferred_element_type=F32)


def _dot_nt(a, b, precision=None):
    return lax.dot_general(a, b, (((1,), (1,)), ((), ())), preferred_element_type=F32, precision=precision)


def _dot_tn(a, b):
    return lax.dot_general(a, b, (((0,), (0,)), ((), ())), preferred_element_type=F32)


def _sigmoid(x):
    return 1.0 / (1.0 + jnp.exp(-x))


def _silu(x):
    return x * _sigmoid(x)


def _softplus(x):
    return jnp.maximum(x, 0.0) + jnp.log(1.0 + jnp.exp(-jnp.abs(x)))


def _rms_scale(x):
    return lax.rsqrt(jnp.mean(x * x, axis=-1, keepdims=True) + EPS)


def _regroup_w_in(w):
    o_mla, o_ssd = 0, MLA_COLS
    o_na, o_gdn = o_ssd + SSD_COLS, o_ssd + SSD_COLS + NA_COLS
    main_segs = [
        (o_ssd + SSD_D_INNER, SSD_CONV_DIM),
        (o_ssd, SSD_D_INNER),
        (o_na, NA_COLS),
        (o_gdn, 4 * GDN_DIM),
        (o_mla + MLA_Q_LORA, MLA_KV_LORA),
        (o_mla, MLA_Q_LORA),
        (o_mla + MLA_Q_LORA + MLA_KV_LORA, MLA_ROPE),
    ]
    small_segs = [(o_ssd + SSD_D_INNER + SSD_CONV_DIM, 2 * SSD_HEADS), (o_gdn + 4 * GDN_DIM, 4 * GDN_HEADS)]
    assert sum(n for _, n in main_segs) == P_MAIN
    main = jnp.concatenate([w[:, a:a + n] for a, n in main_segs], axis=1).astype(BF16)
    n_small = sum(n for _, n in small_segs)
    small = jnp.concatenate([w[:, a:a + n] for a, n in small_segs]
                            + [jnp.zeros((w.shape[0], P_SMALL - n_small), w.dtype)], axis=1).astype(BF16)
    return main, small


def _ada_kernel(c_ref, w_ref, b_ref, o_ref):
    s = _silu(c_ref[...]).astype(BF16)
    o_ref[...] = _dot(s, w_ref[...].astype(BF16)) + b_ref[...]


def _ada_call(cvec, w_ada, b_ada):
    depth, d, n = w_ada.shape
    tn = 1024
    return pl.pallas_call(
        _ada_kernel,
        grid=(depth, n // tn),
        in_specs=[pl.BlockSpec((8, d), lambda l, j: (0, 0)),
                  pl.BlockSpec((None, d, tn), lambda l, j: (l, 0, j)),
                  pl.BlockSpec((None, 1, tn), lambda l, j: (l, 0, j))],
        out_specs=pl.BlockSpec((None, 8, tn), lambda l, j: (l, 0, j)),
        out_shape=jax.ShapeDtypeStruct((depth, 8, n), F32),
        compiler_params=_cp(("parallel", "parallel"), 40),
        name="adaln",
    )(cvec, w_ada, b_ada.reshape(depth, 1, n))


def _inproj_kernel(h_ref, mod_ref, g_ref, w_ref, ws_ref, o_ref, os_ref, u_scr):
    @pl.when(pl.program_id(1) == 0)
    def _():
        x = h_ref[...]
        y = x * _rms_scale(x) * g_ref[...]
        u = (y * (1.0 + mod_ref[1:2, :]) + mod_ref[0:1, :]).astype(BF16)
        u_scr[...] = u
        os_ref[...] = _dot(u, ws_ref[...])

    o_ref[...] = _dot(u_scr[...], w_ref[...]).astype(BF16)


def _inproj_call(h, mod_l, g_pre, w_main, w_small, *, n_b, lx, tm):
    rows, d = h.shape
    tn = 640
    nxt, per_b = n_b * lx // tm, lx // tm

    def mod_idx(i, j):
        return (jnp.where(i < nxt, i // per_b, n_b), 0, 0)

    return pl.pallas_call(
        _inproj_kernel,
        grid=(rows // tm, P_MAIN // tn),
        in_specs=[pl.BlockSpec((tm, d), lambda i, j: (i, 0)),
                  pl.BlockSpec((None, 6, d), mod_idx),
                  pl.BlockSpec((1, d), lambda i, j: (0, 0)),
                  pl.BlockSpec((d, tn), lambda i, j: (0, j)),
                  pl.BlockSpec((d, P_SMALL), lambda i, j: (0, 0))],
        out_specs=[pl.BlockSpec((tm, tn), lambda i, j: (i, j)),
                   pl.BlockSpec((tm, P_SMALL), lambda i, j: (i, 0))],
        out_shape=[jax.ShapeDtypeStruct((rows, P_MAIN), BF16),
                   jax.ShapeDtypeStruct((rows, P_SMALL), F32)],
        scratch_shapes=[pltpu.VMEM((tm, d), BF16)],
        compiler_params=_cp(("parallel", "arbitrary"), 40),
        name="in_proj",
    )(h, mod_l, g_pre.reshape(1, d), w_main, w_small)


HALO = 16


def _conv_kernel(prev_ref, cur_ref, next_ref, w_ref, b_ref, o_ref, ext_scr, *, tr, blocks_x, seq_x, seq_c):
    i = pl.program_id(0)
    in_x = i < blocks_x
    pos = jnp.where(in_x, i % seq_x, (i - blocks_x) % seq_c)
    last_pos = jnp.where(in_x, seq_x - 1, seq_c - 1)
    ext_scr[0:HALO, :] = jnp.where(pos == 0, 0.0, prev_ref[...].astype(F32))
    ext_scr[HALO:HALO + tr, :] = cur_ref[...].astype(F32)
    ext_scr[HALO + tr:2 * HALO + tr, :] = jnp.where(pos == last_pos, 0.0, next_ref[...].astype(F32))
    acc = b_ref[...] + w_ref[0:1, :] * ext_scr[HALO - 2:HALO - 2 + tr, :]
    for k in range(1, CONV_W):
        acc = acc + w_ref[k:k + 1, :] * ext_scr[HALO - 2 + k:HALO - 2 + k + tr, :]
    o_ref[...] = _silu(acc).astype(BF16)


def _conv_call(proj, w, b, *, col_off, n_b, lx, lc, tr):
    rows = proj.shape[0]
    c = w.shape[1]
    cb = col_off // c
    assert cb * c == col_off
    hb = tr // HALO
    n_halo = rows // HALO
    kern = functools.partial(_conv_kernel, tr=tr, blocks_x=n_b * lx // tr, seq_x=lx // tr, seq_c=lc // tr)
    return pl.pallas_call(
        kern,
        grid=(rows // tr,),
        in_specs=[pl.BlockSpec((HALO, c), lambda i: (jnp.maximum(i * hb - 1, 0), cb)),
                  pl.BlockSpec((tr, c), lambda i: (i, cb)),
                  pl.BlockSpec((HALO, c), lambda i: (jnp.minimum((i + 1) * hb, n_halo - 1), cb)),
                  pl.BlockSpec((CONV_W, c), lambda i: (0, 0)),
                  pl.BlockSpec((1, c), lambda i: (0, 0))],
        out_specs=pl.BlockSpec((tr, c), lambda i: (i, 0)),
        out_shape=jax.ShapeDtypeStruct((rows, c), BF16),
        scratch_shapes=[pltpu.VMEM((tr + 2 * HALO, c), F32)],
        compiler_params=_cp(("parallel",), 40),
        name="dwconv_silu",
    )(proj, proj, proj, w, b.reshape(1, c))


def _mla_prep_kernel(p_ref, qn_ref, wq_ref, wqs_ref, kvn_ref, wkv_ref, cos_ref, sin_ref, perm_ref,
                     q_ref, k_ref, v_ref):
    p = p_ref[...].astype(F32)
    ckv = p[:, 0:MLA_KV_LORA]
    ce = p[:, MLA_KV_LORA:]
    lane = lax.broadcasted_iota(jnp.int32, ce.shape, 1)
    ssq = jnp.sum(jnp.where(lane < MLA_Q_LORA, ce * ce, 0.0), axis=-1, keepdims=True)
    cqn = (ce * lax.rsqrt(ssq / MLA_Q_LORA + EPS) * qn_ref[...]).astype(BF16)
    ckvn = (ckv * _rms_scale(ckv) * kvn_ref[...]).astype(BF16)
    q = _dot(cqn, wq_ref[...])
    qs = _dot(cqn, wqs_ref[...])
    kv = _dot(ckvn, wkv_ref[...])
    cos, sin = cos_ref[...], sin_ref[...]
    kr = p_ref[:, MLA_KV_LORA + MLA_Q_LORA:]
    kr_rot = kr.astype(F32) * cos + _dot(kr, perm_ref[...]) * sin
    nr = MLA_HEADS * MLA_NOPE
    for h in range(MLA_HEADS):
        q_ref[h, :, 0:MLA_NOPE] = q[:, h * MLA_NOPE:(h + 1) * MLA_NOPE].astype(BF16)
        qr = q[:, nr + h * MLA_ROPE:nr + (h + 1) * MLA_ROPE] * cos + qs[:, h * MLA_ROPE:(h + 1) * MLA_ROPE] * sin
        q_ref[h, :, MLA_NOPE:] = qr.astype(BF16)
        hv = h * (MLA_NOPE + MLA_V)
        k_ref[h, :, 0:MLA_NOPE] = kv[:, hv:hv + MLA_NOPE].astype(BF16)
        k_ref[h, :, MLA_NOPE:] = kr_rot.astype(BF16)
        v_ref[h] = kv[:, hv + MLA_NOPE:hv + MLA_NOPE + MLA_V].astype(BF16)


def _mla_prep_call(proj, qn_ext, wq, wqs, kvn, wkv, cos_t, sin_t, perm, *, n_b, lx, tm):
    rows = proj.shape[0]
    nxt, per_b = n_b * lx // tm, lx // tm
    dk = MLA_NOPE + MLA_ROPE
    rope_idx = lambda i: (jnp.where(i < nxt, i % per_b, per_b), 0)
    full = lambda a: pl.BlockSpec(a.shape, lambda i: (0,) * a.ndim)
    return pl.pallas_call(
        _mla_prep_kernel,
        grid=(rows // tm,),
        in_specs=[pl.BlockSpec((tm, MLA_COLS), lambda i: (i, P_MLA // MLA_COLS)),
                  full(qn_ext), full(wq), full(wqs), full(kvn), full(wkv),
                  pl.BlockSpec((tm, MLA_ROPE), rope_idx), pl.BlockSpec((tm, MLA_ROPE), rope_idx),
                  full(perm)],
        out_specs=[pl.BlockSpec((MLA_HEADS, tm, dk), lambda i: (0, i, 0)),
                   pl.BlockSpec((MLA_HEADS, tm, dk), lambda i: (0, i, 0)),
                   pl.BlockSpec((MLA_HEADS, tm, MLA_V), lambda i: (0, i, 0))],
        out_shape=[jax.ShapeDtypeStruct((MLA_HEADS, rows, dk), BF16),
                   jax.ShapeDtypeStruct((MLA_HEADS, rows, dk), BF16),
                   jax.ShapeDtypeStruct((MLA_HEADS, rows, MLA_V), BF16)],
        compiler_params=_cp(("parallel",), 40),
        name="mla_prep",
    )(proj, qn_ext, wq, wqs, kvn, wkv, cos_t, sin_t, perm)


def _softmax_pv(scores, values):
    m = functools.reduce(jnp.maximum, [jnp.max(s, axis=-1, keepdims=True) for s in scores])
    ps = [jnp.exp(s - m) for s in scores]
    den = functools.reduce(lambda a, b: a + b, [jnp.sum(p, axis=-1, keepdims=True) for p in ps])
    num = functools.reduce(lambda a, b: a + b, [_dot(p.astype(BF16), v) for p, v in zip(ps, values)])
    return num / den


def _mla_attn_kernel(q_ref, kx_ref, vx_ref, kc_ref, vc_ref, o_ref, *, nqx, scale):
    qi = pl.program_id(2)
    q = q_ref[...]
    sc = _dot_nt(q, kc_ref[...]) * scale

    @pl.when(qi < nqx)
    def _():
        sx = _dot_nt(q, kx_ref[...]) * scale
        o_ref[...] = _softmax_pv([sx, sc], [vx_ref[...], vc_ref[...]]).astype(BF16)

    @pl.when(qi >= nqx)
    def _():
        o_ref[...] = _softmax_pv([sc], [vc_ref[...]]).astype(BF16)


def _mla_attn_call(q, k, v, *, n_b, lx, lc, tq, need_ctx):
    rows = q.shape[1]
    dk = q.shape[2]
    nqx, nqc = lx // tq, lc // tq
    nq = nqx + (nqc if need_ctx else 0)
    nbx = n_b * lx // lc

    def q_row(b, qi):
        return jnp.where(qi < nqx, b * nqx + qi, n_b * nqx + b * nqc + (qi - nqx))

    kern = functools.partial(_mla_attn_kernel, nqx=nqx, scale=dk ** -0.5)
    return pl.pallas_call(
        kern,
        grid=(n_b, MLA_HEADS, nq),
        in_specs=[pl.BlockSpec((None, tq, dk), lambda b, h, qi: (h, q_row(b, qi), 0)),
                  pl.BlockSpec((None, lx, dk), lambda b, h, qi: (h, b, 0)),
                  pl.BlockSpec((None, lx, MLA_V), lambda b, h, qi: (h, b, 0)),
                  pl.BlockSpec((None, lc, dk), lambda b, h, qi: (h, nbx + b, 0)),
                  pl.BlockSpec((None, lc, MLA_V), lambda b, h, qi: (h, nbx + b, 0))],
        out_specs=pl.BlockSpec((tq, MLA_V), lambda b, h, qi: (q_row(b, qi), h)),
        out_shape=jax.ShapeDtypeStruct((rows, MLA_HEADS * MLA_V), BF16),
        compiler_params=_cp(("parallel", "parallel", "arbitrary"), 48),
        name="mla_attn",
    )(q, k, v, k, v)


def _na_plan(g_rows, lc):
    wr = min(NA_WIN_ROWS, g_rows)
    rg = next(r for r in (4, 2, 1) if g_rows % r == 0 and lc % (r * GRID_W) == 0)
    wk = min(rg + wr - 1, g_rows)
    n_groups = g_rows // rg
    ks = np.clip(np.arange(n_groups) * rg - wr // 2, 0, g_rows - wk)
    r = np.arange(g_rows)
    rs = np.clip(r - wr // 2, 0, g_rows - wr)
    q_off = (r - np.repeat(ks, rg)).reshape(n_groups, rg)
    rel = (rs - np.repeat(ks, rg)).reshape(n_groups, rg)
    assert (rel >= 0).all() and (rel + wr <= wk).all()
    pats = [tuple(q_off[g]) + tuple(rel[g]) for g in range(n_groups)]
    uniq = sorted(set(pats))
    var = np.array([uniq.index(p) for p in pats], np.int32)
    q_off_v = np.array([p[:rg] for p in uniq])
    rel_v = np.array([p[rg:] for p in uniq])
    return wr, rg, wk, ks.astype(np.int32), var, q_off_v, rel_v


def _na_bias_table(rpb, g_rows, lc):
    wr, rg, wk, _, _, q_off_v, rel_v = _na_plan(g_rows, lc)
    col_start = np.clip(np.arange(GRID_W) - NA_WIN_COLS // 2, 0, GRID_W - NA_WIN_COLS)
    cc = np.arange(GRID_W)
    col_ok = (cc[None, :] >= col_start[:, None]) & (cc[None, :] < col_start[:, None] + NA_WIN_COLS)
    dc = np.clip(cc[None, :] - cc[:, None] + NA_WIN_COLS - 1, 0, 2 * NA_WIN_COLS - 2)
    n_dc = 2 * NA_WIN_COLS - 1
    onehot = (dc.reshape(-1)[:, None] == np.arange(n_dc)[None, :]).astype(np.float32)
    g = jnp.einsum('hab,yb->hay', rpb.astype(F32), jnp.asarray(onehot), precision=HIGHEST)
    g = jnp.where(col_ok[None, None], g.reshape(rpb.shape[0], -1, GRID_W, GRID_W), NEG)
    neg_blk = jnp.full((rpb.shape[0], GRID_W, GRID_W), NEG, F32)
    tabs = []
    for v in range(q_off_v.shape[0]):
        rows_v = []
        for j in range(rg):
            blks = []
            for w in range(wk):
                ok = rel_v[v, j] <= w < rel_v[v, j] + wr
                blks.append(g[:, w - q_off_v[v, j] + NA_WIN_ROWS - 1] if ok else neg_blk)
            rows_v.append(jnp.concatenate(blks, axis=-1))
        tabs.append(jnp.concatenate(rows_v, axis=-2))
    return jnp.stack(tabs, axis=0)


def _na_kernel(var_ref, ks_ref, q_ref, kx_ref, vx_ref, kc_ref, vc_ref, bias_ref, o_ref, *, n_groups, wk, scale):
    g = pl.program_id(2)
    q = q_ref[...]
    sc = _dot_nt(q, kc_ref[...]) * scale

    @pl.when(g < n_groups)
    def _():
        start = pl.multiple_of(ks_ref[g] * GRID_W, GRID_W)
        kw = kx_ref[pl.ds(start, wk * GRID_W), :]
        vw = vx_ref[pl.ds(start, wk * GRID_W), :]
        sl = _dot_nt(q, kw) * scale + bias_ref[...]
        o_ref[...] = _softmax_pv([sl, sc], [vw, vc_ref[...]]).astype(BF16)

    @pl.when(g >= n_groups)
    def _():
        o_ref[...] = _softmax_pv([sc], [vc_ref[...]]).astype(BF16)


def _na_call(proj, bias_tab, *, n_b, lx, lc, need_ctx):
    rows = proj.shape[0]
    g_rows = lx // GRID_W
    _, rg, wk, ks, var, _, _ = _na_plan(g_rows, lc)
    n_groups = g_rows // rg
    tq = rg * GRID_W
    nqc = lc // tq
    nq = n_groups + (nqc if need_ctx else 0)
    nbx = n_b * lx // lc
    hd = NA_HEAD_DIM
    cq, ck, cv = P_NAQ // hd, P_NAK // hd, P_NAV // hd

    def q_row(b, g):
        return jnp.where(g < n_groups, b * n_groups + g, n_b * n_groups + b * nqc + (g - n_groups))

    kern = functools.partial(_na_kernel, n_groups=n_groups, wk=wk, scale=hd ** -0.5)
    grid_spec = pltpu.PrefetchScalarGridSpec(
        num_scalar_prefetch=2,
        grid=(n_b, NA_HEADS, nq),
        in_specs=[pl.BlockSpec((tq, hd), lambda b, h, g, var_r, ks_r: (q_row(b, g), cq + h)),
                  pl.BlockSpec((lx, hd), lambda b, h, g, var_r, ks_r: (b, ck + h)),
                  pl.BlockSpec((lx, hd), lambda b, h, g, var_r, ks_r: (b, cv + h)),
                  pl.BlockSpec((lc, hd), lambda b, h, g, var_r, ks_r: (nbx + b, ck + h)),
                  pl.BlockSpec((lc, hd), lambda b, h, g, var_r, ks_r: (nbx + b, cv + h)),
                  pl.BlockSpec((None, None, tq, wk * GRID_W),
                               lambda b, h, g, var_r, ks_r: (var_r[jnp.minimum(g, n_groups - 1)], h, 0, 0))],
        out_specs=pl.BlockSpec((tq, hd), lambda b, h, g, var_r, ks_r: (q_row(b, g), h)),
    )
    return pl.pallas_call(
        kern,
        grid_spec=grid_spec,
        out_shape=jax.ShapeDtypeStruct((rows, NA_DIM), BF16),
        compiler_params=_cp(("parallel", "parallel", "arbitrary"), 40),
        name="na_attn",
    )(jnp.asarray(var), jnp.asarray(ks), proj, proj, proj, proj, proj, bias_tab)


def _chunk_block(n_b, nxc, ncc):
    def f(b, d, c):
        cc = jnp.where(d == 0, c, ncc - 1 - c)
        cx = jnp.where(d == 0, c - ncc, nxc - 1 - (c - ncc))
        return jnp.where(c < ncc, n_b * nxc + b * ncc + cc, b * nxc + cx)
    return f


def _dir_masks(d):
    row = lax.broadcasted_iota(jnp.int32, (CHUNK, CHUNK), 0)
    col = lax.broadcasted_iota(jnp.int32, (CHUNK, CHUNK), 1)
    diff = (row - col) * jnp.where(d == 0, 1, -1)
    return diff >= 0, diff > 0


def _cumsum_lanes(x, incl):
    cs = jnp.dot(incl.astype(F32), x, preferred_element_type=F32, precision=HIGHEST)
    return cs, cs.T


def _lane_vec(vals, offset):
    flat = vals.reshape(-1).astype(F32)
    return jnp.pad(flat, (offset, P_SMALL - offset - flat.shape[0])).reshape(1, P_SMALL)


def _ssd_kernel(xbc_ref, sm_ref, alog_ref, dtb_ref, dsk_ref, y_ref, s_scr):
    d, c = pl.program_id(1), pl.program_id(2)
    nh, hp, ns = SSD_HEADS, SSD_HEAD_DIM, SSD_STATE
    gh = nh // SSD_GROUPS

    @pl.when(c == 0)
    def _():
        s_scr[...] = jnp.zeros_like(s_scr)

    dt2 = _softplus(sm_ref[...] + dtb_ref[...])
    dta2 = dt2 * (-jnp.exp(alog_ref[...]))
    incl, _ = _dir_masks(d)
    acum2, acum2_t = _cumsum_lanes(dta2, incl)
    tot2 = jnp.sum(dta2, axis=0, keepdims=True)
    pick = lambda a: jnp.where(d == 0, a[:, 0:nh], a[:, nh:2 * nh])
    dt, acum, tot = pick(dt2), pick(acum2), pick(tot2)
    acum_t = jnp.where(d == 0, acum2_t[0:nh, :], acum2_t[nh:2 * nh, :])
    e_acum = jnp.exp(acum)
    w_end = jnp.exp(tot - acum) * dt
    c_dec = jnp.exp(tot)
    dsk = dsk_ref[...]
    bo, co = SSD_D_INNER, SSD_D_INNER + SSD_GROUPS * ns
    ys, new_states = [], []
    for g in range(SSD_GROUPS):
        bg = xbc_ref[:, bo + g * ns:bo + (g + 1) * ns]
        cg = xbc_ref[:, co + g * ns:co + (g + 1) * ns]
        scores = _dot_nt(cg, bg)
        s_g = s_scr[:, g * gh * hp:(g + 1) * gh * hp]
        y_int = _dot(cg, s_g.astype(BF16))
        xw = []
        for hh in range(gh):
            h = g * gh + hh
            xh = xbc_ref[:, h * hp:(h + 1) * hp].astype(F32)
            seg = acum[:, h:h + 1] - acum_t[h:h + 1, :]
            dec = jnp.where(incl, jnp.exp(jnp.where(incl, seg, 0.0)), 0.0)
            m = (scores * dec).astype(BF16)
            y = _dot(m, (xh * dt[:, h:h + 1]).astype(BF16))
            y = y + y_int[:, hh * hp:(hh + 1) * hp] * e_acum[:, h:h + 1]
            ys.append(y + jnp.where(d == 0, dsk[:, h:h + 1], 0.0) * xh)
            xw.append((xh * w_end[:, h:h + 1]).astype(BF16))
        upd = _dot_tn(bg, jnp.concatenate(xw, axis=1))
        for hh in range(gh):
            h = g * gh + hh
            new_states.append(s_g[:, hh * hp:(hh + 1) * hp] * c_dec[:, h:h + 1] + upd[:, hh * hp:(hh + 1) * hp])
    y_ref[...] = jnp.concatenate(ys, axis=1)
    s_scr[...] = jnp.concatenate(new_states, axis=1)


def _ssd_call(xbc, small, a_log, dt_bias, d_skip, *, n_b, lx, lc):
    rows = xbc.shape[0]
    nxc, ncc = lx // CHUNK, lc // CHUNK
    blk = _chunk_block(n_b, nxc, ncc)
    nh = SSD_HEADS
    return pl.pallas_call(
        _ssd_kernel,
        grid=(n_b, 2, ncc + nxc),
        in_specs=[pl.BlockSpec((CHUNK, SSD_CONV_DIM), lambda b, d, c: (blk(b, d, c), 0)),
                  pl.BlockSpec((CHUNK, P_SMALL), lambda b, d, c: (blk(b, d, c), 0)),
                  pl.BlockSpec((1, P_SMALL), lambda b, d, c: (0, 0)),
                  pl.BlockSpec((1, P_SMALL), lambda b, d, c: (0, 0)),
                  pl.BlockSpec((1, nh), lambda b, d, c: (0, 0))],
        out_specs=pl.BlockSpec((None, CHUNK, SSD_D_INNER), lambda b, d, c: (d, blk(b, d, c), 0)),
        out_shape=jax.ShapeDtypeStruct((2, rows, SSD_D_INNER), F32),
        scratch_shapes=[pltpu.VMEM((SSD_STATE, SSD_D_INNER), F32)],
        compiler_params=_cp(("parallel", "arbitrary", "arbitrary"), 40),
        name="ssd_scan",
    )(xbc, small, _lane_vec(a_log, 0), _lane_vec(dt_bias, 0), d_skip.reshape(1, nh))


SOLVE_BLOCK = 16


def _unit_tri_solve_many(n_mats, rhss):
    ln = n_mats[0].shape[0]
    row = lax.broadcasted_iota(jnp.int32, (ln, ln), 0)
    col = lax.broadcasted_iota(jnp.int32, (ln, ln), 1)
    on_diag_block = (row // SOLVE_BLOCK) == (col // SOLVE_BLOCK)
    eye = jnp.where(row == col, 1.0, 0.0)
    mm = lambda a, b: _dot(a.astype(BF16), b.astype(BF16))
    ms = [jnp.where(on_diag_block, -n, 0.0) for n in n_mats]
    es = [jnp.where(on_diag_block, 0.0, n) for n in n_mats]
    ps = [eye + m for m in ms]
    mps = ms
    k = 1
    while 2 * k < SOLVE_BLOCK:
        mps = [mm(x, x) for x in mps]
        ps = [p + mm(p, x) for p, x in zip(ps, mps)]
        k *= 2
    f_pows = [[-mm(p, e) for p, e in zip(ps, es)]]
    ys = [mm(p, r) for p, r in zip(ps, rhss)]
    k = 1
    while 2 * k < ln // SOLVE_BLOCK:
        f_pows.append([mm(f, f) for f in f_pows[-1]])
        k *= 2
    for fl in reversed(f_pows):
        ys = [y + mm(f, y) for f, y in zip(fl, ys)]
    return ys


GDN_PACK = 5 * GDN_DIM
GDN_G_LANE = 2 * SSD_HEADS
GDN_B_LANE = 2 * SSD_HEADS + 2 * GDN_HEADS


def _gdn_prep_kernel(qkv_ref, sm_ref, alog_ref, dtb_ref, o_ref):
    nh, hd = GDN_HEADS, GDN_HEAD_DIM
    sm = sm_ref[...]
    g2 = -jnp.exp(alog_ref[...]) * _softplus(sm + dtb_ref[...])
    beta2 = _sigmoid(sm)
    row = lax.broadcasted_iota(jnp.int32, (CHUNK, CHUNK), 0)
    col = lax.broadcasted_iota(jnp.int32, (CHUNK, CHUNK), 1)
    incl = [row >= col, row <= col]
    strict = [row > col, row < col]
    cs = [jnp.dot(m.astype(F32), g2, preferred_element_type=F32, precision=HIGHEST) for m in incl]
    cs_t = [x.T for x in cs]
    gtot2 = jnp.sum(g2, axis=0, keepdims=True)
    qn, kn, kn_b, vv, qk_raw = [], [], [], [], []
    for h in range(nh):
        qh = qkv_ref[:, h * hd:(h + 1) * hd].astype(F32)
        kh = qkv_ref[:, GDN_DIM + h * hd:GDN_DIM + (h + 1) * hd].astype(F32)
        vv.append(qkv_ref[:, 2 * GDN_DIM + h * hd:2 * GDN_DIM + (h + 1) * hd].astype(F32))
        qn.append(qh * (lax.rsqrt(jnp.sum(qh * qh, axis=-1, keepdims=True) + EPS) * hd ** -0.5))
        kn.append(kh * lax.rsqrt(jnp.sum(kh * kh, axis=-1, keepdims=True) + EPS))
        kn_b.append(kn[h].astype(BF16))
        qk_raw.append(_dot_nt(qn[h].astype(BF16), kn_b[h]))
    n_mats, rhss, qks, qds, kds = [], [], [], [], []
    for d in range(2):
        for h in range(nh):
            lg, lb = GDN_G_LANE + d * nh + h, GDN_B_LANE + d * nh + h
            gcc, gcr = cs[d][:, lg:lg + 1], cs_t[d][lg:lg + 1, :]
            beta, gtot = beta2[:, lb:lb + 1], gtot2[:, lg:lg + 1]
            dec = jnp.where(incl[d], jnp.exp(jnp.where(incl[d], gcc - gcr, 0.0)), 0.0)
            kb = kn[h] * beta
            n_mats.append(jnp.where(strict[d], _dot_nt(kb.astype(BF16), kn_b[h]) * dec, 0.0))
            e_gc = jnp.exp(gcc)
            rhss.append(jnp.concatenate([vv[h] * beta, kb * e_gc], axis=1))
            qks.append(qk_raw[h] * dec)
            qds.append(qn[h] * e_gc)
            kds.append(kn[h] * jnp.exp(gtot - gcc))
    sols = _unit_tri_solve_many(n_mats, rhss)
    pieces = []
    for d in range(2):
        js = range(d * nh, (d + 1) * nh)
        pieces += [sols[j][:, 0:hd] for j in js] + [sols[j][:, hd:2 * hd] for j in js]
        pieces += [qks[j] for j in js] + [qds[j] for j in js] + [kds[j] for j in js]
    o_ref[...] = jnp.concatenate([p.astype(BF16) for p in pieces], axis=1)


def _gdn_scan_kernel(pk_ref, sm_ref, alog_ref, dtb_ref, o_ref, s_scr):
    d, c = pl.program_id(1), pl.program_id(2)
    nh, hd = GDN_HEADS, GDN_HEAD_DIM

    @pl.when(c == 0)
    def _():
        s_scr[...] = jnp.zeros_like(s_scr)

    g2 = -jnp.exp(alog_ref[...]) * _softplus(sm_ref[...] + dtb_ref[...])
    g_end2 = jnp.exp(jnp.sum(g2, axis=0, keepdims=True))
    outs, states = [], []
    for h in range(nh):
        lg = GDN_G_LANE + h
        g_end = jnp.where(d == 0, g_end2[:, lg:lg + 1], g_end2[:, lg + nh:lg + nh + 1])
        part = lambda j: pk_ref[:, (j * nh + h) * hd:(j * nh + h + 1) * hd]
        u, w, qk, qd, kd = part(0), part(1), part(2), part(3), part(4)
        s_h = s_scr[:, h * hd:(h + 1) * hd]
        s_b = s_h.astype(BF16)
        v_new = (u.astype(F32) - _dot(w, s_b)).astype(BF16)
        outs.append(_dot(qd, s_b) + _dot(qk, v_new))
        states.append(s_h * g_end + _dot_tn(kd, v_new))
    o_ref[...] = jnp.concatenate(outs, axis=1)
    s_scr[...] = jnp.concatenate(states, axis=1)


def _gdn_call(qkv, small, a_log, dt_bias, *, n_b, lx, lc):
    rows = qkv.shape[0]
    nxc, ncc = lx // CHUNK, lc // CHUNK
    blk = _chunk_block(n_b, nxc, ncc)
    alog_v, dtb_v = _lane_vec(a_log, GDN_G_LANE), _lane_vec(dt_bias, GDN_G_LANE)
    packed = pl.pallas_call(
        _gdn_prep_kernel,
        grid=(rows // CHUNK,),
        in_specs=[pl.BlockSpec((CHUNK, 3 * GDN_DIM), lambda i: (i, 0)),
                  pl.BlockSpec((CHUNK, P_SMALL), lambda i: (i, 0)),
                  pl.BlockSpec((1, P_SMALL), lambda i: (0, 0)),
                  pl.BlockSpec((1, P_SMALL), lambda i: (0, 0))],
        out_specs=pl.BlockSpec((CHUNK, 2 * GDN_PACK), lambda i: (i, 0)),
        out_shape=jax.ShapeDtypeStruct((rows, 2 * GDN_PACK), BF16),
        compiler_params=_cp(("parallel",), 40),
        name="gdn_prep",
    )(qkv, small, alog_v, dtb_v)
    return pl.pallas_call(
        _gdn_scan_kernel,
        grid=(n_b, 2, ncc + nxc),
        in_specs=[pl.BlockSpec((CHUNK, GDN_PACK), lambda b, d, c: (blk(b, d, c), d)),
                  pl.BlockSpec((CHUNK, P_SMALL), lambda b, d, c: (blk(b, d, c), 0)),
                  pl.BlockSpec((1, P_SMALL), lambda b, d, c: (0, 0)),
                  pl.BlockSpec((1, P_SMALL), lambda b, d, c: (0, 0))],
        out_specs=pl.BlockSpec((None, CHUNK, GDN_DIM), lambda b, d, c: (d, blk(b, d, c), 0)),
        out_shape=jax.ShapeDtypeStruct((2, rows, GDN_DIM), F32),
        scratch_shapes=[pltpu.VMEM((GDN_HEAD_DIM, GDN_DIM), F32)],
        compiler_params=_cp(("parallel", "arbitrary", "arbitrary"), 40),
        name="gdn_scan",
    )(packed, small, alog_v, dtb_v)


def _mixout_kernel(h_ref, mod_ref, gpost_ref, ya_ref, ys_ref, zs_ref, sn_ref, yn_ref, og_ref, zg_ref, gn_ref,
                   w_ref, o_ref):
    ssd = (ys_ref[0] + ys_ref[1]) * _silu(zs_ref[...].astype(F32))
    yb = (ssd * _rms_scale(ssd) * sn_ref[...]).astype(BF16)
    gd = og_ref[0] + og_ref[1]
    zg = _silu(zg_ref[...].astype(F32))
    hd = GDN_HEAD_DIM
    yd = []
    for h in range(GDN_HEADS):
        oh = gd[:, h * hd:(h + 1) * hd]
        yd.append((oh * _rms_scale(oh) * gn_ref[...] * zg[:, h * hd:(h + 1) * hd]).astype(BF16))
    parts = [ya_ref[...], yb, yn_ref[...]] + yd
    widths = [512, 512, 512] + [hd] * GDN_HEADS
    y = None
    off = 0
    for part, wd in zip(parts, widths):
        t = _dot(part, w_ref[off:off + wd, :])
        y = t if y is None else y + t
        off += wd
    o_ref[...] = h_ref[...] + mod_ref[2:3, :] * (y * _rms_scale(y) * gpost_ref[...])


def _mixout_call(h, mod_l, g_post, ya, ys, proj, ssd_norm, yn, og, gdn_norm, w_out, *, n_b, lx, n_rows, tm):
    d = h.shape[1]
    nxt, per_b = n_b * lx // tm, lx // tm
    mod_idx = lambda i: (jnp.where(i < nxt, i // per_b, n_b), 0, 0)
    row = lambda i: (i, 0)
    const = lambda i: (0, 0)
    return pl.pallas_call(
        _mixout_kernel,
        grid=(n_rows // tm,),
        in_specs=[pl.BlockSpec((tm, d), row),
                  pl.BlockSpec((None, 6, d), mod_idx),
                  pl.BlockSpec((1, d), const),
                  pl.BlockSpec((tm, 512), row),
                  pl.BlockSpec((2, tm, 512), lambda i: (0, i, 0)),
                  pl.BlockSpec((tm, 512), lambda i: (i, P_ZSSD // 512)),
                  pl.BlockSpec((1, 512), const),
                  pl.BlockSpec((tm, 512), row),
                  pl.BlockSpec((2, tm, 512), lambda i: (0, i, 0)),
                  pl.BlockSpec((tm, 512), lambda i: (i, P_GZ // 512)),
                  pl.BlockSpec((1, GDN_HEAD_DIM), const),
                  pl.BlockSpec((d, d), const)],
        out_specs=pl.BlockSpec((tm, d), row),
        out_shape=jax.ShapeDtypeStruct((n_rows, d), F32),
        compiler_params=_cp(("parallel",), 56),
        name="mix_out",
    )(h, mod_l, g_post.reshape(1, d), ya, ys, proj, ssd_norm.reshape(1, 512), yn, og, proj,
      gdn_norm.reshape(1, GDN_HEAD_DIM), w_out)


def _ffn_kernel(h_ref, mod_ref, gpre_ref, gpost_ref, wg_ref, wu_ref, wd_ref, o_ref, u_scr, acc_scr):
    j = pl.program_id(1)

    @pl.when(j == 0)
    def _():
        x = h_ref[...]
        y = x * _rms_scale(x) * gpre_ref[...]
        u_scr[...] = (y * (1.0 + mod_ref[4:5, :]) + mod_ref[3:4, :]).astype(BF16)
        acc_scr[...] = jnp.zeros_like(acc_scr)

    u = u_scr[...]
    mid = (_silu(_dot(u, wg_ref[...])) * _dot(u, wu_ref[...])).astype(BF16)
    acc_scr[...] += _dot(mid, wd_ref[...])

    @pl.when(j == pl.num_programs(1) - 1)
    def _():
        y = acc_scr[...]
        o_ref[...] = h_ref[...] + mod_ref[5:6, :] * (y * _rms_scale(y) * gpost_ref[...])


def _ffn_call(h, mod_l, g_pre, g_post, wg, wu, wd, *, n_b, lx, tm, tf):
    rows, d = h.shape
    ff = wg.shape[1]
    nxt, per_b = n_b * lx // tm, lx // tm
    mod_idx = lambda i, j: (jnp.where(i < nxt, i // per_b, n_b), 0, 0)
    return pl.pallas_call(
        _ffn_kernel,
        grid=(rows // tm, ff // tf),
        in_specs=[pl.BlockSpec((tm, d), lambda i, j: (i, 0)),
                  pl.BlockSpec((None, 6, d), mod_idx),
                  pl.BlockSpec((1, d), lambda i, j: (0, 0)),
                  pl.BlockSpec((1, d), lambda i, j: (0, 0)),
                  pl.BlockSpec((d, tf), lambda i, j: (0, j)),
                  pl.BlockSpec((d, tf), lambda i, j: (0, j)),
                  pl.BlockSpec((tf, d), lambda i, j: (j, 0))],
        out_specs=pl.BlockSpec((tm, d), lambda i, j: (i, 0)),
        out_shape=jax.ShapeDtypeStruct((rows, d), F32),
        scratch_shapes=[pltpu.VMEM((tm, d), BF16), pltpu.VMEM((tm, d), F32)],
        compiler_params=_cp(("parallel", "arbitrary"), 56),
        name="ffn_swiglu",
    )(h, mod_l, g_pre.reshape(1, d), g_post.reshape(1, d), wg, wu, wd)


def _moe_kernel(h_ref, mod_ref, gpre_ref, gpost_ref, wr_ref, wg_ref, wu_ref, wd_ref, o_ref,
                u_scr, gate_scr, acc_scr):
    e, j = pl.program_id(1), pl.program_id(2)
    lanes = gate_scr.shape[1]

    @pl.when((e == 0) & (j == 0))
    def _():
        x = h_ref[...]
        y = x * _rms_scale(x) * gpre_ref[...]
        u = y * (1.0 + mod_ref[4:5, :]) + mod_ref[3:4, :]
        u_scr[...] = u.astype(BF16)
        acc_scr[...] = jnp.zeros_like(acc_scr)
        logits = jnp.dot(u, wr_ref[...], preferred_element_type=F32, precision=HIGHEST)
        lane = lax.broadcasted_iota(jnp.int32, logits.shape, 1).astype(F32)
        lg = jnp.where(lane < N_EXPERTS, logits, NEG)
        m1 = jnp.max(lg, axis=-1, keepdims=True)
        i1 = jnp.min(jnp.where(lg == m1, lane, float(lanes)), axis=-1, keepdims=True)
        lg2 = jnp.where(lane == i1, NEG, lg)
        m2 = jnp.max(lg2, axis=-1, keepdims=True)
        i2 = jnp.min(jnp.where(lg2 == m2, lane, float(lanes)), axis=-1, keepdims=True)
        e2 = jnp.exp(m2 - m1)
        gate_scr[...] = jnp.where(lane == i1, 1.0 / (1.0 + e2), 0.0) + jnp.where(lane == i2, e2 / (1.0 + e2), 0.0)

    lane = lax.broadcasted_iota(jnp.int32, gate_scr.shape, 1)
    ge = jnp.sum(jnp.where(lane == e, gate_scr[...], 0.0), axis=-1, keepdims=True)
    u = u_scr[...]
    mid = (_silu(_dot(u, wg_ref[...])) * _dot(u, wu_ref[...]) * ge).astype(BF16)
    acc_scr[...] += _dot(mid, wd_ref[...])

    @pl.when((e == pl.num_programs(1) - 1) & (j == pl.num_programs(2) - 1))
    def _():
        y = acc_scr[...]
        o_ref[...] = h_ref[...] + mod_ref[5:6, :] * (y * _rms_scale(y) * gpost_ref[...])


def _moe_call(h, mod_l, g_pre, g_post, router_pad, wg, wu, wd, *, n_b, lx, tm, tf):
    rows, d = h.shape
    ne, _, fe = wg.shape
    nxt, per_b = n_b * lx // tm, lx // tm
    mod_idx = lambda i, e, j: (jnp.where(i < nxt, i // per_b, n_b), 0, 0)
    c2 = lambda i, e, j: (0, 0)
    return pl.pallas_call(
        _moe_kernel,
        grid=(rows // tm, ne, fe // tf),
        in_specs=[pl.BlockSpec((tm, d), lambda i, e, j: (i, 0)),
                  pl.BlockSpec((None, 6, d), mod_idx),
                  pl.BlockSpec((1, d), c2),
                  pl.BlockSpec((1, d), c2),
                  pl.BlockSpec((d, 128), c2),
                  pl.BlockSpec((None, d, tf), lambda i, e, j: (e, 0, j)),
                  pl.BlockSpec((None, d, tf), lambda i, e, j: (e, 0, j)),
                  pl.BlockSpec((None, tf, d), lambda i, e, j: (e, j, 0))],
        out_specs=pl.BlockSpec((tm, d), lambda i, e, j: (i, 0)),
        out_shape=jax.ShapeDtypeStruct((rows, d), F32),
        scratch_shapes=[pltpu.VMEM((tm, d), BF16), pltpu.VMEM((tm, 128), F32), pltpu.VMEM((tm, d), F32)],
        compiler_params=_cp(("parallel", "arbitrary", "arbitrary"), 56),
        name="moe_swiglu",
    )(h, mod_l, g_pre.reshape(1, d), g_post.reshape(1, d), router_pad, wg, wu, wd)


def _rope_tables(lx, tm):
    half = MLA_ROPE // 2
    n_axis = half // 2
    inv_freq = ROPE_THETA ** (-jnp.arange(n_axis, dtype=F32) / n_axis)
    pos = jnp.arange(lx)
    rows = (pos // GRID_W).astype(F32)
    cols = (pos % GRID_W).astype(F32)
    ang = jnp.concatenate([rows[:, None] * inv_freq, cols[:, None] * inv_freq], axis=-1)
    cos, sin = jnp.cos(ang), jnp.sin(ang)
    cos_t = jnp.concatenate([cos, cos], axis=-1)
    sin_t = jnp.concatenate([-sin, sin], axis=-1)
    cos_t = jnp.concatenate([cos_t, jnp.ones((tm, MLA_ROPE), F32)], axis=0)
    sin_t = jnp.concatenate([sin_t, jnp.zeros((tm, MLA_ROPE), F32)], axis=0)
    return cos_t, sin_t


def _mla_weights(q_norm, w_uq, kv_norm, w_ukv):
    dq = MLA_NOPE + MLA_ROPE
    half = MLA_ROPE // 2
    cols = lambda a, b: w_uq[:, a:b]
    nope = [cols(h * dq, h * dq + MLA_NOPE) for h in range(MLA_HEADS)]
    rope = [cols(h * dq + MLA_NOPE, (h + 1) * dq) for h in range(MLA_HEADS)]
    rope_sw = [cols(h * dq + MLA_NOPE + s * half, h * dq + MLA_NOPE + (s + 1) * half)
               for h in range(MLA_HEADS) for s in (1, 0)]
    pad = ((0, MLA_ROPE), (0, 0))
    wq = jnp.pad(jnp.concatenate(nope + rope, axis=1), pad).astype(BF16)
    wqs = jnp.pad(jnp.concatenate(rope_sw, axis=1), pad).astype(BF16)
    qn_ext = jnp.pad(q_norm, (0, MLA_ROPE)).reshape(1, -1)
    perm = np.zeros((MLA_ROPE, MLA_ROPE), np.float32)
    perm[(np.arange(MLA_ROPE) + half) % MLA_ROPE, np.arange(MLA_ROPE)] = 1.0
    return qn_ext, wq, wqs, kv_norm.reshape(1, -1), w_ukv.astype(BF16), jnp.asarray(perm, BF16)


def _pick_tile(n, cands):
    for t in cands:
        if n % t == 0:
            return t
    raise ValueError(f"no tile for {n}")


def kernel(x, c, ctx, c_ctx, w_ada, b_ada, g_pre_mix, g_post_mix, g_pre_ffn, g_post_ffn, w_in, w_out, mla_q_norm, mla_w_uq, mla_kv_norm, mla_w_ukv, ssd_conv_w, ssd_conv_b, ssd_a_log, ssd_dt_bias, ssd_d, ssd_norm, na_rpb, gdn_conv_w, gdn_a_log, gdn_dt_bias, gdn_norm, ffn_w_gate, ffn_w_up, ffn_w_down, moe_router, moe_w_gate, moe_w_up, moe_w_down):
    n_b, lx, d = x.shape
    lc = ctx.shape[1]
    depth = w_ada.shape[0]
    rows_x, rows_c = n_b * lx, n_b * lc
    assert n_b + 1 <= 8 and lx % GRID_W == 0 and lx % lc == 0 and lc % CHUNK == 0
    tm = _pick_tile(math.gcd(lx, rows_c), (512, 256, 128))
    tm_in = _pick_tile(math.gcd(lx, rows_c), (1024, 512, 256, 128))
    tq = _pick_tile(lc, (256, 128))
    tr = _pick_tile(lc, (256, 128))

    cvec = jnp.concatenate([c, c_ctx[None, :], jnp.zeros((8 - n_b - 1, d), F32)], axis=0)
    mod = _ada_call(cvec, w_ada, b_ada).reshape(depth, 8, 6, d)
    cos_t, sin_t = _rope_tables(lx, tm)
    h_all = jnp.concatenate([x.reshape(rows_x, d), ctx.reshape(rows_c, d)], axis=0)

    for i in range(depth):
        need_ctx = i < depth - 1
        w_main, w_small = _regroup_w_in(w_in[i])
        proj, small = _inproj_call(h_all, mod[i], g_pre_mix[i], w_main, w_small, n_b=n_b, lx=lx, tm=tm_in)

        mla_w = _mla_weights(mla_q_norm[i], mla_w_uq[i], mla_kv_norm[i], mla_w_ukv[i])
        q_a, k_a, v_a = _mla_prep_call(proj, *mla_w[:5], cos_t, sin_t, mla_w[5], n_b=n_b, lx=lx, tm=tm)
        ya = _mla_attn_call(q_a, k_a, v_a, n_b=n_b, lx=lx, lc=lc, tq=tq, need_ctx=need_ctx)

        xbc = _conv_call(proj, ssd_conv_w[i], ssd_conv_b[i], col_off=P_XBC, n_b=n_b, lx=lx, lc=lc, tr=tr)
        ys = _ssd_call(xbc, small, ssd_a_log[i], ssd_dt_bias[i], ssd_d[i], n_b=n_b, lx=lx, lc=lc)

        yn = _na_call(proj, _na_bias_table(na_rpb[i], lx // GRID_W, lc), n_b=n_b, lx=lx, lc=lc, need_ctx=need_ctx)

        qkv = _conv_call(proj, gdn_conv_w[i], jnp.zeros((3 * GDN_DIM,), F32), col_off=P_GQKV,
                         n_b=n_b, lx=lx, lc=lc, tr=tr)
        og = _gdn_call(qkv, small, gdn_a_log[i], gdn_dt_bias[i], n_b=n_b, lx=lx, lc=lc)

        n_rows = rows_x + rows_c if need_ctx else rows_x
        h_mid = _mixout_call(h_all, mod[i], g_post_mix[i], ya, ys, proj, ssd_norm[i], yn, og, gdn_norm[i],
                             w_out[i].astype(BF16), n_b=n_b, lx=lx, n_rows=n_rows, tm=min(tm, 256))
        j = i // 2
        if i % 2 == 0:
            h_all = _ffn_call(h_mid, mod[i], g_pre_ffn[i], g_post_ffn[i], ffn_w_gate[j].astype(BF16),
                              ffn_w_up[j].astype(BF16), ffn_w_down[j].astype(BF16), n_b=n_b, lx=lx, tm=tm, tf=512)
        else:
            router_pad = jnp.pad(moe_router[j], ((0, 0), (0, 128 - N_EXPERTS)))
            h_all = _moe_call(h_mid, mod[i], g_pre_ffn[i], g_post_ffn[i], router_pad, moe_w_gate[j].astype(BF16),
                              moe_w_up[j].astype(BF16), moe_w_down[j].astype(BF16), n_b=n_b, lx=lx, tm=tm, tf=256)
    return h_all[:rows_x].reshape(n_b, lx, d)
```

```python
import functools
import math

import numpy as np
import jax
import jax.numpy as jnp
from jax import lax
from jax.experimental import pallas as pl
from jax.experimental.pallas import tpu as pltpu

F32 = jnp.float32
BF16 = jnp.bfloat16
HIGHEST = lax.Precision.HIGHEST

GRID_W = 64
EPS = 1e-6
ROPE_THETA = 10000.0
CHUNK = 128
CONV_W = 5
MLA_HEADS, MLA_NOPE, MLA_ROPE, MLA_V = 4, 128, 64, 128
MLA_Q_LORA, MLA_KV_LORA = 448, 128
SSD_HEADS, SSD_HEAD_DIM, SSD_STATE, SSD_GROUPS = 8, 64, 128, 2
SSD_D_INNER = SSD_HEADS * SSD_HEAD_DIM
SSD_CONV_DIM = SSD_D_INNER + 2 * SSD_GROUPS * SSD_STATE
NA_HEADS, NA_HEAD_DIM = 4, 128
NA_DIM = NA_HEADS * NA_HEAD_DIM
NA_WIN_ROWS, NA_WIN_COLS = 8, 16
GDN_HEADS, GDN_HEAD_DIM = 4, 128
GDN_DIM = GDN_HEADS * GDN_HEAD_DIM
N_EXPERTS, TOP_K = 8, 2
MLA_COLS = MLA_Q_LORA + MLA_KV_LORA + MLA_ROPE
SSD_COLS = SSD_D_INNER + SSD_CONV_DIM + 2 * SSD_HEADS
NA_COLS = 3 * NA_DIM
GDN_COLS = 4 * GDN_DIM + 4 * GDN_HEADS

P_XBC, P_ZSSD, P_NAQ, P_NAK, P_NAV = 0, 1024, 1536, 2048, 2560
P_GQKV, P_GZ, P_MLA = 3072, 4608, 5120
P_MAIN = 5760
P_SMALL = 128
NEG = -1e30
VMEM_MB = 1024 * 1024


def _cp(sem, mb):
    return pltpu.CompilerParams(dimension_semantics=sem, vmem_limit_bytes=mb * VMEM_MB)


def _dot(a, b):
    return jnp.dot(a, b, preferred_element_type=F32)


def _dot_nt(a, b, precision=None):
    return lax.dot_general(a, b, (((1,), (1,)), ((), ())), preferred_element_type=F32, precision=precision)


def _dot_tn(a, b):
    return lax.dot_general(a, b, (((0,), (0,)), ((), ())), preferred_element_type=F32)


def _sigmoid(x):
    return 1.0 / (1.0 + jnp.exp(-x))


def _silu(x):
    return x * _sigmoid(x)


def _softplus(x):
    return jnp.maximum(x, 0.0) + jnp.log(1.0 + jnp.exp(-jnp.abs(x)))


def _rms_scale(x):
    return lax.rsqrt(jnp.mean(x * x, axis=-1, keepdims=True) + EPS)


def _regroup_w_in(w):
    o_mla, o_ssd = 0, MLA_COLS
    o_na, o_gdn = o_ssd + SSD_COLS, o_ssd + SSD_COLS + NA_COLS
    main_segs = [
        (o_ssd + SSD_D_INNER, SSD_CONV_DIM),
        (o_ssd, SSD_D_INNER),
        (o_na, NA_COLS),
        (o_gdn, 4 * GDN_DIM),
        (o_mla + MLA_Q_LORA, MLA_KV_LORA),
        (o_mla, MLA_Q_LORA),
        (o_mla + MLA_Q_LORA + MLA_KV_LORA, MLA_ROPE),
    ]
    small_segs = [(o_ssd + SSD_D_INNER + SSD_CONV_DIM, 2 * SSD_HEADS), (o_gdn + 4 * GDN_DIM, 4 * GDN_HEADS)]
    assert sum(n for _, n in main_segs) == P_MAIN
    main = jnp.concatenate([w[:, a:a + n] for a, n in main_segs], axis=1).astype(BF16)
    n_small = sum(n for _, n in small_segs)
    small = jnp.concatenate([w[:, a:a + n] for a, n in small_segs]
                            + [jnp.zeros((w.shape[0], P_SMALL - n_small), w.dtype)], axis=1).astype(BF16)
    return main, small


def _ada_kernel(c_ref, w_ref, b_ref, o_ref):
    s = _silu(c_ref[...]).astype(BF16)
    o_ref[...] = _dot(s, w_ref[...].astype(BF16)) + b_ref[...]


def _ada_call(cvec, w_ada, b_ada):
    depth, d, n = w_ada.shape
    tn = 1024
    return pl.pallas_call(
        _ada_kernel,
        grid=(depth, n // tn),
        in_specs=[pl.BlockSpec((8, d), lambda l, j: (0, 0)),
                  pl.BlockSpec((None, d, tn), lambda l, j: (l, 0, j)),
                  pl.BlockSpec((None, 1, tn), lambda l, j: (l, 0, j))],
        out_specs=pl.BlockSpec((None, 8, tn), lambda l, j: (l, 0, j)),
        out_shape=jax.ShapeDtypeStruct((depth, 8, n), F32),
        compiler_params=_cp(("parallel", "parallel"), 40),
        name="adaln",
    )(cvec, w_ada, b_ada.reshape(depth, 1, n))


def _inproj_kernel(h_ref, mod_ref, g_ref, w_ref, ws_ref, o_ref, os_ref, u_scr):
    @pl.when(pl.program_id(1) == 0)
    def _():
        x = h_ref[...]
        y = x * _rms_scale(x) * g_ref[...]
        u = (y * (1.0 + mod_ref[1:2, :]) + mod_ref[0:1, :]).astype(BF16)
        u_scr[...] = u
        os_ref[...] = _dot(u, ws_ref[...])

    o_ref[...] = _dot(u_scr[...], w_ref[...]).astype(BF16)


def _inproj_call(h, mod_l, g_pre, w_main, w_small, *, n_b, lx, tm):
    rows, d = h.shape
    tn = 1920
    nxt, per_b = n_b * lx // tm, lx // tm

    def mod_idx(i, j):
        return (jnp.where(i < nxt, i // per_b, n_b), 0, 0)

    return pl.pallas_call(
        _inproj_kernel,
        grid=(rows // tm, P_MAIN // tn),
        in_specs=[pl.BlockSpec((tm, d), lambda i, j: (i, 0)),
                  pl.BlockSpec((None, 6, d), mod_idx),
                  pl.BlockSpec((1, d), lambda i, j: (0, 0)),
                  pl.BlockSpec((d, tn), lambda i, j: (0, j)),
                  pl.BlockSpec((d, P_SMALL), lambda i, j: (0, 0))],
        out_specs=[pl.BlockSpec((tm, tn), lambda i, j: (i, j)),
                   pl.BlockSpec((tm, P_SMALL), lambda i, j: (i, 0))],
        out_shape=[jax.ShapeDtypeStruct((rows, P_MAIN), BF16),
                   jax.ShapeDtypeStruct((rows, P_SMALL), F32)],
        scratch_shapes=[pltpu.VMEM((tm, d), BF16)],
        compiler_params=_cp(("parallel", "arbitrary"), 56),
        name="in_proj",
    )(h, mod_l, g_pre.reshape(1, d), w_main, w_small)


HALO = 16


def _conv_kernel(prev_ref, cur_ref, next_ref, w_ref, b_ref, o_ref, ext_scr, *, tr, blocks_x, seq_x, seq_c):
    i = pl.program_id(0)
    in_x = i < blocks_x
    pos = jnp.where(in_x, i % seq_x, (i - blocks_x) % seq_c)
    last_pos = jnp.where(in_x, seq_x - 1, seq_c - 1)
    ext_scr[0:HALO, :] = jnp.where(pos == 0, 0.0, prev_ref[...].astype(F32))
    ext_scr[HALO:HALO + tr, :] = cur_ref[...].astype(F32)
    ext_scr[HALO + tr:2 * HALO + tr, :] = jnp.where(pos == last_pos, 0.0, next_ref[...].astype(F32))
    acc = b_ref[...] + w_ref[0:1, :] * ext_scr[HALO - 2:HALO - 2 + tr, :]
    for k in range(1, CONV_W):
        acc = acc + w_ref[k:k + 1, :] * ext_scr[HALO - 2 + k:HALO - 2 + k + tr, :]
    o_ref[...] = _silu(acc).astype(BF16)


def _conv_call(proj, w, b, *, col_off, n_b, lx, lc, tr):
    rows = proj.shape[0]
    c = w.shape[1]
    cb = col_off // c
    assert cb * c == col_off
    hb = tr // HALO
    n_halo = rows // HALO
    kern = functools.partial(_conv_kernel, tr=tr, blocks_x=n_b * lx // tr, seq_x=lx // tr, seq_c=lc // tr)
    return pl.pallas_call(
        kern,
        grid=(rows // tr,),
        in_specs=[pl.BlockSpec((HALO, c), lambda i: (jnp.maximum(i * hb - 1, 0), cb)),
                  pl.BlockSpec((tr, c), lambda i: (i, cb)),
                  pl.BlockSpec((HALO, c), lambda i: (jnp.minimum((i + 1) * hb, n_halo - 1), cb)),
                  pl.BlockSpec((CONV_W, c), lambda i: (0, 0)),
                  pl.BlockSpec((1, c), lambda i: (0, 0))],
        out_specs=pl.BlockSpec((tr, c), lambda i: (i, 0)),
        out_shape=jax.ShapeDtypeStruct((rows, c), BF16),
        scratch_shapes=[pltpu.VMEM((tr + 2 * HALO, c), F32)],
        compiler_params=_cp(("parallel",), 40),
        name="dwconv_silu",
    )(proj, proj, proj, w, b.reshape(1, c))


def _mla_prep_kernel(p_ref, qn_ref, wq_ref, wqs_ref, kvn_ref, wkv_ref, cos_ref, sin_ref, perm_ref,
                     q_ref, k_ref, v_ref):
    p = p_ref[...].astype(F32)
    ckv = p[:, 0:MLA_KV_LORA]
    ce = p[:, MLA_KV_LORA:]
    lane = lax.broadcasted_iota(jnp.int32, ce.shape, 1)
    ssq = jnp.sum(jnp.where(lane < MLA_Q_LORA, ce * ce, 0.0), axis=-1, keepdims=True)
    cqn = (ce * lax.rsqrt(ssq / MLA_Q_LORA + EPS) * qn_ref[...]).astype(BF16)
    ckvn = (ckv * _rms_scale(ckv) * kvn_ref[...]).astype(BF16)
    q = _dot(cqn, wq_ref[...])
    qs = _dot(cqn, wqs_ref[...])
    kv = _dot(ckvn, wkv_ref[...])
    cos, sin = cos_ref[...], sin_ref[...]
    kr = p_ref[:, MLA_KV_LORA + MLA_Q_LORA:]
    kr_rot = kr.astype(F32) * cos + _dot(kr, perm_ref[...]) * sin
    nr = MLA_HEADS * MLA_NOPE
    scale = (MLA_NOPE + MLA_ROPE) ** -0.5
    ones_col = jnp.where(lax.broadcasted_iota(jnp.int32, (p.shape[0], MLA_V), 1) == 0, 1.0, 0.0).astype(BF16)
    for h in range(MLA_HEADS):
        q_ref[h, :, 0:MLA_NOPE] = (q[:, h * MLA_NOPE:(h + 1) * MLA_NOPE] * scale).astype(BF16)
        qr = q[:, nr + h * MLA_ROPE:nr + (h + 1) * MLA_ROPE] * cos + qs[:, h * MLA_ROPE:(h + 1) * MLA_ROPE] * sin
        q_ref[h, :, MLA_NOPE:] = (qr * scale).astype(BF16)
        hv = h * (MLA_NOPE + MLA_V)
        k_ref[h, :, 0:MLA_NOPE] = kv[:, hv:hv + MLA_NOPE].astype(BF16)
        k_ref[h, :, MLA_NOPE:] = kr_rot.astype(BF16)
        v_ref[h, :, 0:MLA_V] = kv[:, hv + MLA_NOPE:hv + MLA_NOPE + MLA_V].astype(BF16)
        v_ref[h, :, MLA_V:] = ones_col


def _mla_prep_call(proj, qn_ext, wq, wqs, kvn, wkv, cos_t, sin_t, perm, *, n_b, lx, tm):
    rows = proj.shape[0]
    nxt, per_b = n_b * lx // tm, lx // tm
    dk = MLA_NOPE + MLA_ROPE
    rope_idx = lambda i: (jnp.where(i < nxt, i % per_b, per_b), 0)
    full = lambda a: pl.BlockSpec(a.shape, lambda i: (0,) * a.ndim)
    return pl.pallas_call(
        _mla_prep_kernel,
        grid=(rows // tm,),
        in_specs=[pl.BlockSpec((tm, MLA_COLS), lambda i: (i, P_MLA // MLA_COLS)),
                  full(qn_ext), full(wq), full(wqs), full(kvn), full(wkv),
                  pl.BlockSpec((tm, MLA_ROPE), rope_idx), pl.BlockSpec((tm, MLA_ROPE), rope_idx),
                  full(perm)],
        out_specs=[pl.BlockSpec((MLA_HEADS, tm, dk), lambda i: (0, i, 0)),
                   pl.BlockSpec((MLA_HEADS, tm, dk), lambda i: (0, i, 0)),
                   pl.BlockSpec((MLA_HEADS, tm, 2 * MLA_V), lambda i: (0, i, 0))],
        out_shape=[jax.ShapeDtypeStruct((MLA_HEADS, rows, dk), BF16),
                   jax.ShapeDtypeStruct((MLA_HEADS, rows, dk), BF16),
                   jax.ShapeDtypeStruct((MLA_HEADS, rows, 2 * MLA_V), BF16)],
        compiler_params=_cp(("parallel",), 40),
        name="mla_prep",
    )(proj, qn_ext, wq, wqs, kvn, wkv, cos_t, sin_t, perm)


def _softmax_pv(scores, values):
    m = functools.reduce(jnp.maximum, [jnp.max(s, axis=-1, keepdims=True) for s in scores])
    ps = [jnp.exp(s - m) for s in scores]
    den = functools.reduce(lambda a, b: a + b, [jnp.sum(p, axis=-1, keepdims=True) for p in ps])
    num = functools.reduce(lambda a, b: a + b, [_dot(p.astype(BF16), v) for p, v in zip(ps, values)])
    return num / den


def _softmax_pv_aug(scores, values_aug):
    m = functools.reduce(jnp.maximum, [jnp.max(s, axis=-1, keepdims=True) for s in scores])
    acc = functools.reduce(lambda a, b: a + b,
                           [_dot(jnp.exp((s - m).astype(BF16)), v) for s, v in zip(scores, values_aug)])
    return acc[:, 0:MLA_V] / acc[:, MLA_V:MLA_V + 1]


MLA_HEADS_PER_STEP = 2


def _mla_attn_kernel(q_ref, kx_ref, vx_ref, kc_ref, vc_ref, o_ref, *, nqx):
    qi = pl.program_id(2)
    heads = range(q_ref.shape[0])

    @pl.when(qi < nqx)
    def _():
        outs = [_softmax_pv_aug([_dot_nt(q_ref[h], kx_ref[h]), _dot_nt(q_ref[h], kc_ref[h])], [vx_ref[h], vc_ref[h]])
                for h in heads]
        o_ref[...] = jnp.concatenate(outs, axis=1).astype(BF16)

    @pl.when(qi >= nqx)
    def _():
        outs = [_softmax_pv_aug([_dot_nt(q_ref[h], kc_ref[h])], [vc_ref[h]]) for h in heads]
        o_ref[...] = jnp.concatenate(outs, axis=1).astype(BF16)


def _mla_attn_call(q, k, v, *, n_b, lx, lc, tq, need_ctx):
    rows = q.shape[1]
    dk = q.shape[2]
    nqx, nqc = lx // tq, lc // tq
    nq = nqx + (nqc if need_ctx else 0)
    nbx = n_b * lx // lc

    def q_row(b, qi):
        return jnp.where(qi < nqx, b * nqx + qi, n_b * nqx + b * nqc + (qi - nqx))

    kern = functools.partial(_mla_attn_kernel, nqx=nqx)
    out_rows = rows if need_ctx else n_b * lx
    hp = MLA_HEADS_PER_STEP
    return pl.pallas_call(
        kern,
        grid=(n_b, MLA_HEADS // hp, nq),
        in_specs=[pl.BlockSpec((hp, tq, dk), lambda b, h, qi: (h, q_row(b, qi), 0)),
                  pl.BlockSpec((hp, lx, dk), lambda b, h, qi: (h, b, 0)),
                  pl.BlockSpec((hp, lx, 2 * MLA_V), lambda b, h, qi: (h, b, 0)),
                  pl.BlockSpec((hp, lc, dk), lambda b, h, qi: (h, nbx + b, 0)),
                  pl.BlockSpec((hp, lc, 2 * MLA_V), lambda b, h, qi: (h, nbx + b, 0))],
        out_specs=pl.BlockSpec((tq, hp * MLA_V), lambda b, h, qi: (q_row(b, qi), h)),
        out_shape=jax.ShapeDtypeStruct((out_rows, MLA_HEADS * MLA_V), BF16),
        compiler_params=_cp(("parallel", "parallel", "arbitrary"), 48),
        name="mla_attn",
    )(q, k, v, k, v)


def _na_plan(g_rows, lc):
    wr = min(NA_WIN_ROWS, g_rows)
    rg = next(r for r in (4, 2, 1) if g_rows % r == 0 and lc % (r * GRID_W) == 0)
    wk = min(rg + wr - 1, g_rows)
    n_groups = g_rows // rg
    ks = np.clip(np.arange(n_groups) * rg - wr // 2, 0, g_rows - wk)
    r = np.arange(g_rows)
    rs = np.clip(r - wr // 2, 0, g_rows - wr)
    q_off = (r - np.repeat(ks, rg)).reshape(n_groups, rg)
    rel = (rs - np.repeat(ks, rg)).reshape(n_groups, rg)
    assert (rel >= 0).all() and (rel + wr <= wk).all()
    pats = [tuple(q_off[g]) + tuple(rel[g]) for g in range(n_groups)]
    uniq = sorted(set(pats))
    var = np.array([uniq.index(p) for p in pats], np.int32)
    q_off_v = np.array([p[:rg] for p in uniq])
    rel_v = np.array([p[rg:] for p in uniq])
    return wr, rg, wk, ks.astype(np.int32), var, q_off_v, rel_v


def _na_bias_table(rpb, g_rows, lc):
    wr, rg, wk, _, _, q_off_v, rel_v = _na_plan(g_rows, lc)
    col_start = np.clip(np.arange(GRID_W) - NA_WIN_COLS // 2, 0, GRID_W - NA_WIN_COLS)
    cc = np.arange(GRID_W)
    col_ok = (cc[None, :] >= col_start[:, None]) & (cc[None, :] < col_start[:, None] + NA_WIN_COLS)
    dc = np.clip(cc[None, :] - cc[:, None] + NA_WIN_COLS - 1, 0, 2 * NA_WIN_COLS - 2)
    n_dc = 2 * NA_WIN_COLS - 1
    onehot = (dc.reshape(-1)[:, None] == np.arange(n_dc)[None, :]).astype(np.float32)
    g = jnp.einsum('hab,yb->hay', rpb.astype(F32), jnp.asarray(onehot), precision=HIGHEST)
    g = jnp.where(col_ok[None, None], g.reshape(rpb.shape[0], -1, GRID_W, GRID_W), NEG)
    neg_blk = jnp.full((rpb.shape[0], GRID_W, GRID_W), NEG, F32)
    tabs = []
    for v in range(q_off_v.shape[0]):
        rows_v = []
        for j in range(rg):
            blks = []
            for w in range(wk):
                ok = rel_v[v, j] <= w < rel_v[v, j] + wr
                blks.append(g[:, w - q_off_v[v, j] + NA_WIN_ROWS - 1] if ok else neg_blk)
            rows_v.append(jnp.concatenate(blks, axis=-1))
        tabs.append(jnp.concatenate(rows_v, axis=-2))
    return jnp.stack(tabs, axis=0)


def _na_kernel(var_ref, ks_ref, q_ref, kx_ref, vx_ref, kc_ref, vc_ref, bias_ref, o_ref, *, n_groups, wk, scale):
    g = pl.program_id(2)
    q = q_ref[...]
    sc = _dot_nt(q, kc_ref[...]) * scale

    @pl.when(g < n_groups)
    def _():
        start = pl.multiple_of(ks_ref[g] * GRID_W, GRID_W)
        kw = kx_ref[pl.ds(start, wk * GRID_W), :]
        vw = vx_ref[pl.ds(start, wk * GRID_W), :]
        sl = _dot_nt(q, kw) * scale + bias_ref[...]
        o_ref[...] = _softmax_pv([sl, sc], [vw, vc_ref[...]]).astype(BF16)

    @pl.when(g >= n_groups)
    def _():
        o_ref[...] = _softmax_pv([sc], [vc_ref[...]]).astype(BF16)


def _na_call(proj, bias_tab, *, n_b, lx, lc, need_ctx):
    rows = proj.shape[0]
    g_rows = lx // GRID_W
    _, rg, wk, ks, var, _, _ = _na_plan(g_rows, lc)
    n_groups = g_rows // rg
    tq = rg * GRID_W
    nqc = lc // tq
    nq = n_groups + (nqc if need_ctx else 0)
    nbx = n_b * lx // lc
    hd = NA_HEAD_DIM
    cq, ck, cv = P_NAQ // hd, P_NAK // hd, P_NAV // hd

    def q_row(b, g):
        return jnp.where(g < n_groups, b * n_groups + g, n_b * n_groups + b * nqc + (g - n_groups))

    kern = functools.partial(_na_kernel, n_groups=n_groups, wk=wk, scale=hd ** -0.5)
    grid_spec = pltpu.PrefetchScalarGridSpec(
        num_scalar_prefetch=2,
        grid=(n_b, NA_HEADS, nq),
        in_specs=[pl.BlockSpec((tq, hd), lambda b, h, g, var_r, ks_r: (q_row(b, g), cq + h)),
                  pl.BlockSpec((lx, hd), lambda b, h, g, var_r, ks_r: (b, ck + h)),
                  pl.BlockSpec((lx, hd), lambda b, h, g, var_r, ks_r: (b, cv + h)),
                  pl.BlockSpec((lc, hd), lambda b, h, g, var_r, ks_r: (nbx + b, ck + h)),
                  pl.BlockSpec((lc, hd), lambda b, h, g, var_r, ks_r: (nbx + b, cv + h)),
                  pl.BlockSpec((None, None, tq, wk * GRID_W),
                               lambda b, h, g, var_r, ks_r: (var_r[jnp.minimum(g, n_groups - 1)], h, 0, 0))],
        out_specs=pl.BlockSpec((tq, hd), lambda b, h, g, var_r, ks_r: (q_row(b, g), h)),
    )
    return pl.pallas_call(
        kern,
        grid_spec=grid_spec,
        out_shape=jax.ShapeDtypeStruct((rows if need_ctx else n_b * lx, NA_DIM), BF16),
        compiler_params=_cp(("parallel", "parallel", "arbitrary"), 40),
        name="na_attn",
    )(jnp.asarray(var), jnp.asarray(ks), proj, proj, proj, proj, proj, bias_tab)


def _chunk_block(n_b, nxc, ncc):
    def f(b, d, c):
        cc = jnp.where(d == 0, c, ncc - 1 - c)
        cx = jnp.where(d == 0, c - ncc, nxc - 1 - (c - ncc))
        return jnp.where(c < ncc, n_b * nxc + b * ncc + cc, b * nxc + cx)
    return f


def _dir_masks(d):
    row = lax.broadcasted_iota(jnp.int32, (CHUNK, CHUNK), 0)
    col = lax.broadcasted_iota(jnp.int32, (CHUNK, CHUNK), 1)
    diff = (row - col) * jnp.where(d == 0, 1, -1)
    return diff >= 0, diff > 0


def _cumsum_lanes(x, incl):
    cs = jnp.dot(incl.astype(F32), x, preferred_element_type=F32, precision=HIGHEST)
    return cs, cs.T


def _lane_vec(vals, offset):
    flat = vals.reshape(-1).astype(F32)
    return jnp.pad(flat, (offset, P_SMALL - offset - flat.shape[0])).reshape(1, P_SMALL)


def _ssd_kernel(xbc_ref, sm_ref, alog_ref, dtb_ref, dsk_ref, y_ref, s_scr):
    d, c = pl.program_id(1), pl.program_id(2)
    nh, hp, ns = SSD_HEADS, SSD_HEAD_DIM, SSD_STATE
    gh = nh // SSD_GROUPS

    @pl.when(c == 0)
    def _():
        s_scr[...] = jnp.zeros_like(s_scr)

    dt2 = _softplus(sm_ref[...] + dtb_ref[...])
    dta2 = dt2 * (-jnp.exp(alog_ref[...]))
    incl, _ = _dir_masks(d)
    acum2, acum2_t = _cumsum_lanes(dta2, incl)
    tot2 = jnp.sum(dta2, axis=0, keepdims=True)
    pick = lambda a: jnp.where(d == 0, a[:, 0:nh], a[:, nh:2 * nh])
    dt, acum, tot = pick(dt2), pick(acum2), pick(tot2)
    acum_t = jnp.where(d == 0, acum2_t[0:nh, :], acum2_t[nh:2 * nh, :])
    e_acum = jnp.exp(acum)
    w_end = jnp.exp(tot - acum) * dt
    c_dec = jnp.exp(tot)
    dsk = dsk_ref[...]
    bo, co = SSD_D_INNER, SSD_D_INNER + SSD_GROUPS * ns
    ys, new_states = [], []
    for g in range(SSD_GROUPS):
        bg = xbc_ref[:, bo + g * ns:bo + (g + 1) * ns]
        cg = xbc_ref[:, co + g * ns:co + (g + 1) * ns]
        scores = _dot_nt(cg, bg)
        s_g = s_scr[:, g * gh * hp:(g + 1) * gh * hp]
        y_int = _dot(cg, s_g.astype(BF16))
        xw = []
        for hh in range(gh):
            h = g * gh + hh
            xh = xbc_ref[:, h * hp:(h + 1) * hp].astype(F32)
            seg = acum[:, h:h + 1] - acum_t[h:h + 1, :]
            dec = jnp.where(incl, jnp.exp(jnp.where(incl, seg, 0.0)), 0.0)
            m = (scores * dec).astype(BF16)
            y = _dot(m, (xh * dt[:, h:h + 1]).astype(BF16))
            y = y + y_int[:, hh * hp:(hh + 1) * hp] * e_acum[:, h:h + 1]
            ys.append(y + jnp.where(d == 0, dsk[:, h:h + 1], 0.0) * xh)
            xw.append((xh * w_end[:, h:h + 1]).astype(BF16))
        upd = _dot_tn(bg, jnp.concatenate(xw, axis=1))
        for hh in range(gh):
            h = g * gh + hh
            new_states.append(s_g[:, hh * hp:(hh + 1) * hp] * c_dec[:, h:h + 1] + upd[:, hh * hp:(hh + 1) * hp])
    y_ref[...] = jnp.concatenate(ys, axis=1)
    s_scr[...] = jnp.concatenate(new_states, axis=1)


def _ssd_call(xbc, small, a_log, dt_bias, d_skip, *, n_b, lx, lc):
    rows = xbc.shape[0]
    nxc, ncc = lx // CHUNK, lc // CHUNK
    blk = _chunk_block(n_b, nxc, ncc)
    nh = SSD_HEADS
    return pl.pallas_call(
        _ssd_kernel,
        grid=(n_b, 2, ncc + nxc),
        in_specs=[pl.BlockSpec((CHUNK, SSD_CONV_DIM), lambda b, d, c: (blk(b, d, c), 0)),
                  pl.BlockSpec((CHUNK, P_SMALL), lambda b, d, c: (blk(b, d, c), 0)),
                  pl.BlockSpec((1, P_SMALL), lambda b, d, c: (0, 0)),
                  pl.BlockSpec((1, P_SMALL), lambda b, d, c: (0, 0)),
                  pl.BlockSpec((1, nh), lambda b, d, c: (0, 0))],
        out_specs=pl.BlockSpec((None, CHUNK, SSD_D_INNER), lambda b, d, c: (d, blk(b, d, c), 0)),
        out_shape=jax.ShapeDtypeStruct((2, rows, SSD_D_INNER), F32),
        scratch_shapes=[pltpu.VMEM((SSD_STATE, SSD_D_INNER), F32)],
        compiler_params=_cp(("parallel", "arbitrary", "arbitrary"), 40),
        name="ssd_scan",
    )(xbc, small, _lane_vec(a_log, 0), _lane_vec(dt_bias, 0), d_skip.reshape(1, nh))


SOLVE_BLOCK = 16


def _unit_tri_solve_many(n_mats, rhss):
    ln = n_mats[0].shape[0]
    row = lax.broadcasted_iota(jnp.int32, (ln, ln), 0)
    col = lax.broadcasted_iota(jnp.int32, (ln, ln), 1)
    on_diag_block = (row // SOLVE_BLOCK) == (col // SOLVE_BLOCK)
    eye = jnp.where(row == col, 1.0, 0.0)
    mm = lambda a, b: _dot(a.astype(BF16), b.astype(BF16))
    ms = [jnp.where(on_diag_block, -n, 0.0) for n in n_mats]
    es = [jnp.where(on_diag_block, 0.0, n) for n in n_mats]
    ps = [eye + m for m in ms]
    mps = ms
    k = 1
    while 2 * k < SOLVE_BLOCK:
        mps = [mm(x, x) for x in mps]
        ps = [p + mm(p, x) for p, x in zip(ps, mps)]
        k *= 2
    f_pows = [[-mm(p, e) for p, e in zip(ps, es)]]
    ys = [mm(p, r) for p, r in zip(ps, rhss)]
    k = 1
    while 2 * k < ln // SOLVE_BLOCK:
        f_pows.append([mm(f, f) for f in f_pows[-1]])
        k *= 2
    for fl in reversed(f_pows):
        ys = [y + mm(f, y) for f, y in zip(fl, ys)]
    return ys


GDN_PACK = 5 * GDN_DIM
GDN_G_LANE = 2 * SSD_HEADS
GDN_B_LANE = 2 * SSD_HEADS + 2 * GDN_HEADS


def _gdn_prep_kernel(qkv_ref, sm_ref, alog_ref, dtb_ref, o_ref):
    nh, hd = GDN_HEADS, GDN_HEAD_DIM
    sm = sm_ref[...]
    g2 = -jnp.exp(alog_ref[...]) * _softplus(sm + dtb_ref[...])
    beta2 = _sigmoid(sm)
    row = lax.broadcasted_iota(jnp.int32, (CHUNK, CHUNK), 0)
    col = lax.broadcasted_iota(jnp.int32, (CHUNK, CHUNK), 1)
    incl = [row >= col, row <= col]
    strict = [row > col, row < col]
    cs = [jnp.dot(m.astype(F32), g2, preferred_element_type=F32, precision=HIGHEST) for m in incl]
    cs_t = [x.T for x in cs]
    gtot2 = jnp.sum(g2, axis=0, keepdims=True)
    qn, kn, kn_b, vv, qk_raw = [], [], [], [], []
    for h in range(nh):
        qh = qkv_ref[:, h * hd:(h + 1) * hd].astype(F32)
        kh = qkv_ref[:, GDN_DIM + h * hd:GDN_DIM + (h + 1) * hd].astype(F32)
        vv.append(qkv_ref[:, 2 * GDN_DIM + h * hd:2 * GDN_DIM + (h + 1) * hd].astype(F32))
        qn.append(qh * (lax.rsqrt(jnp.sum(qh * qh, axis=-1, keepdims=True) + EPS) * hd ** -0.5))
        kn.append(kh * lax.rsqrt(jnp.sum(kh * kh, axis=-1, keepdims=True) + EPS))
        kn_b.append(kn[h].astype(BF16))
        qk_raw.append(_dot_nt(qn[h].astype(BF16), kn_b[h]))
    n_mats, rhss, qks, qds, kds = [], [], [], [], []
    for d in range(2):
        for h in range(nh):
            lg, lb = GDN_G_LANE + d * nh + h, GDN_B_LANE + d * nh + h
            gcc, gcr = cs[d][:, lg:lg + 1], cs_t[d][lg:lg + 1, :]
            beta, gtot = beta2[:, lb:lb + 1], gtot2[:, lg:lg + 1]
            dec = jnp.where(incl[d], jnp.exp(jnp.where(incl[d], gcc - gcr, 0.0)), 0.0)
            kb = kn[h] * beta
            n_mats.append(jnp.where(strict[d], _dot_nt(kb.astype(BF16), kn_b[h]) * dec, 0.0))
            e_gc = jnp.exp(gcc)
            rhss.append(jnp.concatenate([vv[h] * beta, kb * e_gc], axis=1))
            qks.append(qk_raw[h] * dec)
            qds.append(qn[h] * e_gc)
            kds.append(kn[h] * jnp.exp(gtot - gcc))
    sols = _unit_tri_solve_many(n_mats, rhss)
    pieces = []
    for d in range(2):
        js = range(d * nh, (d + 1) * nh)
        pieces += [sols[j][:, 0:hd] for j in js] + [sols[j][:, hd:2 * hd] for j in js]
        pieces += [qks[j] for j in js] + [qds[j] for j in js] + [kds[j] for j in js]
    o_ref[...] = jnp.concatenate([p.astype(BF16) for p in pieces], axis=1)


def _gdn_scan_kernel(pk_ref, sm_ref, alog_ref, dtb_ref, o_ref, s_scr):
    d, c = pl.program_id(1), pl.program_id(2)
    nh, hd = GDN_HEADS, GDN_HEAD_DIM

    @pl.when(c == 0)
    def _():
        s_scr[...] = jnp.zeros_like(s_scr)

    g2 = -jnp.exp(alog_ref[...]) * _softplus(sm_ref[...] + dtb_ref[...])
    g_end2 = jnp.exp(jnp.sum(g2, axis=0, keepdims=True))
    outs, states = [], []
    for h in range(nh):
        lg = GDN_G_LANE + h
        g_end = jnp.where(d == 0, g_end2[:, lg:lg + 1], g_end2[:, lg + nh:lg + nh + 1])
        part = lambda j: pk_ref[:, (j * nh + h) * hd:(j * nh + h + 1) * hd]
        u, w, qk, qd, kd = part(0), part(1), part(2), part(3), part(4)
        s_h = s_scr[:, h * hd:(h + 1) * hd]
        s_b = s_h.astype(BF16)
        v_new = (u.astype(F32) - _dot(w, s_b)).astype(BF16)
        outs.append(_dot(qd, s_b) + _dot(qk, v_new))
        states.append(s_h * g_end + _dot_tn(kd, v_new))
    o_ref[...] = jnp.concatenate(outs, axis=1)
    s_scr[...] = jnp.concatenate(states, axis=1)


def _gdn_call(qkv, small, a_log, dt_bias, *, n_b, lx, lc):
    rows = qkv.shape[0]
    nxc, ncc = lx // CHUNK, lc // CHUNK
    blk = _chunk_block(n_b, nxc, ncc)
    alog_v, dtb_v = _lane_vec(a_log, GDN_G_LANE), _lane_vec(dt_bias, GDN_G_LANE)
    packed = pl.pallas_call(
        _gdn_prep_kernel,
        grid=(rows // CHUNK,),
        in_specs=[pl.BlockSpec((CHUNK, 3 * GDN_DIM), lambda i: (i, 0)),
                  pl.BlockSpec((CHUNK, P_SMALL), lambda i: (i, 0)),
                  pl.BlockSpec((1, P_SMALL), lambda i: (0, 0)),
                  pl.BlockSpec((1, P_SMALL), lambda i: (0, 0))],
        out_specs=pl.BlockSpec((CHUNK, 2 * GDN_PACK), lambda i: (i, 0)),
        out_shape=jax.ShapeDtypeStruct((rows, 2 * GDN_PACK), BF16),
        compiler_params=_cp(("parallel",), 40),
        name="gdn_prep",
    )(qkv, small, alog_v, dtb_v)
    return pl.pallas_call(
        _gdn_scan_kernel,
        grid=(n_b, 2, ncc + nxc),
        in_specs=[pl.BlockSpec((CHUNK, GDN_PACK), lambda b, d, c: (blk(b, d, c), d)),
                  pl.BlockSpec((CHUNK, P_SMALL), lambda b, d, c: (blk(b, d, c), 0)),
                  pl.BlockSpec((1, P_SMALL), lambda b, d, c: (0, 0)),
                  pl.BlockSpec((1, P_SMALL), lambda b, d, c: (0, 0))],
        out_specs=pl.BlockSpec((None, CHUNK, GDN_DIM), lambda b, d, c: (d, blk(b, d, c), 0)),
        out_shape=jax.ShapeDtypeStruct((2, rows, GDN_DIM), F32),
        scratch_shapes=[pltpu.VMEM((GDN_HEAD_DIM, GDN_DIM), F32)],
        compiler_params=_cp(("parallel", "arbitrary", "arbitrary"), 40),
        name="gdn_scan",
    )(packed, small, alog_v, dtb_v)


def _mixout_kernel(h_ref, mod_ref, gpost_ref, ya_ref, ys_ref, zs_ref, sn_ref, yn_ref, og_ref, zg_ref, gn_ref,
                   w_ref, o_ref):
    ssd = (ys_ref[0] + ys_ref[1]) * _silu(zs_ref[...].astype(F32))
    yb = (ssd * _rms_scale(ssd) * sn_ref[...]).astype(BF16)
    gd = og_ref[0] + og_ref[1]
    zg = _silu(zg_ref[...].astype(F32))
    hd = GDN_HEAD_DIM
    yd = []
    for h in range(GDN_HEADS):
        oh = gd[:, h * hd:(h + 1) * hd]
        yd.append((oh * _rms_scale(oh) * gn_ref[...] * zg[:, h * hd:(h + 1) * hd]).astype(BF16))
    parts = [ya_ref[...], yb, yn_ref[...]] + yd
    widths = [512, 512, 512] + [hd] * GDN_HEADS
    y = None
    off = 0
    for part, wd in zip(parts, widths):
        t = _dot(part, w_ref[off:off + wd, :])
        y = t if y is None else y + t
        off += wd
    o_ref[...] = h_ref[...] + mod_ref[2:3, :] * (y * _rms_scale(y) * gpost_ref[...])


def _mixout_call(h, mod_l, g_post, ya, ys, proj, ssd_norm, yn, og, gdn_norm, w_out, *, n_b, lx, n_rows, tm):
    d = h.shape[1]
    nxt, per_b = n_b * lx // tm, lx // tm
    mod_idx = lambda i: (jnp.where(i < nxt, i // per_b, n_b), 0, 0)
    row = lambda i: (i, 0)
    const = lambda i: (0, 0)
    return pl.pallas_call(
        _mixout_kernel,
        grid=(n_rows // tm,),
        in_specs=[pl.BlockSpec((tm, d), row),
                  pl.BlockSpec((None, 6, d), mod_idx),
                  pl.BlockSpec((1, d), const),
                  pl.BlockSpec((tm, 512), row),
                  pl.BlockSpec((2, tm, 512), lambda i: (0, i, 0)),
                  pl.BlockSpec((tm, 512), lambda i: (i, P_ZSSD // 512)),
                  pl.BlockSpec((1, 512), const),
                  pl.BlockSpec((tm, 512), row),
                  pl.BlockSpec((2, tm, 512), lambda i: (0, i, 0)),
                  pl.BlockSpec((tm, 512), lambda i: (i, P_GZ // 512)),
                  pl.BlockSpec((1, GDN_HEAD_DIM), const),
                  pl.BlockSpec((d, d), const)],
        out_specs=pl.BlockSpec((tm, d), row),
        out_shape=jax.ShapeDtypeStruct((n_rows, d), F32),
        compiler_params=_cp(("parallel",), 56),
        name="mix_out",
    )(h, mod_l, g_post.reshape(1, d), ya, ys, proj, ssd_norm.reshape(1, 512), yn, og, proj,
      gdn_norm.reshape(1, GDN_HEAD_DIM), w_out)


def _ffn_kernel(h_ref, mod_ref, gpre_ref, gpost_ref, wg_ref, wu_ref, wd_ref, o_ref, u_scr, acc_scr):
    j = pl.program_id(1)

    @pl.when(j == 0)
    def _():
        x = h_ref[...]
        y = x * _rms_scale(x) * gpre_ref[...]
        u_scr[...] = (y * (1.0 + mod_ref[4:5, :]) + mod_ref[3:4, :]).astype(BF16)
        acc_scr[...] = jnp.zeros_like(acc_scr)

    u = u_scr[...]
    mid = (_silu(_dot(u, wg_ref[...])) * _dot(u, wu_ref[...])).astype(BF16)
    acc_scr[...] += _dot(mid, wd_ref[...])

    @pl.when(j == pl.num_programs(1) - 1)
    def _():
        y = acc_scr[...]
        o_ref[...] = h_ref[...] + mod_ref[5:6, :] * (y * _rms_scale(y) * gpost_ref[...])


def _ffn_call(h, mod_l, g_pre, g_post, wg, wu, wd, *, n_b, lx, tm, tf):
    rows, d = h.shape
    ff = wg.shape[1]
    nxt, per_b = n_b * lx // tm, lx // tm
    mod_idx = lambda i, j: (jnp.where(i < nxt, i // per_b, n_b), 0, 0)
    return pl.pallas_call(
        _ffn_kernel,
        grid=(rows // tm, ff // tf),
        in_specs=[pl.BlockSpec((tm, d), lambda i, j: (i, 0)),
                  pl.BlockSpec((None, 6, d), mod_idx),
                  pl.BlockSpec((1, d), lambda i, j: (0, 0)),
                  pl.BlockSpec((1, d), lambda i, j: (0, 0)),
                  pl.BlockSpec((d, tf), lambda i, j: (0, j)),
                  pl.BlockSpec((d, tf), lambda i, j: (0, j)),
                  pl.BlockSpec((tf, d), lambda i, j: (j, 0))],
        out_specs=pl.BlockSpec((tm, d), lambda i, j: (i, 0)),
        out_shape=jax.ShapeDtypeStruct((rows, d), F32),
        scratch_shapes=[pltpu.VMEM((tm, d), BF16), pltpu.VMEM((tm, d), F32)],
        compiler_params=_cp(("parallel", "arbitrary"), 56),
        name="ffn_swiglu",
    )(h, mod_l, g_pre.reshape(1, d), g_post.reshape(1, d), wg, wu, wd)


MOE_ALIGN = 16
MOE_GROUP_TILE = 512


def _moe_local_rows(ts):
    return -(-(TOP_K * ts + N_EXPERTS * MOE_ALIGN) // 128) * 128


def _moe_route_kernel(h_ref, mod_ref, gpre_ref, wr_ref, u_ref, dest_ref, gate_ref, cnt_ref):
    ts, lanes = dest_ref.shape
    x = h_ref[...]
    y = x * _rms_scale(x) * gpre_ref[...]
    u = y * (1.0 + mod_ref[4:5, :]) + mod_ref[3:4, :]
    u_ref[...] = u.astype(BF16)
    logits = jnp.dot(u, wr_ref[...], preferred_element_type=F32, precision=HIGHEST)
    lane = lax.broadcasted_iota(jnp.int32, logits.shape, 1).astype(F32)
    lg = jnp.where(lane < N_EXPERTS, logits, NEG)
    m1 = jnp.max(lg, axis=-1, keepdims=True)
    i1 = jnp.min(jnp.where(lg == m1, lane, float(lanes)), axis=-1, keepdims=True)
    lg2 = jnp.where(lane == i1, NEG, lg)
    m2 = jnp.max(lg2, axis=-1, keepdims=True)
    i2 = jnp.min(jnp.where(lg2 == m2, lane, float(lanes)), axis=-1, keepdims=True)
    e2 = jnp.exp(m2 - m1)
    gate_ref[...] = jnp.where(lane == i1, 1.0 / (1.0 + e2), 0.0) + jnp.where(lane == i2, e2 / (1.0 + e2), 0.0)
    sel = jnp.where(lane == i1, 1.0, 0.0) + jnp.where(lane == i2, 1.0, 0.0)
    cnt = jnp.sum(sel, axis=0, keepdims=True)
    cnt_al = jnp.floor((cnt + (MOE_ALIGN - 1)) / MOE_ALIGN) * MOE_ALIGN
    cnt_ref[...] = jnp.broadcast_to(cnt_al, cnt_ref.shape)
    before = lax.broadcasted_iota(jnp.int32, (lanes, lanes), 0) < lax.broadcasted_iota(jnp.int32, (lanes, lanes), 1)
    seg_lo = jnp.dot(jnp.broadcast_to(cnt_al, (8, lanes)), before.astype(F32),
                     preferred_element_type=F32, precision=HIGHEST)[0:1]
    earlier = lax.broadcasted_iota(jnp.int32, (ts, ts), 1) < lax.broadcasted_iota(jnp.int32, (ts, ts), 0)
    rank = _dot(earlier.astype(BF16), sel.astype(BF16))
    dest_ref[...] = jnp.where(sel > 0.0, seg_lo + rank, -1.0)


def _moe_segment_copies(i, off_ref, n_ref, hbm_ref, loc_ref, sem, *, to_hbm):
    lo = jnp.int32(0)
    for e in range(N_EXPERTS):
        n_chunks = n_ref[i * N_EXPERTS + e]
        go = off_ref[i * N_EXPERTS + e]

        def body(k, carry, lo=lo, go=go):
            loc = loc_ref.at[pl.ds(pl.multiple_of(lo + k * MOE_ALIGN, MOE_ALIGN), MOE_ALIGN), :]
            far = hbm_ref.at[pl.ds(pl.multiple_of(go + k * MOE_ALIGN, MOE_ALIGN), MOE_ALIGN), :]
            src, dst = (loc, far) if to_hbm else (far, loc)
            pltpu.make_async_copy(src, dst, sem).start()
            return carry

        lax.fori_loop(0, n_chunks, body, 0)
        lo = lo + n_chunks * MOE_ALIGN
    return lo // MOE_ALIGN


def _moe_wait_copies(n_chunks, hbm_ref, loc_ref, sem, *, to_hbm):
    def body(k, carry):
        loc = loc_ref.at[pl.ds(0, MOE_ALIGN), :]
        far = hbm_ref.at[pl.ds(0, MOE_ALIGN), :]
        src, dst = (loc, far) if to_hbm else (far, loc)
        pltpu.make_async_copy(src, dst, sem).wait()
        return carry

    lax.fori_loop(0, n_chunks, body, 0)


def _moe_sort_kernel(off_ref, n_ref, u_ref, dest_ref, init_ref, us_ref, loc_scr, sem):
    del init_ref
    i = pl.program_id(0)
    dest = dest_ref[...]
    slot = lax.broadcasted_iota(jnp.int32, (dest.shape[0], loc_scr.shape[0]), 1).astype(F32)
    onehot = jnp.zeros(slot.shape, F32)
    for e in range(N_EXPERTS):
        onehot = onehot + jnp.where(slot == dest[:, e:e + 1], 1.0, 0.0)
    loc_scr[...] = _dot_tn(onehot.astype(BF16), u_ref[...]).astype(BF16)
    n = _moe_segment_copies(i, off_ref, n_ref, us_ref, loc_scr, sem, to_hbm=True)
    _moe_wait_copies(n, us_ref, loc_scr, sem, to_hbm=True)


def _moe_expert_kernel(te_ref, tv_ref, u_ref, wg_ref, wu_ref, wd_ref, y_ref, acc_scr):
    k, j = pl.program_id(0), pl.program_id(1)
    last = pl.num_programs(1) - 1
    valid = tv_ref[k] > 0

    @pl.when(valid)
    def _():
        @pl.when(j == 0)
        def _():
            acc_scr[...] = jnp.zeros_like(acc_scr)

        u = u_ref[...]
        mid = (_silu(_dot(u, wg_ref[...])) * _dot(u, wu_ref[...])).astype(BF16)
        acc_scr[...] += _dot(mid, wd_ref[...])

        @pl.when(j == last)
        def _():
            y_ref[...] = acc_scr[...].astype(BF16)

    @pl.when(jnp.logical_not(valid) & (j == last))
    def _():
        y_ref[...] = jnp.zeros_like(y_ref)


def _moe_combine_kernel(off_ref, n_ref, h_ref, mod_ref, gpost_ref, dest_ref, gate_ref, ys_ref, o_ref, loc_scr, sem):
    i = pl.program_id(0)
    n = _moe_segment_copies(i, off_ref, n_ref, ys_ref, loc_scr, sem, to_hbm=False)
    dest, gate = dest_ref[...], gate_ref[...]
    slot = lax.broadcasted_iota(jnp.int32, (dest.shape[0], loc_scr.shape[0]), 1).astype(F32)
    w = jnp.zeros(slot.shape, F32)
    for e in range(N_EXPERTS):
        w = w + jnp.where(slot == dest[:, e:e + 1], gate[:, e:e + 1], 0.0)
    w_hi = w.astype(BF16)
    w_lo = (w - w_hi.astype(F32)).astype(BF16)
    _moe_wait_copies(n, ys_ref, loc_scr, sem, to_hbm=False)
    filled = lax.broadcasted_iota(jnp.int32, (loc_scr.shape[0], 1), 0) < n * MOE_ALIGN
    y_loc = jnp.where(filled, loc_scr[...], jnp.zeros_like(loc_scr))
    y = _dot(w_hi, y_loc) + _dot(w_lo, y_loc)
    o_ref[...] = h_ref[...] + mod_ref[5:6, :] * (y * _rms_scale(y) * gpost_ref[...])


def _moe_call(h, mod_l, g_pre, g_post, router_pad, wg, wu, wd, *, n_b, lx, tm, tf):
    rows, d = h.shape
    ne, _, fe = wg.shape
    ts, tg = tm, MOE_GROUP_TILE
    n_tiles = rows // ts
    lrows = _moe_local_rows(ts)
    nt_max = -(-(TOP_K * rows + n_tiles * ne * (MOE_ALIGN - 1)) // tg) + ne
    n_pad = nt_max * tg
    nxt, per_b = n_b * lx // ts, lx // ts
    mod_row = lambda i: jnp.where(i < nxt, i // per_b, n_b)

    u, dest, gate, cnt = pl.pallas_call(
        _moe_route_kernel,
        grid=(n_tiles,),
        in_specs=[pl.BlockSpec((ts, d), lambda i: (i, 0)),
                  pl.BlockSpec((None, 6, d), lambda i: (mod_row(i), 0, 0)),
                  pl.BlockSpec((1, d), lambda i: (0, 0)),
                  pl.BlockSpec((d, 128), lambda i: (0, 0))],
        out_specs=[pl.BlockSpec((ts, d), lambda i: (i, 0)),
                   pl.BlockSpec((ts, 128), lambda i: (i, 0)),
                   pl.BlockSpec((ts, 128), lambda i: (i, 0)),
                   pl.BlockSpec((None, 8, 128), lambda i: (i, 0, 0))],
        out_shape=[jax.ShapeDtypeStruct((rows, d), BF16),
                   jax.ShapeDtypeStruct((rows, 128), F32),
                   jax.ShapeDtypeStruct((rows, 128), F32),
                   jax.ShapeDtypeStruct((n_tiles, 8, 128), F32)],
        compiler_params=_cp(("parallel",), 40),
        name="moe_route",
    )(h, mod_l, g_pre.reshape(1, d), router_pad)

    cnt_al = cnt[:, 0, :ne].astype(jnp.int32)
    group = -(-jnp.sum(cnt_al, axis=0) // tg) * tg
    group_end = jnp.cumsum(group)
    seg_off = (group_end - group)[None, :] + jnp.cumsum(cnt_al, axis=0) - cnt_al
    n_used = group_end[-1] // tg
    tile_id = jnp.arange(nt_max, dtype=jnp.int32)
    tile_valid = (tile_id < n_used).astype(jnp.int32)
    tile_expert = jnp.searchsorted(group_end // tg, jnp.minimum(tile_id, n_used - 1), side='right')
    tile_expert = jnp.minimum(tile_expert, ne - 1).astype(jnp.int32)
    seg_off = seg_off.reshape(-1).astype(jnp.int32)
    seg_chunks = (cnt_al // MOE_ALIGN).reshape(-1)

    u_sorted = pl.pallas_call(
        _moe_sort_kernel,
        grid_spec=pltpu.PrefetchScalarGridSpec(
            num_scalar_prefetch=2,
            grid=(n_tiles,),
            in_specs=[pl.BlockSpec((ts, d), lambda i, o_r, n_r: (i, 0)),
                      pl.BlockSpec((ts, 128), lambda i, o_r, n_r: (i, 0)),
                      pl.BlockSpec(memory_space=pl.ANY)],
            out_specs=pl.BlockSpec(memory_space=pl.ANY),
            scratch_shapes=[pltpu.VMEM((lrows, d), BF16), pltpu.SemaphoreType.DMA(())]),
        out_shape=jax.ShapeDtypeStruct((n_pad, d), BF16),
        input_output_aliases={4: 0},
        compiler_params=_cp(("arbitrary",), 40),
        name="moe_sort",
    )(seg_off, seg_chunks, u, dest, jnp.zeros((n_pad, d), BF16))

    nj = fe // tf
    w_col = lambda k, j, te_r, tv_r: jnp.where(tv_r[k] > 0, j, nj - 1)
    y_sorted = pl.pallas_call(
        _moe_expert_kernel,
        grid_spec=pltpu.PrefetchScalarGridSpec(
            num_scalar_prefetch=2,
            grid=(nt_max, nj),
            in_specs=[pl.BlockSpec((tg, d), lambda k, j, te_r, tv_r: (k, 0)),
                      pl.BlockSpec((None, d, tf), lambda k, j, te_r, tv_r: (te_r[k], 0, w_col(k, j, te_r, tv_r))),
                      pl.BlockSpec((None, d, tf), lambda k, j, te_r, tv_r: (te_r[k], 0, w_col(k, j, te_r, tv_r))),
                      pl.BlockSpec((None, tf, d), lambda k, j, te_r, tv_r: (te_r[k], w_col(k, j, te_r, tv_r), 0))],
            out_specs=pl.BlockSpec((tg, d), lambda k, j, te_r, tv_r: (k, 0)),
            scratch_shapes=[pltpu.VMEM((tg, d), F32)]),
        out_shape=jax.ShapeDtypeStruct((n_pad, d), BF16),
        compiler_params=_cp(("parallel", "arbitrary"), 48),
        name="moe_experts",
    )(tile_expert, tile_valid, u_sorted, wg, wu, wd)

    return pl.pallas_call(
        _moe_combine_kernel,
        grid_spec=pltpu.PrefetchScalarGridSpec(
            num_scalar_prefetch=2,
            grid=(n_tiles,),
            in_specs=[pl.BlockSpec((ts, d), lambda i, o_r, n_r: (i, 0)),
                      pl.BlockSpec((None, 6, d), lambda i, o_r, n_r: (mod_row(i), 0, 0)),
                      pl.BlockSpec((1, d), lambda i, o_r, n_r: (0, 0)),
                      pl.BlockSpec((ts, 128), lambda i, o_r, n_r: (i, 0)),
                      pl.BlockSpec((ts, 128), lambda i, o_r, n_r: (i, 0)),
                      pl.BlockSpec(memory_space=pl.ANY)],
            out_specs=pl.BlockSpec((ts, d), lambda i, o_r, n_r: (i, 0)),
            scratch_shapes=[pltpu.VMEM((lrows, d), BF16), pltpu.SemaphoreType.DMA(())]),
        out_shape=jax.ShapeDtypeStruct((rows, d), F32),
        compiler_params=_cp(("arbitrary",), 48),
        name="moe_combine",
    )(seg_off, seg_chunks, h, mod_l, g_post.reshape(1, d), dest, gate, y_sorted)


def _rope_tables(lx, tm):
    half = MLA_ROPE // 2
    n_axis = half // 2
    inv_freq = ROPE_THETA ** (-jnp.arange(n_axis, dtype=F32) / n_axis)
    pos = jnp.arange(lx)
    rows = (pos // GRID_W).astype(F32)
    cols = (pos % GRID_W).astype(F32)
    ang = jnp.concatenate([rows[:, None] * inv_freq, cols[:, None] * inv_freq], axis=-1)
    cos, sin = jnp.cos(ang), jnp.sin(ang)
    cos_t = jnp.concatenate([cos, cos], axis=-1)
    sin_t = jnp.concatenate([-sin, sin], axis=-1)
    cos_t = jnp.concatenate([cos_t, jnp.ones((tm, MLA_ROPE), F32)], axis=0)
    sin_t = jnp.concatenate([sin_t, jnp.zeros((tm, MLA_ROPE), F32)], axis=0)
    return cos_t, sin_t


def _mla_weights(q_norm, w_uq, kv_norm, w_ukv):
    dq = MLA_NOPE + MLA_ROPE
    half = MLA_ROPE // 2
    cols = lambda a, b: w_uq[:, a:b]
    nope = [cols(h * dq, h * dq + MLA_NOPE) for h in range(MLA_HEADS)]
    rope = [cols(h * dq + MLA_NOPE, (h + 1) * dq) for h in range(MLA_HEADS)]
    rope_sw = [cols(h * dq + MLA_NOPE + s * half, h * dq + MLA_NOPE + (s + 1) * half)
               for h in range(MLA_HEADS) for s in (1, 0)]
    pad = ((0, MLA_ROPE), (0, 0))
    wq = jnp.pad(jnp.concatenate(nope + rope, axis=1), pad).astype(BF16)
    wqs = jnp.pad(jnp.concatenate(rope_sw, axis=1), pad).astype(BF16)
    qn_ext = jnp.pad(q_norm, (0, MLA_ROPE)).reshape(1, -1)
    perm = np.zeros((MLA_ROPE, MLA_ROPE), np.float32)
    perm[(np.arange(MLA_ROPE) + half) % MLA_ROPE, np.arange(MLA_ROPE)] = 1.0
    return qn_ext, wq, wqs, kv_norm.reshape(1, -1), w_ukv.astype(BF16), jnp.asarray(perm, BF16)


def _pick_tile(n, cands):
    for t in cands:
        if n % t == 0:
            return t
    raise ValueError(f"no tile for {n}")


def kernel(x, c, ctx, c_ctx, w_ada, b_ada, g_pre_mix, g_post_mix, g_pre_ffn, g_post_ffn, w_in, w_out, mla_q_norm, mla_w_uq, mla_kv_norm, mla_w_ukv, ssd_conv_w, ssd_conv_b, ssd_a_log, ssd_dt_bias, ssd_d, ssd_norm, na_rpb, gdn_conv_w, gdn_a_log, gdn_dt_bias, gdn_norm, ffn_w_gate, ffn_w_up, ffn_w_down, moe_router, moe_w_gate, moe_w_up, moe_w_down):
    n_b, lx, d = x.shape
    lc = ctx.shape[1]
    depth = w_ada.shape[0]
    rows_x, rows_c = n_b * lx, n_b * lc
    assert n_b + 1 <= 8 and lx % GRID_W == 0 and lx % lc == 0 and lc % CHUNK == 0
    tm = _pick_tile(math.gcd(lx, rows_c), (512, 256, 128))
    tm_in = _pick_tile(math.gcd(lx, rows_c), (1024, 512, 256, 128))
    tq = _pick_tile(lc, (256, 128))
    tr = _pick_tile(lc, (256, 128))

    cvec = jnp.concatenate([c, c_ctx[None, :], jnp.zeros((8 - n_b - 1, d), F32)], axis=0)
    mod = _ada_call(cvec, w_ada, b_ada).reshape(depth, 8, 6, d)
    cos_t, sin_t = _rope_tables(lx, tm)
    h_all = jnp.concatenate([x.reshape(rows_x, d), ctx.reshape(rows_c, d)], axis=0)

    for i in range(depth):
        need_ctx = i < depth - 1
        w_main, w_small = _regroup_w_in(w_in[i])
        proj, small = _inproj_call(h_all, mod[i], g_pre_mix[i], w_main, w_small, n_b=n_b, lx=lx, tm=tm_in)

        mla_w = _mla_weights(mla_q_norm[i], mla_w_uq[i], mla_kv_norm[i], mla_w_ukv[i])
        q_a, k_a, v_a = _mla_prep_call(proj, *mla_w[:5], cos_t, sin_t, mla_w[5], n_b=n_b, lx=lx, tm=tm)
        ya = _mla_attn_call(q_a, k_a, v_a, n_b=n_b, lx=lx, lc=lc, tq=tq, need_ctx=need_ctx)

        xbc = _conv_call(proj, ssd_conv_w[i], ssd_conv_b[i], col_off=P_XBC, n_b=n_b, lx=lx, lc=lc, tr=tr)
        ys = _ssd_call(xbc, small, ssd_a_log[i], ssd_dt_bias[i], ssd_d[i], n_b=n_b, lx=lx, lc=lc)

        yn = _na_call(proj, _na_bias_table(na_rpb[i], lx // GRID_W, lc), n_b=n_b, lx=lx, lc=lc, need_ctx=need_ctx)

        qkv = _conv_call(proj, gdn_conv_w[i], jnp.zeros((3 * GDN_DIM,), F32), col_off=P_GQKV,
                         n_b=n_b, lx=lx, lc=lc, tr=tr)
        og = _gdn_call(qkv, small, gdn_a_log[i], gdn_dt_bias[i], n_b=n_b, lx=lx, lc=lc)

        n_rows = rows_x + rows_c if need_ctx else rows_x
        h_mid = _mixout_call(h_all, mod[i], g_post_mix[i], ya, ys, proj, ssd_norm[i], yn, og, gdn_norm[i],
                             w_out[i].astype(BF16), n_b=n_b, lx=lx, n_rows=n_rows, tm=min(tm, 256))
        j = i // 2
        if i % 2 == 0:
            h_all = _ffn_call(h_mid, mod[i], g_pre_ffn[i], g_post_ffn[i], ffn_w_gate[j].astype(BF16),
                              ffn_w_up[j].astype(BF16), ffn_w_down[j].astype(BF16), n_b=n_b, lx=lx, tm=tm, tf=512)
        else:
            router_pad = jnp.pad(moe_router[j], ((0, 0), (0, 128 - N_EXPERTS)))
            h_all = _moe_call(h_mid, mod[i], g_pre_ffn[i], g_post_ffn[i], router_pad, moe_w_gate[j].astype(BF16),
                              moe_w_up[j].astype(BF16), moe_w_down[j].astype(BF16), n_b=n_b, lx=lx, tm=tm, tf=256)
    return h_all[:rows_x].reshape(n_b, lx, d)
```

```python
import functools
import math

import numpy as np
import jax
import jax.numpy as jnp
from jax import lax
from jax.experimental import pallas as pl
from jax.experimental.pallas import tpu as pltpu

F32 = jnp.float32
BF16 = jnp.bfloat16
HIGHEST = lax.Precision.HIGHEST

GRID_W = 64
EPS = 1e-6
ROPE_THETA = 10000.0
CHUNK = 128
CONV_W = 5
MLA_HEADS, MLA_NOPE, MLA_ROPE, MLA_V = 4, 128, 64, 128
MLA_Q_LORA, MLA_KV_LORA = 448, 128
SSD_HEADS, SSD_HEAD_DIM, SSD_STATE, SSD_GROUPS = 8, 64, 128, 2
SSD_D_INNER = SSD_HEADS * SSD_HEAD_DIM
SSD_CONV_DIM = SSD_D_INNER + 2 * SSD_GROUPS * SSD_STATE
NA_HEADS, NA_HEAD_DIM = 4, 128
NA_DIM = NA_HEADS * NA_HEAD_DIM
NA_WIN_ROWS, NA_WIN_COLS = 8, 16
GDN_HEADS, GDN_HEAD_DIM = 4, 128
GDN_DIM = GDN_HEADS * GDN_HEAD_DIM
N_EXPERTS, TOP_K = 8, 2
MLA_COLS = MLA_Q_LORA + MLA_KV_LORA + MLA_ROPE
SSD_COLS = SSD_D_INNER + SSD_CONV_DIM + 2 * SSD_HEADS
NA_COLS = 3 * NA_DIM
GDN_COLS = 4 * GDN_DIM + 4 * GDN_HEADS

P_XBC, P_ZSSD, P_NAQ, P_NAK, P_NAV = 0, 1024, 1536, 2048, 2560
P_GQKV, P_GZ, P_MLA = 3072, 4608, 5120
P_MAIN = 5760
P_SMALL = 128
NEG = -1e30
VMEM_MB = 1024 * 1024


def _cp(sem, mb):
    return pltpu.CompilerParams(dimension_semantics=sem, vmem_limit_bytes=mb * VMEM_MB)


def _dot(a, b):
    return jnp.dot(a, b, preferred_element_type=F32)


def _dot_nt(a, b, precision=None):
    return lax.dot_general(a, b, (((1,), (1,)), ((), ())), preferred_element_type=F32, precision=precision)


def _dot_tn(a, b):
    return lax.dot_general(a, b, (((0,), (0,)), ((), ())), preferred_element_type=F32)


def _sigmoid(x):
    return 1.0 / (1.0 + jnp.exp(-x))


def _silu(x):
    return x * _sigmoid(x)


def _softplus(x):
    return jnp.maximum(x, 0.0) + jnp.log(1.0 + jnp.exp(-jnp.abs(x)))


def _rms_scale(x):
    return lax.rsqrt(jnp.mean(x * x, axis=-1, keepdims=True) + EPS)


def _regroup_w_in(w):
    o_mla, o_ssd = 0, MLA_COLS
    o_na, o_gdn = o_ssd + SSD_COLS, o_ssd + SSD_COLS + NA_COLS
    main_segs = [
        (o_ssd + SSD_D_INNER, SSD_CONV_DIM),
        (o_ssd, SSD_D_INNER),
        (o_na, NA_COLS),
        (o_gdn, 4 * GDN_DIM),
        (o_mla + MLA_Q_LORA, MLA_KV_LORA),
        (o_mla, MLA_Q_LORA),
        (o_mla + MLA_Q_LORA + MLA_KV_LORA, MLA_ROPE),
    ]
    small_segs = [(o_ssd + SSD_D_INNER + SSD_CONV_DIM, 2 * SSD_HEADS), (o_gdn + 4 * GDN_DIM, 4 * GDN_HEADS)]
    assert sum(n for _, n in main_segs) == P_MAIN
    main = jnp.concatenate([w[:, a:a + n] for a, n in main_segs], axis=1).astype(BF16)
    n_small = sum(n for _, n in small_segs)
    small = jnp.concatenate([w[:, a:a + n] for a, n in small_segs]
                            + [jnp.zeros((w.shape[0], P_SMALL - n_small), w.dtype)], axis=1).astype(BF16)
    return main, small


def _ada_kernel(c_ref, w_ref, b_ref, o_ref):
    s = _silu(c_ref[...]).astype(BF16)
    o_ref[...] = _dot(s, w_ref[...].astype(BF16)) + b_ref[...]


def _ada_call(cvec, w_ada, b_ada):
    depth, d, n = w_ada.shape
    tn = 1024
    return pl.pallas_call(
        _ada_kernel,
        grid=(depth, n // tn),
        in_specs=[pl.BlockSpec((8, d), lambda l, j: (0, 0)),
                  pl.BlockSpec((None, d, tn), lambda l, j: (l, 0, j)),
                  pl.BlockSpec((None, 1, tn), lambda l, j: (l, 0, j))],
        out_specs=pl.BlockSpec((None, 8, tn), lambda l, j: (l, 0, j)),
        out_shape=jax.ShapeDtypeStruct((depth, 8, n), F32),
        compiler_params=_cp(("parallel", "parallel"), 40),
        name="adaln",
    )(cvec, w_ada, b_ada.reshape(depth, 1, n))


def _inproj_kernel(h_ref, mod_ref, g_ref, w_ref, ws_ref, o_ref, os_ref, u_scr):
    @pl.when(pl.program_id(1) == 0)
    def _():
        x = h_ref[...]
        y = x * _rms_scale(x) * g_ref[...]
        u = (y * (1.0 + mod_ref[1:2, :]) + mod_ref[0:1, :]).astype(BF16)
        u_scr[...] = u
        os_ref[...] = _dot(u, ws_ref[...])

    o_ref[...] = _dot(u_scr[...], w_ref[...]).astype(BF16)


def _inproj_call(h, mod_l, g_pre, w_main, w_small, *, n_b, lx, tm):
    rows, d = h.shape
    tn = 1920
    nxt, per_b = n_b * lx // tm, lx // tm

    def mod_idx(i, j):
        return (jnp.where(i < nxt, i // per_b, n_b), 0, 0)

    return pl.pallas_call(
        _inproj_kernel,
        grid=(rows // tm, P_MAIN // tn),
        in_specs=[pl.BlockSpec((tm, d), lambda i, j: (i, 0)),
                  pl.BlockSpec((None, 6, d), mod_idx),
                  pl.BlockSpec((1, d), lambda i, j: (0, 0)),
                  pl.BlockSpec((d, tn), lambda i, j: (0, j)),
                  pl.BlockSpec((d, P_SMALL), lambda i, j: (0, 0))],
        out_specs=[pl.BlockSpec((tm, tn), lambda i, j: (i, j)),
                   pl.BlockSpec((tm, P_SMALL), lambda i, j: (i, 0))],
        out_shape=[jax.ShapeDtypeStruct((rows, P_MAIN), BF16),
                   jax.ShapeDtypeStruct((rows, P_SMALL), F32)],
        scratch_shapes=[pltpu.VMEM((tm, d), BF16)],
        compiler_params=_cp(("parallel", "arbitrary"), 56),
        name="in_proj",
    )(h, mod_l, g_pre.reshape(1, d), w_main, w_small)


HALO = 16


def _conv_kernel(prev_ref, cur_ref, next_ref, w_ref, b_ref, o_ref, ext_scr, *, tr, blocks_x, seq_x, seq_c):
    i = pl.program_id(0)
    in_x = i < blocks_x
    pos = jnp.where(in_x, i % seq_x, (i - blocks_x) % seq_c)
    last_pos = jnp.where(in_x, seq_x - 1, seq_c - 1)
    ext_scr[0:HALO, :] = jnp.where(pos == 0, 0.0, prev_ref[...].astype(F32))
    ext_scr[HALO:HALO + tr, :] = cur_ref[...].astype(F32)
    ext_scr[HALO + tr:2 * HALO + tr, :] = jnp.where(pos == last_pos, 0.0, next_ref[...].astype(F32))
    acc = b_ref[...] + w_ref[0:1, :] * ext_scr[HALO - 2:HALO - 2 + tr, :]
    for k in range(1, CONV_W):
        acc = acc + w_ref[k:k + 1, :] * ext_scr[HALO - 2 + k:HALO - 2 + k + tr, :]
    o_ref[...] = _silu(acc).astype(BF16)


def _conv_call(proj, w, b, *, col_off, n_b, lx, lc, tr):
    rows = proj.shape[0]
    c = w.shape[1]
    cb = col_off // c
    assert cb * c == col_off
    hb = tr // HALO
    n_halo = rows // HALO
    kern = functools.partial(_conv_kernel, tr=tr, blocks_x=n_b * lx // tr, seq_x=lx // tr, seq_c=lc // tr)
    return pl.pallas_call(
        kern,
        grid=(rows // tr,),
        in_specs=[pl.BlockSpec((HALO, c), lambda i: (jnp.maximum(i * hb - 1, 0), cb)),
                  pl.BlockSpec((tr, c), lambda i: (i, cb)),
                  pl.BlockSpec((HALO, c), lambda i: (jnp.minimum((i + 1) * hb, n_halo - 1), cb)),
                  pl.BlockSpec((CONV_W, c), lambda i: (0, 0)),
                  pl.BlockSpec((1, c), lambda i: (0, 0))],
        out_specs=pl.BlockSpec((tr, c), lambda i: (i, 0)),
        out_shape=jax.ShapeDtypeStruct((rows, c), BF16),
        scratch_shapes=[pltpu.VMEM((tr + 2 * HALO, c), F32)],
        compiler_params=_cp(("parallel",), 40),
        name="dwconv_silu",
    )(proj, proj, proj, w, b.reshape(1, c))


def _mla_prep_kernel(p_ref, qn_ref, wq_ref, wqs_ref, kvn_ref, wkv_ref, cos_ref, sin_ref, perm_ref,
                     q_ref, k_ref, v_ref):
    p = p_ref[...].astype(F32)
    ckv = p[:, 0:MLA_KV_LORA]
    ce = p[:, MLA_KV_LORA:]
    lane = lax.broadcasted_iota(jnp.int32, ce.shape, 1)
    ssq = jnp.sum(jnp.where(lane < MLA_Q_LORA, ce * ce, 0.0), axis=-1, keepdims=True)
    cqn = (ce * lax.rsqrt(ssq / MLA_Q_LORA + EPS) * qn_ref[...]).astype(BF16)
    ckvn = (ckv * _rms_scale(ckv) * kvn_ref[...]).astype(BF16)
    q = _dot(cqn, wq_ref[...])
    qs = _dot(cqn, wqs_ref[...])
    kv = _dot(ckvn, wkv_ref[...])
    cos, sin = cos_ref[...], sin_ref[...]
    kr = p_ref[:, MLA_KV_LORA + MLA_Q_LORA:]
    kr_rot = kr.astype(F32) * cos + _dot(kr, perm_ref[...]) * sin
    nr = MLA_HEADS * MLA_NOPE
    scale = (MLA_NOPE + MLA_ROPE) ** -0.5
    ones_col = jnp.where(lax.broadcasted_iota(jnp.int32, (p.shape[0], MLA_V), 1) == 0, 1.0, 0.0).astype(BF16)
    for h in range(MLA_HEADS):
        q_ref[h, :, 0:MLA_NOPE] = (q[:, h * MLA_NOPE:(h + 1) * MLA_NOPE] * scale).astype(BF16)
        qr = q[:, nr + h * MLA_ROPE:nr + (h + 1) * MLA_ROPE] * cos + qs[:, h * MLA_ROPE:(h + 1) * MLA_ROPE] * sin
        q_ref[h, :, MLA_NOPE:] = (qr * scale).astype(BF16)
        hv = h * (MLA_NOPE + MLA_V)
        k_ref[h, :, 0:MLA_NOPE] = kv[:, hv:hv + MLA_NOPE].astype(BF16)
        k_ref[h, :, MLA_NOPE:] = kr_rot.astype(BF16)
        v_ref[h, :, 0:MLA_V] = kv[:, hv + MLA_NOPE:hv + MLA_NOPE + MLA_V].astype(BF16)
        v_ref[h, :, MLA_V:] = ones_col


def _mla_prep_call(proj, qn_ext, wq, wqs, kvn, wkv, cos_t, sin_t, perm, *, n_b, lx, tm):
    rows = proj.shape[0]
    nxt, per_b = n_b * lx // tm, lx // tm
    dk = MLA_NOPE + MLA_ROPE
    rope_idx = lambda i: (jnp.where(i < nxt, i % per_b, per_b), 0)
    full = lambda a: pl.BlockSpec(a.shape, lambda i: (0,) * a.ndim)
    return pl.pallas_call(
        _mla_prep_kernel,
        grid=(rows // tm,),
        in_specs=[pl.BlockSpec((tm, MLA_COLS), lambda i: (i, P_MLA // MLA_COLS)),
                  full(qn_ext), full(wq), full(wqs), full(kvn), full(wkv),
                  pl.BlockSpec((tm, MLA_ROPE), rope_idx), pl.BlockSpec((tm, MLA_ROPE), rope_idx),
                  full(perm)],
        out_specs=[pl.BlockSpec((MLA_HEADS, tm, dk), lambda i: (0, i, 0)),
                   pl.BlockSpec((MLA_HEADS, tm, dk), lambda i: (0, i, 0)),
                   pl.BlockSpec((MLA_HEADS, tm, 2 * MLA_V), lambda i: (0, i, 0))],
        out_shape=[jax.ShapeDtypeStruct((MLA_HEADS, rows, dk), BF16),
                   jax.ShapeDtypeStruct((MLA_HEADS, rows, dk), BF16),
                   jax.ShapeDtypeStruct((MLA_HEADS, rows, 2 * MLA_V), BF16)],
        compiler_params=_cp(("parallel",), 40),
        name="mla_prep",
    )(proj, qn_ext, wq, wqs, kvn, wkv, cos_t, sin_t, perm)


def _softmax_pv(scores, values):
    m = functools.reduce(jnp.maximum, [jnp.max(s, axis=-1, keepdims=True) for s in scores])
    ps = [jnp.exp(s - m) for s in scores]
    den = functools.reduce(lambda a, b: a + b, [jnp.sum(p, axis=-1, keepdims=True) for p in ps])
    num = functools.reduce(lambda a, b: a + b, [_dot(p.astype(BF16), v) for p, v in zip(ps, values)])
    return num / den


def _softmax_pv_aug(scores, values_aug):
    m = functools.reduce(jnp.maximum, [jnp.max(s, axis=-1, keepdims=True) for s in scores])
    acc = functools.reduce(lambda a, b: a + b,
                           [_dot(jnp.exp((s - m).astype(BF16)), v) for s, v in zip(scores, values_aug)])
    return acc[:, 0:MLA_V] / acc[:, MLA_V:MLA_V + 1]


MLA_HEADS_PER_STEP = 4


def _mla_attn_kernel(q_ref, kx_ref, vx_ref, kc_ref, vc_ref, o_ref, *, nqx):
    qi = pl.program_id(2)
    heads = range(q_ref.shape[0])

    @pl.when(qi < nqx)
    def _():
        outs = [_softmax_pv_aug([_dot_nt(q_ref[h], kx_ref[h]), _dot_nt(q_ref[h], kc_ref[h])], [vx_ref[h], vc_ref[h]])
                for h in heads]
        o_ref[...] = jnp.concatenate(outs, axis=1).astype(BF16)

    @pl.when(qi >= nqx)
    def _():
        outs = [_softmax_pv_aug([_dot_nt(q_ref[h], kc_ref[h])], [vc_ref[h]]) for h in heads]
        o_ref[...] = jnp.concatenate(outs, axis=1).astype(BF16)


def _mla_attn_call(q, k, v, *, n_b, lx, lc, tq, need_ctx):
    rows = q.shape[1]
    dk = q.shape[2]
    nqx, nqc = lx // tq, lc // tq
    nq = nqx + (nqc if need_ctx else 0)
    nbx = n_b * lx // lc

    def q_row(b, qi):
        return jnp.where(qi < nqx, b * nqx + qi, n_b * nqx + b * nqc + (qi - nqx))

    kern = functools.partial(_mla_attn_kernel, nqx=nqx)
    out_rows = rows if need_ctx else n_b * lx
    hp = MLA_HEADS_PER_STEP
    return pl.pallas_call(
        kern,
        grid=(n_b, MLA_HEADS // hp, nq),
        in_specs=[pl.BlockSpec((hp, tq, dk), lambda b, h, qi: (h, q_row(b, qi), 0)),
                  pl.BlockSpec((hp, lx, dk), lambda b, h, qi: (h, b, 0)),
                  pl.BlockSpec((hp, lx, 2 * MLA_V), lambda b, h, qi: (h, b, 0)),
                  pl.BlockSpec((hp, lc, dk), lambda b, h, qi: (h, nbx + b, 0)),
                  pl.BlockSpec((hp, lc, 2 * MLA_V), lambda b, h, qi: (h, nbx + b, 0))],
        out_specs=pl.BlockSpec((tq, hp * MLA_V), lambda b, h, qi: (q_row(b, qi), h)),
        out_shape=jax.ShapeDtypeStruct((out_rows, MLA_HEADS * MLA_V), BF16),
        compiler_params=_cp(("parallel", "parallel", "arbitrary"), 48),
        name="mla_attn",
    )(q, k, v, k, v)


def _na_plan(g_rows, lc):
    wr = min(NA_WIN_ROWS, g_rows)
    rg = next(r for r in (4, 2, 1) if g_rows % r == 0 and lc % (r * GRID_W) == 0)
    wk = min(rg + wr - 1, g_rows)
    n_groups = g_rows // rg
    ks = np.clip(np.arange(n_groups) * rg - wr // 2, 0, g_rows - wk)
    r = np.arange(g_rows)
    rs = np.clip(r - wr // 2, 0, g_rows - wr)
    q_off = (r - np.repeat(ks, rg)).reshape(n_groups, rg)
    rel = (rs - np.repeat(ks, rg)).reshape(n_groups, rg)
    assert (rel >= 0).all() and (rel + wr <= wk).all()
    pats = [tuple(q_off[g]) + tuple(rel[g]) for g in range(n_groups)]
    uniq = sorted(set(pats))
    var = np.array([uniq.index(p) for p in pats], np.int32)
    q_off_v = np.array([p[:rg] for p in uniq])
    rel_v = np.array([p[rg:] for p in uniq])
    return wr, rg, wk, ks.astype(np.int32), var, q_off_v, rel_v


def _na_bias_table(rpb, g_rows, lc):
    wr, rg, wk, _, _, q_off_v, rel_v = _na_plan(g_rows, lc)
    col_start = np.clip(np.arange(GRID_W) - NA_WIN_COLS // 2, 0, GRID_W - NA_WIN_COLS)
    cc = np.arange(GRID_W)
    col_ok = (cc[None, :] >= col_start[:, None]) & (cc[None, :] < col_start[:, None] + NA_WIN_COLS)
    dc = np.clip(cc[None, :] - cc[:, None] + NA_WIN_COLS - 1, 0, 2 * NA_WIN_COLS - 2)
    n_dc = 2 * NA_WIN_COLS - 1
    onehot = (dc.reshape(-1)[:, None] == np.arange(n_dc)[None, :]).astype(np.float32)
    g = jnp.einsum('hab,yb->hay', rpb.astype(F32), jnp.asarray(onehot), precision=HIGHEST)
    g = jnp.where(col_ok[None, None], g.reshape(rpb.shape[0], -1, GRID_W, GRID_W), NEG)
    neg_blk = jnp.full((rpb.shape[0], GRID_W, GRID_W), NEG, F32)
    tabs = []
    for v in range(q_off_v.shape[0]):
        rows_v = []
        for j in range(rg):
            blks = []
            for w in range(wk):
                ok = rel_v[v, j] <= w < rel_v[v, j] + wr
                blks.append(g[:, w - q_off_v[v, j] + NA_WIN_ROWS - 1] if ok else neg_blk)
            rows_v.append(jnp.concatenate(blks, axis=-1))
        tabs.append(jnp.concatenate(rows_v, axis=-2))
    return jnp.stack(tabs, axis=0)


def _na_kernel(var_ref, ks_ref, q_ref, kx_ref, vx_ref, kc_ref, vc_ref, bias_ref, o_ref, *, n_groups, wk, scale):
    g = pl.program_id(1)
    hd = NA_HEAD_DIM
    cols = [slice(h * hd, (h + 1) * hd) for h in range(NA_HEADS)]

    @pl.when(g < n_groups)
    def _():
        start = pl.multiple_of(ks_ref[g] * GRID_W, GRID_W)
        rows = pl.ds(start, wk * GRID_W)
        outs = []
        for h, c in enumerate(cols):
            q = q_ref[:, c]
            sl = _dot_nt(q, kx_ref[rows, c]) * scale + bias_ref[h]
            sc = _dot_nt(q, kc_ref[:, c]) * scale
            outs.append(_softmax_pv([sl, sc], [vx_ref[rows, c], vc_ref[:, c]]))
        o_ref[...] = jnp.concatenate(outs, axis=1).astype(BF16)

    @pl.when(g >= n_groups)
    def _():
        outs = [_softmax_pv([_dot_nt(q_ref[:, c], kc_ref[:, c]) * scale], [vc_ref[:, c]]) for c in cols]
        o_ref[...] = jnp.concatenate(outs, axis=1).astype(BF16)


def _na_call(proj, bias_tab, *, n_b, lx, lc, need_ctx):
    rows = proj.shape[0]
    g_rows = lx // GRID_W
    _, rg, wk, ks, var, _, _ = _na_plan(g_rows, lc)
    n_groups = g_rows // rg
    tq = rg * GRID_W
    nqc = lc // tq
    nq = n_groups + (nqc if need_ctx else 0)
    nbx = n_b * lx // lc
    hd, nd = NA_HEAD_DIM, NA_DIM
    cq, ck, cv = P_NAQ // nd, P_NAK // nd, P_NAV // nd

    def q_row(b, g):
        return jnp.where(g < n_groups, b * n_groups + g, n_b * n_groups + b * nqc + (g - n_groups))

    kern = functools.partial(_na_kernel, n_groups=n_groups, wk=wk, scale=hd ** -0.5)
    grid_spec = pltpu.PrefetchScalarGridSpec(
        num_scalar_prefetch=2,
        grid=(n_b, nq),
        in_specs=[pl.BlockSpec((tq, nd), lambda b, g, var_r, ks_r: (q_row(b, g), cq)),
                  pl.BlockSpec((lx, nd), lambda b, g, var_r, ks_r: (b, ck)),
                  pl.BlockSpec((lx, nd), lambda b, g, var_r, ks_r: (b, cv)),
                  pl.BlockSpec((lc, nd), lambda b, g, var_r, ks_r: (nbx + b, ck)),
                  pl.BlockSpec((lc, nd), lambda b, g, var_r, ks_r: (nbx + b, cv)),
                  pl.BlockSpec((None, NA_HEADS, tq, wk * GRID_W),
                               lambda b, g, var_r, ks_r: (var_r[jnp.minimum(g, n_groups - 1)], 0, 0, 0))],
        out_specs=pl.BlockSpec((tq, nd), lambda b, g, var_r, ks_r: (q_row(b, g), 0)),
    )
    return pl.pallas_call(
        kern,
        grid_spec=grid_spec,
        out_shape=jax.ShapeDtypeStruct((rows if need_ctx else n_b * lx, NA_DIM), BF16),
        compiler_params=_cp(("parallel", "arbitrary"), 40),
        name="na_attn",
    )(jnp.asarray(var), jnp.asarray(ks), proj, proj, proj, proj, proj, bias_tab)


def _chunk_block(n_b, nxc, ncc):
    def f(b, d, c):
        cc = jnp.where(d == 0, c, ncc - 1 - c)
        cx = jnp.where(d == 0, c - ncc, nxc - 1 - (c - ncc))
        return jnp.where(c < ncc, n_b * nxc + b * ncc + cc, b * nxc + cx)
    return f


def _dir_masks(d):
    row = lax.broadcasted_iota(jnp.int32, (CHUNK, CHUNK), 0)
    col = lax.broadcasted_iota(jnp.int32, (CHUNK, CHUNK), 1)
    diff = (row - col) * jnp.where(d == 0, 1, -1)
    return diff >= 0, diff > 0


def _cumsum_lanes(x, incl):
    cs = jnp.dot(incl.astype(F32), x, preferred_element_type=F32, precision=HIGHEST)
    return cs, cs.T


def _lane_vec(vals, offset):
    flat = vals.reshape(-1).astype(F32)
    return jnp.pad(flat, (offset, P_SMALL - offset - flat.shape[0])).reshape(1, P_SMALL)


def _ssd_prep_kernel(xbc_ref, sm_ref, alog_ref, dtb_ref, dsk_ref, yp_ref, xw_ref, ea_ref):
    nh, hp, ns = SSD_HEADS, SSD_HEAD_DIM, SSD_STATE
    gh = nh // SSD_GROUPS
    dt2 = _softplus(sm_ref[...] + dtb_ref[...])
    dta2 = dt2 * (-jnp.exp(alog_ref[...]))
    row = lax.broadcasted_iota(jnp.int32, (CHUNK, CHUNK), 0)
    col = lax.broadcasted_iota(jnp.int32, (CHUNK, CHUNK), 1)
    incl = [row >= col, row <= col]
    cs = [jnp.dot(m.astype(F32), dta2, preferred_element_type=F32, precision=HIGHEST) for m in incl]
    cs_t = [x.T for x in cs]
    tot2 = jnp.sum(dta2, axis=0, keepdims=True)
    dsk = dsk_ref[...]
    bo, co = SSD_D_INNER, SSD_D_INNER + SSD_GROUPS * ns
    scores = [_dot_nt(xbc_ref[:, co + g * ns:co + (g + 1) * ns], xbc_ref[:, bo + g * ns:bo + (g + 1) * ns])
              for g in range(SSD_GROUPS)]
    xs = [xbc_ref[:, h * hp:(h + 1) * hp].astype(F32) for h in range(nh)]
    yp, xw = [], []
    for d in range(2):
        for h in range(nh):
            ln = d * nh + h
            a_c, a_r = cs[d][:, ln:ln + 1], cs_t[d][ln:ln + 1, :]
            dec = jnp.where(incl[d], jnp.exp(jnp.where(incl[d], a_c - a_r, 0.0)), 0.0)
            m = (scores[h // gh] * dec).astype(BF16)
            y = _dot(m, (xs[h] * dt2[:, ln:ln + 1]).astype(BF16))
            yp.append(y + dsk[:, h:h + 1] * xs[h] if d == 0 else y)
            xw.append((xs[h] * (jnp.exp(tot2[:, ln:ln + 1] - a_c) * dt2[:, ln:ln + 1])).astype(BF16))
    yp_ref[...] = jnp.concatenate(yp, axis=1)
    xw_ref[...] = jnp.concatenate(xw, axis=1)
    ea_ref[...] = jnp.concatenate([jnp.exp(cs[0]), jnp.exp(cs[1])], axis=1)


def _ssd_scan_kernel(*refs):
    (bc0, yp0, xw0, ea0, sm0, bc1, yp1, xw1, ea1, sm1, alog_ref, dtb_ref, y0_ref, y1_ref, s_scr) = refs
    nh, hp, ns = SSD_HEADS, SSD_HEAD_DIM, SSD_STATE
    gh = nh // SSD_GROUPS

    @pl.when(pl.program_id(1) == 0)
    def _():
        s_scr[...] = jnp.zeros_like(s_scr)

    neg_a = -jnp.exp(alog_ref[...])
    for d, (bc, yp, xw, ea, sm, y_ref) in enumerate([(bc0, yp0, xw0, ea0, sm0, y0_ref),
                                                      (bc1, yp1, xw1, ea1, sm1, y1_ref)]):
        c_dec = jnp.exp(jnp.sum(_softplus(sm[...] + dtb_ref[...]) * neg_a, axis=0, keepdims=True))
        e_acum = ea[...]
        ys, states = [], []
        for g in range(SSD_GROUPS):
            bg = bc[:, g * ns:(g + 1) * ns]
            cg = bc[:, SSD_GROUPS * ns + g * ns:SSD_GROUPS * ns + (g + 1) * ns]
            s_g = s_scr[d, :, g * gh * hp:(g + 1) * gh * hp]
            y_int = _dot(cg, s_g.astype(BF16))
            upd = _dot_tn(bg, xw[:, g * gh * hp:(g + 1) * gh * hp])
            for hh in range(gh):
                h = g * gh + hh
                ln = d * nh + h
                sl = slice(hh * hp, (hh + 1) * hp)
                ys.append(yp[:, h * hp:(h + 1) * hp] + y_int[:, sl] * e_acum[:, ln:ln + 1])
                states.append(s_g[:, sl] * c_dec[:, ln:ln + 1] + upd[:, sl])
        y_ref[...] = jnp.concatenate(ys, axis=1)
        s_scr[d] = jnp.concatenate(states, axis=1)


def _ssd_call(xbc, small, a_log, dt_bias, d_skip, *, n_b, lx, lc):
    rows = xbc.shape[0]
    nxc, ncc = lx // CHUNK, lc // CHUNK
    blk = _chunk_block(n_b, nxc, ncc)
    nh, di = SSD_HEADS, SSD_D_INNER
    alog_v, dtb_v = _lane_vec(a_log, 0), _lane_vec(dt_bias, 0)
    ypart, xw, ea = pl.pallas_call(
        _ssd_prep_kernel,
        grid=(rows // CHUNK,),
        in_specs=[pl.BlockSpec((CHUNK, SSD_CONV_DIM), lambda i: (i, 0)),
                  pl.BlockSpec((CHUNK, P_SMALL), lambda i: (i, 0)),
                  pl.BlockSpec((1, P_SMALL), lambda i: (0, 0)),
                  pl.BlockSpec((1, P_SMALL), lambda i: (0, 0)),
                  pl.BlockSpec((1, nh), lambda i: (0, 0))],
        out_specs=[pl.BlockSpec((CHUNK, 2 * di), lambda i: (i, 0)),
                   pl.BlockSpec((CHUNK, 2 * di), lambda i: (i, 0)),
                   pl.BlockSpec((CHUNK, 2 * P_SMALL), lambda i: (i, 0))],
        out_shape=[jax.ShapeDtypeStruct((rows, 2 * di), F32),
                   jax.ShapeDtypeStruct((rows, 2 * di), BF16),
                   jax.ShapeDtypeStruct((rows, 2 * P_SMALL), F32)],
        compiler_params=_cp(("parallel",), 40),
        name="ssd_prep",
    )(xbc, small, alog_v, dtb_v, d_skip.reshape(1, nh))

    def dir_specs(d):
        at = lambda b, c: blk(b, d, c)
        return [pl.BlockSpec((CHUNK, SSD_CONV_DIM - di), lambda b, c: (at(b, c), 1)),
                pl.BlockSpec((CHUNK, di), lambda b, c: (at(b, c), d)),
                pl.BlockSpec((CHUNK, di), lambda b, c: (at(b, c), d)),
                pl.BlockSpec((CHUNK, P_SMALL), lambda b, c: (at(b, c), d)),
                pl.BlockSpec((CHUNK, P_SMALL), lambda b, c: (at(b, c), 0))]

    const = pl.BlockSpec((1, P_SMALL), lambda b, c: (0, 0))
    return pl.pallas_call(
        _ssd_scan_kernel,
        grid=(n_b, ncc + nxc),
        in_specs=dir_specs(0) + dir_specs(1) + [const, const],
        out_specs=[pl.BlockSpec((CHUNK, di), lambda b, c: (blk(b, 0, c), 0)),
                   pl.BlockSpec((CHUNK, di), lambda b, c: (blk(b, 1, c), 0))],
        out_shape=[jax.ShapeDtypeStruct((rows, di), F32), jax.ShapeDtypeStruct((rows, di), F32)],
        scratch_shapes=[pltpu.VMEM((2, SSD_STATE, di), F32)],
        compiler_params=_cp(("parallel", "arbitrary"), 40),
        name="ssd_scan",
    )(xbc, ypart, xw, ea, small, xbc, ypart, xw, ea, small, alog_v, dtb_v)


SOLVE_BLOCK = 16


def _unit_tri_solve_many(n_mats, rhss):
    ln = n_mats[0].shape[0]
    row = lax.broadcasted_iota(jnp.int32, (ln, ln), 0)
    col = lax.broadcasted_iota(jnp.int32, (ln, ln), 1)
    on_diag_block = (row // SOLVE_BLOCK) == (col // SOLVE_BLOCK)
    eye = jnp.where(row == col, 1.0, 0.0)
    mm = lambda a, b: _dot(a.astype(BF16), b.astype(BF16))
    ms = [jnp.where(on_diag_block, -n, 0.0) for n in n_mats]
    es = [jnp.where(on_diag_block, 0.0, n) for n in n_mats]
    ps = [eye + m for m in ms]
    mps = ms
    k = 1
    while 2 * k < SOLVE_BLOCK:
        mps = [mm(x, x) for x in mps]
        ps = [p + mm(p, x) for p, x in zip(ps, mps)]
        k *= 2
    f_pows = [[-mm(p, e) for p, e in zip(ps, es)]]
    ys = [mm(p, r) for p, r in zip(ps, rhss)]
    k = 1
    while 2 * k < ln // SOLVE_BLOCK:
        f_pows.append([mm(f, f) for f in f_pows[-1]])
        k *= 2
    for fl in reversed(f_pows):
        ys = [y + mm(f, y) for f, y in zip(fl, ys)]
    return ys


GDN_PACK = 5 * GDN_DIM
GDN_G_LANE = 2 * SSD_HEADS
GDN_B_LANE = 2 * SSD_HEADS + 2 * GDN_HEADS


def _gdn_prep_kernel(qkv_ref, sm_ref, alog_ref, dtb_ref, o_ref):
    nh, hd = GDN_HEADS, GDN_HEAD_DIM
    sm = sm_ref[...]
    g2 = -jnp.exp(alog_ref[...]) * _softplus(sm + dtb_ref[...])
    beta2 = _sigmoid(sm)
    row = lax.broadcasted_iota(jnp.int32, (CHUNK, CHUNK), 0)
    col = lax.broadcasted_iota(jnp.int32, (CHUNK, CHUNK), 1)
    incl = [row >= col, row <= col]
    strict = [row > col, row < col]
    cs = [jnp.dot(m.astype(F32), g2, preferred_element_type=F32, precision=HIGHEST) for m in incl]
    cs_t = [x.T for x in cs]
    gtot2 = jnp.sum(g2, axis=0, keepdims=True)
    qn, kn, kn_b, vv, qk_raw = [], [], [], [], []
    for h in range(nh):
        qh = qkv_ref[:, h * hd:(h + 1) * hd].astype(F32)
        kh = qkv_ref[:, GDN_DIM + h * hd:GDN_DIM + (h + 1) * hd].astype(F32)
        vv.append(qkv_ref[:, 2 * GDN_DIM + h * hd:2 * GDN_DIM + (h + 1) * hd].astype(F32))
        qn.append(qh * (lax.rsqrt(jnp.sum(qh * qh, axis=-1, keepdims=True) + EPS) * hd ** -0.5))
        kn.append(kh * lax.rsqrt(jnp.sum(kh * kh, axis=-1, keepdims=True) + EPS))
        kn_b.append(kn[h].astype(BF16))
        qk_raw.append(_dot_nt(qn[h].astype(BF16), kn_b[h]))
    n_mats, rhss, qks, qds, kds = [], [], [], [], []
    for d in range(2):
        for h in range(nh):
            lg, lb = GDN_G_LANE + d * nh + h, GDN_B_LANE + d * nh + h
            gcc, gcr = cs[d][:, lg:lg + 1], cs_t[d][lg:lg + 1, :]
            beta, gtot = beta2[:, lb:lb + 1], gtot2[:, lg:lg + 1]
            dec = jnp.where(incl[d], jnp.exp(jnp.where(incl[d], gcc - gcr, 0.0)), 0.0)
            kb = kn[h] * beta
            n_mats.append(jnp.where(strict[d], _dot_nt(kb.astype(BF16), kn_b[h]) * dec, 0.0))
            e_gc = jnp.exp(gcc)
            rhss.append(jnp.concatenate([vv[h] * beta, kb * e_gc], axis=1))
            qks.append(qk_raw[h] * dec)
            qds.append(qn[h] * e_gc)
            kds.append(kn[h] * jnp.exp(gtot - gcc))
    sols = _unit_tri_solve_many(n_mats, rhss)
    pieces = []
    for d in range(2):
        js = range(d * nh, (d + 1) * nh)
        pieces += [sols[j][:, 0:hd] for j in js] + [sols[j][:, hd:2 * hd] for j in js]
        pieces += [qks[j] for j in js] + [qds[j] for j in js] + [kds[j] for j in js]
    o_ref[...] = jnp.concatenate([p.astype(BF16) for p in pieces], axis=1)


def _gdn_scan_kernel(pk0, sm0, pk1, sm1, alog_ref, dtb_ref, o0_ref, o1_ref, s_scr):
    nh, hd = GDN_HEADS, GDN_HEAD_DIM

    @pl.when(pl.program_id(1) == 0)
    def _():
        s_scr[...] = jnp.zeros_like(s_scr)

    neg_a = -jnp.exp(alog_ref[...])
    for d, (pk_ref, sm_ref, o_ref) in enumerate([(pk0, sm0, o0_ref), (pk1, sm1, o1_ref)]):
        g_end2 = jnp.exp(jnp.sum(neg_a * _softplus(sm_ref[...] + dtb_ref[...]), axis=0, keepdims=True))
        outs, states = [], []
        for h in range(nh):
            lg = GDN_G_LANE + d * nh + h
            part = lambda j: pk_ref[:, (j * nh + h) * hd:(j * nh + h + 1) * hd]
            u, w, qk, qd, kd = part(0), part(1), part(2), part(3), part(4)
            s_h = s_scr[d, :, h * hd:(h + 1) * hd]
            s_b = s_h.astype(BF16)
            v_new = (u.astype(F32) - _dot(w, s_b)).astype(BF16)
            outs.append(_dot(qd, s_b) + _dot(qk, v_new))
            states.append(s_h * g_end2[:, lg:lg + 1] + _dot_tn(kd, v_new))
        o_ref[...] = jnp.concatenate(outs, axis=1)
        s_scr[d] = jnp.concatenate(states, axis=1)


def _gdn_call(qkv, small, a_log, dt_bias, *, n_b, lx, lc):
    rows = qkv.shape[0]
    nxc, ncc = lx // CHUNK, lc // CHUNK
    blk = _chunk_block(n_b, nxc, ncc)
    alog_v, dtb_v = _lane_vec(a_log, GDN_G_LANE), _lane_vec(dt_bias, GDN_G_LANE)
    packed = pl.pallas_call(
        _gdn_prep_kernel,
        grid=(rows // CHUNK,),
        in_specs=[pl.BlockSpec((CHUNK, 3 * GDN_DIM), lambda i: (i, 0)),
                  pl.BlockSpec((CHUNK, P_SMALL), lambda i: (i, 0)),
                  pl.BlockSpec((1, P_SMALL), lambda i: (0, 0)),
                  pl.BlockSpec((1, P_SMALL), lambda i: (0, 0))],
        out_specs=pl.BlockSpec((CHUNK, 2 * GDN_PACK), lambda i: (i, 0)),
        out_shape=jax.ShapeDtypeStruct((rows, 2 * GDN_PACK), BF16),
        compiler_params=_cp(("parallel",), 40),
        name="gdn_prep",
    )(qkv, small, alog_v, dtb_v)
    def dir_specs(d):
        return [pl.BlockSpec((CHUNK, GDN_PACK), lambda b, c: (blk(b, d, c), d)),
                pl.BlockSpec((CHUNK, P_SMALL), lambda b, c: (blk(b, d, c), 0))]

    const = pl.BlockSpec((1, P_SMALL), lambda b, c: (0, 0))
    return pl.pallas_call(
        _gdn_scan_kernel,
        grid=(n_b, ncc + nxc),
        in_specs=dir_specs(0) + dir_specs(1) + [const, const],
        out_specs=[pl.BlockSpec((CHUNK, GDN_DIM), lambda b, c: (blk(b, 0, c), 0)),
                   pl.BlockSpec((CHUNK, GDN_DIM), lambda b, c: (blk(b, 1, c), 0))],
        out_shape=[jax.ShapeDtypeStruct((rows, GDN_DIM), F32), jax.ShapeDtypeStruct((rows, GDN_DIM), F32)],
        scratch_shapes=[pltpu.VMEM((2, GDN_HEAD_DIM, GDN_DIM), F32)],
        compiler_params=_cp(("parallel", "arbitrary"), 40),
        name="gdn_scan",
    )(packed, small, packed, small, alog_v, dtb_v)


def _mixout_kernel(h_ref, mod_ref, gpost_ref, ya_ref, ys0_ref, ys1_ref, zs_ref, sn_ref, yn_ref, og0_ref, og1_ref,
                   zg_ref, gn_ref, w_ref, o_ref):
    ssd = (ys0_ref[...] + ys1_ref[...]) * _silu(zs_ref[...].astype(F32))
    yb = (ssd * _rms_scale(ssd) * sn_ref[...]).astype(BF16)
    gd = og0_ref[...] + og1_ref[...]
    zg = _silu(zg_ref[...].astype(F32))
    hd = GDN_HEAD_DIM
    yd = []
    for h in range(GDN_HEADS):
        oh = gd[:, h * hd:(h + 1) * hd]
        yd.append((oh * _rms_scale(oh) * gn_ref[...] * zg[:, h * hd:(h + 1) * hd]).astype(BF16))
    parts = [ya_ref[...], yb, yn_ref[...]] + yd
    widths = [512, 512, 512] + [hd] * GDN_HEADS
    y = None
    off = 0
    for part, wd in zip(parts, widths):
        t = _dot(part, w_ref[off:off + wd, :])
        y = t if y is None else y + t
        off += wd
    o_ref[...] = h_ref[...] + mod_ref[2:3, :] * (y * _rms_scale(y) * gpost_ref[...])


def _mixout_call(h, mod_l, g_post, ya, ys, proj, ssd_norm, yn, og, gdn_norm, w_out, *, n_b, lx, n_rows, tm):
    d = h.shape[1]
    nxt, per_b = n_b * lx // tm, lx // tm
    mod_idx = lambda i: (jnp.where(i < nxt, i // per_b, n_b), 0, 0)
    row = lambda i: (i, 0)
    const = lambda i: (0, 0)
    return pl.pallas_call(
        _mixout_kernel,
        grid=(n_rows // tm,),
        in_specs=[pl.BlockSpec((tm, d), row),
                  pl.BlockSpec((None, 6, d), mod_idx),
                  pl.BlockSpec((1, d), const),
                  pl.BlockSpec((tm, 512), row),
                  pl.BlockSpec((tm, 512), row),
                  pl.BlockSpec((tm, 512), row),
                  pl.BlockSpec((tm, 512), lambda i: (i, P_ZSSD // 512)),
                  pl.BlockSpec((1, 512), const),
                  pl.BlockSpec((tm, 512), row),
                  pl.BlockSpec((tm, 512), row),
                  pl.BlockSpec((tm, 512), row),
                  pl.BlockSpec((tm, 512), lambda i: (i, P_GZ // 512)),
                  pl.BlockSpec((1, GDN_HEAD_DIM), const),
                  pl.BlockSpec((d, d), const)],
        out_specs=pl.BlockSpec((tm, d), row),
        out_shape=jax.ShapeDtypeStruct((n_rows, d), F32),
        compiler_params=_cp(("parallel",), 56),
        name="mix_out",
    )(h, mod_l, g_post.reshape(1, d), ya, ys[0], ys[1], proj, ssd_norm.reshape(1, 512), yn, og[0], og[1], proj,
      gdn_norm.reshape(1, GDN_HEAD_DIM), w_out)


def _ffn_kernel(h_ref, mod_ref, gpre_ref, gpost_ref, wg_ref, wu_ref, wd_ref, o_ref, u_scr, acc_scr):
    j = pl.program_id(1)

    @pl.when(j == 0)
    def _():
        x = h_ref[...]
        y = x * _rms_scale(x) * gpre_ref[...]
        u_scr[...] = (y * (1.0 + mod_ref[4:5, :]) + mod_ref[3:4, :]).astype(BF16)
        acc_scr[...] = jnp.zeros_like(acc_scr)

    u = u_scr[...]
    mid = (_silu(_dot(u, wg_ref[...])) * _dot(u, wu_ref[...])).astype(BF16)
    acc_scr[...] += _dot(mid, wd_ref[...])

    @pl.when(j == pl.num_programs(1) - 1)
    def _():
        y = acc_scr[...]
        o_ref[...] = h_ref[...] + mod_ref[5:6, :] * (y * _rms_scale(y) * gpost_ref[...])


def _ffn_call(h, mod_l, g_pre, g_post, wg, wu, wd, *, n_b, lx, tm, tf):
    rows, d = h.shape
    ff = wg.shape[1]
    nxt, per_b = n_b * lx // tm, lx // tm
    mod_idx = lambda i, j: (jnp.where(i < nxt, i // per_b, n_b), 0, 0)
    return pl.pallas_call(
        _ffn_kernel,
        grid=(rows // tm, ff // tf),
        in_specs=[pl.BlockSpec((tm, d), lambda i, j: (i, 0)),
                  pl.BlockSpec((None, 6, d), mod_idx),
                  pl.BlockSpec((1, d), lambda i, j: (0, 0)),
                  pl.BlockSpec((1, d), lambda i, j: (0, 0)),
                  pl.BlockSpec((d, tf), lambda i, j: (0, j)),
                  pl.BlockSpec((d, tf), lambda i, j: (0, j)),
                  pl.BlockSpec((tf, d), lambda i, j: (j, 0))],
        out_specs=pl.BlockSpec((tm, d), lambda i, j: (i, 0)),
        out_shape=jax.ShapeDtypeStruct((rows, d), F32),
        scratch_shapes=[pltpu.VMEM((tm, d), BF16), pltpu.VMEM((tm, d), F32)],
        compiler_params=_cp(("parallel", "arbitrary"), 56),
        name="ffn_swiglu",
    )(h, mod_l, g_pre.reshape(1, d), g_post.reshape(1, d), wg, wu, wd)


MOE_ALIGN = 16
MOE_GROUP_TILE = 512


def _moe_local_rows(ts):
    return -(-(TOP_K * ts + N_EXPERTS * MOE_ALIGN) // 128) * 128


def _moe_route_kernel(h_ref, mod_ref, gpre_ref, wr_ref, u_ref, dest_ref, gate_ref, cnt_ref):
    ts, lanes = dest_ref.shape
    x = h_ref[...]
    y = x * _rms_scale(x) * gpre_ref[...]
    u = y * (1.0 + mod_ref[4:5, :]) + mod_ref[3:4, :]
    u_ref[...] = u.astype(BF16)
    logits = jnp.dot(u, wr_ref[...], preferred_element_type=F32, precision=HIGHEST)
    lane = lax.broadcasted_iota(jnp.int32, logits.shape, 1).astype(F32)
    lg = jnp.where(lane < N_EXPERTS, logits, NEG)
    m1 = jnp.max(lg, axis=-1, keepdims=True)
    i1 = jnp.min(jnp.where(lg == m1, lane, float(lanes)), axis=-1, keepdims=True)
    lg2 = jnp.where(lane == i1, NEG, lg)
    m2 = jnp.max(lg2, axis=-1, keepdims=True)
    i2 = jnp.min(jnp.where(lg2 == m2, lane, float(lanes)), axis=-1, keepdims=True)
    e2 = jnp.exp(m2 - m1)
    gate_ref[...] = jnp.where(lane == i1, 1.0 / (1.0 + e2), 0.0) + jnp.where(lane == i2, e2 / (1.0 + e2), 0.0)
    sel = jnp.where(lane == i1, 1.0, 0.0) + jnp.where(lane == i2, 1.0, 0.0)
    cnt = jnp.sum(sel, axis=0, keepdims=True)
    cnt_al = jnp.floor((cnt + (MOE_ALIGN - 1)) / MOE_ALIGN) * MOE_ALIGN
    cnt_ref[...] = jnp.broadcast_to(cnt_al, cnt_ref.shape)
    before = lax.broadcasted_iota(jnp.int32, (lanes, lanes), 0) < lax.broadcasted_iota(jnp.int32, (lanes, lanes), 1)
    seg_lo = jnp.dot(jnp.broadcast_to(cnt_al, (8, lanes)), before.astype(F32),
                     preferred_element_type=F32, precision=HIGHEST)[0:1]
    earlier = lax.broadcasted_iota(jnp.int32, (ts, ts), 1) < lax.broadcasted_iota(jnp.int32, (ts, ts), 0)
    rank = _dot(earlier.astype(BF16), sel.astype(BF16))
    dest_ref[...] = jnp.where(sel > 0.0, seg_lo + rank, -1.0)


def _moe_segment_copies(i, off_ref, n_ref, hbm_ref, loc_ref, sem, *, to_hbm):
    lo = jnp.int32(0)
    for e in range(N_EXPERTS):
        n_chunks = n_ref[i * N_EXPERTS + e]
        go = off_ref[i * N_EXPERTS + e]

        def body(k, carry, lo=lo, go=go):
            loc = loc_ref.at[pl.ds(pl.multiple_of(lo + k * MOE_ALIGN, MOE_ALIGN), MOE_ALIGN), :]
            far = hbm_ref.at[pl.ds(pl.multiple_of(go + k * MOE_ALIGN, MOE_ALIGN), MOE_ALIGN), :]
            src, dst = (loc, far) if to_hbm else (far, loc)
            pltpu.make_async_copy(src, dst, sem).start()
            return carry

        lax.fori_loop(0, n_chunks, body, 0)
        lo = lo + n_chunks * MOE_ALIGN
    return lo // MOE_ALIGN


def _moe_wait_copies(n_chunks, hbm_ref, loc_ref, sem, *, to_hbm):
    def body(k, carry):
        loc = loc_ref.at[pl.ds(0, MOE_ALIGN), :]
        far = hbm_ref.at[pl.ds(0, MOE_ALIGN), :]
        src, dst = (loc, far) if to_hbm else (far, loc)
        pltpu.make_async_copy(src, dst, sem).wait()
        return carry

    lax.fori_loop(0, n_chunks, body, 0)


def _moe_sort_kernel(off_ref, n_ref, u_ref, dest_ref, init_ref, us_ref, loc_scr, sem):
    del init_ref
    i = pl.program_id(0)
    dest = dest_ref[...]
    slot = lax.broadcasted_iota(jnp.int32, (dest.shape[0], loc_scr.shape[0]), 1).astype(F32)
    onehot = jnp.zeros(slot.shape, F32)
    for e in range(N_EXPERTS):
        onehot = onehot + jnp.where(slot == dest[:, e:e + 1], 1.0, 0.0)
    loc_scr[...] = _dot_tn(onehot.astype(BF16), u_ref[...]).astype(BF16)
    n = _moe_segment_copies(i, off_ref, n_ref, us_ref, loc_scr, sem, to_hbm=True)
    _moe_wait_copies(n, us_ref, loc_scr, sem, to_hbm=True)


def _moe_expert_kernel(te_ref, tv_ref, u_ref, wg_ref, wu_ref, wd_ref, y_ref, acc_scr):
    k, j = pl.program_id(0), pl.program_id(1)
    last = pl.num_programs(1) - 1
    valid = tv_ref[k] > 0

    @pl.when(valid)
    def _():
        @pl.when(j == 0)
        def _():
            acc_scr[...] = jnp.zeros_like(acc_scr)

        u = u_ref[...]
        mid = (_silu(_dot(u, wg_ref[...])) * _dot(u, wu_ref[...])).astype(BF16)
        acc_scr[...] += _dot(mid, wd_ref[...])

        @pl.when(j == last)
        def _():
            y_ref[...] = acc_scr[...].astype(BF16)

    @pl.when(jnp.logical_not(valid) & (j == last))
    def _():
        y_ref[...] = jnp.zeros_like(y_ref)


def _moe_combine_kernel(off_ref, n_ref, h_ref, mod_ref, gpost_ref, dest_ref, gate_ref, ys_ref, o_ref, loc_scr, sem):
    i = pl.program_id(0)
    n = _moe_segment_copies(i, off_ref, n_ref, ys_ref, loc_scr, sem, to_hbm=False)
    dest, gate = dest_ref[...], gate_ref[...]
    slot = lax.broadcasted_iota(jnp.int32, (dest.shape[0], loc_scr.shape[0]), 1).astype(F32)
    w = jnp.zeros(slot.shape, F32)
    for e in range(N_EXPERTS):
        w = w + jnp.where(slot == dest[:, e:e + 1], gate[:, e:e + 1], 0.0)
    w_hi = w.astype(BF16)
    w_lo = (w - w_hi.astype(F32)).astype(BF16)
    _moe_wait_copies(n, ys_ref, loc_scr, sem, to_hbm=False)
    filled = lax.broadcasted_iota(jnp.int32, (loc_scr.shape[0], 1), 0) < n * MOE_ALIGN
    y_loc = jnp.where(filled, loc_scr[...], jnp.zeros_like(loc_scr))
    y = _dot(w_hi, y_loc) + _dot(w_lo, y_loc)
    o_ref[...] = h_ref[...] + mod_ref[5:6, :] * (y * _rms_scale(y) * gpost_ref[...])


def _moe_call(h, mod_l, g_pre, g_post, router_pad, wg, wu, wd, *, n_b, lx, tm, tf):
    rows, d = h.shape
    ne, _, fe = wg.shape
    ts, tg = tm, MOE_GROUP_TILE
    n_tiles = rows // ts
    lrows = _moe_local_rows(ts)
    nt_max = -(-(TOP_K * rows + n_tiles * ne * (MOE_ALIGN - 1)) // tg) + ne
    n_pad = nt_max * tg
    nxt, per_b = n_b * lx // ts, lx // ts
    mod_row = lambda i: jnp.where(i < nxt, i // per_b, n_b)

    u, dest, gate, cnt = pl.pallas_call(
        _moe_route_kernel,
        grid=(n_tiles,),
        in_specs=[pl.BlockSpec((ts, d), lambda i: (i, 0)),
                  pl.BlockSpec((None, 6, d), lambda i: (mod_row(i), 0, 0)),
                  pl.BlockSpec((1, d), lambda i: (0, 0)),
                  pl.BlockSpec((d, 128), lambda i: (0, 0))],
        out_specs=[pl.BlockSpec((ts, d), lambda i: (i, 0)),
                   pl.BlockSpec((ts, 128), lambda i: (i, 0)),
                   pl.BlockSpec((ts, 128), lambda i: (i, 0)),
                   pl.BlockSpec((None, 8, 128), lambda i: (i, 0, 0))],
        out_shape=[jax.ShapeDtypeStruct((rows, d), BF16),
                   jax.ShapeDtypeStruct((rows, 128), F32),
                   jax.ShapeDtypeStruct((rows, 128), F32),
                   jax.ShapeDtypeStruct((n_tiles, 8, 128), F32)],
        compiler_params=_cp(("parallel",), 40),
        name="moe_route",
    )(h, mod_l, g_pre.reshape(1, d), router_pad)

    cnt_al = cnt[:, 0, :ne].astype(jnp.int32)
    group = -(-jnp.sum(cnt_al, axis=0) // tg) * tg
    group_end = jnp.cumsum(group)
    seg_off = (group_end - group)[None, :] + jnp.cumsum(cnt_al, axis=0) - cnt_al
    n_used = group_end[-1] // tg
    tile_id = jnp.arange(nt_max, dtype=jnp.int32)
    tile_valid = (tile_id < n_used).astype(jnp.int32)
    tile_expert = jnp.searchsorted(group_end // tg, jnp.minimum(tile_id, n_used - 1), side='right')
    tile_expert = jnp.minimum(tile_expert, ne - 1).astype(jnp.int32)
    seg_off = seg_off.reshape(-1).astype(jnp.int32)
    seg_chunks = (cnt_al // MOE_ALIGN).reshape(-1)

    u_sorted = pl.pallas_call(
        _moe_sort_kernel,
        grid_spec=pltpu.PrefetchScalarGridSpec(
            num_scalar_prefetch=2,
            grid=(n_tiles,),
            in_specs=[pl.BlockSpec((ts, d), lambda i, o_r, n_r: (i, 0)),
                      pl.BlockSpec((ts, 128), lambda i, o_r, n_r: (i, 0)),
                      pl.BlockSpec(memory_space=pl.ANY)],
            out_specs=pl.BlockSpec(memory_space=pl.ANY),
            scratch_shapes=[pltpu.VMEM((lrows, d), BF16), pltpu.SemaphoreType.DMA(())]),
        out_shape=jax.ShapeDtypeStruct((n_pad, d), BF16),
        input_output_aliases={4: 0},
        compiler_params=_cp(("arbitrary",), 40),
        name="moe_sort",
    )(seg_off, seg_chunks, u, dest, jnp.zeros((n_pad, d), BF16))

    nj = fe // tf
    w_col = lambda k, j, te_r, tv_r: jnp.where(tv_r[k] > 0, j, nj - 1)
    y_sorted = pl.pallas_call(
        _moe_expert_kernel,
        grid_spec=pltpu.PrefetchScalarGridSpec(
            num_scalar_prefetch=2,
            grid=(nt_max, nj),
            in_specs=[pl.BlockSpec((tg, d), lambda k, j, te_r, tv_r: (k, 0)),
                      pl.BlockSpec((None, d, tf), lambda k, j, te_r, tv_r: (te_r[k], 0, w_col(k, j, te_r, tv_r))),
                      pl.BlockSpec((None, d, tf), lambda k, j, te_r, tv_r: (te_r[k], 0, w_col(k, j, te_r, tv_r))),
                      pl.BlockSpec((None, tf, d), lambda k, j, te_r, tv_r: (te_r[k], w_col(k, j, te_r, tv_r), 0))],
            out_specs=pl.BlockSpec((tg, d), lambda k, j, te_r, tv_r: (k, 0)),
            scratch_shapes=[pltpu.VMEM((tg, d), F32)]),
        out_shape=jax.ShapeDtypeStruct((n_pad, d), BF16),
        compiler_params=_cp(("parallel", "arbitrary"), 48),
        name="moe_experts",
    )(tile_expert, tile_valid, u_sorted, wg, wu, wd)

    return pl.pallas_call(
        _moe_combine_kernel,
        grid_spec=pltpu.PrefetchScalarGridSpec(
            num_scalar_prefetch=2,
            grid=(n_tiles,),
            in_specs=[pl.BlockSpec((ts, d), lambda i, o_r, n_r: (i, 0)),
                      pl.BlockSpec((None, 6, d), lambda i, o_r, n_r: (mod_row(i), 0, 0)),
                      pl.BlockSpec((1, d), lambda i, o_r, n_r: (0, 0)),
                      pl.BlockSpec((ts, 128), lambda i, o_r, n_r: (i, 0)),
                      pl.BlockSpec((ts, 128), lambda i, o_r, n_r: (i, 0)),
                      pl.BlockSpec(memory_space=pl.ANY)],
            out_specs=pl.BlockSpec((ts, d), lambda i, o_r, n_r: (i, 0)),
            scratch_shapes=[pltpu.VMEM((lrows, d), BF16), pltpu.SemaphoreType.DMA(())]),
        out_shape=jax.ShapeDtypeStruct((rows, d), F32),
        compiler_params=_cp(("arbitrary",), 48),
        name="moe_combine",
    )(seg_off, seg_chunks, h, mod_l, g_post.reshape(1, d), dest, gate, y_sorted)


def _rope_tables(lx, tm):
    half = MLA_ROPE // 2
    n_axis = half // 2
    inv_freq = ROPE_THETA ** (-jnp.arange(n_axis, dtype=F32) / n_axis)
    pos = jnp.arange(lx)
    rows = (pos // GRID_W).astype(F32)
    cols = (pos % GRID_W).astype(F32)
    ang = jnp.concatenate([rows[:, None] * inv_freq, cols[:, None] * inv_freq], axis=-1)
    cos, sin = jnp.cos(ang), jnp.sin(ang)
    cos_t = jnp.concatenate([cos, cos], axis=-1)
    sin_t = jnp.concatenate([-sin, sin], axis=-1)
    cos_t = jnp.concatenate([cos_t, jnp.ones((tm, MLA_ROPE), F32)], axis=0)
    sin_t = jnp.concatenate([sin_t, jnp.zeros((tm, MLA_ROPE), F32)], axis=0)
    return cos_t, sin_t


def _mla_weights(q_norm, w_uq, kv_norm, w_ukv):
    dq = MLA_NOPE + MLA_ROPE
    half = MLA_ROPE // 2
    cols = lambda a, b: w_uq[:, a:b]
    nope = [cols(h * dq, h * dq + MLA_NOPE) for h in range(MLA_HEADS)]
    rope = [cols(h * dq + MLA_NOPE, (h + 1) * dq) for h in range(MLA_HEADS)]
    rope_sw = [cols(h * dq + MLA_NOPE + s * half, h * dq + MLA_NOPE + (s + 1) * half)
               for h in range(MLA_HEADS) for s in (1, 0)]
    pad = ((0, MLA_ROPE), (0, 0))
    wq = jnp.pad(jnp.concatenate(nope + rope, axis=1), pad).astype(BF16)
    wqs = jnp.pad(jnp.concatenate(rope_sw, axis=1), pad).astype(BF16)
    qn_ext = jnp.pad(q_norm, (0, MLA_ROPE)).reshape(1, -1)
    perm = np.zeros((MLA_ROPE, MLA_ROPE), np.float32)
    perm[(np.arange(MLA_ROPE) + half) % MLA_ROPE, np.arange(MLA_ROPE)] = 1.0
    return qn_ext, wq, wqs, kv_norm.reshape(1, -1), w_ukv.astype(BF16), jnp.asarray(perm, BF16)


def _pick_tile(n, cands):
    for t in cands:
        if n % t == 0:
            return t
    raise ValueError(f"no tile for {n}")


def kernel(x, c, ctx, c_ctx, w_ada, b_ada, g_pre_mix, g_post_mix, g_pre_ffn, g_post_ffn, w_in, w_out, mla_q_norm, mla_w_uq, mla_kv_norm, mla_w_ukv, ssd_conv_w, ssd_conv_b, ssd_a_log, ssd_dt_bias, ssd_d, ssd_norm, na_rpb, gdn_conv_w, gdn_a_log, gdn_dt_bias, gdn_norm, ffn_w_gate, ffn_w_up, ffn_w_down, moe_router, moe_w_gate, moe_w_up, moe_w_down):
    n_b, lx, d = x.shape
    lc = ctx.shape[1]
    depth = w_ada.shape[0]
    rows_x, rows_c = n_b * lx, n_b * lc
    assert n_b + 1 <= 8 and lx % GRID_W == 0 and lx % lc == 0 and lc % CHUNK == 0
    tm = _pick_tile(math.gcd(lx, rows_c), (512, 256, 128))
    tm_in = _pick_tile(math.gcd(lx, rows_c), (1024, 512, 256, 128))
    tq = _pick_tile(lc, (256, 128))
    tr = _pick_tile(lc, (256, 128))

    cvec = jnp.concatenate([c, c_ctx[None, :], jnp.zeros((8 - n_b - 1, d), F32)], axis=0)
    mod = _ada_call(cvec, w_ada, b_ada).reshape(depth, 8, 6, d)
    cos_t, sin_t = _rope_tables(lx, tm)
    h_all = jnp.concatenate([x.reshape(rows_x, d), ctx.reshape(rows_c, d)], axis=0)

    for i in range(depth):
        need_ctx = i < depth - 1
        w_main, w_small = _regroup_w_in(w_in[i])
        proj, small = _inproj_call(h_all, mod[i], g_pre_mix[i], w_main, w_small, n_b=n_b, lx=lx, tm=tm_in)

        mla_w = _mla_weights(mla_q_norm[i], mla_w_uq[i], mla_kv_norm[i], mla_w_ukv[i])
        q_a, k_a, v_a = _mla_prep_call(proj, *mla_w[:5], cos_t, sin_t, mla_w[5], n_b=n_b, lx=lx, tm=tm)
        ya = _mla_attn_call(q_a, k_a, v_a, n_b=n_b, lx=lx, lc=lc, tq=tq, need_ctx=need_ctx)

        xbc = _conv_call(proj, ssd_conv_w[i], ssd_conv_b[i], col_off=P_XBC, n_b=n_b, lx=lx, lc=lc, tr=tr)
        ys = _ssd_call(xbc, small, ssd_a_log[i], ssd_dt_bias[i], ssd_d[i], n_b=n_b, lx=lx, lc=lc)

        yn = _na_call(proj, _na_bias_table(na_rpb[i], lx // GRID_W, lc), n_b=n_b, lx=lx, lc=lc, need_ctx=need_ctx)

        qkv = _conv_call(proj, gdn_conv_w[i], jnp.zeros((3 * GDN_DIM,), F32), col_off=P_GQKV,
                         n_b=n_b, lx=lx, lc=lc, tr=tr)
        og = _gdn_call(qkv, small, gdn_a_log[i], gdn_dt_bias[i], n_b=n_b, lx=lx, lc=lc)

        n_rows = rows_x + rows_c if need_ctx else rows_x
        h_mid = _mixout_call(h_all, mod[i], g_post_mix[i], ya, ys, proj, ssd_norm[i], yn, og, gdn_norm[i],
                             w_out[i].astype(BF16), n_b=n_b, lx=lx, n_rows=n_rows, tm=min(tm, 256))
        j = i // 2
        if i % 2 == 0:
            h_all = _ffn_call(h_mid, mod[i], g_pre_ffn[i], g_post_ffn[i], ffn_w_gate[j].astype(BF16),
                              ffn_w_up[j].astype(BF16), ffn_w_down[j].astype(BF16), n_b=n_b, lx=lx, tm=tm, tf=512)
        else:
            router_pad = jnp.pad(moe_router[j], ((0, 0), (0, 128 - N_EXPERTS)))
            h_all = _moe_call(h_mid, mod[i], g_pre_ffn[i], g_post_ffn[i], router_pad, moe_w_gate[j].astype(BF16),
                              moe_w_up[j].astype(BF16), moe_w_down[j].astype(BF16), n_b=n_b, lx=lx, tm=tm, tf=256)
    return h_all[:rows_x].reshape(n_b, lx, d)
```

```python
import functools
import math

import numpy as np
import jax
import jax.numpy as jnp
from jax import lax
from jax.experimental import pallas as pl
from jax.experimental.pallas import tpu as pltpu

F32 = jnp.float32
BF16 = jnp.bfloat16
HIGHEST = lax.Precision.HIGHEST

GRID_W = 64
EPS = 1e-6
ROPE_THETA = 10000.0
CHUNK = 128
CONV_W = 5
MLA_HEADS, MLA_NOPE, MLA_ROPE, MLA_V = 4, 128, 64, 128
MLA_Q_LORA, MLA_KV_LORA = 448, 128
SSD_HEADS, SSD_HEAD_DIM, SSD_STATE, SSD_GROUPS = 8, 64, 128, 2
SSD_D_INNER = SSD_HEADS * SSD_HEAD_DIM
SSD_CONV_DIM = SSD_D_INNER + 2 * SSD_GROUPS * SSD_STATE
NA_HEADS, NA_HEAD_DIM = 4, 128
NA_DIM = NA_HEADS * NA_HEAD_DIM
NA_WIN_ROWS, NA_WIN_COLS = 8, 16
GDN_HEADS, GDN_HEAD_DIM = 4, 128
GDN_DIM = GDN_HEADS * GDN_HEAD_DIM
N_EXPERTS, TOP_K = 8, 2
MLA_COLS = MLA_Q_LORA + MLA_KV_LORA + MLA_ROPE
SSD_COLS = SSD_D_INNER + SSD_CONV_DIM + 2 * SSD_HEADS
NA_COLS = 3 * NA_DIM
GDN_COLS = 4 * GDN_DIM + 4 * GDN_HEADS

P_XBC, P_ZSSD, P_NAQ, P_NAK, P_NAV = 0, 1024, 1536, 2048, 2560
P_GQKV, P_GZ, P_MLA = 3072, 4608, 5120
P_MAIN = 5760
P_SMALL = 128
NEG = -1e30
VMEM_MB = 1024 * 1024


def _cp(sem, mb):
    return pltpu.CompilerParams(dimension_semantics=sem, vmem_limit_bytes=mb * VMEM_MB)


def _dot(a, b):
    return jnp.dot(a, b, preferred_element_type=F32)


def _dot_nt(a, b, precision=None):
    return lax.dot_general(a, b, (((1,), (1,)), ((), ())), preferred_element_type=F32, precision=precision)


def _dot_tn(a, b):
    return lax.dot_general(a, b, (((0,), (0,)), ((), ())), preferred_element_type=F32)


def _sigmoid(x):
    return 1.0 / (1.0 + jnp.exp(-x))


def _silu(x):
    return x * _sigmoid(x)


def _softplus(x):
    return jnp.maximum(x, 0.0) + jnp.log(1.0 + jnp.exp(-jnp.abs(x)))


def _rms_scale(x):
    return lax.rsqrt(jnp.mean(x * x, axis=-1, keepdims=True) + EPS)


def _regroup_w_in(w):
    o_mla, o_ssd = 0, MLA_COLS
    o_na, o_gdn = o_ssd + SSD_COLS, o_ssd + SSD_COLS + NA_COLS
    main_segs = [
        (o_ssd + SSD_D_INNER, SSD_CONV_DIM),
        (o_ssd, SSD_D_INNER),
        (o_na, NA_COLS),
        (o_gdn, 4 * GDN_DIM),
        (o_mla + MLA_Q_LORA, MLA_KV_LORA),
        (o_mla, MLA_Q_LORA),
        (o_mla + MLA_Q_LORA + MLA_KV_LORA, MLA_ROPE),
    ]
    small_segs = [(o_ssd + SSD_D_INNER + SSD_CONV_DIM, 2 * SSD_HEADS), (o_gdn + 4 * GDN_DIM, 4 * GDN_HEADS)]
    assert sum(n for _, n in main_segs) == P_MAIN
    main = jnp.concatenate([w[:, a:a + n] for a, n in main_segs], axis=1).astype(BF16)
    n_small = sum(n for _, n in small_segs)
    small = jnp.concatenate([w[:, a:a + n] for a, n in small_segs]
                            + [jnp.zeros((w.shape[0], P_SMALL - n_small), w.dtype)], axis=1).astype(BF16)
    return main, small


def _ada_kernel(c_ref, w_ref, b_ref, o_ref):
    s = _silu(c_ref[...]).astype(BF16)
    o_ref[...] = _dot(s, w_ref[...].astype(BF16)) + b_ref[...]


def _ada_call(cvec, w_ada, b_ada):
    depth, d, n = w_ada.shape
    tn = 1024
    return pl.pallas_call(
        _ada_kernel,
        grid=(depth, n // tn),
        in_specs=[pl.BlockSpec((8, d), lambda l, j: (0, 0)),
                  pl.BlockSpec((None, d, tn), lambda l, j: (l, 0, j)),
                  pl.BlockSpec((None, 1, tn), lambda l, j: (l, 0, j))],
        out_specs=pl.BlockSpec((None, 8, tn), lambda l, j: (l, 0, j)),
        out_shape=jax.ShapeDtypeStruct((depth, 8, n), F32),
        compiler_params=_cp(("parallel", "parallel"), 40),
        name="adaln",
    )(cvec, w_ada, b_ada.reshape(depth, 1, n))


def _inproj_kernel(h_ref, mod_ref, g_ref, w_ref, ws_ref, o_ref, os_ref, u_scr):
    @pl.when(pl.program_id(1) == 0)
    def _():
        x = h_ref[...]
        y = x * _rms_scale(x) * g_ref[...]
        u = (y * (1.0 + mod_ref[1:2, :]) + mod_ref[0:1, :]).astype(BF16)
        u_scr[...] = u
        os_ref[...] = _dot(u, ws_ref[...])

    o_ref[...] = _dot(u_scr[...], w_ref[...]).astype(BF16)


def _inproj_call(h, mod_l, g_pre, w_main, w_small, *, n_b, lx, tm):
    rows, d = h.shape
    tn = 1920
    nxt, per_b = n_b * lx // tm, lx // tm

    def mod_idx(i, j):
        return (jnp.where(i < nxt, i // per_b, n_b), 0, 0)

    return pl.pallas_call(
        _inproj_kernel,
        grid=(rows // tm, P_MAIN // tn),
        in_specs=[pl.BlockSpec((tm, d), lambda i, j: (i, 0)),
                  pl.BlockSpec((None, 6, d), mod_idx),
                  pl.BlockSpec((1, d), lambda i, j: (0, 0)),
                  pl.BlockSpec((d, tn), lambda i, j: (0, j)),
                  pl.BlockSpec((d, P_SMALL), lambda i, j: (0, 0))],
        out_specs=[pl.BlockSpec((tm, tn), lambda i, j: (i, j)),
                   pl.BlockSpec((tm, P_SMALL), lambda i, j: (i, 0))],
        out_shape=[jax.ShapeDtypeStruct((rows, P_MAIN), BF16),
                   jax.ShapeDtypeStruct((rows, P_SMALL), F32)],
        scratch_shapes=[pltpu.VMEM((tm, d), BF16)],
        compiler_params=_cp(("parallel", "arbitrary"), 56),
        name="in_proj",
    )(h, mod_l, g_pre.reshape(1, d), w_main, w_small)


HALO = 16


def _conv_kernel(prev_ref, cur_ref, next_ref, w_ref, b_ref, o_ref, ext_scr, *, tr, blocks_x, seq_x, seq_c):
    i = pl.program_id(0)
    in_x = i < blocks_x
    pos = jnp.where(in_x, i % seq_x, (i - blocks_x) % seq_c)
    last_pos = jnp.where(in_x, seq_x - 1, seq_c - 1)
    ext_scr[0:HALO, :] = jnp.where(pos == 0, 0.0, prev_ref[...].astype(F32))
    ext_scr[HALO:HALO + tr, :] = cur_ref[...].astype(F32)
    ext_scr[HALO + tr:2 * HALO + tr, :] = jnp.where(pos == last_pos, 0.0, next_ref[...].astype(F32))
    acc = b_ref[...] + w_ref[0:1, :] * ext_scr[HALO - 2:HALO - 2 + tr, :]
    for k in range(1, CONV_W):
        acc = acc + w_ref[k:k + 1, :] * ext_scr[HALO - 2 + k:HALO - 2 + k + tr, :]
    o_ref[...] = _silu(acc).astype(BF16)


def _conv_call(proj, w, b, *, col_off, n_b, lx, lc, tr):
    rows = proj.shape[0]
    c = w.shape[1]
    cb = col_off // c
    assert cb * c == col_off
    hb = tr // HALO
    n_halo = rows // HALO
    kern = functools.partial(_conv_kernel, tr=tr, blocks_x=n_b * lx // tr, seq_x=lx // tr, seq_c=lc // tr)
    return pl.pallas_call(
        kern,
        grid=(rows // tr,),
        in_specs=[pl.BlockSpec((HALO, c), lambda i: (jnp.maximum(i * hb - 1, 0), cb)),
                  pl.BlockSpec((tr, c), lambda i: (i, cb)),
                  pl.BlockSpec((HALO, c), lambda i: (jnp.minimum((i + 1) * hb, n_halo - 1), cb)),
                  pl.BlockSpec((CONV_W, c), lambda i: (0, 0)),
                  pl.BlockSpec((1, c), lambda i: (0, 0))],
        out_specs=pl.BlockSpec((tr, c), lambda i: (i, 0)),
        out_shape=jax.ShapeDtypeStruct((rows, c), BF16),
        scratch_shapes=[pltpu.VMEM((tr + 2 * HALO, c), F32)],
        compiler_params=_cp(("parallel",), 40),
        name="dwconv_silu",
    )(proj, proj, proj, w, b.reshape(1, c))


def _mla_prep_kernel(p_ref, qn_ref, wq_ref, wqs_ref, kvn_ref, wkv_ref, cos_ref, sin_ref, perm_ref,
                     q_ref, k_ref, v_ref):
    p = p_ref[...].astype(F32)
    ckv = p[:, 0:MLA_KV_LORA]
    ce = p[:, MLA_KV_LORA:]
    lane = lax.broadcasted_iota(jnp.int32, ce.shape, 1)
    ssq = jnp.sum(jnp.where(lane < MLA_Q_LORA, ce * ce, 0.0), axis=-1, keepdims=True)
    cqn = (ce * lax.rsqrt(ssq / MLA_Q_LORA + EPS) * qn_ref[...]).astype(BF16)
    ckvn = (ckv * _rms_scale(ckv) * kvn_ref[...]).astype(BF16)
    q = _dot(cqn, wq_ref[...])
    qs = _dot(cqn, wqs_ref[...])
    kv = _dot(ckvn, wkv_ref[...])
    cos, sin = cos_ref[...], sin_ref[...]
    kr = p_ref[:, MLA_KV_LORA + MLA_Q_LORA:]
    kr_rot = kr.astype(F32) * cos + _dot(kr, perm_ref[...]) * sin
    nr = MLA_HEADS * MLA_NOPE
    scale = (MLA_NOPE + MLA_ROPE) ** -0.5
    ones_col = jnp.where(lax.broadcasted_iota(jnp.int32, (p.shape[0], MLA_V), 1) == 0, 1.0, 0.0).astype(BF16)
    for h in range(MLA_HEADS):
        q_ref[h, :, 0:MLA_NOPE] = (q[:, h * MLA_NOPE:(h + 1) * MLA_NOPE] * scale).astype(BF16)
        qr = q[:, nr + h * MLA_ROPE:nr + (h + 1) * MLA_ROPE] * cos + qs[:, h * MLA_ROPE:(h + 1) * MLA_ROPE] * sin
        q_ref[h, :, MLA_NOPE:] = (qr * scale).astype(BF16)
        hv = h * (MLA_NOPE + MLA_V)
        k_ref[h, :, 0:MLA_NOPE] = kv[:, hv:hv + MLA_NOPE].astype(BF16)
        k_ref[h, :, MLA_NOPE:] = kr_rot.astype(BF16)
        v_ref[h, :, 0:MLA_V] = kv[:, hv + MLA_NOPE:hv + MLA_NOPE + MLA_V].astype(BF16)
        v_ref[h, :, MLA_V:] = ones_col


def _mla_prep_call(proj, qn_ext, wq, wqs, kvn, wkv, cos_t, sin_t, perm, *, n_b, lx, tm):
    rows = proj.shape[0]
    nxt, per_b = n_b * lx // tm, lx // tm
    dk = MLA_NOPE + MLA_ROPE
    rope_idx = lambda i: (jnp.where(i < nxt, i % per_b, per_b), 0)
    full = lambda a: pl.BlockSpec(a.shape, lambda i: (0,) * a.ndim)
    return pl.pallas_call(
        _mla_prep_kernel,
        grid=(rows // tm,),
        in_specs=[pl.BlockSpec((tm, MLA_COLS), lambda i: (i, P_MLA // MLA_COLS)),
                  full(qn_ext), full(wq), full(wqs), full(kvn), full(wkv),
                  pl.BlockSpec((tm, MLA_ROPE), rope_idx), pl.BlockSpec((tm, MLA_ROPE), rope_idx),
                  full(perm)],
        out_specs=[pl.BlockSpec((MLA_HEADS, tm, dk), lambda i: (0, i, 0)),
                   pl.BlockSpec((MLA_HEADS, tm, dk), lambda i: (0, i, 0)),
                   pl.BlockSpec((MLA_HEADS, tm, 2 * MLA_V), lambda i: (0, i, 0))],
        out_shape=[jax.ShapeDtypeStruct((MLA_HEADS, rows, dk), BF16),
                   jax.ShapeDtypeStruct((MLA_HEADS, rows, dk), BF16),
                   jax.ShapeDtypeStruct((MLA_HEADS, rows, 2 * MLA_V), BF16)],
        compiler_params=_cp(("parallel",), 40),
        name="mla_prep",
    )(proj, qn_ext, wq, wqs, kvn, wkv, cos_t, sin_t, perm)


def _softmax_pv(scores, values):
    m = functools.reduce(jnp.maximum, [jnp.max(s, axis=-1, keepdims=True) for s in scores])
    ps = [jnp.exp(s - m) for s in scores]
    den = functools.reduce(lambda a, b: a + b, [jnp.sum(p, axis=-1, keepdims=True) for p in ps])
    num = functools.reduce(lambda a, b: a + b, [_dot(p.astype(BF16), v) for p, v in zip(ps, values)])
    return num / den


def _softmax_pv_aug(scores, values_aug):
    m = functools.reduce(jnp.maximum, [jnp.max(s, axis=-1, keepdims=True) for s in scores])
    acc = functools.reduce(lambda a, b: a + b,
                           [_dot(jnp.exp((s - m).astype(BF16)), v) for s, v in zip(scores, values_aug)])
    return acc[:, 0:MLA_V] / acc[:, MLA_V:MLA_V + 1]


MLA_HEADS_PER_STEP = 4


def _mla_attn_kernel(q_ref, kx_ref, vx_ref, kc_ref, vc_ref, o_ref, *, nqx):
    qi = pl.program_id(2)
    heads = range(q_ref.shape[0])

    @pl.when(qi < nqx)
    def _():
        outs = [_softmax_pv_aug([_dot_nt(q_ref[h], kx_ref[h]), _dot_nt(q_ref[h], kc_ref[h])], [vx_ref[h], vc_ref[h]])
                for h in heads]
        o_ref[...] = jnp.concatenate(outs, axis=1).astype(BF16)

    @pl.when(qi >= nqx)
    def _():
        outs = [_softmax_pv_aug([_dot_nt(q_ref[h], kc_ref[h])], [vc_ref[h]]) for h in heads]
        o_ref[...] = jnp.concatenate(outs, axis=1).astype(BF16)


def _mla_attn_call(q, k, v, *, n_b, lx, lc, tq, need_ctx):
    rows = q.shape[1]
    dk = q.shape[2]
    nqx, nqc = lx // tq, lc // tq
    nq = nqx + (nqc if need_ctx else 0)
    nbx = n_b * lx // lc

    def q_row(b, qi):
        return jnp.where(qi < nqx, b * nqx + qi, n_b * nqx + b * nqc + (qi - nqx))

    kern = functools.partial(_mla_attn_kernel, nqx=nqx)
    out_rows = rows if need_ctx else n_b * lx
    hp = MLA_HEADS_PER_STEP
    return pl.pallas_call(
        kern,
        grid=(n_b, MLA_HEADS // hp, nq),
        in_specs=[pl.BlockSpec((hp, tq, dk), lambda b, h, qi: (h, q_row(b, qi), 0)),
                  pl.BlockSpec((hp, lx, dk), lambda b, h, qi: (h, b, 0)),
                  pl.BlockSpec((hp, lx, 2 * MLA_V), lambda b, h, qi: (h, b, 0)),
                  pl.BlockSpec((hp, lc, dk), lambda b, h, qi: (h, nbx + b, 0)),
                  pl.BlockSpec((hp, lc, 2 * MLA_V), lambda b, h, qi: (h, nbx + b, 0))],
        out_specs=pl.BlockSpec((tq, hp * MLA_V), lambda b, h, qi: (q_row(b, qi), h)),
        out_shape=jax.ShapeDtypeStruct((out_rows, MLA_HEADS * MLA_V), BF16),
        compiler_params=_cp(("parallel", "parallel", "arbitrary"), 48),
        name="mla_attn",
    )(q, k, v, k, v)


def _na_plan(g_rows, lc):
    wr = min(NA_WIN_ROWS, g_rows)
    rg = next(r for r in (4, 2, 1) if g_rows % r == 0 and lc % (r * GRID_W) == 0)
    wk = min(rg + wr - 1, g_rows)
    n_groups = g_rows // rg
    ks = np.clip(np.arange(n_groups) * rg - wr // 2, 0, g_rows - wk)
    r = np.arange(g_rows)
    rs = np.clip(r - wr // 2, 0, g_rows - wr)
    q_off = (r - np.repeat(ks, rg)).reshape(n_groups, rg)
    rel = (rs - np.repeat(ks, rg)).reshape(n_groups, rg)
    assert (rel >= 0).all() and (rel + wr <= wk).all()
    pats = [tuple(q_off[g]) + tuple(rel[g]) for g in range(n_groups)]
    uniq = sorted(set(pats))
    var = np.array([uniq.index(p) for p in pats], np.int32)
    q_off_v = np.array([p[:rg] for p in uniq])
    rel_v = np.array([p[rg:] for p in uniq])
    return wr, rg, wk, ks.astype(np.int32), var, q_off_v, rel_v


def _na_bias_table(rpb, g_rows, lc):
    wr, rg, wk, _, _, q_off_v, rel_v = _na_plan(g_rows, lc)
    col_start = np.clip(np.arange(GRID_W) - NA_WIN_COLS // 2, 0, GRID_W - NA_WIN_COLS)
    cc = np.arange(GRID_W)
    col_ok = (cc[None, :] >= col_start[:, None]) & (cc[None, :] < col_start[:, None] + NA_WIN_COLS)
    dc = np.clip(cc[None, :] - cc[:, None] + NA_WIN_COLS - 1, 0, 2 * NA_WIN_COLS - 2)
    n_dc = 2 * NA_WIN_COLS - 1
    nh = rpb.shape[0]
    onehot = (dc.reshape(-1)[:, None] == np.arange(n_dc)[None, :]).astype(np.float32)
    g = jnp.einsum('hab,yb->hay', rpb.astype(F32), jnp.asarray(onehot), precision=HIGHEST)
    g = jnp.where(col_ok[None, None], g.reshape(nh, -1, GRID_W, GRID_W), NEG)
    g = jnp.pad(g, ((0, 0), (wk, wk), (0, 0), (0, 0)), constant_values=NEG)
    w = np.arange(wk)
    tabs = []
    for v in range(q_off_v.shape[0]):
        rows_v = []
        for j in range(rg):
            a0 = wk - q_off_v[v, j] + NA_WIN_ROWS - 1
            own = (w >= rel_v[v, j]) & (w < rel_v[v, j] + wr)
            blk = jnp.where(jnp.asarray(own)[None, :, None, None], g[:, a0:a0 + wk], NEG)
            rows_v.append(jnp.transpose(blk, (0, 2, 1, 3)).reshape(nh, GRID_W, wk * GRID_W))
        tabs.append(jnp.concatenate(rows_v, axis=-2))
    return jnp.stack(tabs, axis=0)


def _na_kernel(var_ref, ks_ref, q_ref, kx_ref, vx_ref, kc_ref, vc_ref, bias_ref, o_ref, *, n_groups, wk, scale):
    g = pl.program_id(1)
    hd = NA_HEAD_DIM
    cols = [slice(h * hd, (h + 1) * hd) for h in range(NA_HEADS)]

    @pl.when(g < n_groups)
    def _():
        start = pl.multiple_of(ks_ref[g] * GRID_W, GRID_W)
        rows = pl.ds(start, wk * GRID_W)
        outs = []
        for h, c in enumerate(cols):
            q = q_ref[:, c]
            sl = _dot_nt(q, kx_ref[rows, c]) * scale + bias_ref[h]
            sc = _dot_nt(q, kc_ref[:, c]) * scale
            outs.append(_softmax_pv([sl, sc], [vx_ref[rows, c], vc_ref[:, c]]))
        o_ref[...] = jnp.concatenate(outs, axis=1).astype(BF16)

    @pl.when(g >= n_groups)
    def _():
        outs = [_softmax_pv([_dot_nt(q_ref[:, c], kc_ref[:, c]) * scale], [vc_ref[:, c]]) for c in cols]
        o_ref[...] = jnp.concatenate(outs, axis=1).astype(BF16)


def _na_call(proj, bias_tab, *, layer, n_b, lx, lc, need_ctx):
    rows = proj.shape[0]
    g_rows = lx // GRID_W
    _, rg, wk, ks, var, _, _ = _na_plan(g_rows, lc)
    n_groups = g_rows // rg
    tq = rg * GRID_W
    nqc = lc // tq
    nq = n_groups + (nqc if need_ctx else 0)
    nbx = n_b * lx // lc
    hd, nd = NA_HEAD_DIM, NA_DIM
    cq, ck, cv = P_NAQ // nd, P_NAK // nd, P_NAV // nd

    def q_row(b, g):
        return jnp.where(g < n_groups, b * n_groups + g, n_b * n_groups + b * nqc + (g - n_groups))

    kern = functools.partial(_na_kernel, n_groups=n_groups, wk=wk, scale=hd ** -0.5)
    grid_spec = pltpu.PrefetchScalarGridSpec(
        num_scalar_prefetch=2,
        grid=(n_b, nq),
        in_specs=[pl.BlockSpec((tq, nd), lambda b, g, var_r, ks_r: (q_row(b, g), cq)),
                  pl.BlockSpec((lx, nd), lambda b, g, var_r, ks_r: (b, ck)),
                  pl.BlockSpec((lx, nd), lambda b, g, var_r, ks_r: (b, cv)),
                  pl.BlockSpec((lc, nd), lambda b, g, var_r, ks_r: (nbx + b, ck)),
                  pl.BlockSpec((lc, nd), lambda b, g, var_r, ks_r: (nbx + b, cv)),
                  pl.BlockSpec((None, NA_HEADS, tq, wk * GRID_W),
                               lambda b, g, var_r, ks_r: (var_r[jnp.minimum(g, n_groups - 1)], layer, 0, 0))],
        out_specs=pl.BlockSpec((tq, nd), lambda b, g, var_r, ks_r: (q_row(b, g), 0)),
    )
    return pl.pallas_call(
        kern,
        grid_spec=grid_spec,
        out_shape=jax.ShapeDtypeStruct((rows if need_ctx else n_b * lx, NA_DIM), BF16),
        compiler_params=_cp(("parallel", "arbitrary"), 40),
        name="na_attn",
    )(jnp.asarray(var), jnp.asarray(ks), proj, proj, proj, proj, proj, bias_tab)


def _chunk_block(n_b, nxc, ncc):
    def f(b, d, c):
        cc = jnp.where(d == 0, c, ncc - 1 - c)
        cx = jnp.where(d == 0, c - ncc, nxc - 1 - (c - ncc))
        return jnp.where(c < ncc, n_b * nxc + b * ncc + cc, b * nxc + cx)
    return f


def _dir_masks(d):
    row = lax.broadcasted_iota(jnp.int32, (CHUNK, CHUNK), 0)
    col = lax.broadcasted_iota(jnp.int32, (CHUNK, CHUNK), 1)
    diff = (row - col) * jnp.where(d == 0, 1, -1)
    return diff >= 0, diff > 0


def _cumsum_lanes(x, incl):
    cs = jnp.dot(incl.astype(F32), x, preferred_element_type=F32, precision=HIGHEST)
    return cs, cs.T


def _lane_vec(vals, offset):
    flat = vals.reshape(-1).astype(F32)
    return jnp.pad(flat, (offset, P_SMALL - offset - flat.shape[0])).reshape(1, P_SMALL)


def _ssd_prep_kernel(xbc_ref, sm_ref, alog_ref, dtb_ref, dsk_ref, yp_ref, xw_ref, ea_ref):
    nh, hp, ns = SSD_HEADS, SSD_HEAD_DIM, SSD_STATE
    gh = nh // SSD_GROUPS
    dt2 = _softplus(sm_ref[...] + dtb_ref[...])
    dta2 = dt2 * (-jnp.exp(alog_ref[...]))
    row = lax.broadcasted_iota(jnp.int32, (CHUNK, CHUNK), 0)
    col = lax.broadcasted_iota(jnp.int32, (CHUNK, CHUNK), 1)
    incl = [row >= col, row <= col]
    cs = [jnp.dot(m.astype(F32), dta2, preferred_element_type=F32, precision=HIGHEST) for m in incl]
    cs_t = [x.T for x in cs]
    tot2 = jnp.sum(dta2, axis=0, keepdims=True)
    dsk = dsk_ref[...]
    bo, co = SSD_D_INNER, SSD_D_INNER + SSD_GROUPS * ns
    scores = [_dot_nt(xbc_ref[:, co + g * ns:co + (g + 1) * ns], xbc_ref[:, bo + g * ns:bo + (g + 1) * ns])
              for g in range(SSD_GROUPS)]
    xs = [xbc_ref[:, h * hp:(h + 1) * hp].astype(F32) for h in range(nh)]
    yp, xw = [], []
    for d in range(2):
        for h in range(nh):
            ln = d * nh + h
            a_c, a_r = cs[d][:, ln:ln + 1], cs_t[d][ln:ln + 1, :]
            dec = jnp.where(incl[d], jnp.exp(jnp.where(incl[d], a_c - a_r, 0.0)), 0.0)
            m = (scores[h // gh] * dec).astype(BF16)
            y = _dot(m, (xs[h] * dt2[:, ln:ln + 1]).astype(BF16))
            yp.append(y + dsk[:, h:h + 1] * xs[h] if d == 0 else y)
            xw.append((xs[h] * (jnp.exp(tot2[:, ln:ln + 1] - a_c) * dt2[:, ln:ln + 1])).astype(BF16))
    yp_ref[...] = jnp.concatenate(yp, axis=1)
    xw_ref[...] = jnp.concatenate(xw, axis=1)
    ea_ref[...] = jnp.concatenate([jnp.exp(cs[0]), jnp.exp(cs[1])], axis=1)


def _ssd_scan_kernel(*refs):
    (bc0, yp0, xw0, ea0, sm0, bc1, yp1, xw1, ea1, sm1, alog_ref, dtb_ref, y0_ref, y1_ref, s_scr) = refs
    nh, hp, ns = SSD_HEADS, SSD_HEAD_DIM, SSD_STATE
    gh = nh // SSD_GROUPS

    @pl.when(pl.program_id(1) == 0)
    def _():
        s_scr[...] = jnp.zeros_like(s_scr)

    neg_a = -jnp.exp(alog_ref[...])
    for d, (bc, yp, xw, ea, sm, y_ref) in enumerate([(bc0, yp0, xw0, ea0, sm0, y0_ref),
                                                      (bc1, yp1, xw1, ea1, sm1, y1_ref)]):
        c_dec = jnp.exp(jnp.sum(_softplus(sm[...] + dtb_ref[...]) * neg_a, axis=0, keepdims=True))
        e_acum = ea[...]
        ys, states = [], []
        for g in range(SSD_GROUPS):
            bg = bc[:, g * ns:(g + 1) * ns]
            cg = bc[:, SSD_GROUPS * ns + g * ns:SSD_GROUPS * ns + (g + 1) * ns]
            s_g = s_scr[d, :, g * gh * hp:(g + 1) * gh * hp]
            y_int = _dot(cg, s_g.astype(BF16))
            upd = _dot_tn(bg, xw[:, g * gh * hp:(g + 1) * gh * hp])
            for hh in range(gh):
                h = g * gh + hh
                ln = d * nh + h
                sl = slice(hh * hp, (hh + 1) * hp)
                ys.append(yp[:, h * hp:(h + 1) * hp] + y_int[:, sl] * e_acum[:, ln:ln + 1])
                states.append(s_g[:, sl] * c_dec[:, ln:ln + 1] + upd[:, sl])
        y_ref[...] = jnp.concatenate(ys, axis=1)
        s_scr[d] = jnp.concatenate(states, axis=1)


def _ssd_call(xbc, small, a_log, dt_bias, d_skip, *, n_b, lx, lc):
    rows = xbc.shape[0]
    nxc, ncc = lx // CHUNK, lc // CHUNK
    blk = _chunk_block(n_b, nxc, ncc)
    nh, di = SSD_HEADS, SSD_D_INNER
    alog_v, dtb_v = _lane_vec(a_log, 0), _lane_vec(dt_bias, 0)
    ypart, xw, ea = pl.pallas_call(
        _ssd_prep_kernel,
        grid=(rows // CHUNK,),
        in_specs=[pl.BlockSpec((CHUNK, SSD_CONV_DIM), lambda i: (i, 0)),
                  pl.BlockSpec((CHUNK, P_SMALL), lambda i: (i, 0)),
                  pl.BlockSpec((1, P_SMALL), lambda i: (0, 0)),
                  pl.BlockSpec((1, P_SMALL), lambda i: (0, 0)),
                  pl.BlockSpec((1, nh), lambda i: (0, 0))],
        out_specs=[pl.BlockSpec((CHUNK, 2 * di), lambda i: (i, 0)),
                   pl.BlockSpec((CHUNK, 2 * di), lambda i: (i, 0)),
                   pl.BlockSpec((CHUNK, 2 * P_SMALL), lambda i: (i, 0))],
        out_shape=[jax.ShapeDtypeStruct((rows, 2 * di), F32),
                   jax.ShapeDtypeStruct((rows, 2 * di), BF16),
                   jax.ShapeDtypeStruct((rows, 2 * P_SMALL), F32)],
        compiler_params=_cp(("parallel",), 40),
        name="ssd_prep",
    )(xbc, small, alog_v, dtb_v, d_skip.reshape(1, nh))

    def dir_specs(d):
        at = lambda b, c: blk(b, d, c)
        return [pl.BlockSpec((CHUNK, SSD_CONV_DIM - di), lambda b, c: (at(b, c), 1)),
                pl.BlockSpec((CHUNK, di), lambda b, c: (at(b, c), d)),
                pl.BlockSpec((CHUNK, di), lambda b, c: (at(b, c), d)),
                pl.BlockSpec((CHUNK, P_SMALL), lambda b, c: (at(b, c), d)),
                pl.BlockSpec((CHUNK, P_SMALL), lambda b, c: (at(b, c), 0))]

    const = pl.BlockSpec((1, P_SMALL), lambda b, c: (0, 0))
    return pl.pallas_call(
        _ssd_scan_kernel,
        grid=(n_b, ncc + nxc),
        in_specs=dir_specs(0) + dir_specs(1) + [const, const],
        out_specs=[pl.BlockSpec((CHUNK, di), lambda b, c: (blk(b, 0, c), 0)),
                   pl.BlockSpec((CHUNK, di), lambda b, c: (blk(b, 1, c), 0))],
        out_shape=[jax.ShapeDtypeStruct((rows, di), F32), jax.ShapeDtypeStruct((rows, di), F32)],
        scratch_shapes=[pltpu.VMEM((2, SSD_STATE, di), F32)],
        compiler_params=_cp(("parallel", "arbitrary"), 40),
        name="ssd_scan",
    )(xbc, ypart, xw, ea, small, xbc, ypart, xw, ea, small, alog_v, dtb_v)


SOLVE_BLOCK = 16


def _unit_tri_solve_many(n_mats, rhss):
    ln = n_mats[0].shape[0]
    row = lax.broadcasted_iota(jnp.int32, (ln, ln), 0)
    col = lax.broadcasted_iota(jnp.int32, (ln, ln), 1)
    on_diag_block = (row // SOLVE_BLOCK) == (col // SOLVE_BLOCK)
    eye = jnp.where(row == col, 1.0, 0.0)
    mm = lambda a, b: _dot(a.astype(BF16), b.astype(BF16))
    ms = [jnp.where(on_diag_block, -n, 0.0) for n in n_mats]
    es = [jnp.where(on_diag_block, 0.0, n) for n in n_mats]
    ps = [eye + m for m in ms]
    mps = ms
    k = 1
    while 2 * k < SOLVE_BLOCK:
        mps = [mm(x, x) for x in mps]
        ps = [p + mm(p, x) for p, x in zip(ps, mps)]
        k *= 2
    f_pows = [[-mm(p, e) for p, e in zip(ps, es)]]
    ys = [mm(p, r) for p, r in zip(ps, rhss)]
    k = 1
    while 2 * k < ln // SOLVE_BLOCK:
        f_pows.append([mm(f, f) for f in f_pows[-1]])
        k *= 2
    for fl in reversed(f_pows):
        ys = [y + mm(f, y) for f, y in zip(fl, ys)]
    return ys


GDN_PACK = 5 * GDN_DIM
GDN_G_LANE = 2 * SSD_HEADS
GDN_B_LANE = 2 * SSD_HEADS + 2 * GDN_HEADS


def _gdn_prep_kernel(qkv_ref, sm_ref, alog_ref, dtb_ref, o_ref):
    nh, hd = GDN_HEADS, GDN_HEAD_DIM
    sm = sm_ref[...]
    g2 = -jnp.exp(alog_ref[...]) * _softplus(sm + dtb_ref[...])
    beta2 = _sigmoid(sm)
    row = lax.broadcasted_iota(jnp.int32, (CHUNK, CHUNK), 0)
    col = lax.broadcasted_iota(jnp.int32, (CHUNK, CHUNK), 1)
    incl = [row >= col, row <= col]
    strict = [row > col, row < col]
    cs = [jnp.dot(m.astype(F32), g2, preferred_element_type=F32, precision=HIGHEST) for m in incl]
    cs_t = [x.T for x in cs]
    gtot2 = jnp.sum(g2, axis=0, keepdims=True)
    qn, kn, kn_b, vv, qk_raw = [], [], [], [], []
    for h in range(nh):
        qh = qkv_ref[:, h * hd:(h + 1) * hd].astype(F32)
        kh = qkv_ref[:, GDN_DIM + h * hd:GDN_DIM + (h + 1) * hd].astype(F32)
        vv.append(qkv_ref[:, 2 * GDN_DIM + h * hd:2 * GDN_DIM + (h + 1) * hd].astype(F32))
        qn.append(qh * (lax.rsqrt(jnp.sum(qh * qh, axis=-1, keepdims=True) + EPS) * hd ** -0.5))
        kn.append(kh * lax.rsqrt(jnp.sum(kh * kh, axis=-1, keepdims=True) + EPS))
        kn_b.append(kn[h].astype(BF16))
        qk_raw.append(_dot_nt(qn[h].astype(BF16), kn_b[h]))
    n_mats, rhss, qks, qds, kds = [], [], [], [], []
    for d in range(2):
        for h in range(nh):
            lg, lb = GDN_G_LANE + d * nh + h, GDN_B_LANE + d * nh + h
            gcc, gcr = cs[d][:, lg:lg + 1], cs_t[d][lg:lg + 1, :]
            beta, gtot = beta2[:, lb:lb + 1], gtot2[:, lg:lg + 1]
            dec = jnp.where(incl[d], jnp.exp(jnp.where(incl[d], gcc - gcr, 0.0)), 0.0)
            kb = kn[h] * beta
            n_mats.append(jnp.where(strict[d], _dot_nt(kb.astype(BF16), kn_b[h]) * dec, 0.0))
            e_gc = jnp.exp(gcc)
            rhss.append(jnp.concatenate([vv[h] * beta, kb * e_gc], axis=1))
            qks.append(qk_raw[h] * dec)
            qds.append(qn[h] * e_gc)
            kds.append(kn[h] * jnp.exp(gtot - gcc))
    sols = _unit_tri_solve_many(n_mats, rhss)
    pieces = []
    for d in range(2):
        js = range(d * nh, (d + 1) * nh)
        pieces += [sols[j][:, 0:hd] for j in js] + [sols[j][:, hd:2 * hd] for j in js]
        pieces += [qks[j] for j in js] + [qds[j] for j in js] + [kds[j] for j in js]
    o_ref[...] = jnp.concatenate([p.astype(BF16) for p in pieces], axis=1)


def _gdn_scan_kernel(pk0, sm0, pk1, sm1, alog_ref, dtb_ref, o0_ref, o1_ref, s_scr):
    nh, hd = GDN_HEADS, GDN_HEAD_DIM

    @pl.when(pl.program_id(1) == 0)
    def _():
        s_scr[...] = jnp.zeros_like(s_scr)

    neg_a = -jnp.exp(alog_ref[...])
    for d, (pk_ref, sm_ref, o_ref) in enumerate([(pk0, sm0, o0_ref), (pk1, sm1, o1_ref)]):
        g_end2 = jnp.exp(jnp.sum(neg_a * _softplus(sm_ref[...] + dtb_ref[...]), axis=0, keepdims=True))
        outs, states = [], []
        for h in range(nh):
            lg = GDN_G_LANE + d * nh + h
            part = lambda j: pk_ref[:, (j * nh + h) * hd:(j * nh + h + 1) * hd]
            u, w, qk, qd, kd = part(0), part(1), part(2), part(3), part(4)
            s_h = s_scr[d, :, h * hd:(h + 1) * hd]
            s_b = s_h.astype(BF16)
            v_new = (u.astype(F32) - _dot(w, s_b)).astype(BF16)
            outs.append(_dot(qd, s_b) + _dot(qk, v_new))
            states.append(s_h * g_end2[:, lg:lg + 1] + _dot_tn(kd, v_new))
        o_ref[...] = jnp.concatenate(outs, axis=1)
        s_scr[d] = jnp.concatenate(states, axis=1)


def _gdn_call(qkv, small, a_log, dt_bias, *, n_b, lx, lc):
    rows = qkv.shape[0]
    nxc, ncc = lx // CHUNK, lc // CHUNK
    blk = _chunk_block(n_b, nxc, ncc)
    alog_v, dtb_v = _lane_vec(a_log, GDN_G_LANE), _lane_vec(dt_bias, GDN_G_LANE)
    packed = pl.pallas_call(
        _gdn_prep_kernel,
        grid=(rows // CHUNK,),
        in_specs=[pl.BlockSpec((CHUNK, 3 * GDN_DIM), lambda i: (i, 0)),
                  pl.BlockSpec((CHUNK, P_SMALL), lambda i: (i, 0)),
                  pl.BlockSpec((1, P_SMALL), lambda i: (0, 0)),
                  pl.BlockSpec((1, P_SMALL), lambda i: (0, 0))],
        out_specs=pl.BlockSpec((CHUNK, 2 * GDN_PACK), lambda i: (i, 0)),
        out_shape=jax.ShapeDtypeStruct((rows, 2 * GDN_PACK), BF16),
        compiler_params=_cp(("parallel",), 40),
        name="gdn_prep",
    )(qkv, small, alog_v, dtb_v)
    def dir_specs(d):
        return [pl.BlockSpec((CHUNK, GDN_PACK), lambda b, c: (blk(b, d, c), d)),
                pl.BlockSpec((CHUNK, P_SMALL), lambda b, c: (blk(b, d, c), 0))]

    const = pl.BlockSpec((1, P_SMALL), lambda b, c: (0, 0))
    return pl.pallas_call(
        _gdn_scan_kernel,
        grid=(n_b, ncc + nxc),
        in_specs=dir_specs(0) + dir_specs(1) + [const, const],
        out_specs=[pl.BlockSpec((CHUNK, GDN_DIM), lambda b, c: (blk(b, 0, c), 0)),
                   pl.BlockSpec((CHUNK, GDN_DIM), lambda b, c: (blk(b, 1, c), 0))],
        out_shape=[jax.ShapeDtypeStruct((rows, GDN_DIM), F32), jax.ShapeDtypeStruct((rows, GDN_DIM), F32)],
        scratch_shapes=[pltpu.VMEM((2, GDN_HEAD_DIM, GDN_DIM), F32)],
        compiler_params=_cp(("parallel", "arbitrary"), 40),
        name="gdn_scan",
    )(packed, small, packed, small, alog_v, dtb_v)


MIXOUT_SUB_ROWS = 256


def _mixout_kernel(h_ref, mod_ref, gpost_ref, ya_ref, ys0_ref, ys1_ref, zs_ref, sn_ref, yn_ref, og0_ref, og1_ref,
                   zg_ref, gn_ref, w_ref, o_ref):
    hd = GDN_HEAD_DIM
    tm = h_ref.shape[0]
    sub = min(tm, MIXOUT_SUB_ROWS)
    outs = []
    for s in range(0, tm, sub):
        r = slice(s, s + sub)
        ssd = (ys0_ref[r, :] + ys1_ref[r, :]) * _silu(zs_ref[r, :].astype(F32))
        yb = (ssd * _rms_scale(ssd) * sn_ref[...]).astype(BF16)
        gd = og0_ref[r, :] + og1_ref[r, :]
        zg = _silu(zg_ref[r, :].astype(F32))
        yd = []
        for h in range(GDN_HEADS):
            oh = gd[:, h * hd:(h + 1) * hd]
            yd.append((oh * _rms_scale(oh) * gn_ref[...] * zg[:, h * hd:(h + 1) * hd]).astype(BF16))
        parts = [ya_ref[r, :], yb, yn_ref[r, :]] + yd
        widths = [512, 512, 512] + [hd] * GDN_HEADS
        y = None
        off = 0
        for part, wd in zip(parts, widths):
            t = _dot(part, w_ref[off:off + wd, :])
            y = t if y is None else y + t
            off += wd
        outs.append(h_ref[r, :] + mod_ref[2:3, :] * (y * _rms_scale(y) * gpost_ref[...]))
    o_ref[...] = jnp.concatenate(outs, axis=0)


def _mixout_call(h, mod_l, g_post, ya, ys, proj, ssd_norm, yn, og, gdn_norm, w_out, *, n_b, lx, n_rows, tm):
    d = h.shape[1]
    nxt, per_b = n_b * lx // tm, lx // tm
    mod_idx = lambda i: (jnp.where(i < nxt, i // per_b, n_b), 0, 0)
    row = lambda i: (i, 0)
    const = lambda i: (0, 0)
    return pl.pallas_call(
        _mixout_kernel,
        grid=(n_rows // tm,),
        in_specs=[pl.BlockSpec((tm, d), row),
                  pl.BlockSpec((None, 6, d), mod_idx),
                  pl.BlockSpec((1, d), const),
                  pl.BlockSpec((tm, 512), row),
                  pl.BlockSpec((tm, 512), row),
                  pl.BlockSpec((tm, 512), row),
                  pl.BlockSpec((tm, 512), lambda i: (i, P_ZSSD // 512)),
                  pl.BlockSpec((1, 512), const),
                  pl.BlockSpec((tm, 512), row),
                  pl.BlockSpec((tm, 512), row),
                  pl.BlockSpec((tm, 512), row),
                  pl.BlockSpec((tm, 512), lambda i: (i, P_GZ // 512)),
                  pl.BlockSpec((1, GDN_HEAD_DIM), const),
                  pl.BlockSpec((d, d), const)],
        out_specs=pl.BlockSpec((tm, d), row),
        out_shape=jax.ShapeDtypeStruct((n_rows, d), F32),
        compiler_params=_cp(("parallel",), 56),
        name="mix_out",
    )(h, mod_l, g_post.reshape(1, d), ya, ys[0], ys[1], proj, ssd_norm.reshape(1, 512), yn, og[0], og[1], proj,
      gdn_norm.reshape(1, GDN_HEAD_DIM), w_out)


def _ffn_kernel(h_ref, mod_ref, gpre_ref, gpost_ref, wg_ref, wu_ref, wd_ref, o_ref, u_scr, acc_scr):
    j = pl.program_id(1)

    @pl.when(j == 0)
    def _():
        x = h_ref[...]
        y = x * _rms_scale(x) * gpre_ref[...]
        u_scr[...] = (y * (1.0 + mod_ref[4:5, :]) + mod_ref[3:4, :]).astype(BF16)
        acc_scr[...] = jnp.zeros_like(acc_scr)

    u = u_scr[...]
    mid = (_silu(_dot(u, wg_ref[...])) * _dot(u, wu_ref[...])).astype(BF16)
    acc_scr[...] += _dot(mid, wd_ref[...])

    @pl.when(j == pl.num_programs(1) - 1)
    def _():
        y = acc_scr[...]
        o_ref[...] = h_ref[...] + mod_ref[5:6, :] * (y * _rms_scale(y) * gpost_ref[...])


def _ffn_call(h, mod_l, g_pre, g_post, wg, wu, wd, *, n_b, lx, tm, tf):
    rows, d = h.shape
    ff = wg.shape[1]
    nxt, per_b = n_b * lx // tm, lx // tm
    mod_idx = lambda i, j: (jnp.where(i < nxt, i // per_b, n_b), 0, 0)
    return pl.pallas_call(
        _ffn_kernel,
        grid=(rows // tm, ff // tf),
        in_specs=[pl.BlockSpec((tm, d), lambda i, j: (i, 0)),
                  pl.BlockSpec((None, 6, d), mod_idx),
                  pl.BlockSpec((1, d), lambda i, j: (0, 0)),
                  pl.BlockSpec((1, d), lambda i, j: (0, 0)),
                  pl.BlockSpec((d, tf), lambda i, j: (0, j)),
                  pl.BlockSpec((d, tf), lambda i, j: (0, j)),
                  pl.BlockSpec((tf, d), lambda i, j: (j, 0))],
        out_specs=pl.BlockSpec((tm, d), lambda i, j: (i, 0)),
        out_shape=jax.ShapeDtypeStruct((rows, d), F32),
        scratch_shapes=[pltpu.VMEM((tm, d), BF16), pltpu.VMEM((tm, d), F32)],
        compiler_params=_cp(("parallel", "arbitrary"), 56),
        name="ffn_swiglu",
    )(h, mod_l, g_pre.reshape(1, d), g_post.reshape(1, d), wg, wu, wd)


MOE_ALIGN = 16
MOE_GROUP_TILE = 512


def _moe_local_rows(ts):
    return -(-(TOP_K * ts + N_EXPERTS * MOE_ALIGN) // 128) * 128


def _moe_route_kernel(h_ref, mod_ref, gpre_ref, wr_ref, u_ref, dest_ref, gate_ref, cnt_ref):
    ts, lanes = dest_ref.shape
    x = h_ref[...]
    y = x * _rms_scale(x) * gpre_ref[...]
    u = y * (1.0 + mod_ref[4:5, :]) + mod_ref[3:4, :]
    u_ref[...] = u.astype(BF16)
    logits = jnp.dot(u, wr_ref[...], preferred_element_type=F32, precision=HIGHEST)
    lane = lax.broadcasted_iota(jnp.int32, logits.shape, 1).astype(F32)
    lg = jnp.where(lane < N_EXPERTS, logits, NEG)
    m1 = jnp.max(lg, axis=-1, keepdims=True)
    i1 = jnp.min(jnp.where(lg == m1, lane, float(lanes)), axis=-1, keepdims=True)
    lg2 = jnp.where(lane == i1, NEG, lg)
    m2 = jnp.max(lg2, axis=-1, keepdims=True)
    i2 = jnp.min(jnp.where(lg2 == m2, lane, float(lanes)), axis=-1, keepdims=True)
    e2 = jnp.exp(m2 - m1)
    gate_ref[...] = jnp.where(lane == 0.0, 1.0 / (1.0 + e2), jnp.where(lane == 1.0, e2 / (1.0 + e2), 0.0))
    sel = jnp.where(lane == i1, 1.0, 0.0) + jnp.where(lane == i2, 1.0, 0.0)
    cnt = jnp.sum(sel, axis=0, keepdims=True)
    cnt_al = jnp.floor((cnt + (MOE_ALIGN - 1)) / MOE_ALIGN) * MOE_ALIGN
    cnt_ref[...] = jnp.broadcast_to(cnt_al, cnt_ref.shape)
    before = lax.broadcasted_iota(jnp.int32, (lanes, lanes), 0) < lax.broadcasted_iota(jnp.int32, (lanes, lanes), 1)
    seg_lo = jnp.dot(jnp.broadcast_to(cnt_al, (8, lanes)), before.astype(F32),
                     preferred_element_type=F32, precision=HIGHEST)[0:1]
    earlier = lax.broadcasted_iota(jnp.int32, (ts, ts), 1) < lax.broadcasted_iota(jnp.int32, (ts, ts), 0)
    rank = _dot(earlier.astype(BF16), sel.astype(BF16))
    slot = seg_lo + rank
    slot_1 = jnp.sum(jnp.where(lane == i1, slot, 0.0), axis=-1, keepdims=True)
    slot_2 = jnp.sum(jnp.where(lane == i2, slot, 0.0), axis=-1, keepdims=True)
    dest_ref[...] = jnp.where(lane == 0.0, slot_1, jnp.where(lane == 1.0, slot_2, -1.0))


def _moe_segment_copies(i, off_ref, n_ref, hbm_ref, loc_ref, sem, *, to_hbm):
    lo = jnp.int32(0)
    for e in range(N_EXPERTS):
        n_chunks = n_ref[i * N_EXPERTS + e]
        go = off_ref[i * N_EXPERTS + e]

        def body(k, carry, lo=lo, go=go):
            loc = loc_ref.at[pl.ds(pl.multiple_of(lo + k * MOE_ALIGN, MOE_ALIGN), MOE_ALIGN), :]
            far = hbm_ref.at[pl.ds(pl.multiple_of(go + k * MOE_ALIGN, MOE_ALIGN), MOE_ALIGN), :]
            src, dst = (loc, far) if to_hbm else (far, loc)
            pltpu.make_async_copy(src, dst, sem).start()
            return carry

        lax.fori_loop(0, n_chunks, body, 0)
        lo = lo + n_chunks * MOE_ALIGN
    return lo // MOE_ALIGN


def _moe_wait_copies(n_chunks, hbm_ref, loc_ref, sem, *, to_hbm):
    def body(k, carry):
        loc = loc_ref.at[pl.ds(0, MOE_ALIGN), :]
        far = hbm_ref.at[pl.ds(0, MOE_ALIGN), :]
        src, dst = (loc, far) if to_hbm else (far, loc)
        pltpu.make_async_copy(src, dst, sem).wait()
        return carry

    lax.fori_loop(0, n_chunks, body, 0)


def _moe_sort_kernel(off_ref, n_ref, u_ref, dest_ref, init_ref, us_ref, loc_scr, sem):
    del init_ref
    i = pl.program_id(0)
    dest = dest_ref[...]
    slot = lax.broadcasted_iota(jnp.int32, (dest.shape[0], loc_scr.shape[0]), 1).astype(F32)
    onehot = jnp.where(slot == dest[:, 0:1], 1.0, jnp.where(slot == dest[:, 1:2], 1.0, 0.0))
    loc_scr[...] = _dot_tn(onehot.astype(BF16), u_ref[...]).astype(BF16)
    n = _moe_segment_copies(i, off_ref, n_ref, us_ref, loc_scr, sem, to_hbm=True)
    _moe_wait_copies(n, us_ref, loc_scr, sem, to_hbm=True)


def _moe_expert_kernel(te_ref, tv_ref, u_ref, wg_ref, wu_ref, wd_ref, y_ref, acc_scr):
    k, j = pl.program_id(0), pl.program_id(1)
    last = pl.num_programs(1) - 1
    valid = tv_ref[k] > 0

    @pl.when(valid)
    def _():
        @pl.when(j == 0)
        def _():
            acc_scr[...] = jnp.zeros_like(acc_scr)

        u = u_ref[...]
        mid = (_silu(_dot(u, wg_ref[...])) * _dot(u, wu_ref[...])).astype(BF16)
        acc_scr[...] += _dot(mid, wd_ref[...])

        @pl.when(j == last)
        def _():
            y_ref[...] = acc_scr[...].astype(BF16)

    @pl.when(jnp.logical_not(valid) & (j == last))
    def _():
        y_ref[...] = jnp.zeros_like(y_ref)


def _moe_combine_kernel(off_ref, n_ref, h_ref, mod_ref, gpost_ref, dest_ref, gate_ref, ys_ref, o_ref, loc_scr, sem):
    i = pl.program_id(0)
    n = _moe_segment_copies(i, off_ref, n_ref, ys_ref, loc_scr, sem, to_hbm=False)
    dest, gate = dest_ref[...], gate_ref[...]
    slot = lax.broadcasted_iota(jnp.int32, (dest.shape[0], loc_scr.shape[0]), 1).astype(F32)
    w = jnp.where(slot == dest[:, 0:1], gate[:, 0:1], jnp.where(slot == dest[:, 1:2], gate[:, 1:2], 0.0))
    w_hi = w.astype(BF16)
    w_lo = (w - w_hi.astype(F32)).astype(BF16)
    _moe_wait_copies(n, ys_ref, loc_scr, sem, to_hbm=False)
    filled = lax.broadcasted_iota(jnp.int32, (loc_scr.shape[0], 1), 0) < n * MOE_ALIGN
    y_loc = jnp.where(filled, loc_scr[...], jnp.zeros_like(loc_scr))
    y = _dot(w_hi, y_loc) + _dot(w_lo, y_loc)
    o_ref[...] = h_ref[...] + mod_ref[5:6, :] * (y * _rms_scale(y) * gpost_ref[...])


def _moe_call(h, mod_l, g_pre, g_post, router_pad, wg, wu, wd, *, n_b, lx, tm, tf):
    rows, d = h.shape
    ne, _, fe = wg.shape
    ts, tg = tm, MOE_GROUP_TILE
    n_tiles = rows // ts
    lrows = _moe_local_rows(ts)
    nt_max = -(-(TOP_K * rows + n_tiles * ne * (MOE_ALIGN - 1)) // tg) + ne
    n_pad = nt_max * tg
    nxt, per_b = n_b * lx // ts, lx // ts
    mod_row = lambda i: jnp.where(i < nxt, i // per_b, n_b)

    u, dest, gate, cnt = pl.pallas_call(
        _moe_route_kernel,
        grid=(n_tiles,),
        in_specs=[pl.BlockSpec((ts, d), lambda i: (i, 0)),
                  pl.BlockSpec((None, 6, d), lambda i: (mod_row(i), 0, 0)),
                  pl.BlockSpec((1, d), lambda i: (0, 0)),
                  pl.BlockSpec((d, 128), lambda i: (0, 0))],
        out_specs=[pl.BlockSpec((ts, d), lambda i: (i, 0)),
                   pl.BlockSpec((ts, 128), lambda i: (i, 0)),
                   pl.BlockSpec((ts, 128), lambda i: (i, 0)),
                   pl.BlockSpec((None, 8, 128), lambda i: (i, 0, 0))],
        out_shape=[jax.ShapeDtypeStruct((rows, d), BF16),
                   jax.ShapeDtypeStruct((rows, 128), F32),
                   jax.ShapeDtypeStruct((rows, 128), F32),
                   jax.ShapeDtypeStruct((n_tiles, 8, 128), F32)],
        compiler_params=_cp(("parallel",), 40),
        name="moe_route",
    )(h, mod_l, g_pre.reshape(1, d), router_pad)

    cnt_al = cnt[:, 0, :ne].astype(jnp.int32)
    group = -(-jnp.sum(cnt_al, axis=0) // tg) * tg
    group_end = jnp.cumsum(group)
    seg_off = (group_end - group)[None, :] + jnp.cumsum(cnt_al, axis=0) - cnt_al
    n_used = group_end[-1] // tg
    tile_id = jnp.arange(nt_max, dtype=jnp.int32)
    tile_valid = (tile_id < n_used).astype(jnp.int32)
    tile_expert = jnp.searchsorted(group_end // tg, jnp.minimum(tile_id, n_used - 1), side='right')
    tile_expert = jnp.minimum(tile_expert, ne - 1).astype(jnp.int32)
    seg_off = seg_off.reshape(-1).astype(jnp.int32)
    seg_chunks = (cnt_al // MOE_ALIGN).reshape(-1)

    u_sorted = pl.pallas_call(
        _moe_sort_kernel,
        grid_spec=pltpu.PrefetchScalarGridSpec(
            num_scalar_prefetch=2,
            grid=(n_tiles,),
            in_specs=[pl.BlockSpec((ts, d), lambda i, o_r, n_r: (i, 0)),
                      pl.BlockSpec((ts, 128), lambda i, o_r, n_r: (i, 0)),
                      pl.BlockSpec(memory_space=pl.ANY)],
            out_specs=pl.BlockSpec(memory_space=pl.ANY),
            scratch_shapes=[pltpu.VMEM((lrows, d), BF16), pltpu.SemaphoreType.DMA(())]),
        out_shape=jax.ShapeDtypeStruct((n_pad, d), BF16),
        input_output_aliases={4: 0},
        compiler_params=_cp(("arbitrary",), 40),
        name="moe_sort",
    )(seg_off, seg_chunks, u, dest, jnp.zeros((n_pad, d), BF16))

    nj = fe // tf
    w_col = lambda k, j, te_r, tv_r: jnp.where(tv_r[k] > 0, j, nj - 1)
    y_sorted = pl.pallas_call(
        _moe_expert_kernel,
        grid_spec=pltpu.PrefetchScalarGridSpec(
            num_scalar_prefetch=2,
            grid=(nt_max, nj),
            in_specs=[pl.BlockSpec((tg, d), lambda k, j, te_r, tv_r: (k, 0)),
                      pl.BlockSpec((None, d, tf), lambda k, j, te_r, tv_r: (te_r[k], 0, w_col(k, j, te_r, tv_r))),
                      pl.BlockSpec((None, d, tf), lambda k, j, te_r, tv_r: (te_r[k], 0, w_col(k, j, te_r, tv_r))),
                      pl.BlockSpec((None, tf, d), lambda k, j, te_r, tv_r: (te_r[k], w_col(k, j, te_r, tv_r), 0))],
            out_specs=pl.BlockSpec((tg, d), lambda k, j, te_r, tv_r: (k, 0)),
            scratch_shapes=[pltpu.VMEM((tg, d), F32)]),
        out_shape=jax.ShapeDtypeStruct((n_pad, d), BF16),
        compiler_params=_cp(("parallel", "arbitrary"), 48),
        name="moe_experts",
    )(tile_expert, tile_valid, u_sorted, wg, wu, wd)

    return pl.pallas_call(
        _moe_combine_kernel,
        grid_spec=pltpu.PrefetchScalarGridSpec(
            num_scalar_prefetch=2,
            grid=(n_tiles,),
            in_specs=[pl.BlockSpec((ts, d), lambda i, o_r, n_r: (i, 0)),
                      pl.BlockSpec((None, 6, d), lambda i, o_r, n_r: (mod_row(i), 0, 0)),
                      pl.BlockSpec((1, d), lambda i, o_r, n_r: (0, 0)),
                      pl.BlockSpec((ts, 128), lambda i, o_r, n_r: (i, 0)),
                      pl.BlockSpec((ts, 128), lambda i, o_r, n_r: (i, 0)),
                      pl.BlockSpec(memory_space=pl.ANY)],
            out_specs=pl.BlockSpec((ts, d), lambda i, o_r, n_r: (i, 0)),
            scratch_shapes=[pltpu.VMEM((lrows, d), BF16), pltpu.SemaphoreType.DMA(())]),
        out_shape=jax.ShapeDtypeStruct((rows, d), F32),
        compiler_params=_cp(("arbitrary",), 48),
        name="moe_combine",
    )(seg_off, seg_chunks, h, mod_l, g_post.reshape(1, d), dest, gate, y_sorted)


def _rope_tables(lx, tm):
    half = MLA_ROPE // 2
    n_axis = half // 2
    inv_freq = ROPE_THETA ** (-jnp.arange(n_axis, dtype=F32) / n_axis)
    pos = jnp.arange(lx)
    rows = (pos // GRID_W).astype(F32)
    cols = (pos % GRID_W).astype(F32)
    ang = jnp.concatenate([rows[:, None] * inv_freq, cols[:, None] * inv_freq], axis=-1)
    cos, sin = jnp.cos(ang), jnp.sin(ang)
    cos_t = jnp.concatenate([cos, cos], axis=-1)
    sin_t = jnp.concatenate([-sin, sin], axis=-1)
    cos_t = jnp.concatenate([cos_t, jnp.ones((tm, MLA_ROPE), F32)], axis=0)
    sin_t = jnp.concatenate([sin_t, jnp.zeros((tm, MLA_ROPE), F32)], axis=0)
    return cos_t, sin_t


def _mla_weights(q_norm, w_uq, kv_norm, w_ukv):
    dq = MLA_NOPE + MLA_ROPE
    half = MLA_ROPE // 2
    cols = lambda a, b: w_uq[:, a:b]
    nope = [cols(h * dq, h * dq + MLA_NOPE) for h in range(MLA_HEADS)]
    rope = [cols(h * dq + MLA_NOPE, (h + 1) * dq) for h in range(MLA_HEADS)]
    rope_sw = [cols(h * dq + MLA_NOPE + s * half, h * dq + MLA_NOPE + (s + 1) * half)
               for h in range(MLA_HEADS) for s in (1, 0)]
    pad = ((0, MLA_ROPE), (0, 0))
    wq = jnp.pad(jnp.concatenate(nope + rope, axis=1), pad).astype(BF16)
    wqs = jnp.pad(jnp.concatenate(rope_sw, axis=1), pad).astype(BF16)
    qn_ext = jnp.pad(q_norm, (0, MLA_ROPE)).reshape(1, -1)
    perm = np.zeros((MLA_ROPE, MLA_ROPE), np.float32)
    perm[(np.arange(MLA_ROPE) + half) % MLA_ROPE, np.arange(MLA_ROPE)] = 1.0
    return qn_ext, wq, wqs, kv_norm.reshape(1, -1), w_ukv.astype(BF16), jnp.asarray(perm, BF16)


def _pick_tile(n, cands):
    for t in cands:
        if n % t == 0:
            return t
    raise ValueError(f"no tile for {n}")


def kernel(x, c, ctx, c_ctx, w_ada, b_ada, g_pre_mix, g_post_mix, g_pre_ffn, g_post_ffn, w_in, w_out, mla_q_norm, mla_w_uq, mla_kv_norm, mla_w_ukv, ssd_conv_w, ssd_conv_b, ssd_a_log, ssd_dt_bias, ssd_d, ssd_norm, na_rpb, gdn_conv_w, gdn_a_log, gdn_dt_bias, gdn_norm, ffn_w_gate, ffn_w_up, ffn_w_down, moe_router, moe_w_gate, moe_w_up, moe_w_down):
    n_b, lx, d = x.shape
    lc = ctx.shape[1]
    depth = w_ada.shape[0]
    rows_x, rows_c = n_b * lx, n_b * lc
    assert n_b + 1 <= 8 and lx % GRID_W == 0 and lx % lc == 0 and lc % CHUNK == 0
    tm = _pick_tile(math.gcd(lx, rows_c), (512, 256, 128))
    tm_in = _pick_tile(math.gcd(lx, rows_c), (1024, 512, 256, 128))
    tq = _pick_tile(lc, (256, 128))
    tr = _pick_tile(lc, (256, 128))

    cvec = jnp.concatenate([c, c_ctx[None, :], jnp.zeros((8 - n_b - 1, d), F32)], axis=0)
    mod = _ada_call(cvec, w_ada, b_ada).reshape(depth, 8, 6, d)
    cos_t, sin_t = _rope_tables(lx, tm)
    na_bias = _na_bias_table(na_rpb.reshape((-1,) + na_rpb.shape[2:]), lx // GRID_W, lc)
    h_all = jnp.concatenate([x.reshape(rows_x, d), ctx.reshape(rows_c, d)], axis=0)

    for i in range(depth):
        need_ctx = i < depth - 1
        w_main, w_small = _regroup_w_in(w_in[i])
        proj, small = _inproj_call(h_all, mod[i], g_pre_mix[i], w_main, w_small, n_b=n_b, lx=lx, tm=tm_in)

        mla_w = _mla_weights(mla_q_norm[i], mla_w_uq[i], mla_kv_norm[i], mla_w_ukv[i])
        q_a, k_a, v_a = _mla_prep_call(proj, *mla_w[:5], cos_t, sin_t, mla_w[5], n_b=n_b, lx=lx, tm=tm)
        ya = _mla_attn_call(q_a, k_a, v_a, n_b=n_b, lx=lx, lc=lc, tq=tq, need_ctx=need_ctx)

        xbc = _conv_call(proj, ssd_conv_w[i], ssd_conv_b[i], col_off=P_XBC, n_b=n_b, lx=lx, lc=lc, tr=tr)
        ys = _ssd_call(xbc, small, ssd_a_log[i], ssd_dt_bias[i], ssd_d[i], n_b=n_b, lx=lx, lc=lc)

        yn = _na_call(proj, na_bias, layer=i, n_b=n_b, lx=lx, lc=lc, need_ctx=need_ctx)

        qkv = _conv_call(proj, gdn_conv_w[i], jnp.zeros((3 * GDN_DIM,), F32), col_off=P_GQKV,
                         n_b=n_b, lx=lx, lc=lc, tr=tr)
        og = _gdn_call(qkv, small, gdn_a_log[i], gdn_dt_bias[i], n_b=n_b, lx=lx, lc=lc)

        n_rows = rows_x + rows_c if need_ctx else rows_x
        h_mid = _mixout_call(h_all, mod[i], g_post_mix[i], ya, ys, proj, ssd_norm[i], yn, og, gdn_norm[i],
                             w_out[i].astype(BF16), n_b=n_b, lx=lx, n_rows=n_rows, tm=tm)
        j = i // 2
        if i % 2 == 0:
            h_all = _ffn_call(h_mid, mod[i], g_pre_ffn[i], g_post_ffn[i], ffn_w_gate[j].astype(BF16),
                              ffn_w_up[j].astype(BF16), ffn_w_down[j].astype(BF16), n_b=n_b, lx=lx, tm=tm, tf=512)
        else:
            router_pad = jnp.pad(moe_router[j], ((0, 0), (0, 128 - N_EXPERTS)))
            h_all = _moe_call(h_mid, mod[i], g_pre_ffn[i], g_post_ffn[i], router_pad, moe_w_gate[j].astype(BF16),
                              moe_w_up[j].astype(BF16), moe_w_down[j].astype(BF16), n_b=n_b, lx=lx, tm=tm, tf=256)
    return h_all[:rows_x].reshape(n_b, lx, d)
```

```python
import functools
import math

import numpy as np
import jax
import jax.numpy as jnp
from jax import lax
from jax.experimental import pallas as pl
from jax.experimental.pallas import tpu as pltpu

F32 = jnp.float32
BF16 = jnp.bfloat16
HIGHEST = lax.Precision.HIGHEST

GRID_W = 64
EPS = 1e-6
ROPE_THETA = 10000.0
CHUNK = 128
CONV_W = 5
MLA_HEADS, MLA_NOPE, MLA_ROPE, MLA_V = 4, 128, 64, 128
MLA_Q_LORA, MLA_KV_LORA = 448, 128
SSD_HEADS, SSD_HEAD_DIM, SSD_STATE, SSD_GROUPS = 8, 64, 128, 2
SSD_D_INNER = SSD_HEADS * SSD_HEAD_DIM
SSD_CONV_DIM = SSD_D_INNER + 2 * SSD_GROUPS * SSD_STATE
NA_HEADS, NA_HEAD_DIM = 4, 128
NA_DIM = NA_HEADS * NA_HEAD_DIM
NA_WIN_ROWS, NA_WIN_COLS = 8, 16
GDN_HEADS, GDN_HEAD_DIM = 4, 128
GDN_DIM = GDN_HEADS * GDN_HEAD_DIM
N_EXPERTS, TOP_K = 8, 2
MLA_COLS = MLA_Q_LORA + MLA_KV_LORA + MLA_ROPE
SSD_COLS = SSD_D_INNER + SSD_CONV_DIM + 2 * SSD_HEADS
NA_COLS = 3 * NA_DIM
GDN_COLS = 4 * GDN_DIM + 4 * GDN_HEADS

P_XBC, P_ZSSD, P_NAQ, P_NAK, P_NAV = 0, 1024, 1536, 2048, 2560
P_GQKV, P_GZ, P_MLA = 3072, 4608, 5120
P_MAIN = 5760
P_SMALL = 128
NEG = -1e30
VMEM_MB = 1024 * 1024


def _cp(sem, mb):
    return pltpu.CompilerParams(dimension_semantics=sem, vmem_limit_bytes=mb * VMEM_MB)


def _dot(a, b):
    return jnp.dot(a, b, preferred_element_type=F32)


def _dot_nt(a, b, precision=None):
    return lax.dot_general(a, b, (((1,), (1,)), ((), ())), preferred_element_type=F32, precision=precision)


def _dot_tn(a, b):
    return lax.dot_general(a, b, (((0,), (0,)), ((), ())), preferred_element_type=F32)


def _sigmoid(x):
    return 1.0 / (1.0 + jnp.exp(-x))


def _silu(x):
    return x * _sigmoid(x)


def _softplus(x):
    return jnp.maximum(x, 0.0) + jnp.log(1.0 + jnp.exp(-jnp.abs(x)))


def _rms_scale(x):
    return lax.rsqrt(jnp.mean(x * x, axis=-1, keepdims=True) + EPS)


def _regroup_w_in(w):
    o_mla, o_ssd = 0, MLA_COLS
    o_na, o_gdn = o_ssd + SSD_COLS, o_ssd + SSD_COLS + NA_COLS
    main_segs = [
        (o_ssd + SSD_D_INNER, SSD_CONV_DIM),
        (o_ssd, SSD_D_INNER),
        (o_na, NA_COLS),
        (o_gdn, 4 * GDN_DIM),
        (o_mla + MLA_Q_LORA, MLA_KV_LORA),
        (o_mla, MLA_Q_LORA),
        (o_mla + MLA_Q_LORA + MLA_KV_LORA, MLA_ROPE),
    ]
    small_segs = [(o_ssd + SSD_D_INNER + SSD_CONV_DIM, 2 * SSD_HEADS), (o_gdn + 4 * GDN_DIM, 4 * GDN_HEADS)]
    assert sum(n for _, n in main_segs) == P_MAIN
    main = jnp.concatenate([w[:, a:a + n] for a, n in main_segs], axis=1).astype(BF16)
    n_small = sum(n for _, n in small_segs)
    small = jnp.concatenate([w[:, a:a + n] for a, n in small_segs]
                            + [jnp.zeros((w.shape[0], P_SMALL - n_small), w.dtype)], axis=1).astype(BF16)
    return main, small


def _ada_kernel(c_ref, w_ref, b_ref, o_ref):
    s = _silu(c_ref[...]).astype(BF16)
    o_ref[...] = _dot(s, w_ref[...].astype(BF16)) + b_ref[...]


def _ada_call(cvec, w_ada, b_ada):
    depth, d, n = w_ada.shape
    tn = 1024
    return pl.pallas_call(
        _ada_kernel,
        grid=(depth, n // tn),
        in_specs=[pl.BlockSpec((8, d), lambda l, j: (0, 0)),
                  pl.BlockSpec((None, d, tn), lambda l, j: (l, 0, j)),
                  pl.BlockSpec((None, 1, tn), lambda l, j: (l, 0, j))],
        out_specs=pl.BlockSpec((None, 8, tn), lambda l, j: (l, 0, j)),
        out_shape=jax.ShapeDtypeStruct((depth, 8, n), F32),
        compiler_params=_cp(("parallel", "parallel"), 40),
        name="adaln",
    )(cvec, w_ada, b_ada.reshape(depth, 1, n))


def _inproj_kernel(h_ref, mod_ref, g_ref, w_ref, ws_ref, o_ref, os_ref, u_scr):
    @pl.when(pl.program_id(1) == 0)
    def _():
        x = h_ref[...]
        y = x * _rms_scale(x) * g_ref[...]
        u = (y * (1.0 + mod_ref[1:2, :]) + mod_ref[0:1, :]).astype(BF16)
        u_scr[...] = u
        os_ref[...] = _dot(u, ws_ref[...])

    o_ref[...] = _dot(u_scr[...], w_ref[...]).astype(BF16)


def _inproj_call(h, mod_l, g_pre, w_main, w_small, *, n_b, lx, tm):
    rows, d = h.shape
    tn = 1920
    nxt, per_b = n_b * lx // tm, lx // tm

    def mod_idx(i, j):
        return (jnp.where(i < nxt, i // per_b, n_b), 0, 0)

    return pl.pallas_call(
        _inproj_kernel,
        grid=(rows // tm, P_MAIN // tn),
        in_specs=[pl.BlockSpec((tm, d), lambda i, j: (i, 0)),
                  pl.BlockSpec((None, 6, d), mod_idx),
                  pl.BlockSpec((1, d), lambda i, j: (0, 0)),
                  pl.BlockSpec((d, tn), lambda i, j: (0, j)),
                  pl.BlockSpec((d, P_SMALL), lambda i, j: (0, 0))],
        out_specs=[pl.BlockSpec((tm, tn), lambda i, j: (i, j)),
                   pl.BlockSpec((tm, P_SMALL), lambda i, j: (i, 0))],
        out_shape=[jax.ShapeDtypeStruct((rows, P_MAIN), BF16),
                   jax.ShapeDtypeStruct((rows, P_SMALL), F32)],
        scratch_shapes=[pltpu.VMEM((tm, d), BF16)],
        compiler_params=_cp(("parallel", "arbitrary"), 56),
        name="in_proj",
    )(h, mod_l, g_pre.reshape(1, d), w_main, w_small)


HALO = 16


def _conv_kernel(prev_ref, cur_ref, next_ref, w_ref, b_ref, o_ref, ext_scr, *, tr, blocks_x, seq_x, seq_c):
    i = pl.program_id(0)
    in_x = i < blocks_x
    pos = jnp.where(in_x, i % seq_x, (i - blocks_x) % seq_c)
    last_pos = jnp.where(in_x, seq_x - 1, seq_c - 1)
    ext_scr[0:HALO, :] = jnp.where(pos == 0, 0.0, prev_ref[...].astype(F32))
    ext_scr[HALO:HALO + tr, :] = cur_ref[...].astype(F32)
    ext_scr[HALO + tr:2 * HALO + tr, :] = jnp.where(pos == last_pos, 0.0, next_ref[...].astype(F32))
    acc = b_ref[...] + w_ref[0:1, :] * ext_scr[HALO - 2:HALO - 2 + tr, :]
    for k in range(1, CONV_W):
        acc = acc + w_ref[k:k + 1, :] * ext_scr[HALO - 2 + k:HALO - 2 + k + tr, :]
    o_ref[...] = _silu(acc).astype(BF16)


def _conv_call(proj, w, b, *, col_off, n_b, lx, lc, tr):
    rows = proj.shape[0]
    c = w.shape[1]
    cb = col_off // c
    assert cb * c == col_off
    hb = tr // HALO
    n_halo = rows // HALO
    kern = functools.partial(_conv_kernel, tr=tr, blocks_x=n_b * lx // tr, seq_x=lx // tr, seq_c=lc // tr)
    return pl.pallas_call(
        kern,
        grid=(rows // tr,),
        in_specs=[pl.BlockSpec((HALO, c), lambda i: (jnp.maximum(i * hb - 1, 0), cb)),
                  pl.BlockSpec((tr, c), lambda i: (i, cb)),
                  pl.BlockSpec((HALO, c), lambda i: (jnp.minimum((i + 1) * hb, n_halo - 1), cb)),
                  pl.BlockSpec((CONV_W, c), lambda i: (0, 0)),
                  pl.BlockSpec((1, c), lambda i: (0, 0))],
        out_specs=pl.BlockSpec((tr, c), lambda i: (i, 0)),
        out_shape=jax.ShapeDtypeStruct((rows, c), BF16),
        scratch_shapes=[pltpu.VMEM((tr + 2 * HALO, c), F32)],
        compiler_params=_cp(("parallel",), 40),
        name="dwconv_silu",
    )(proj, proj, proj, w, b.reshape(1, c))


def _mla_prep_kernel(p_ref, qn_ref, wq_ref, wqs_ref, kvn_ref, wkv_ref, cos_ref, sin_ref, perm_ref,
                     q_ref, k_ref, v_ref):
    p = p_ref[...].astype(F32)
    ckv = p[:, 0:MLA_KV_LORA]
    ce = p[:, MLA_KV_LORA:]
    lane = lax.broadcasted_iota(jnp.int32, ce.shape, 1)
    ssq = jnp.sum(jnp.where(lane < MLA_Q_LORA, ce * ce, 0.0), axis=-1, keepdims=True)
    cqn = (ce * lax.rsqrt(ssq / MLA_Q_LORA + EPS) * qn_ref[...]).astype(BF16)
    ckvn = (ckv * _rms_scale(ckv) * kvn_ref[...]).astype(BF16)
    q = _dot(cqn, wq_ref[...])
    qs = _dot(cqn, wqs_ref[...])
    kv = _dot(ckvn, wkv_ref[...])
    cos, sin = cos_ref[...], sin_ref[...]
    kr = p_ref[:, MLA_KV_LORA + MLA_Q_LORA:]
    kr_rot = kr.astype(F32) * cos + _dot(kr, perm_ref[...]) * sin
    nr = MLA_HEADS * MLA_NOPE
    scale = (MLA_NOPE + MLA_ROPE) ** -0.5
    ones_col = jnp.where(lax.broadcasted_iota(jnp.int32, (p.shape[0], MLA_V), 1) == 0, 1.0, 0.0).astype(BF16)
    for h in range(MLA_HEADS):
        q_ref[h, :, 0:MLA_NOPE] = (q[:, h * MLA_NOPE:(h + 1) * MLA_NOPE] * scale).astype(BF16)
        qr = q[:, nr + h * MLA_ROPE:nr + (h + 1) * MLA_ROPE] * cos + qs[:, h * MLA_ROPE:(h + 1) * MLA_ROPE] * sin
        q_ref[h, :, MLA_NOPE:] = (qr * scale).astype(BF16)
        hv = h * (MLA_NOPE + MLA_V)
        k_ref[h, :, 0:MLA_NOPE] = kv[:, hv:hv + MLA_NOPE].astype(BF16)
        k_ref[h, :, MLA_NOPE:] = kr_rot.astype(BF16)
        v_ref[h, :, 0:MLA_V] = kv[:, hv + MLA_NOPE:hv + MLA_NOPE + MLA_V].astype(BF16)
        v_ref[h, :, MLA_V:] = ones_col


def _mla_prep_call(proj, qn_ext, wq, wqs, kvn, wkv, cos_t, sin_t, perm, *, n_b, lx, tm):
    rows = proj.shape[0]
    nxt, per_b = n_b * lx // tm, lx // tm
    dk = MLA_NOPE + MLA_ROPE
    rope_idx = lambda i: (jnp.where(i < nxt, i % per_b, per_b), 0)
    full = lambda a: pl.BlockSpec(a.shape, lambda i: (0,) * a.ndim)
    return pl.pallas_call(
        _mla_prep_kernel,
        grid=(rows // tm,),
        in_specs=[pl.BlockSpec((tm, MLA_COLS), lambda i: (i, P_MLA // MLA_COLS)),
                  full(qn_ext), full(wq), full(wqs), full(kvn), full(wkv),
                  pl.BlockSpec((tm, MLA_ROPE), rope_idx), pl.BlockSpec((tm, MLA_ROPE), rope_idx),
                  full(perm)],
        out_specs=[pl.BlockSpec((MLA_HEADS, tm, dk), lambda i: (0, i, 0)),
                   pl.BlockSpec((MLA_HEADS, tm, dk), lambda i: (0, i, 0)),
                   pl.BlockSpec((MLA_HEADS, tm, 2 * MLA_V), lambda i: (0, i, 0))],
        out_shape=[jax.ShapeDtypeStruct((MLA_HEADS, rows, dk), BF16),
                   jax.ShapeDtypeStruct((MLA_HEADS, rows, dk), BF16),
                   jax.ShapeDtypeStruct((MLA_HEADS, rows, 2 * MLA_V), BF16)],
        compiler_params=_cp(("parallel",), 40),
        name="mla_prep",
    )(proj, qn_ext, wq, wqs, kvn, wkv, cos_t, sin_t, perm)


def _softmax_pv(scores, values):
    m = functools.reduce(jnp.maximum, [jnp.max(s, axis=-1, keepdims=True) for s in scores])
    ps = [jnp.exp(s - m) for s in scores]
    den = functools.reduce(lambda a, b: a + b, [jnp.sum(p, axis=-1, keepdims=True) for p in ps])
    num = functools.reduce(lambda a, b: a + b, [_dot(p.astype(BF16), v) for p, v in zip(ps, values)])
    return num / den


def _softmax_pv_aug(scores, values_aug):
    m = functools.reduce(jnp.maximum, [jnp.max(s, axis=-1, keepdims=True) for s in scores])
    acc = functools.reduce(lambda a, b: a + b,
                           [_dot(jnp.exp((s - m).astype(BF16)), v) for s, v in zip(scores, values_aug)])
    return acc[:, 0:MLA_V] / acc[:, MLA_V:MLA_V + 1]


MLA_HEADS_PER_STEP = 4


def _mla_attn_kernel(q_ref, kx_ref, vx_ref, kc_ref, vc_ref, o_ref, *, nqx):
    qi = pl.program_id(2)
    heads = range(q_ref.shape[0])

    @pl.when(qi < nqx)
    def _():
        outs = [_softmax_pv_aug([_dot_nt(q_ref[h], kx_ref[h]), _dot_nt(q_ref[h], kc_ref[h])], [vx_ref[h], vc_ref[h]])
                for h in heads]
        o_ref[...] = jnp.concatenate(outs, axis=1).astype(BF16)

    @pl.when(qi >= nqx)
    def _():
        outs = [_softmax_pv_aug([_dot_nt(q_ref[h], kc_ref[h])], [vc_ref[h]]) for h in heads]
        o_ref[...] = jnp.concatenate(outs, axis=1).astype(BF16)


def _mla_attn_call(q, k, v, *, n_b, lx, lc, tq, need_ctx):
    rows = q.shape[1]
    dk = q.shape[2]
    nqx, nqc = lx // tq, lc // tq
    nq = nqx + (nqc if need_ctx else 0)
    nbx = n_b * lx // lc

    def q_row(b, qi):
        return jnp.where(qi < nqx, b * nqx + qi, n_b * nqx + b * nqc + (qi - nqx))

    kern = functools.partial(_mla_attn_kernel, nqx=nqx)
    out_rows = rows if need_ctx else n_b * lx
    hp = MLA_HEADS_PER_STEP
    return pl.pallas_call(
        kern,
        grid=(n_b, MLA_HEADS // hp, nq),
        in_specs=[pl.BlockSpec((hp, tq, dk), lambda b, h, qi: (h, q_row(b, qi), 0)),
                  pl.BlockSpec((hp, lx, dk), lambda b, h, qi: (h, b, 0)),
                  pl.BlockSpec((hp, lx, 2 * MLA_V), lambda b, h, qi: (h, b, 0)),
                  pl.BlockSpec((hp, lc, dk), lambda b, h, qi: (h, nbx + b, 0)),
                  pl.BlockSpec((hp, lc, 2 * MLA_V), lambda b, h, qi: (h, nbx + b, 0))],
        out_specs=pl.BlockSpec((tq, hp * MLA_V), lambda b, h, qi: (q_row(b, qi), h)),
        out_shape=jax.ShapeDtypeStruct((out_rows, MLA_HEADS * MLA_V), BF16),
        compiler_params=_cp(("parallel", "parallel", "arbitrary"), 48),
        name="mla_attn",
    )(q, k, v, k, v)


def _na_plan(g_rows, lc):
    wr = min(NA_WIN_ROWS, g_rows)
    rg = next(r for r in (4, 2, 1) if g_rows % r == 0 and lc % (r * GRID_W) == 0)
    wk = min(rg + wr - 1, g_rows)
    n_groups = g_rows // rg
    ks = np.clip(np.arange(n_groups) * rg - wr // 2, 0, g_rows - wk)
    r = np.arange(g_rows)
    rs = np.clip(r - wr // 2, 0, g_rows - wr)
    q_off = (r - np.repeat(ks, rg)).reshape(n_groups, rg)
    rel = (rs - np.repeat(ks, rg)).reshape(n_groups, rg)
    assert (rel >= 0).all() and (rel + wr <= wk).all()
    pats = [tuple(q_off[g]) + tuple(rel[g]) for g in range(n_groups)]
    uniq = sorted(set(pats))
    var = np.array([uniq.index(p) for p in pats], np.int32)
    q_off_v = np.array([p[:rg] for p in uniq])
    rel_v = np.array([p[rg:] for p in uniq])
    return wr, rg, wk, ks.astype(np.int32), var, q_off_v, rel_v


def _na_bias_table(rpb, g_rows, lc):
    wr, rg, wk, _, _, q_off_v, rel_v = _na_plan(g_rows, lc)
    col_start = np.clip(np.arange(GRID_W) - NA_WIN_COLS // 2, 0, GRID_W - NA_WIN_COLS)
    cc = np.arange(GRID_W)
    col_ok = (cc[None, :] >= col_start[:, None]) & (cc[None, :] < col_start[:, None] + NA_WIN_COLS)
    dc = np.clip(cc[None, :] - cc[:, None] + NA_WIN_COLS - 1, 0, 2 * NA_WIN_COLS - 2)
    n_dc = 2 * NA_WIN_COLS - 1
    nh = rpb.shape[0]
    onehot = (dc.reshape(-1)[:, None] == np.arange(n_dc)[None, :]).astype(np.float32)
    g = jnp.einsum('hab,yb->hay', rpb.astype(F32), jnp.asarray(onehot), precision=HIGHEST)
    g = jnp.where(col_ok[None, None], g.reshape(nh, -1, GRID_W, GRID_W), NEG)
    g = jnp.pad(g, ((0, 0), (wk, wk), (0, 0), (0, 0)), constant_values=NEG)
    w = np.arange(wk)
    tabs = []
    for v in range(q_off_v.shape[0]):
        rows_v = []
        for j in range(rg):
            a0 = wk - q_off_v[v, j] + NA_WIN_ROWS - 1
            own = (w >= rel_v[v, j]) & (w < rel_v[v, j] + wr)
            blk = jnp.where(jnp.asarray(own)[None, :, None, None], g[:, a0:a0 + wk], NEG)
            rows_v.append(jnp.transpose(blk, (0, 2, 1, 3)).reshape(nh, GRID_W, wk * GRID_W))
        tabs.append(jnp.concatenate(rows_v, axis=-2))
    return jnp.stack(tabs, axis=0)


def _na_kernel(var_ref, ks_ref, q_ref, kx_ref, vx_ref, kc_ref, vc_ref, bias_ref, o_ref, *, n_groups, wk, scale):
    g = pl.program_id(1)
    hd = NA_HEAD_DIM
    cols = [slice(h * hd, (h + 1) * hd) for h in range(NA_HEADS)]

    @pl.when(g < n_groups)
    def _():
        start = pl.multiple_of(ks_ref[g] * GRID_W, GRID_W)
        rows = pl.ds(start, wk * GRID_W)
        outs = []
        for h, c in enumerate(cols):
            q = q_ref[:, c]
            sl = _dot_nt(q, kx_ref[rows, c]) * scale + bias_ref[h]
            sc = _dot_nt(q, kc_ref[:, c]) * scale
            outs.append(_softmax_pv([sl, sc], [vx_ref[rows, c], vc_ref[:, c]]))
        o_ref[...] = jnp.concatenate(outs, axis=1).astype(BF16)

    @pl.when(g >= n_groups)
    def _():
        outs = [_softmax_pv([_dot_nt(q_ref[:, c], kc_ref[:, c]) * scale], [vc_ref[:, c]]) for c in cols]
        o_ref[...] = jnp.concatenate(outs, axis=1).astype(BF16)


def _na_call(proj, bias_tab, *, layer, n_b, lx, lc, need_ctx):
    rows = proj.shape[0]
    g_rows = lx // GRID_W
    _, rg, wk, ks, var, _, _ = _na_plan(g_rows, lc)
    n_groups = g_rows // rg
    tq = rg * GRID_W
    nqc = lc // tq
    nq = n_groups + (nqc if need_ctx else 0)
    nbx = n_b * lx // lc
    hd, nd = NA_HEAD_DIM, NA_DIM
    cq, ck, cv = P_NAQ // nd, P_NAK // nd, P_NAV // nd

    def q_row(b, g):
        return jnp.where(g < n_groups, b * n_groups + g, n_b * n_groups + b * nqc + (g - n_groups))

    kern = functools.partial(_na_kernel, n_groups=n_groups, wk=wk, scale=hd ** -0.5)
    grid_spec = pltpu.PrefetchScalarGridSpec(
        num_scalar_prefetch=2,
        grid=(n_b, nq),
        in_specs=[pl.BlockSpec((tq, nd), lambda b, g, var_r, ks_r: (q_row(b, g), cq)),
                  pl.BlockSpec((lx, nd), lambda b, g, var_r, ks_r: (b, ck)),
                  pl.BlockSpec((lx, nd), lambda b, g, var_r, ks_r: (b, cv)),
                  pl.BlockSpec((lc, nd), lambda b, g, var_r, ks_r: (nbx + b, ck)),
                  pl.BlockSpec((lc, nd), lambda b, g, var_r, ks_r: (nbx + b, cv)),
                  pl.BlockSpec((None, NA_HEADS, tq, wk * GRID_W),
                               lambda b, g, var_r, ks_r: (var_r[jnp.minimum(g, n_groups - 1)], layer, 0, 0))],
        out_specs=pl.BlockSpec((tq, nd), lambda b, g, var_r, ks_r: (q_row(b, g), 0)),
    )
    return pl.pallas_call(
        kern,
        grid_spec=grid_spec,
        out_shape=jax.ShapeDtypeStruct((rows if need_ctx else n_b * lx, NA_DIM), BF16),
        compiler_params=_cp(("parallel", "arbitrary"), 40),
        name="na_attn",
    )(jnp.asarray(var), jnp.asarray(ks), proj, proj, proj, proj, proj, bias_tab)


def _chunk_block(n_b, nxc, ncc):
    def f(b, d, c):
        cc = jnp.where(d == 0, c, ncc - 1 - c)
        cx = jnp.where(d == 0, c - ncc, nxc - 1 - (c - ncc))
        return jnp.where(c < ncc, n_b * nxc + b * ncc + cc, b * nxc + cx)
    return f


def _dir_masks(d):
    row = lax.broadcasted_iota(jnp.int32, (CHUNK, CHUNK), 0)
    col = lax.broadcasted_iota(jnp.int32, (CHUNK, CHUNK), 1)
    diff = (row - col) * jnp.where(d == 0, 1, -1)
    return diff >= 0, diff > 0


def _cumsum_lanes(x, incl):
    cs = jnp.dot(incl.astype(F32), x, preferred_element_type=F32, precision=HIGHEST)
    return cs, cs.T


def _lane_vec(vals, offset):
    flat = vals.reshape(-1).astype(F32)
    return jnp.pad(flat, (offset, P_SMALL - offset - flat.shape[0])).reshape(1, P_SMALL)


def _ssd_prep_kernel(xbc_ref, sm_ref, alog_ref, dtb_ref, dsk_ref, yp_ref, xw_ref, ea_ref):
    nh, hp, ns = SSD_HEADS, SSD_HEAD_DIM, SSD_STATE
    gh = nh // SSD_GROUPS
    dt2 = _softplus(sm_ref[...] + dtb_ref[...])
    dta2 = dt2 * (-jnp.exp(alog_ref[...]))
    row = lax.broadcasted_iota(jnp.int32, (CHUNK, CHUNK), 0)
    col = lax.broadcasted_iota(jnp.int32, (CHUNK, CHUNK), 1)
    incl = [row >= col, row <= col]
    cs = [jnp.dot(m.astype(F32), dta2, preferred_element_type=F32, precision=HIGHEST) for m in incl]
    cs_t = [x.T for x in cs]
    tot2 = jnp.sum(dta2, axis=0, keepdims=True)
    dsk = dsk_ref[...]
    bo, co = SSD_D_INNER, SSD_D_INNER + SSD_GROUPS * ns
    scores = [_dot_nt(xbc_ref[:, co + g * ns:co + (g + 1) * ns], xbc_ref[:, bo + g * ns:bo + (g + 1) * ns])
              for g in range(SSD_GROUPS)]
    xs = [xbc_ref[:, h * hp:(h + 1) * hp].astype(F32) for h in range(nh)]
    yp, xw = [], []
    for d in range(2):
        for h in range(nh):
            ln = d * nh + h
            a_c, a_r = cs[d][:, ln:ln + 1], cs_t[d][ln:ln + 1, :]
            dec = jnp.where(incl[d], jnp.exp(jnp.where(incl[d], a_c - a_r, 0.0)), 0.0)
            m = (scores[h // gh] * dec).astype(BF16)
            y = _dot(m, (xs[h] * dt2[:, ln:ln + 1]).astype(BF16))
            yp.append(y + dsk[:, h:h + 1] * xs[h] if d == 0 else y)
            xw.append((xs[h] * (jnp.exp(tot2[:, ln:ln + 1] - a_c) * dt2[:, ln:ln + 1])).astype(BF16))
    yp_ref[...] = jnp.concatenate(yp, axis=1)
    xw_ref[...] = jnp.concatenate(xw, axis=1)
    ea_ref[...] = jnp.concatenate([jnp.exp(cs[0]), jnp.exp(cs[1])], axis=1)


def _ssd_scan_kernel(*refs):
    (bc0, yp0, xw0, ea0, sm0, bc1, yp1, xw1, ea1, sm1, alog_ref, dtb_ref, y0_ref, y1_ref, s_scr) = refs
    nh, hp, ns = SSD_HEADS, SSD_HEAD_DIM, SSD_STATE
    gh = nh // SSD_GROUPS

    @pl.when(pl.program_id(1) == 0)
    def _():
        s_scr[...] = jnp.zeros_like(s_scr)

    neg_a = -jnp.exp(alog_ref[...])
    for d, (bc, yp, xw, ea, sm, y_ref) in enumerate([(bc0, yp0, xw0, ea0, sm0, y0_ref),
                                                      (bc1, yp1, xw1, ea1, sm1, y1_ref)]):
        c_dec = jnp.exp(jnp.sum(_softplus(sm[...] + dtb_ref[...]) * neg_a, axis=0, keepdims=True))
        e_acum = ea[...]
        ys, states = [], []
        for g in range(SSD_GROUPS):
            bg = bc[:, g * ns:(g + 1) * ns]
            cg = bc[:, SSD_GROUPS * ns + g * ns:SSD_GROUPS * ns + (g + 1) * ns]
            s_g = s_scr[d, :, g * gh * hp:(g + 1) * gh * hp]
            y_int = _dot(cg, s_g.astype(BF16))
            upd = _dot_tn(bg, xw[:, g * gh * hp:(g + 1) * gh * hp])
            for hh in range(gh):
                h = g * gh + hh
                ln = d * nh + h
                sl = slice(hh * hp, (hh + 1) * hp)
                ys.append(yp[:, h * hp:(h + 1) * hp] + y_int[:, sl] * e_acum[:, ln:ln + 1])
                states.append(s_g[:, sl] * c_dec[:, ln:ln + 1] + upd[:, sl])
        y_ref[...] = jnp.concatenate(ys, axis=1)
        s_scr[d] = jnp.concatenate(states, axis=1)


def _ssd_call(xbc, small, a_log, dt_bias, d_skip, *, n_b, lx, lc):
    rows = xbc.shape[0]
    nxc, ncc = lx // CHUNK, lc // CHUNK
    blk = _chunk_block(n_b, nxc, ncc)
    nh, di = SSD_HEADS, SSD_D_INNER
    alog_v, dtb_v = _lane_vec(a_log, 0), _lane_vec(dt_bias, 0)
    ypart, xw, ea = pl.pallas_call(
        _ssd_prep_kernel,
        grid=(rows // CHUNK,),
        in_specs=[pl.BlockSpec((CHUNK, SSD_CONV_DIM), lambda i: (i, 0)),
                  pl.BlockSpec((CHUNK, P_SMALL), lambda i: (i, 0)),
                  pl.BlockSpec((1, P_SMALL), lambda i: (0, 0)),
                  pl.BlockSpec((1, P_SMALL), lambda i: (0, 0)),
                  pl.BlockSpec((1, nh), lambda i: (0, 0))],
        out_specs=[pl.BlockSpec((CHUNK, 2 * di), lambda i: (i, 0)),
                   pl.BlockSpec((CHUNK, 2 * di), lambda i: (i, 0)),
                   pl.BlockSpec((CHUNK, 2 * P_SMALL), lambda i: (i, 0))],
        out_shape=[jax.ShapeDtypeStruct((rows, 2 * di), F32),
                   jax.ShapeDtypeStruct((rows, 2 * di), BF16),
                   jax.ShapeDtypeStruct((rows, 2 * P_SMALL), F32)],
        compiler_params=_cp(("parallel",), 40),
        name="ssd_prep",
    )(xbc, small, alog_v, dtb_v, d_skip.reshape(1, nh))

    def dir_specs(d):
        at = lambda b, c: blk(b, d, c)
        return [pl.BlockSpec((CHUNK, SSD_CONV_DIM - di), lambda b, c: (at(b, c), 1)),
                pl.BlockSpec((CHUNK, di), lambda b, c: (at(b, c), d)),
                pl.BlockSpec((CHUNK, di), lambda b, c: (at(b, c), d)),
                pl.BlockSpec((CHUNK, P_SMALL), lambda b, c: (at(b, c), d)),
                pl.BlockSpec((CHUNK, P_SMALL), lambda b, c: (at(b, c), 0))]

    const = pl.BlockSpec((1, P_SMALL), lambda b, c: (0, 0))
    return pl.pallas_call(
        _ssd_scan_kernel,
        grid=(n_b, ncc + nxc),
        in_specs=dir_specs(0) + dir_specs(1) + [const, const],
        out_specs=[pl.BlockSpec((CHUNK, di), lambda b, c: (blk(b, 0, c), 0)),
                   pl.BlockSpec((CHUNK, di), lambda b, c: (blk(b, 1, c), 0))],
        out_shape=[jax.ShapeDtypeStruct((rows, di), F32), jax.ShapeDtypeStruct((rows, di), F32)],
        scratch_shapes=[pltpu.VMEM((2, SSD_STATE, di), F32)],
        compiler_params=_cp(("parallel", "arbitrary"), 40),
        name="ssd_scan",
    )(xbc, ypart, xw, ea, small, xbc, ypart, xw, ea, small, alog_v, dtb_v)


SOLVE_BLOCK = 16


def _unit_tri_solve_many(n_mats, rhss):
    ln = n_mats[0].shape[0]
    row = lax.broadcasted_iota(jnp.int32, (ln, ln), 0)
    col = lax.broadcasted_iota(jnp.int32, (ln, ln), 1)
    on_diag_block = (row // SOLVE_BLOCK) == (col // SOLVE_BLOCK)
    eye = jnp.where(row == col, 1.0, 0.0)
    mm = lambda a, b: _dot(a.astype(BF16), b.astype(BF16))
    ms = [jnp.where(on_diag_block, -n, 0.0) for n in n_mats]
    es = [jnp.where(on_diag_block, 0.0, n) for n in n_mats]
    ps = [eye + m for m in ms]
    mps = ms
    k = 1
    while 2 * k < SOLVE_BLOCK:
        mps = [mm(x, x) for x in mps]
        ps = [p + mm(p, x) for p, x in zip(ps, mps)]
        k *= 2
    f_pows = [[-mm(p, e) for p, e in zip(ps, es)]]
    ys = [mm(p, r) for p, r in zip(ps, rhss)]
    k = 1
    while 2 * k < ln // SOLVE_BLOCK:
        f_pows.append([mm(f, f) for f in f_pows[-1]])
        k *= 2
    for fl in reversed(f_pows):
        ys = [y + mm(f, y) for f, y in zip(fl, ys)]
    return ys


GDN_PACK = 5 * GDN_DIM
GDN_G_LANE = 2 * SSD_HEADS
GDN_B_LANE = 2 * SSD_HEADS + 2 * GDN_HEADS


def _gdn_prep_kernel(qkv_ref, sm_ref, alog_ref, dtb_ref, o_ref):
    nh, hd = GDN_HEADS, GDN_HEAD_DIM
    sm = sm_ref[...]
    g2 = -jnp.exp(alog_ref[...]) * _softplus(sm + dtb_ref[...])
    beta2 = _sigmoid(sm)
    row = lax.broadcasted_iota(jnp.int32, (CHUNK, CHUNK), 0)
    col = lax.broadcasted_iota(jnp.int32, (CHUNK, CHUNK), 1)
    incl = [row >= col, row <= col]
    strict = [row > col, row < col]
    cs = [jnp.dot(m.astype(F32), g2, preferred_element_type=F32, precision=HIGHEST) for m in incl]
    cs_t = [x.T for x in cs]
    gtot2 = jnp.sum(g2, axis=0, keepdims=True)
    qn, kn, kn_b, vv, qk_raw = [], [], [], [], []
    for h in range(nh):
        qh = qkv_ref[:, h * hd:(h + 1) * hd].astype(F32)
        kh = qkv_ref[:, GDN_DIM + h * hd:GDN_DIM + (h + 1) * hd].astype(F32)
        vv.append(qkv_ref[:, 2 * GDN_DIM + h * hd:2 * GDN_DIM + (h + 1) * hd].astype(F32))
        qn.append(qh * (lax.rsqrt(jnp.sum(qh * qh, axis=-1, keepdims=True) + EPS) * hd ** -0.5))
        kn.append(kh * lax.rsqrt(jnp.sum(kh * kh, axis=-1, keepdims=True) + EPS))
        kn_b.append(kn[h].astype(BF16))
        qk_raw.append(_dot_nt(qn[h].astype(BF16), kn_b[h]))
    n_mats, rhss, qks, qds, kds = [], [], [], [], []
    for d in range(2):
        for h in range(nh):
            lg, lb = GDN_G_LANE + d * nh + h, GDN_B_LANE + d * nh + h
            gcc, gcr = cs[d][:, lg:lg + 1], cs_t[d][lg:lg + 1, :]
            beta, gtot = beta2[:, lb:lb + 1], gtot2[:, lg:lg + 1]
            dec = jnp.where(incl[d], jnp.exp(jnp.where(incl[d], gcc - gcr, 0.0)), 0.0)
            kb = kn[h] * beta
            n_mats.append(jnp.where(strict[d], _dot_nt(kb.astype(BF16), kn_b[h]) * dec, 0.0))
            e_gc = jnp.exp(gcc)
            rhss.append(jnp.concatenate([vv[h] * beta, kb * e_gc], axis=1))
            qks.append(qk_raw[h] * dec)
            qds.append(qn[h] * e_gc)
            kds.append(kn[h] * jnp.exp(gtot - gcc))
    sols = _unit_tri_solve_many(n_mats, rhss)
    pieces = []
    for d in range(2):
        js = range(d * nh, (d + 1) * nh)
        pieces += [sols[j][:, 0:hd] for j in js] + [sols[j][:, hd:2 * hd] for j in js]
        pieces += [qks[j] for j in js] + [qds[j] for j in js] + [kds[j] for j in js]
    o_ref[...] = jnp.concatenate([p.astype(BF16) for p in pieces], axis=1)


def _gdn_scan_kernel(pk0, sm0, pk1, sm1, alog_ref, dtb_ref, o0_ref, o1_ref, s_scr):
    nh, hd = GDN_HEADS, GDN_HEAD_DIM

    @pl.when(pl.program_id(1) == 0)
    def _():
        s_scr[...] = jnp.zeros_like(s_scr)

    neg_a = -jnp.exp(alog_ref[...])
    for d, (pk_ref, sm_ref, o_ref) in enumerate([(pk0, sm0, o0_ref), (pk1, sm1, o1_ref)]):
        g_end2 = jnp.exp(jnp.sum(neg_a * _softplus(sm_ref[...] + dtb_ref[...]), axis=0, keepdims=True))
        outs, states = [], []
        for h in range(nh):
            lg = GDN_G_LANE + d * nh + h
            part = lambda j: pk_ref[:, (j * nh + h) * hd:(j * nh + h + 1) * hd]
            u, w, qk, qd, kd = part(0), part(1), part(2), part(3), part(4)
            s_h = s_scr[d, :, h * hd:(h + 1) * hd]
            s_b = s_h.astype(BF16)
            v_new = (u.astype(F32) - _dot(w, s_b)).astype(BF16)
            outs.append(_dot(qd, s_b) + _dot(qk, v_new))
            states.append(s_h * g_end2[:, lg:lg + 1] + _dot_tn(kd, v_new))
        o_ref[...] = jnp.concatenate(outs, axis=1)
        s_scr[d] = jnp.concatenate(states, axis=1)


def _gdn_call(qkv, small, a_log, dt_bias, *, n_b, lx, lc):
    rows = qkv.shape[0]
    nxc, ncc = lx // CHUNK, lc // CHUNK
    blk = _chunk_block(n_b, nxc, ncc)
    alog_v, dtb_v = _lane_vec(a_log, GDN_G_LANE), _lane_vec(dt_bias, GDN_G_LANE)
    packed = pl.pallas_call(
        _gdn_prep_kernel,
        grid=(rows // CHUNK,),
        in_specs=[pl.BlockSpec((CHUNK, 3 * GDN_DIM), lambda i: (i, 0)),
                  pl.BlockSpec((CHUNK, P_SMALL), lambda i: (i, 0)),
                  pl.BlockSpec((1, P_SMALL), lambda i: (0, 0)),
                  pl.BlockSpec((1, P_SMALL), lambda i: (0, 0))],
        out_specs=pl.BlockSpec((CHUNK, 2 * GDN_PACK), lambda i: (i, 0)),
        out_shape=jax.ShapeDtypeStruct((rows, 2 * GDN_PACK), BF16),
        compiler_params=_cp(("parallel",), 40),
        name="gdn_prep",
    )(qkv, small, alog_v, dtb_v)
    def dir_specs(d):
        return [pl.BlockSpec((CHUNK, GDN_PACK), lambda b, c: (blk(b, d, c), d)),
                pl.BlockSpec((CHUNK, P_SMALL), lambda b, c: (blk(b, d, c), 0))]

    const = pl.BlockSpec((1, P_SMALL), lambda b, c: (0, 0))
    return pl.pallas_call(
        _gdn_scan_kernel,
        grid=(n_b, ncc + nxc),
        in_specs=dir_specs(0) + dir_specs(1) + [const, const],
        out_specs=[pl.BlockSpec((CHUNK, GDN_DIM), lambda b, c: (blk(b, 0, c), 0)),
                   pl.BlockSpec((CHUNK, GDN_DIM), lambda b, c: (blk(b, 1, c), 0))],
        out_shape=[jax.ShapeDtypeStruct((rows, GDN_DIM), F32), jax.ShapeDtypeStruct((rows, GDN_DIM), F32)],
        scratch_shapes=[pltpu.VMEM((2, GDN_HEAD_DIM, GDN_DIM), F32)],
        compiler_params=_cp(("parallel", "arbitrary"), 40),
        name="gdn_scan",
    )(packed, small, packed, small, alog_v, dtb_v)


MIXOUT_SUB_ROWS = 256


def _mixout_kernel(h_ref, mod_ref, gpost_ref, ya_ref, ys0_ref, ys1_ref, zs_ref, sn_ref, yn_ref, og0_ref, og1_ref,
                   zg_ref, gn_ref, w_ref, o_ref):
    hd = GDN_HEAD_DIM
    tm = h_ref.shape[0]
    sub = min(tm, MIXOUT_SUB_ROWS)
    outs = []
    for s in range(0, tm, sub):
        r = slice(s, s + sub)
        ssd = (ys0_ref[r, :] + ys1_ref[r, :]) * _silu(zs_ref[r, :].astype(F32))
        yb = (ssd * _rms_scale(ssd) * sn_ref[...]).astype(BF16)
        gd = og0_ref[r, :] + og1_ref[r, :]
        zg = _silu(zg_ref[r, :].astype(F32))
        yd = []
        for h in range(GDN_HEADS):
            oh = gd[:, h * hd:(h + 1) * hd]
            yd.append((oh * _rms_scale(oh) * gn_ref[...] * zg[:, h * hd:(h + 1) * hd]).astype(BF16))
        parts = [ya_ref[r, :], yb, yn_ref[r, :]] + yd
        widths = [512, 512, 512] + [hd] * GDN_HEADS
        y = None
        off = 0
        for part, wd in zip(parts, widths):
            t = _dot(part, w_ref[off:off + wd, :])
            y = t if y is None else y + t
            off += wd
        outs.append(h_ref[r, :] + mod_ref[2:3, :] * (y * _rms_scale(y) * gpost_ref[...]))
    o_ref[...] = jnp.concatenate(outs, axis=0)


def _mixout_call(h, mod_l, g_post, ya, ys, proj, ssd_norm, yn, og, gdn_norm, w_out, *, n_b, lx, n_rows, tm):
    d = h.shape[1]
    nxt, per_b = n_b * lx // tm, lx // tm
    mod_idx = lambda i: (jnp.where(i < nxt, i // per_b, n_b), 0, 0)
    row = lambda i: (i, 0)
    const = lambda i: (0, 0)
    return pl.pallas_call(
        _mixout_kernel,
        grid=(n_rows // tm,),
        in_specs=[pl.BlockSpec((tm, d), row),
                  pl.BlockSpec((None, 6, d), mod_idx),
                  pl.BlockSpec((1, d), const),
                  pl.BlockSpec((tm, 512), row),
                  pl.BlockSpec((tm, 512), row),
                  pl.BlockSpec((tm, 512), row),
                  pl.BlockSpec((tm, 512), lambda i: (i, P_ZSSD // 512)),
                  pl.BlockSpec((1, 512), const),
                  pl.BlockSpec((tm, 512), row),
                  pl.BlockSpec((tm, 512), row),
                  pl.BlockSpec((tm, 512), row),
                  pl.BlockSpec((tm, 512), lambda i: (i, P_GZ // 512)),
                  pl.BlockSpec((1, GDN_HEAD_DIM), const),
                  pl.BlockSpec((d, d), const)],
        out_specs=pl.BlockSpec((tm, d), row),
        out_shape=jax.ShapeDtypeStruct((n_rows, d), F32),
        compiler_params=_cp(("parallel",), 56),
        name="mix_out",
    )(h, mod_l, g_post.reshape(1, d), ya, ys[0], ys[1], proj, ssd_norm.reshape(1, 512), yn, og[0], og[1], proj,
      gdn_norm.reshape(1, GDN_HEAD_DIM), w_out)


def _ffn_kernel(h_ref, mod_ref, gpre_ref, gpost_ref, wg_ref, wu_ref, wd_ref, o_ref, u_scr, acc_scr):
    j = pl.program_id(1)

    @pl.when(j == 0)
    def _():
        x = h_ref[...]
        y = x * _rms_scale(x) * gpre_ref[...]
        u_scr[...] = (y * (1.0 + mod_ref[4:5, :]) + mod_ref[3:4, :]).astype(BF16)
        acc_scr[...] = jnp.zeros_like(acc_scr)

    u = u_scr[...]
    mid = (_silu(_dot(u, wg_ref[...])) * _dot(u, wu_ref[...])).astype(BF16)
    acc_scr[...] += _dot(mid, wd_ref[...])

    @pl.when(j == pl.num_programs(1) - 1)
    def _():
        y = acc_scr[...]
        o_ref[...] = h_ref[...] + mod_ref[5:6, :] * (y * _rms_scale(y) * gpost_ref[...])


def _ffn_call(h, mod_l, g_pre, g_post, wg, wu, wd, *, n_b, lx, tm, tf):
    rows, d = h.shape
    ff = wg.shape[1]
    nxt, per_b = n_b * lx // tm, lx // tm
    mod_idx = lambda i, j: (jnp.where(i < nxt, i // per_b, n_b), 0, 0)
    return pl.pallas_call(
        _ffn_kernel,
        grid=(rows // tm, ff // tf),
        in_specs=[pl.BlockSpec((tm, d), lambda i, j: (i, 0)),
                  pl.BlockSpec((None, 6, d), mod_idx),
                  pl.BlockSpec((1, d), lambda i, j: (0, 0)),
                  pl.BlockSpec((1, d), lambda i, j: (0, 0)),
                  pl.BlockSpec((d, tf), lambda i, j: (0, j)),
                  pl.BlockSpec((d, tf), lambda i, j: (0, j)),
                  pl.BlockSpec((tf, d), lambda i, j: (j, 0))],
        out_specs=pl.BlockSpec((tm, d), lambda i, j: (i, 0)),
        out_shape=jax.ShapeDtypeStruct((rows, d), F32),
        scratch_shapes=[pltpu.VMEM((tm, d), BF16), pltpu.VMEM((tm, d), F32)],
        compiler_params=_cp(("parallel", "arbitrary"), 56),
        name="ffn_swiglu",
    )(h, mod_l, g_pre.reshape(1, d), g_post.reshape(1, d), wg, wu, wd)


MOE_ALIGN = 16
MOE_GROUP_TILE = 512


def _moe_local_rows(ts):
    return -(-(TOP_K * ts + N_EXPERTS * MOE_ALIGN) // 128) * 128


def _moe_route_kernel(h_ref, mod_ref, gpre_ref, wr_ref, u_ref, dest_ref, gate_ref, cnt_ref):
    ts, lanes = dest_ref.shape
    x = h_ref[...]
    y = x * _rms_scale(x) * gpre_ref[...]
    u = y * (1.0 + mod_ref[4:5, :]) + mod_ref[3:4, :]
    u_hi = u.astype(BF16)
    u_ref[...] = u_hi
    u_lo = (u - u_hi.astype(F32)).astype(BF16)
    logits = _dot(u_hi, wr_ref[0]) + (_dot(u_hi, wr_ref[1]) + _dot(u_lo, wr_ref[0]))
    lane = lax.broadcasted_iota(jnp.int32, logits.shape, 1).astype(F32)
    lg = jnp.where(lane < N_EXPERTS, logits, NEG)
    m1 = jnp.max(lg, axis=-1, keepdims=True)
    i1 = jnp.min(jnp.where(lg == m1, lane, float(lanes)), axis=-1, keepdims=True)
    lg2 = jnp.where(lane == i1, NEG, lg)
    m2 = jnp.max(lg2, axis=-1, keepdims=True)
    i2 = jnp.min(jnp.where(lg2 == m2, lane, float(lanes)), axis=-1, keepdims=True)
    e2 = jnp.exp(m2 - m1)
    gate_ref[...] = jnp.where(lane == 0.0, 1.0 / (1.0 + e2), jnp.where(lane == 1.0, e2 / (1.0 + e2), 0.0))
    sel = jnp.where(lane == i1, 1.0, 0.0) + jnp.where(lane == i2, 1.0, 0.0)
    cnt = jnp.sum(sel, axis=0, keepdims=True)
    cnt_al = jnp.floor((cnt + (MOE_ALIGN - 1)) / MOE_ALIGN) * MOE_ALIGN
    cnt_ref[...] = jnp.broadcast_to(cnt_al, cnt_ref.shape)
    before = lax.broadcasted_iota(jnp.int32, (lanes, lanes), 0) < lax.broadcasted_iota(jnp.int32, (lanes, lanes), 1)
    seg_lo = jnp.dot(jnp.broadcast_to(cnt_al, (8, lanes)), before.astype(F32),
                     preferred_element_type=F32, precision=HIGHEST)[0:1]
    earlier = lax.broadcasted_iota(jnp.int32, (ts, ts), 1) < lax.broadcasted_iota(jnp.int32, (ts, ts), 0)
    rank = _dot(earlier.astype(BF16), sel.astype(BF16))
    slot = seg_lo + rank
    slot_1 = jnp.sum(jnp.where(lane == i1, slot, 0.0), axis=-1, keepdims=True)
    slot_2 = jnp.sum(jnp.where(lane == i2, slot, 0.0), axis=-1, keepdims=True)
    dest_ref[...] = jnp.where(lane == 0.0, slot_1, jnp.where(lane == 1.0, slot_2, -1.0))


MOE_BIG = 4


def _moe_copy(hbm_ref, loc_ref, sems, far_row, loc_row, size_idx, *, to_hbm):
    rows = (MOE_BIG * MOE_ALIGN, MOE_ALIGN)[size_idx]
    loc = loc_ref.at[pl.ds(pl.multiple_of(loc_row, MOE_ALIGN), rows), :]
    far = hbm_ref.at[pl.ds(pl.multiple_of(far_row, MOE_ALIGN), rows), :]
    src, dst = (loc, far) if to_hbm else (far, loc)
    return pltpu.make_async_copy(src, dst, sems.at[size_idx])


def _moe_segment_copies(i, off_ref, n_ref, hbm_ref, loc_ref, sems, *, to_hbm):
    lo = jnp.int32(0)
    n_big_all, n_small_all = jnp.int32(0), jnp.int32(0)
    for e in range(N_EXPERTS):
        n_chunks = n_ref[i * N_EXPERTS + e]
        go = off_ref[i * N_EXPERTS + e]
        n_big = n_chunks // MOE_BIG
        n_small = n_chunks - n_big * MOE_BIG
        done = n_big * (MOE_BIG * MOE_ALIGN)

        def big(k, carry, lo=lo, go=go):
            step = k * (MOE_BIG * MOE_ALIGN)
            _moe_copy(hbm_ref, loc_ref, sems, go + step, lo + step, 0, to_hbm=to_hbm).start()
            return carry

        def small(k, carry, lo=lo, go=go, done=done):
            step = done + k * MOE_ALIGN
            _moe_copy(hbm_ref, loc_ref, sems, go + step, lo + step, 1, to_hbm=to_hbm).start()
            return carry

        lax.fori_loop(0, n_big, big, 0)
        lax.fori_loop(0, n_small, small, 0)
        lo = lo + n_chunks * MOE_ALIGN
        n_big_all, n_small_all = n_big_all + n_big, n_small_all + n_small
    return lo, n_big_all, n_small_all


def _moe_wait_copies(n_big, n_small, hbm_ref, loc_ref, sems, *, to_hbm):
    for size_idx, n in ((0, n_big), (1, n_small)):
        def body(k, carry, size_idx=size_idx):
            _moe_copy(hbm_ref, loc_ref, sems, 0, 0, size_idx, to_hbm=to_hbm).wait()
            return carry

        lax.fori_loop(0, n, body, 0)


def _moe_sort_kernel(off_ref, n_ref, u_ref, dest_ref, init_ref, us_ref, loc_scr, sem):
    del init_ref
    i = pl.program_id(0)
    dest = dest_ref[...]
    slot = lax.broadcasted_iota(jnp.int32, (dest.shape[0], loc_scr.shape[0]), 1).astype(F32)
    onehot = jnp.where(slot == dest[:, 0:1], 1.0, jnp.where(slot == dest[:, 1:2], 1.0, 0.0))
    loc_scr[...] = _dot_tn(onehot.astype(BF16), u_ref[...]).astype(BF16)
    _, n_big, n_small = _moe_segment_copies(i, off_ref, n_ref, us_ref, loc_scr, sem, to_hbm=True)
    _moe_wait_copies(n_big, n_small, us_ref, loc_scr, sem, to_hbm=True)


def _moe_expert_kernel(te_ref, tv_ref, u_ref, wg_ref, wu_ref, wd_ref, y_ref, acc_scr):
    k, j = pl.program_id(0), pl.program_id(1)
    last = pl.num_programs(1) - 1
    valid = tv_ref[k] > 0

    @pl.when(valid)
    def _():
        @pl.when(j == 0)
        def _():
            acc_scr[...] = jnp.zeros_like(acc_scr)

        u = u_ref[...]
        mid = (_silu(_dot(u, wg_ref[...])) * _dot(u, wu_ref[...])).astype(BF16)
        acc_scr[...] += _dot(mid, wd_ref[...])

        @pl.when(j == last)
        def _():
            y_ref[...] = acc_scr[...].astype(BF16)

    @pl.when(jnp.logical_not(valid) & (j == last))
    def _():
        y_ref[...] = jnp.zeros_like(y_ref)


def _moe_combine_kernel(off_ref, n_ref, h_ref, mod_ref, gpost_ref, dest_ref, gate_ref, ys_ref, o_ref, loc_scr, sem):
    i = pl.program_id(0)
    n_rows, n_big, n_small = _moe_segment_copies(i, off_ref, n_ref, ys_ref, loc_scr, sem, to_hbm=False)
    dest, gate = dest_ref[...], gate_ref[...]
    slot = lax.broadcasted_iota(jnp.int32, (dest.shape[0], loc_scr.shape[0]), 1).astype(F32)
    w = jnp.where(slot == dest[:, 0:1], gate[:, 0:1], jnp.where(slot == dest[:, 1:2], gate[:, 1:2], 0.0))
    w_hi = w.astype(BF16)
    w_lo = (w - w_hi.astype(F32)).astype(BF16)
    _moe_wait_copies(n_big, n_small, ys_ref, loc_scr, sem, to_hbm=False)
    filled = lax.broadcasted_iota(jnp.int32, (loc_scr.shape[0], 1), 0) < n_rows
    y_loc = jnp.where(filled, loc_scr[...], jnp.zeros_like(loc_scr))
    y = _dot(w_hi, y_loc) + _dot(w_lo, y_loc)
    o_ref[...] = h_ref[...] + mod_ref[5:6, :] * (y * _rms_scale(y) * gpost_ref[...])


def _moe_call(h, mod_l, g_pre, g_post, router_pad, wg, wu, wd, *, n_b, lx, tm, tf):
    rows, d = h.shape
    ne, _, fe = wg.shape
    ts, tg = tm, MOE_GROUP_TILE
    n_tiles = rows // ts
    lrows = _moe_local_rows(ts)
    nt_max = -(-(TOP_K * rows + n_tiles * ne * (MOE_ALIGN - 1)) // tg) + ne
    n_pad = nt_max * tg
    nxt, per_b = n_b * lx // ts, lx // ts
    mod_row = lambda i: jnp.where(i < nxt, i // per_b, n_b)

    u, dest, gate, cnt = pl.pallas_call(
        _moe_route_kernel,
        grid=(n_tiles,),
        in_specs=[pl.BlockSpec((ts, d), lambda i: (i, 0)),
                  pl.BlockSpec((None, 6, d), lambda i: (mod_row(i), 0, 0)),
                  pl.BlockSpec((1, d), lambda i: (0, 0)),
                  pl.BlockSpec((2, d, 128), lambda i: (0, 0, 0))],
        out_specs=[pl.BlockSpec((ts, d), lambda i: (i, 0)),
                   pl.BlockSpec((ts, 128), lambda i: (i, 0)),
                   pl.BlockSpec((ts, 128), lambda i: (i, 0)),
                   pl.BlockSpec((None, 8, 128), lambda i: (i, 0, 0))],
        out_shape=[jax.ShapeDtypeStruct((rows, d), BF16),
                   jax.ShapeDtypeStruct((rows, 128), F32),
                   jax.ShapeDtypeStruct((rows, 128), F32),
                   jax.ShapeDtypeStruct((n_tiles, 8, 128), F32)],
        compiler_params=_cp(("parallel",), 40),
        name="moe_route",
    )(h, mod_l, g_pre.reshape(1, d), router_pad)

    cnt_al = cnt[:, 0, :ne].astype(jnp.int32)
    group = -(-jnp.sum(cnt_al, axis=0) // tg) * tg
    group_end = jnp.cumsum(group)
    seg_off = (group_end - group)[None, :] + jnp.cumsum(cnt_al, axis=0) - cnt_al
    n_used = group_end[-1] // tg
    tile_id = jnp.arange(nt_max, dtype=jnp.int32)
    tile_valid = (tile_id < n_used).astype(jnp.int32)
    tile_expert = jnp.searchsorted(group_end // tg, jnp.minimum(tile_id, n_used - 1), side='right')
    tile_expert = jnp.minimum(tile_expert, ne - 1).astype(jnp.int32)
    seg_off = seg_off.reshape(-1).astype(jnp.int32)
    seg_chunks = (cnt_al // MOE_ALIGN).reshape(-1)

    u_sorted = pl.pallas_call(
        _moe_sort_kernel,
        grid_spec=pltpu.PrefetchScalarGridSpec(
            num_scalar_prefetch=2,
            grid=(n_tiles,),
            in_specs=[pl.BlockSpec((ts, d), lambda i, o_r, n_r: (i, 0)),
                      pl.BlockSpec((ts, 128), lambda i, o_r, n_r: (i, 0)),
                      pl.BlockSpec(memory_space=pl.ANY)],
            out_specs=pl.BlockSpec(memory_space=pl.ANY),
            scratch_shapes=[pltpu.VMEM((lrows, d), BF16), pltpu.SemaphoreType.DMA((2,))]),
        out_shape=jax.ShapeDtypeStruct((n_pad, d), BF16),
        input_output_aliases={4: 0},
        compiler_params=_cp(("arbitrary",), 40),
        name="moe_sort",
    )(seg_off, seg_chunks, u, dest, jnp.zeros((n_pad, d), BF16))

    nj = fe // tf
    w_col = lambda k, j, te_r, tv_r: jnp.where(tv_r[k] > 0, j, nj - 1)
    y_sorted = pl.pallas_call(
        _moe_expert_kernel,
        grid_spec=pltpu.PrefetchScalarGridSpec(
            num_scalar_prefetch=2,
            grid=(nt_max, nj),
            in_specs=[pl.BlockSpec((tg, d), lambda k, j, te_r, tv_r: (k, 0)),
                      pl.BlockSpec((None, d, tf), lambda k, j, te_r, tv_r: (te_r[k], 0, w_col(k, j, te_r, tv_r))),
                      pl.BlockSpec((None, d, tf), lambda k, j, te_r, tv_r: (te_r[k], 0, w_col(k, j, te_r, tv_r))),
                      pl.BlockSpec((None, tf, d), lambda k, j, te_r, tv_r: (te_r[k], w_col(k, j, te_r, tv_r), 0))],
            out_specs=pl.BlockSpec((tg, d), lambda k, j, te_r, tv_r: (k, 0)),
            scratch_shapes=[pltpu.VMEM((tg, d), F32)]),
        out_shape=jax.ShapeDtypeStruct((n_pad, d), BF16),
        compiler_params=_cp(("parallel", "arbitrary"), 48),
        name="moe_experts",
    )(tile_expert, tile_valid, u_sorted, wg, wu, wd)

    return pl.pallas_call(
        _moe_combine_kernel,
        grid_spec=pltpu.PrefetchScalarGridSpec(
            num_scalar_prefetch=2,
            grid=(n_tiles,),
            in_specs=[pl.BlockSpec((ts, d), lambda i, o_r, n_r: (i, 0)),
                      pl.BlockSpec((None, 6, d), lambda i, o_r, n_r: (mod_row(i), 0, 0)),
                      pl.BlockSpec((1, d), lambda i, o_r, n_r: (0, 0)),
                      pl.BlockSpec((ts, 128), lambda i, o_r, n_r: (i, 0)),
                      pl.BlockSpec((ts, 128), lambda i, o_r, n_r: (i, 0)),
                      pl.BlockSpec(memory_space=pl.ANY)],
            out_specs=pl.BlockSpec((ts, d), lambda i, o_r, n_r: (i, 0)),
            scratch_shapes=[pltpu.VMEM((lrows, d), BF16), pltpu.SemaphoreType.DMA((2,))]),
        out_shape=jax.ShapeDtypeStruct((rows, d), F32),
        compiler_params=_cp(("arbitrary",), 48),
        name="moe_combine",
    )(seg_off, seg_chunks, h, mod_l, g_post.reshape(1, d), dest, gate, y_sorted)


def _rope_tables(lx, tm):
    half = MLA_ROPE // 2
    n_axis = half // 2
    inv_freq = ROPE_THETA ** (-jnp.arange(n_axis, dtype=F32) / n_axis)
    pos = jnp.arange(lx)
    rows = (pos // GRID_W).astype(F32)
    cols = (pos % GRID_W).astype(F32)
    ang = jnp.concatenate([rows[:, None] * inv_freq, cols[:, None] * inv_freq], axis=-1)
    cos, sin = jnp.cos(ang), jnp.sin(ang)
    cos_t = jnp.concatenate([cos, cos], axis=-1)
    sin_t = jnp.concatenate([-sin, sin], axis=-1)
    cos_t = jnp.concatenate([cos_t, jnp.ones((tm, MLA_ROPE), F32)], axis=0)
    sin_t = jnp.concatenate([sin_t, jnp.zeros((tm, MLA_ROPE), F32)], axis=0)
    return cos_t, sin_t


def _mla_weights(q_norm, w_uq, kv_norm, w_ukv):
    dq = MLA_NOPE + MLA_ROPE
    half = MLA_ROPE // 2
    cols = lambda a, b: w_uq[:, a:b]
    nope = [cols(h * dq, h * dq + MLA_NOPE) for h in range(MLA_HEADS)]
    rope = [cols(h * dq + MLA_NOPE, (h + 1) * dq) for h in range(MLA_HEADS)]
    rope_sw = [cols(h * dq + MLA_NOPE + s * half, h * dq + MLA_NOPE + (s + 1) * half)
               for h in range(MLA_HEADS) for s in (1, 0)]
    pad = ((0, MLA_ROPE), (0, 0))
    wq = jnp.pad(jnp.concatenate(nope + rope, axis=1), pad).astype(BF16)
    wqs = jnp.pad(jnp.concatenate(rope_sw, axis=1), pad).astype(BF16)
    qn_ext = jnp.pad(q_norm, (0, MLA_ROPE)).reshape(1, -1)
    perm = np.zeros((MLA_ROPE, MLA_ROPE), np.float32)
    perm[(np.arange(MLA_ROPE) + half) % MLA_ROPE, np.arange(MLA_ROPE)] = 1.0
    return qn_ext, wq, wqs, kv_norm.reshape(1, -1), w_ukv.astype(BF16), jnp.asarray(perm, BF16)


def _pick_tile(n, cands):
    for t in cands:
        if n % t == 0:
            return t
    raise ValueError(f"no tile for {n}")


def kernel(x, c, ctx, c_ctx, w_ada, b_ada, g_pre_mix, g_post_mix, g_pre_ffn, g_post_ffn, w_in, w_out, mla_q_norm, mla_w_uq, mla_kv_norm, mla_w_ukv, ssd_conv_w, ssd_conv_b, ssd_a_log, ssd_dt_bias, ssd_d, ssd_norm, na_rpb, gdn_conv_w, gdn_a_log, gdn_dt_bias, gdn_norm, ffn_w_gate, ffn_w_up, ffn_w_down, moe_router, moe_w_gate, moe_w_up, moe_w_down):
    n_b, lx, d = x.shape
    lc = ctx.shape[1]
    depth = w_ada.shape[0]
    rows_x, rows_c = n_b * lx, n_b * lc
    assert n_b + 1 <= 8 and lx % GRID_W == 0 and lx % lc == 0 and lc % CHUNK == 0
    tm = _pick_tile(math.gcd(lx, rows_c), (512, 256, 128))
    tm_in = _pick_tile(math.gcd(lx, rows_c), (1024, 512, 256, 128))
    tq = _pick_tile(lc, (256, 128))
    tr = _pick_tile(lc, (256, 128))

    cvec = jnp.concatenate([c, c_ctx[None, :], jnp.zeros((8 - n_b - 1, d), F32)], axis=0)
    mod = _ada_call(cvec, w_ada, b_ada).reshape(depth, 8, 6, d)
    cos_t, sin_t = _rope_tables(lx, tm)
    na_bias = _na_bias_table(na_rpb.reshape((-1,) + na_rpb.shape[2:]), lx // GRID_W, lc)
    h_all = jnp.concatenate([x.reshape(rows_x, d), ctx.reshape(rows_c, d)], axis=0)

    for i in range(depth):
        need_ctx = i < depth - 1
        w_main, w_small = _regroup_w_in(w_in[i])
        proj, small = _inproj_call(h_all, mod[i], g_pre_mix[i], w_main, w_small, n_b=n_b, lx=lx, tm=tm_in)

        mla_w = _mla_weights(mla_q_norm[i], mla_w_uq[i], mla_kv_norm[i], mla_w_ukv[i])
        q_a, k_a, v_a = _mla_prep_call(proj, *mla_w[:5], cos_t, sin_t, mla_w[5], n_b=n_b, lx=lx, tm=tm)
        ya = _mla_attn_call(q_a, k_a, v_a, n_b=n_b, lx=lx, lc=lc, tq=tq, need_ctx=need_ctx)

        xbc = _conv_call(proj, ssd_conv_w[i], ssd_conv_b[i], col_off=P_XBC, n_b=n_b, lx=lx, lc=lc, tr=tr)
        ys = _ssd_call(xbc, small, ssd_a_log[i], ssd_dt_bias[i], ssd_d[i], n_b=n_b, lx=lx, lc=lc)

        yn = _na_call(proj, na_bias, layer=i, n_b=n_b, lx=lx, lc=lc, need_ctx=need_ctx)

        qkv = _conv_call(proj, gdn_conv_w[i], jnp.zeros((3 * GDN_DIM,), F32), col_off=P_GQKV,
                         n_b=n_b, lx=lx, lc=lc, tr=tr)
        og = _gdn_call(qkv, small, gdn_a_log[i], gdn_dt_bias[i], n_b=n_b, lx=lx, lc=lc)

        n_rows = rows_x + rows_c if need_ctx else rows_x
        h_mid = _mixout_call(h_all, mod[i], g_post_mix[i], ya, ys, proj, ssd_norm[i], yn, og, gdn_norm[i],
                             w_out[i].astype(BF16), n_b=n_b, lx=lx, n_rows=n_rows, tm=tm)
        j = i // 2
        if i % 2 == 0:
            h_all = _ffn_call(h_mid, mod[i], g_pre_ffn[i], g_post_ffn[i], ffn_w_gate[j].astype(BF16),
                              ffn_w_up[j].astype(BF16), ffn_w_down[j].astype(BF16), n_b=n_b, lx=lx, tm=tm, tf=512)
        else:
            router_f32 = jnp.pad(moe_router[j], ((0, 0), (0, 128 - N_EXPERTS)))
            router_hi = router_f32.astype(BF16)
            router_pad = jnp.stack([router_hi, (router_f32 - router_hi.astype(F32)).astype(BF16)])
            h_all = _moe_call(h_mid, mod[i], g_pre_ffn[i], g_post_ffn[i], router_pad, moe_w_gate[j].astype(BF16),
                              moe_w_up[j].astype(BF16), moe_w_down[j].astype(BF16), n_b=n_b, lx=lx, tm=tm, tf=256)
    return h_all[:rows_x].reshape(n_b, lx, d)
```

```python
import functools
import math

import numpy as np
import jax
import jax.numpy as jnp
from jax import lax
from jax.experimental import pallas as pl
from jax.experimental.pallas import tpu as pltpu

F32 = jnp.float32
BF16 = jnp.bfloat16
HIGHEST = lax.Precision.HIGHEST

GRID_W = 64
EPS = 1e-6
ROPE_THETA = 10000.0
CHUNK = 128
CONV_W = 5
MLA_HEADS, MLA_NOPE, MLA_ROPE, MLA_V = 4, 128, 64, 128
MLA_Q_LORA, MLA_KV_LORA = 448, 128
SSD_HEADS, SSD_HEAD_DIM, SSD_STATE, SSD_GROUPS = 8, 64, 128, 2
SSD_D_INNER = SSD_HEADS * SSD_HEAD_DIM
SSD_CONV_DIM = SSD_D_INNER + 2 * SSD_GROUPS * SSD_STATE
NA_HEADS, NA_HEAD_DIM = 4, 128
NA_DIM = NA_HEADS * NA_HEAD_DIM
NA_WIN_ROWS, NA_WIN_COLS = 8, 16
GDN_HEADS, GDN_HEAD_DIM = 4, 128
GDN_DIM = GDN_HEADS * GDN_HEAD_DIM
N_EXPERTS, TOP_K = 8, 2
MLA_COLS = MLA_Q_LORA + MLA_KV_LORA + MLA_ROPE
SSD_COLS = SSD_D_INNER + SSD_CONV_DIM + 2 * SSD_HEADS
NA_COLS = 3 * NA_DIM
GDN_COLS = 4 * GDN_DIM + 4 * GDN_HEADS

P_XBC, P_ZSSD, P_NAQ, P_NAK, P_NAV = 0, 1024, 1536, 2048, 2560
P_GQKV, P_GZ, P_MLA = 3072, 4608, 5120
P_MAIN = 5760
P_SMALL = 128
NEG = -1e30
VMEM_MB = 1024 * 1024
VMEM_STREAM_MB, VMEM_ATTN_MB, VMEM_WEIGHTS_MB = 40, 48, 56


def _cp(sem, mb):
    return pltpu.CompilerParams(dimension_semantics=sem, vmem_limit_bytes=mb * VMEM_MB)


def _dot(a, b):
    return jnp.dot(a, b, preferred_element_type=F32)


def _dot_nt(a, b, precision=None):
    return lax.dot_general(a, b, (((1,), (1,)), ((), ())), preferred_element_type=F32, precision=precision)


def _dot_tn(a, b):
    return lax.dot_general(a, b, (((0,), (0,)), ((), ())), preferred_element_type=F32)


def _sigmoid(x):
    return 1.0 / (1.0 + jnp.exp(-x))


def _silu(x):
    return x * _sigmoid(x)


def _softplus(x):
    return jnp.maximum(x, 0.0) + jnp.log(1.0 + jnp.exp(-jnp.abs(x)))


def _rms_scale(x):
    return lax.rsqrt(jnp.mean(x * x, axis=-1, keepdims=True) + EPS)


def _regroup_w_in(w):
    o_mla, o_ssd = 0, MLA_COLS
    o_na, o_gdn = o_ssd + SSD_COLS, o_ssd + SSD_COLS + NA_COLS
    main_segs = [
        (o_ssd + SSD_D_INNER, SSD_CONV_DIM),
        (o_ssd, SSD_D_INNER),
        (o_na, NA_COLS),
        (o_gdn, 4 * GDN_DIM),
        (o_mla + MLA_Q_LORA, MLA_KV_LORA),
        (o_mla, MLA_Q_LORA),
        (o_mla + MLA_Q_LORA + MLA_KV_LORA, MLA_ROPE),
    ]
    small_segs = [(o_ssd + SSD_D_INNER + SSD_CONV_DIM, 2 * SSD_HEADS), (o_gdn + 4 * GDN_DIM, 4 * GDN_HEADS)]
    assert sum(n for _, n in main_segs) == P_MAIN
    main = jnp.concatenate([w[:, a:a + n] for a, n in main_segs], axis=1).astype(BF16)
    n_small = sum(n for _, n in small_segs)
    small = jnp.concatenate([w[:, a:a + n] for a, n in small_segs]
                            + [jnp.zeros((w.shape[0], P_SMALL - n_small), w.dtype)], axis=1).astype(BF16)
    return main, small


def _ada_kernel(c_ref, w_ref, b_ref, o_ref):
    s = _silu(c_ref[...]).astype(BF16)
    o_ref[...] = _dot(s, w_ref[...].astype(BF16)) + b_ref[...]


def _ada_call(cvec, w_ada, b_ada):
    depth, d, n = w_ada.shape
    tn = 1024
    return pl.pallas_call(
        _ada_kernel,
        grid=(depth, n // tn),
        in_specs=[pl.BlockSpec((8, d), lambda l, j: (0, 0)),
                  pl.BlockSpec((None, d, tn), lambda l, j: (l, 0, j)),
                  pl.BlockSpec((None, 1, tn), lambda l, j: (l, 0, j))],
        out_specs=pl.BlockSpec((None, 8, tn), lambda l, j: (l, 0, j)),
        out_shape=jax.ShapeDtypeStruct((depth, 8, n), F32),
        compiler_params=_cp(("parallel", "parallel"), VMEM_STREAM_MB),
        name="adaln",
    )(cvec, w_ada, b_ada.reshape(depth, 1, n))


def _inproj_kernel(h_ref, mod_ref, g_ref, w_ref, ws_ref, o_ref, os_ref, u_scr):
    @pl.when(pl.program_id(1) == 0)
    def _():
        x = h_ref[...]
        y = x * _rms_scale(x) * g_ref[...]
        u = (y * (1.0 + mod_ref[1:2, :]) + mod_ref[0:1, :]).astype(BF16)
        u_scr[...] = u
        os_ref[...] = _dot(u, ws_ref[...])

    o_ref[...] = _dot(u_scr[...], w_ref[...]).astype(BF16)


def _inproj_call(h, mod_l, g_pre, w_main, w_small, *, n_b, lx, tm):
    rows, d = h.shape
    tn = _pick_tile(P_MAIN, (1920, 640, 128))
    nxt, per_b = n_b * lx // tm, lx // tm

    def mod_idx(i, j):
        return (jnp.where(i < nxt, i // per_b, n_b), 0, 0)

    return pl.pallas_call(
        _inproj_kernel,
        grid=(rows // tm, P_MAIN // tn),
        in_specs=[pl.BlockSpec((tm, d), lambda i, j: (i, 0)),
                  pl.BlockSpec((None, 6, d), mod_idx),
                  pl.BlockSpec((1, d), lambda i, j: (0, 0)),
                  pl.BlockSpec((d, tn), lambda i, j: (0, j)),
                  pl.BlockSpec((d, P_SMALL), lambda i, j: (0, 0))],
        out_specs=[pl.BlockSpec((tm, tn), lambda i, j: (i, j)),
                   pl.BlockSpec((tm, P_SMALL), lambda i, j: (i, 0))],
        out_shape=[jax.ShapeDtypeStruct((rows, P_MAIN), BF16),
                   jax.ShapeDtypeStruct((rows, P_SMALL), F32)],
        scratch_shapes=[pltpu.VMEM((tm, d), BF16)],
        compiler_params=_cp(("parallel", "arbitrary"), VMEM_WEIGHTS_MB),
        name="in_proj",
    )(h, mod_l, g_pre.reshape(1, d), w_main, w_small)


HALO = 16


def _conv_kernel(prev_ref, cur_ref, next_ref, w_ref, b_ref, o_ref, ext_scr, *, tr, blocks_x, seq_x, seq_c):
    i = pl.program_id(0)
    in_x = i < blocks_x
    pos = jnp.where(in_x, i % seq_x, (i - blocks_x) % seq_c)
    last_pos = jnp.where(in_x, seq_x - 1, seq_c - 1)
    ext_scr[0:HALO, :] = jnp.where(pos == 0, 0.0, prev_ref[...].astype(F32))
    ext_scr[HALO:HALO + tr, :] = cur_ref[...].astype(F32)
    ext_scr[HALO + tr:2 * HALO + tr, :] = jnp.where(pos == last_pos, 0.0, next_ref[...].astype(F32))
    acc = b_ref[...] + w_ref[0:1, :] * ext_scr[HALO - 2:HALO - 2 + tr, :]
    for k in range(1, CONV_W):
        acc = acc + w_ref[k:k + 1, :] * ext_scr[HALO - 2 + k:HALO - 2 + k + tr, :]
    o_ref[...] = _silu(acc).astype(BF16)


def _conv_call(proj, w, b, *, col_off, n_b, lx, lc, tr):
    rows = proj.shape[0]
    c = w.shape[1]
    cb = col_off // c
    assert cb * c == col_off
    hb = tr // HALO
    n_halo = rows // HALO
    kern = functools.partial(_conv_kernel, tr=tr, blocks_x=n_b * lx // tr, seq_x=lx // tr, seq_c=lc // tr)
    return pl.pallas_call(
        kern,
        grid=(rows // tr,),
        in_specs=[pl.BlockSpec((HALO, c), lambda i: (jnp.maximum(i * hb - 1, 0), cb)),
                  pl.BlockSpec((tr, c), lambda i: (i, cb)),
                  pl.BlockSpec((HALO, c), lambda i: (jnp.minimum((i + 1) * hb, n_halo - 1), cb)),
                  pl.BlockSpec((CONV_W, c), lambda i: (0, 0)),
                  pl.BlockSpec((1, c), lambda i: (0, 0))],
        out_specs=pl.BlockSpec((tr, c), lambda i: (i, 0)),
        out_shape=jax.ShapeDtypeStruct((rows, c), BF16),
        scratch_shapes=[pltpu.VMEM((tr + 2 * HALO, c), F32)],
        compiler_params=_cp(("parallel",), VMEM_STREAM_MB),
        name="dwconv_silu",
    )(proj, proj, proj, w, b.reshape(1, c))


def _mla_prep_kernel(p_ref, qn_ref, wq_ref, wqs_ref, kvn_ref, wkv_ref, cos_ref, sin_ref, perm_ref,
                     q_ref, k_ref, v_ref):
    p = p_ref[...].astype(F32)
    ckv = p[:, 0:MLA_KV_LORA]
    ce = p[:, MLA_KV_LORA:]
    lane = lax.broadcasted_iota(jnp.int32, ce.shape, 1)
    ssq = jnp.sum(jnp.where(lane < MLA_Q_LORA, ce * ce, 0.0), axis=-1, keepdims=True)
    cqn = (ce * lax.rsqrt(ssq / MLA_Q_LORA + EPS) * qn_ref[...]).astype(BF16)
    ckvn = (ckv * _rms_scale(ckv) * kvn_ref[...]).astype(BF16)
    q = _dot(cqn, wq_ref[...])
    qs = _dot(cqn, wqs_ref[...])
    kv = _dot(ckvn, wkv_ref[...])
    cos, sin = cos_ref[...], sin_ref[...]
    kr = p_ref[:, MLA_KV_LORA + MLA_Q_LORA:]
    kr_rot = kr.astype(F32) * cos + _dot(kr, perm_ref[...]) * sin
    nr = MLA_HEADS * MLA_NOPE
    scale = (MLA_NOPE + MLA_ROPE) ** -0.5
    ones_col = jnp.where(lax.broadcasted_iota(jnp.int32, (p.shape[0], MLA_V), 1) == 0, 1.0, 0.0).astype(BF16)
    for h in range(MLA_HEADS):
        q_ref[h, :, 0:MLA_NOPE] = (q[:, h * MLA_NOPE:(h + 1) * MLA_NOPE] * scale).astype(BF16)
        qr = q[:, nr + h * MLA_ROPE:nr + (h + 1) * MLA_ROPE] * cos + qs[:, h * MLA_ROPE:(h + 1) * MLA_ROPE] * sin
        q_ref[h, :, MLA_NOPE:] = (qr * scale).astype(BF16)
        hv = h * (MLA_NOPE + MLA_V)
        k_ref[h, :, 0:MLA_NOPE] = kv[:, hv:hv + MLA_NOPE].astype(BF16)
        k_ref[h, :, MLA_NOPE:] = kr_rot.astype(BF16)
        v_ref[h, :, 0:MLA_V] = kv[:, hv + MLA_NOPE:hv + MLA_NOPE + MLA_V].astype(BF16)
        v_ref[h, :, MLA_V:] = ones_col


def _mla_prep_call(proj, qn_ext, wq, wqs, kvn, wkv, cos_t, sin_t, perm, *, n_b, lx, tm):
    rows = proj.shape[0]
    nxt, per_b = n_b * lx // tm, lx // tm
    dk = MLA_NOPE + MLA_ROPE
    rope_idx = lambda i: (jnp.where(i < nxt, i % per_b, per_b), 0)
    full = lambda a: pl.BlockSpec(a.shape, lambda i: (0,) * a.ndim)
    return pl.pallas_call(
        _mla_prep_kernel,
        grid=(rows // tm,),
        in_specs=[pl.BlockSpec((tm, MLA_COLS), lambda i: (i, P_MLA // MLA_COLS)),
                  full(qn_ext), full(wq), full(wqs), full(kvn), full(wkv),
                  pl.BlockSpec((tm, MLA_ROPE), rope_idx), pl.BlockSpec((tm, MLA_ROPE), rope_idx),
                  full(perm)],
        out_specs=[pl.BlockSpec((MLA_HEADS, tm, dk), lambda i: (0, i, 0)),
                   pl.BlockSpec((MLA_HEADS, tm, dk), lambda i: (0, i, 0)),
                   pl.BlockSpec((MLA_HEADS, tm, 2 * MLA_V), lambda i: (0, i, 0))],
        out_shape=[jax.ShapeDtypeStruct((MLA_HEADS, rows, dk), BF16),
                   jax.ShapeDtypeStruct((MLA_HEADS, rows, dk), BF16),
                   jax.ShapeDtypeStruct((MLA_HEADS, rows, 2 * MLA_V), BF16)],
        compiler_params=_cp(("parallel",), VMEM_STREAM_MB),
        name="mla_prep",
    )(proj, qn_ext, wq, wqs, kvn, wkv, cos_t, sin_t, perm)


def _softmax_pv(scores, values):
    m = functools.reduce(jnp.maximum, [jnp.max(s, axis=-1, keepdims=True) for s in scores])
    ps = [jnp.exp(s - m) for s in scores]
    den = functools.reduce(lambda a, b: a + b, [jnp.sum(p, axis=-1, keepdims=True) for p in ps])
    num = functools.reduce(lambda a, b: a + b, [_dot(p.astype(BF16), v) for p, v in zip(ps, values)])
    return num / den


def _softmax_pv_aug(scores, values_aug):
    m = functools.reduce(jnp.maximum, [jnp.max(s, axis=-1, keepdims=True) for s in scores])
    acc = functools.reduce(lambda a, b: a + b,
                           [_dot(jnp.exp((s - m).astype(BF16)), v) for s, v in zip(scores, values_aug)])
    return acc[:, 0:MLA_V] / acc[:, MLA_V:MLA_V + 1]


MLA_HEADS_PER_STEP = 4


def _mla_attn_kernel(q_ref, kx_ref, vx_ref, kc_ref, vc_ref, o_ref, *, nqx):
    qi = pl.program_id(2)
    heads = range(q_ref.shape[0])

    @pl.when(qi < nqx)
    def _():
        outs = [_softmax_pv_aug([_dot_nt(q_ref[h], kx_ref[h]), _dot_nt(q_ref[h], kc_ref[h])], [vx_ref[h], vc_ref[h]])
                for h in heads]
        o_ref[...] = jnp.concatenate(outs, axis=1).astype(BF16)

    @pl.when(qi >= nqx)
    def _():
        outs = [_softmax_pv_aug([_dot_nt(q_ref[h], kc_ref[h])], [vc_ref[h]]) for h in heads]
        o_ref[...] = jnp.concatenate(outs, axis=1).astype(BF16)


def _mla_attn_call(q, k, v, *, n_b, lx, lc, tq, need_ctx):
    rows = q.shape[1]
    dk = q.shape[2]
    nqx, nqc = lx // tq, lc // tq
    nq = nqx + (nqc if need_ctx else 0)
    nbx = n_b * lx // lc

    def q_row(b, qi):
        return jnp.where(qi < nqx, b * nqx + qi, n_b * nqx + b * nqc + (qi - nqx))

    kern = functools.partial(_mla_attn_kernel, nqx=nqx)
    out_rows = rows if need_ctx else n_b * lx
    hp = MLA_HEADS_PER_STEP
    return pl.pallas_call(
        kern,
        grid=(n_b, MLA_HEADS // hp, nq),
        in_specs=[pl.BlockSpec((hp, tq, dk), lambda b, h, qi: (h, q_row(b, qi), 0)),
                  pl.BlockSpec((hp, lx, dk), lambda b, h, qi: (h, b, 0)),
                  pl.BlockSpec((hp, lx, 2 * MLA_V), lambda b, h, qi: (h, b, 0)),
                  pl.BlockSpec((hp, lc, dk), lambda b, h, qi: (h, nbx + b, 0)),
                  pl.BlockSpec((hp, lc, 2 * MLA_V), lambda b, h, qi: (h, nbx + b, 0))],
        out_specs=pl.BlockSpec((tq, hp * MLA_V), lambda b, h, qi: (q_row(b, qi), h)),
        out_shape=jax.ShapeDtypeStruct((out_rows, MLA_HEADS * MLA_V), BF16),
        compiler_params=_cp(("parallel", "parallel", "arbitrary"), VMEM_ATTN_MB),
        name="mla_attn",
    )(q, k, v, k, v)


def _na_plan(g_rows, lc):
    wr = min(NA_WIN_ROWS, g_rows)
    rg = next(r for r in (4, 2, 1) if g_rows % r == 0 and lc % (r * GRID_W) == 0)
    wk = min(rg + wr - 1, g_rows)
    n_groups = g_rows // rg
    ks = np.clip(np.arange(n_groups) * rg - wr // 2, 0, g_rows - wk)
    r = np.arange(g_rows)
    rs = np.clip(r - wr // 2, 0, g_rows - wr)
    q_off = (r - np.repeat(ks, rg)).reshape(n_groups, rg)
    rel = (rs - np.repeat(ks, rg)).reshape(n_groups, rg)
    assert (rel >= 0).all() and (rel + wr <= wk).all()
    pats = [tuple(q_off[g]) + tuple(rel[g]) for g in range(n_groups)]
    uniq = sorted(set(pats))
    var = np.array([uniq.index(p) for p in pats], np.int32)
    q_off_v = np.array([p[:rg] for p in uniq])
    rel_v = np.array([p[rg:] for p in uniq])
    return wr, rg, wk, ks.astype(np.int32), var, q_off_v, rel_v


def _na_bias_table(rpb, g_rows, lc):
    wr, rg, wk, _, _, q_off_v, rel_v = _na_plan(g_rows, lc)
    col_start = np.clip(np.arange(GRID_W) - NA_WIN_COLS // 2, 0, GRID_W - NA_WIN_COLS)
    cc = np.arange(GRID_W)
    col_ok = (cc[None, :] >= col_start[:, None]) & (cc[None, :] < col_start[:, None] + NA_WIN_COLS)
    dc = np.clip(cc[None, :] - cc[:, None] + NA_WIN_COLS - 1, 0, 2 * NA_WIN_COLS - 2)
    n_dc = 2 * NA_WIN_COLS - 1
    nh = rpb.shape[0]
    onehot = (dc.reshape(-1)[:, None] == np.arange(n_dc)[None, :]).astype(np.float32)
    g = jnp.einsum('hab,yb->hay', rpb.astype(F32), jnp.asarray(onehot), precision=HIGHEST)
    g = jnp.where(col_ok[None, None], g.reshape(nh, -1, GRID_W, GRID_W), NEG)
    g = jnp.pad(g, ((0, 0), (wk, wk), (0, 0), (0, 0)), constant_values=NEG)
    w = np.arange(wk)
    tabs = []
    for v in range(q_off_v.shape[0]):
        rows_v = []
        for j in range(rg):
            a0 = wk - q_off_v[v, j] + NA_WIN_ROWS - 1
            own = (w >= rel_v[v, j]) & (w < rel_v[v, j] + wr)
            blk = jnp.where(jnp.asarray(own)[None, :, None, None], g[:, a0:a0 + wk], NEG)
            rows_v.append(jnp.transpose(blk, (0, 2, 1, 3)).reshape(nh, GRID_W, wk * GRID_W))
        tabs.append(jnp.concatenate(rows_v, axis=-2))
    return jnp.stack(tabs, axis=0)


def _na_kernel(var_ref, ks_ref, q_ref, kx_ref, vx_ref, kc_ref, vc_ref, bias_ref, o_ref, *, n_groups, wk, scale):
    g = pl.program_id(1)
    hd = NA_HEAD_DIM
    cols = [slice(h * hd, (h + 1) * hd) for h in range(NA_HEADS)]

    @pl.when(g < n_groups)
    def _():
        start = pl.multiple_of(ks_ref[g] * GRID_W, GRID_W)
        rows = pl.ds(start, wk * GRID_W)
        outs = []
        for h, c in enumerate(cols):
            q = q_ref[:, c]
            sl = _dot_nt(q, kx_ref[rows, c]) * scale + bias_ref[h]
            sc = _dot_nt(q, kc_ref[:, c]) * scale
            outs.append(_softmax_pv([sl, sc], [vx_ref[rows, c], vc_ref[:, c]]))
        o_ref[...] = jnp.concatenate(outs, axis=1).astype(BF16)

    @pl.when(g >= n_groups)
    def _():
        outs = [_softmax_pv([_dot_nt(q_ref[:, c], kc_ref[:, c]) * scale], [vc_ref[:, c]]) for c in cols]
        o_ref[...] = jnp.concatenate(outs, axis=1).astype(BF16)


def _na_call(proj, bias_tab, *, layer, n_b, lx, lc, need_ctx):
    rows = proj.shape[0]
    g_rows = lx // GRID_W
    _, rg, wk, ks, var, _, _ = _na_plan(g_rows, lc)
    n_groups = g_rows // rg
    tq = rg * GRID_W
    nqc = lc // tq
    nq = n_groups + (nqc if need_ctx else 0)
    nbx = n_b * lx // lc
    hd, nd = NA_HEAD_DIM, NA_DIM
    cq, ck, cv = P_NAQ // nd, P_NAK // nd, P_NAV // nd

    def q_row(b, g):
        return jnp.where(g < n_groups, b * n_groups + g, n_b * n_groups + b * nqc + (g - n_groups))

    kern = functools.partial(_na_kernel, n_groups=n_groups, wk=wk, scale=hd ** -0.5)
    grid_spec = pltpu.PrefetchScalarGridSpec(
        num_scalar_prefetch=2,
        grid=(n_b, nq),
        in_specs=[pl.BlockSpec((tq, nd), lambda b, g, var_r, ks_r: (q_row(b, g), cq)),
                  pl.BlockSpec((lx, nd), lambda b, g, var_r, ks_r: (b, ck)),
                  pl.BlockSpec((lx, nd), lambda b, g, var_r, ks_r: (b, cv)),
                  pl.BlockSpec((lc, nd), lambda b, g, var_r, ks_r: (nbx + b, ck)),
                  pl.BlockSpec((lc, nd), lambda b, g, var_r, ks_r: (nbx + b, cv)),
                  pl.BlockSpec((None, NA_HEADS, tq, wk * GRID_W),
                               lambda b, g, var_r, ks_r: (var_r[jnp.minimum(g, n_groups - 1)], layer, 0, 0))],
        out_specs=pl.BlockSpec((tq, nd), lambda b, g, var_r, ks_r: (q_row(b, g), 0)),
    )
    return pl.pallas_call(
        kern,
        grid_spec=grid_spec,
        out_shape=jax.ShapeDtypeStruct((rows if need_ctx else n_b * lx, NA_DIM), BF16),
        compiler_params=_cp(("parallel", "arbitrary"), VMEM_STREAM_MB),
        name="na_attn",
    )(jnp.asarray(var), jnp.asarray(ks), proj, proj, proj, proj, proj, bias_tab)


def _chunk_block(n_b, nxc, ncc):
    def f(b, d, c):
        cc = jnp.where(d == 0, c, ncc - 1 - c)
        cx = jnp.where(d == 0, c - ncc, nxc - 1 - (c - ncc))
        return jnp.where(c < ncc, n_b * nxc + b * ncc + cc, b * nxc + cx)
    return f


def _dir_masks(d):
    row = lax.broadcasted_iota(jnp.int32, (CHUNK, CHUNK), 0)
    col = lax.broadcasted_iota(jnp.int32, (CHUNK, CHUNK), 1)
    diff = (row - col) * jnp.where(d == 0, 1, -1)
    return diff >= 0, diff > 0


def _cumsum_lanes(x, incl):
    cs = jnp.dot(incl.astype(F32), x, preferred_element_type=F32, precision=HIGHEST)
    return cs, cs.T


def _lane_vec(vals, offset):
    flat = vals.reshape(-1).astype(F32)
    return jnp.pad(flat, (offset, P_SMALL - offset - flat.shape[0])).reshape(1, P_SMALL)


def _ssd_prep_kernel(xbc_ref, sm_ref, alog_ref, dtb_ref, dsk_ref, yp_ref, xw_ref, ea_ref):
    nh, hp, ns = SSD_HEADS, SSD_HEAD_DIM, SSD_STATE
    gh = nh // SSD_GROUPS
    dt2 = _softplus(sm_ref[...] + dtb_ref[...])
    dta2 = dt2 * (-jnp.exp(alog_ref[...]))
    row = lax.broadcasted_iota(jnp.int32, (CHUNK, CHUNK), 0)
    col = lax.broadcasted_iota(jnp.int32, (CHUNK, CHUNK), 1)
    incl = [row >= col, row <= col]
    cs = [jnp.dot(m.astype(F32), dta2, preferred_element_type=F32, precision=HIGHEST) for m in incl]
    cs_t = [x.T for x in cs]
    tot2 = jnp.sum(dta2, axis=0, keepdims=True)
    dsk = dsk_ref[...]
    bo, co = SSD_D_INNER, SSD_D_INNER + SSD_GROUPS * ns
    scores = [_dot_nt(xbc_ref[:, co + g * ns:co + (g + 1) * ns], xbc_ref[:, bo + g * ns:bo + (g + 1) * ns])
              for g in range(SSD_GROUPS)]
    xs = [xbc_ref[:, h * hp:(h + 1) * hp].astype(F32) for h in range(nh)]
    yp, xw = [], []
    for d in range(2):
        for h in range(nh):
            ln = d * nh + h
            a_c, a_r = cs[d][:, ln:ln + 1], cs_t[d][ln:ln + 1, :]
            dec = jnp.where(incl[d], jnp.exp(jnp.where(incl[d], a_c - a_r, 0.0)), 0.0)
            m = (scores[h // gh] * dec).astype(BF16)
            y = _dot(m, (xs[h] * dt2[:, ln:ln + 1]).astype(BF16))
            yp.append(y + dsk[:, h:h + 1] * xs[h] if d == 0 else y)
            xw.append((xs[h] * (jnp.exp(tot2[:, ln:ln + 1] - a_c) * dt2[:, ln:ln + 1])).astype(BF16))
    yp_ref[...] = jnp.concatenate(yp, axis=1)
    xw_ref[...] = jnp.concatenate(xw, axis=1)
    ea_ref[...] = jnp.concatenate([jnp.exp(cs[0]), jnp.exp(cs[1])], axis=1)


def _ssd_scan_kernel(*refs):
    (bc0, yp0, xw0, ea0, sm0, bc1, yp1, xw1, ea1, sm1, alog_ref, dtb_ref, y0_ref, y1_ref, s_scr) = refs
    nh, hp, ns = SSD_HEADS, SSD_HEAD_DIM, SSD_STATE
    gh = nh // SSD_GROUPS

    @pl.when(pl.program_id(1) == 0)
    def _():
        s_scr[...] = jnp.zeros_like(s_scr)

    neg_a = -jnp.exp(alog_ref[...])
    for d, (bc, yp, xw, ea, sm, y_ref) in enumerate([(bc0, yp0, xw0, ea0, sm0, y0_ref),
                                                      (bc1, yp1, xw1, ea1, sm1, y1_ref)]):
        c_dec = jnp.exp(jnp.sum(_softplus(sm[...] + dtb_ref[...]) * neg_a, axis=0, keepdims=True))
        e_acum = ea[...]
        ys, states = [], []
        for g in range(SSD_GROUPS):
            bg = bc[:, g * ns:(g + 1) * ns]
            cg = bc[:, SSD_GROUPS * ns + g * ns:SSD_GROUPS * ns + (g + 1) * ns]
            s_g = s_scr[d, :, g * gh * hp:(g + 1) * gh * hp]
            y_int = _dot(cg, s_g.astype(BF16))
            upd = _dot_tn(bg, xw[:, g * gh * hp:(g + 1) * gh * hp])
            for hh in range(gh):
                h = g * gh + hh
                ln = d * nh + h
                sl = slice(hh * hp, (hh + 1) * hp)
                ys.append(yp[:, h * hp:(h + 1) * hp] + y_int[:, sl] * e_acum[:, ln:ln + 1])
                states.append(s_g[:, sl] * c_dec[:, ln:ln + 1] + upd[:, sl])
        y_ref[...] = jnp.concatenate(ys, axis=1)
        s_scr[d] = jnp.concatenate(states, axis=1)


def _ssd_call(xbc, small, a_log, dt_bias, d_skip, *, n_b, lx, lc):
    rows = xbc.shape[0]
    nxc, ncc = lx // CHUNK, lc // CHUNK
    blk = _chunk_block(n_b, nxc, ncc)
    nh, di = SSD_HEADS, SSD_D_INNER
    alog_v, dtb_v = _lane_vec(a_log, 0), _lane_vec(dt_bias, 0)
    ypart, xw, ea = pl.pallas_call(
        _ssd_prep_kernel,
        grid=(rows // CHUNK,),
        in_specs=[pl.BlockSpec((CHUNK, SSD_CONV_DIM), lambda i: (i, 0)),
                  pl.BlockSpec((CHUNK, P_SMALL), lambda i: (i, 0)),
                  pl.BlockSpec((1, P_SMALL), lambda i: (0, 0)),
                  pl.BlockSpec((1, P_SMALL), lambda i: (0, 0)),
                  pl.BlockSpec((1, nh), lambda i: (0, 0))],
        out_specs=[pl.BlockSpec((CHUNK, 2 * di), lambda i: (i, 0)),
                   pl.BlockSpec((CHUNK, 2 * di), lambda i: (i, 0)),
                   pl.BlockSpec((CHUNK, 2 * P_SMALL), lambda i: (i, 0))],
        out_shape=[jax.ShapeDtypeStruct((rows, 2 * di), F32),
                   jax.ShapeDtypeStruct((rows, 2 * di), BF16),
                   jax.ShapeDtypeStruct((rows, 2 * P_SMALL), F32)],
        compiler_params=_cp(("parallel",), VMEM_STREAM_MB),
        name="ssd_prep",
    )(xbc, small, alog_v, dtb_v, d_skip.reshape(1, nh))

    def dir_specs(d):
        at = lambda b, c: blk(b, d, c)
        return [pl.BlockSpec((CHUNK, SSD_CONV_DIM - di), lambda b, c: (at(b, c), 1)),
                pl.BlockSpec((CHUNK, di), lambda b, c: (at(b, c), d)),
                pl.BlockSpec((CHUNK, di), lambda b, c: (at(b, c), d)),
                pl.BlockSpec((CHUNK, P_SMALL), lambda b, c: (at(b, c), d)),
                pl.BlockSpec((CHUNK, P_SMALL), lambda b, c: (at(b, c), 0))]

    const = pl.BlockSpec((1, P_SMALL), lambda b, c: (0, 0))
    return pl.pallas_call(
        _ssd_scan_kernel,
        grid=(n_b, ncc + nxc),
        in_specs=dir_specs(0) + dir_specs(1) + [const, const],
        out_specs=[pl.BlockSpec((CHUNK, di), lambda b, c: (blk(b, 0, c), 0)),
                   pl.BlockSpec((CHUNK, di), lambda b, c: (blk(b, 1, c), 0))],
        out_shape=[jax.ShapeDtypeStruct((rows, di), F32), jax.ShapeDtypeStruct((rows, di), F32)],
        scratch_shapes=[pltpu.VMEM((2, SSD_STATE, di), F32)],
        compiler_params=_cp(("parallel", "arbitrary"), VMEM_STREAM_MB),
        name="ssd_scan",
    )(xbc, ypart, xw, ea, small, xbc, ypart, xw, ea, small, alog_v, dtb_v)


SOLVE_BLOCK = 16


def _unit_tri_solve_many(n_mats, rhss):
    ln = n_mats[0].shape[0]
    row = lax.broadcasted_iota(jnp.int32, (ln, ln), 0)
    col = lax.broadcasted_iota(jnp.int32, (ln, ln), 1)
    same_block = lambda s: (row // s) == (col // s)
    eye = jnp.where(row == col, 1.0, 0.0)
    mm = lambda a, b: _dot(a.astype(BF16), b.astype(BF16))
    ms = [jnp.where(same_block(SOLVE_BLOCK), -n, 0.0) for n in n_mats]
    xs = [eye + m for m in ms]
    mps = ms
    k = 1
    while 2 * k < SOLVE_BLOCK:
        mps = [mm(m, m) for m in mps]
        xs = [x + mm(x, m) for x, m in zip(xs, mps)]
        k *= 2
    s = SOLVE_BLOCK
    while s < ln:
        joins = [jnp.where(same_block(2 * s), jnp.where(same_block(s), 0.0, n), 0.0) for n in n_mats]
        xs = [x - mm(mm(x, e), x) for x, e in zip(xs, joins)]
        s *= 2
    return [mm(x, r) for x, r in zip(xs, rhss)]


GDN_PACK = 5 * GDN_DIM
GDN_G_LANE = 2 * SSD_HEADS
GDN_B_LANE = 2 * SSD_HEADS + 2 * GDN_HEADS


def _gdn_prep_kernel(qkv_ref, sm_ref, alog_ref, dtb_ref, o_ref):
    nh, hd = GDN_HEADS, GDN_HEAD_DIM
    sm = sm_ref[...]
    g2 = -jnp.exp(alog_ref[...]) * _softplus(sm + dtb_ref[...])
    beta2 = _sigmoid(sm)
    row = lax.broadcasted_iota(jnp.int32, (CHUNK, CHUNK), 0)
    col = lax.broadcasted_iota(jnp.int32, (CHUNK, CHUNK), 1)
    incl = [row >= col, row <= col]
    strict = [row > col, row < col]
    cs = [jnp.dot(m.astype(F32), g2, preferred_element_type=F32, precision=HIGHEST) for m in incl]
    cs_t = [x.T for x in cs]
    gtot2 = jnp.sum(g2, axis=0, keepdims=True)
    qn, kn, kn_b, vv, qk_raw = [], [], [], [], []
    for h in range(nh):
        qh = qkv_ref[:, h * hd:(h + 1) * hd].astype(F32)
        kh = qkv_ref[:, GDN_DIM + h * hd:GDN_DIM + (h + 1) * hd].astype(F32)
        vv.append(qkv_ref[:, 2 * GDN_DIM + h * hd:2 * GDN_DIM + (h + 1) * hd].astype(F32))
        qn.append(qh * (lax.rsqrt(jnp.sum(qh * qh, axis=-1, keepdims=True) + EPS) * hd ** -0.5))
        kn.append(kh * lax.rsqrt(jnp.sum(kh * kh, axis=-1, keepdims=True) + EPS))
        kn_b.append(kn[h].astype(BF16))
        qk_raw.append(_dot_nt(qn[h].astype(BF16), kn_b[h]))
    n_mats, rhss, qks, qds, kds = [], [], [], [], []
    for d in range(2):
        for h in range(nh):
            lg, lb = GDN_G_LANE + d * nh + h, GDN_B_LANE + d * nh + h
            gcc, gcr = cs[d][:, lg:lg + 1], cs_t[d][lg:lg + 1, :]
            beta, gtot = beta2[:, lb:lb + 1], gtot2[:, lg:lg + 1]
            dec = jnp.where(incl[d], jnp.exp(jnp.where(incl[d], gcc - gcr, 0.0)), 0.0)
            kb = kn[h] * beta
            n_mats.append(jnp.where(strict[d], _dot_nt(kb.astype(BF16), kn_b[h]) * dec, 0.0))
            e_gc = jnp.exp(gcc)
            rhss.append(jnp.concatenate([vv[h] * beta, kb * e_gc], axis=1))
            qks.append(qk_raw[h] * dec)
            qds.append(qn[h] * e_gc)
            kds.append(kn[h] * jnp.exp(gtot - gcc))
    sols = _unit_tri_solve_many(n_mats, rhss)
    pieces = []
    for d in range(2):
        js = range(d * nh, (d + 1) * nh)
        pieces += [sols[j][:, 0:hd] for j in js] + [sols[j][:, hd:2 * hd] for j in js]
        pieces += [qks[j] for j in js] + [qds[j] for j in js] + [kds[j] for j in js]
    o_ref[...] = jnp.concatenate([p.astype(BF16) for p in pieces], axis=1)


def _gdn_scan_kernel(pk0, sm0, pk1, sm1, alog_ref, dtb_ref, o0_ref, o1_ref, s_scr):
    nh, hd = GDN_HEADS, GDN_HEAD_DIM

    @pl.when(pl.program_id(1) == 0)
    def _():
        s_scr[...] = jnp.zeros_like(s_scr)

    neg_a = -jnp.exp(alog_ref[...])
    for d, (pk_ref, sm_ref, o_ref) in enumerate([(pk0, sm0, o0_ref), (pk1, sm1, o1_ref)]):
        g_end2 = jnp.exp(jnp.sum(neg_a * _softplus(sm_ref[...] + dtb_ref[...]), axis=0, keepdims=True))
        outs, states = [], []
        for h in range(nh):
            lg = GDN_G_LANE + d * nh + h
            part = lambda j: pk_ref[:, (j * nh + h) * hd:(j * nh + h + 1) * hd]
            u, w, qk, qd, kd = part(0), part(1), part(2), part(3), part(4)
            s_h = s_scr[d, :, h * hd:(h + 1) * hd]
            s_b = s_h.astype(BF16)
            v_new = (u.astype(F32) - _dot(w, s_b)).astype(BF16)
            outs.append(_dot(qd, s_b) + _dot(qk, v_new))
            states.append(s_h * g_end2[:, lg:lg + 1] + _dot_tn(kd, v_new))
        o_ref[...] = jnp.concatenate(outs, axis=1)
        s_scr[d] = jnp.concatenate(states, axis=1)


def _gdn_call(qkv, small, a_log, dt_bias, *, n_b, lx, lc):
    rows = qkv.shape[0]
    nxc, ncc = lx // CHUNK, lc // CHUNK
    blk = _chunk_block(n_b, nxc, ncc)
    alog_v, dtb_v = _lane_vec(a_log, GDN_G_LANE), _lane_vec(dt_bias, GDN_G_LANE)
    packed = pl.pallas_call(
        _gdn_prep_kernel,
        grid=(rows // CHUNK,),
        in_specs=[pl.BlockSpec((CHUNK, 3 * GDN_DIM), lambda i: (i, 0)),
                  pl.BlockSpec((CHUNK, P_SMALL), lambda i: (i, 0)),
                  pl.BlockSpec((1, P_SMALL), lambda i: (0, 0)),
                  pl.BlockSpec((1, P_SMALL), lambda i: (0, 0))],
        out_specs=pl.BlockSpec((CHUNK, 2 * GDN_PACK), lambda i: (i, 0)),
        out_shape=jax.ShapeDtypeStruct((rows, 2 * GDN_PACK), BF16),
        compiler_params=_cp(("parallel",), VMEM_STREAM_MB),
        name="gdn_prep",
    )(qkv, small, alog_v, dtb_v)
    def dir_specs(d):
        return [pl.BlockSpec((CHUNK, GDN_PACK), lambda b, c: (blk(b, d, c), d)),
                pl.BlockSpec((CHUNK, P_SMALL), lambda b, c: (blk(b, d, c), 0))]

    const = pl.BlockSpec((1, P_SMALL), lambda b, c: (0, 0))
    return pl.pallas_call(
        _gdn_scan_kernel,
        grid=(n_b, ncc + nxc),
        in_specs=dir_specs(0) + dir_specs(1) + [const, const],
        out_specs=[pl.BlockSpec((CHUNK, GDN_DIM), lambda b, c: (blk(b, 0, c), 0)),
                   pl.BlockSpec((CHUNK, GDN_DIM), lambda b, c: (blk(b, 1, c), 0))],
        out_shape=[jax.ShapeDtypeStruct((rows, GDN_DIM), F32), jax.ShapeDtypeStruct((rows, GDN_DIM), F32)],
        scratch_shapes=[pltpu.VMEM((2, GDN_HEAD_DIM, GDN_DIM), F32)],
        compiler_params=_cp(("parallel", "arbitrary"), VMEM_STREAM_MB),
        name="gdn_scan",
    )(packed, small, packed, small, alog_v, dtb_v)


MIXOUT_SUB_ROWS = 256


def _mixout_kernel(h_ref, mod_ref, gpost_ref, ya_ref, ys0_ref, ys1_ref, zs_ref, sn_ref, yn_ref, og0_ref, og1_ref,
                   zg_ref, gn_ref, w_ref, o_ref):
    hd = GDN_HEAD_DIM
    tm = h_ref.shape[0]
    sub = min(tm, MIXOUT_SUB_ROWS)
    outs = []
    for s in range(0, tm, sub):
        r = slice(s, s + sub)
        ssd = (ys0_ref[r, :] + ys1_ref[r, :]) * _silu(zs_ref[r, :].astype(F32))
        yb = (ssd * _rms_scale(ssd) * sn_ref[...]).astype(BF16)
        gd = og0_ref[r, :] + og1_ref[r, :]
        zg = _silu(zg_ref[r, :].astype(F32))
        yd = []
        for h in range(GDN_HEADS):
            oh = gd[:, h * hd:(h + 1) * hd]
            yd.append((oh * _rms_scale(oh) * gn_ref[...] * zg[:, h * hd:(h + 1) * hd]).astype(BF16))
        parts = [ya_ref[r, :], yb, yn_ref[r, :]] + yd
        widths = [512, 512, 512] + [hd] * GDN_HEADS
        y = None
        off = 0
        for part, wd in zip(parts, widths):
            t = _dot(part, w_ref[off:off + wd, :])
            y = t if y is None else y + t
            off += wd
        outs.append(h_ref[r, :] + mod_ref[2:3, :] * (y * _rms_scale(y) * gpost_ref[...]))
    o_ref[...] = jnp.concatenate(outs, axis=0)


def _mixout_call(h, mod_l, g_post, ya, ys, proj, ssd_norm, yn, og, gdn_norm, w_out, *, n_b, lx, n_rows, tm):
    d = h.shape[1]
    nxt, per_b = n_b * lx // tm, lx // tm
    mod_idx = lambda i: (jnp.where(i < nxt, i // per_b, n_b), 0, 0)
    row = lambda i: (i, 0)
    const = lambda i: (0, 0)
    return pl.pallas_call(
        _mixout_kernel,
        grid=(n_rows // tm,),
        in_specs=[pl.BlockSpec((tm, d), row),
                  pl.BlockSpec((None, 6, d), mod_idx),
                  pl.BlockSpec((1, d), const),
                  pl.BlockSpec((tm, 512), row),
                  pl.BlockSpec((tm, 512), row),
                  pl.BlockSpec((tm, 512), row),
                  pl.BlockSpec((tm, 512), lambda i: (i, P_ZSSD // 512)),
                  pl.BlockSpec((1, 512), const),
                  pl.BlockSpec((tm, 512), row),
                  pl.BlockSpec((tm, 512), row),
                  pl.BlockSpec((tm, 512), row),
                  pl.BlockSpec((tm, 512), lambda i: (i, P_GZ // 512)),
                  pl.BlockSpec((1, GDN_HEAD_DIM), const),
                  pl.BlockSpec((d, d), const)],
        out_specs=pl.BlockSpec((tm, d), row),
        out_shape=jax.ShapeDtypeStruct((n_rows, d), F32),
        compiler_params=_cp(("parallel",), VMEM_WEIGHTS_MB),
        name="mix_out",
    )(h, mod_l, g_post.reshape(1, d), ya, ys[0], ys[1], proj, ssd_norm.reshape(1, 512), yn, og[0], og[1], proj,
      gdn_norm.reshape(1, GDN_HEAD_DIM), w_out)


def _ffn_kernel(h_ref, mod_ref, gpre_ref, gpost_ref, wg_ref, wu_ref, wd_ref, o_ref, u_scr, acc_scr):
    j = pl.program_id(1)

    @pl.when(j == 0)
    def _():
        x = h_ref[...]
        y = x * _rms_scale(x) * gpre_ref[...]
        u_scr[...] = (y * (1.0 + mod_ref[4:5, :]) + mod_ref[3:4, :]).astype(BF16)
        acc_scr[...] = jnp.zeros_like(acc_scr)

    u = u_scr[...]
    mid = (_silu(_dot(u, wg_ref[...])) * _dot(u, wu_ref[...])).astype(BF16)
    acc_scr[...] += _dot(mid, wd_ref[...])

    @pl.when(j == pl.num_programs(1) - 1)
    def _():
        y = acc_scr[...]
        o_ref[...] = h_ref[...] + mod_ref[5:6, :] * (y * _rms_scale(y) * gpost_ref[...])


def _ffn_call(h, mod_l, g_pre, g_post, wg, wu, wd, *, n_b, lx, tm, tf):
    rows, d = h.shape
    ff = wg.shape[1]
    nxt, per_b = n_b * lx // tm, lx // tm
    mod_idx = lambda i, j: (jnp.where(i < nxt, i // per_b, n_b), 0, 0)
    return pl.pallas_call(
        _ffn_kernel,
        grid=(rows // tm, ff // tf),
        in_specs=[pl.BlockSpec((tm, d), lambda i, j: (i, 0)),
                  pl.BlockSpec((None, 6, d), mod_idx),
                  pl.BlockSpec((1, d), lambda i, j: (0, 0)),
                  pl.BlockSpec((1, d), lambda i, j: (0, 0)),
                  pl.BlockSpec((d, tf), lambda i, j: (0, j)),
                  pl.BlockSpec((d, tf), lambda i, j: (0, j)),
                  pl.BlockSpec((tf, d), lambda i, j: (j, 0))],
        out_specs=pl.BlockSpec((tm, d), lambda i, j: (i, 0)),
        out_shape=jax.ShapeDtypeStruct((rows, d), F32),
        scratch_shapes=[pltpu.VMEM((tm, d), BF16), pltpu.VMEM((tm, d), F32)],
        compiler_params=_cp(("parallel", "arbitrary"), VMEM_WEIGHTS_MB),
        name="ffn_swiglu",
    )(h, mod_l, g_pre.reshape(1, d), g_post.reshape(1, d), wg, wu, wd)


MOE_ALIGN = 16
MOE_GROUP_TILE = 512


def _moe_local_rows(ts):
    return -(-(TOP_K * ts + N_EXPERTS * MOE_ALIGN) // 128) * 128


def _moe_route_kernel(h_ref, mod_ref, gpre_ref, wr_ref, u_ref, dest_ref, gate_ref, cnt_ref):
    ts, lanes = dest_ref.shape
    x = h_ref[...]
    y = x * _rms_scale(x) * gpre_ref[...]
    u = y * (1.0 + mod_ref[4:5, :]) + mod_ref[3:4, :]
    u_hi = u.astype(BF16)
    u_ref[...] = u_hi
    u_lo = (u - u_hi.astype(F32)).astype(BF16)
    logits = _dot(u_hi, wr_ref[0]) + (_dot(u_hi, wr_ref[1]) + _dot(u_lo, wr_ref[0]))
    lane = lax.broadcasted_iota(jnp.int32, logits.shape, 1).astype(F32)
    lg = jnp.where(lane < N_EXPERTS, logits, NEG)
    m1 = jnp.max(lg, axis=-1, keepdims=True)
    i1 = jnp.min(jnp.where(lg == m1, lane, float(lanes)), axis=-1, keepdims=True)
    lg2 = jnp.where(lane == i1, NEG, lg)
    m2 = jnp.max(lg2, axis=-1, keepdims=True)
    i2 = jnp.min(jnp.where(lg2 == m2, lane, float(lanes)), axis=-1, keepdims=True)
    e2 = jnp.exp(m2 - m1)
    gate_ref[...] = jnp.where(lane == 0.0, 1.0 / (1.0 + e2), jnp.where(lane == 1.0, e2 / (1.0 + e2), 0.0))
    sel = jnp.where(lane == i1, 1.0, 0.0) + jnp.where(lane == i2, 1.0, 0.0)
    cnt = jnp.sum(sel, axis=0, keepdims=True)
    cnt_al = jnp.floor((cnt + (MOE_ALIGN - 1)) / MOE_ALIGN) * MOE_ALIGN
    cnt_ref[...] = jnp.broadcast_to(cnt_al, cnt_ref.shape)
    before = lax.broadcasted_iota(jnp.int32, (lanes, lanes), 0) < lax.broadcasted_iota(jnp.int32, (lanes, lanes), 1)
    seg_lo = jnp.dot(jnp.broadcast_to(cnt_al, (8, lanes)), before.astype(F32),
                     preferred_element_type=F32, precision=HIGHEST)[0:1]
    earlier = lax.broadcasted_iota(jnp.int32, (ts, ts), 1) < lax.broadcasted_iota(jnp.int32, (ts, ts), 0)
    rank = _dot(earlier.astype(BF16), sel.astype(BF16))
    slot = seg_lo + rank
    slot_1 = jnp.sum(jnp.where(lane == i1, slot, 0.0), axis=-1, keepdims=True)
    slot_2 = jnp.sum(jnp.where(lane == i2, slot, 0.0), axis=-1, keepdims=True)
    dest_ref[...] = jnp.where(lane == 0.0, slot_1, jnp.where(lane == 1.0, slot_2, -1.0))


MOE_BIG = 4


def _moe_copy(hbm_ref, loc_ref, sems, far_row, loc_row, size_idx, *, to_hbm):
    rows = (MOE_BIG * MOE_ALIGN, MOE_ALIGN)[size_idx]
    loc = loc_ref.at[pl.ds(pl.multiple_of(loc_row, MOE_ALIGN), rows), :]
    far = hbm_ref.at[pl.ds(pl.multiple_of(far_row, MOE_ALIGN), rows), :]
    src, dst = (loc, far) if to_hbm else (far, loc)
    return pltpu.make_async_copy(src, dst, sems.at[size_idx])


def _moe_segment_copies(i, off_ref, n_ref, hbm_ref, loc_ref, sems, *, to_hbm):
    lo = jnp.int32(0)
    n_big_all, n_small_all = jnp.int32(0), jnp.int32(0)
    for e in range(N_EXPERTS):
        n_chunks = n_ref[i * N_EXPERTS + e]
        go = off_ref[i * N_EXPERTS + e]
        n_big = n_chunks // MOE_BIG
        n_small = n_chunks - n_big * MOE_BIG
        done = n_big * (MOE_BIG * MOE_ALIGN)

        def big(k, carry, lo=lo, go=go):
            step = k * (MOE_BIG * MOE_ALIGN)
            _moe_copy(hbm_ref, loc_ref, sems, go + step, lo + step, 0, to_hbm=to_hbm).start()
            return carry

        def small(k, carry, lo=lo, go=go, done=done):
            step = done + k * MOE_ALIGN
            _moe_copy(hbm_ref, loc_ref, sems, go + step, lo + step, 1, to_hbm=to_hbm).start()
            return carry

        lax.fori_loop(0, n_big, big, 0)
        lax.fori_loop(0, n_small, small, 0)
        lo = lo + n_chunks * MOE_ALIGN
        n_big_all, n_small_all = n_big_all + n_big, n_small_all + n_small
    return lo, n_big_all, n_small_all


def _moe_wait_copies(n_big, n_small, hbm_ref, loc_ref, sems, *, to_hbm):
    for size_idx, n in ((0, n_big), (1, n_small)):
        def body(k, carry, size_idx=size_idx):
            _moe_copy(hbm_ref, loc_ref, sems, 0, 0, size_idx, to_hbm=to_hbm).wait()
            return carry

        lax.fori_loop(0, n, body, 0)


def _moe_sort_kernel(off_ref, n_ref, u_ref, dest_ref, init_ref, us_ref, loc_scr, sem):
    del init_ref
    i = pl.program_id(0)
    dest = dest_ref[...]
    slot = lax.broadcasted_iota(jnp.int32, (dest.shape[0], loc_scr.shape[0]), 1).astype(F32)
    onehot = jnp.where(slot == dest[:, 0:1], 1.0, jnp.where(slot == dest[:, 1:2], 1.0, 0.0))
    loc_scr[...] = _dot_tn(onehot.astype(BF16), u_ref[...]).astype(BF16)
    _, n_big, n_small = _moe_segment_copies(i, off_ref, n_ref, us_ref, loc_scr, sem, to_hbm=True)
    _moe_wait_copies(n_big, n_small, us_ref, loc_scr, sem, to_hbm=True)


def _moe_expert_kernel(te_ref, tv_ref, u_ref, wg_ref, wu_ref, wd_ref, y_ref, acc_scr):
    k, j = pl.program_id(0), pl.program_id(1)
    last = pl.num_programs(1) - 1
    valid = tv_ref[k] > 0

    @pl.when(valid)
    def _():
        @pl.when(j == 0)
        def _():
            acc_scr[...] = jnp.zeros_like(acc_scr)

        u = u_ref[...]
        mid = (_silu(_dot(u, wg_ref[...])) * _dot(u, wu_ref[...])).astype(BF16)
        acc_scr[...] += _dot(mid, wd_ref[...])

        @pl.when(j == last)
        def _():
            y_ref[...] = acc_scr[...].astype(BF16)

    @pl.when(jnp.logical_not(valid) & (j == last))
    def _():
        y_ref[...] = jnp.zeros_like(y_ref)


def _moe_combine_kernel(off_ref, n_ref, h_ref, mod_ref, gpost_ref, dest_ref, gate_ref, ys_ref, o_ref, loc_scr, sem):
    i = pl.program_id(0)
    n_rows, n_big, n_small = _moe_segment_copies(i, off_ref, n_ref, ys_ref, loc_scr, sem, to_hbm=False)
    dest, gate = dest_ref[...], gate_ref[...]
    slot = lax.broadcasted_iota(jnp.int32, (dest.shape[0], loc_scr.shape[0]), 1).astype(F32)
    w = jnp.where(slot == dest[:, 0:1], gate[:, 0:1], jnp.where(slot == dest[:, 1:2], gate[:, 1:2], 0.0))
    w_hi = w.astype(BF16)
    w_lo = (w - w_hi.astype(F32)).astype(BF16)
    _moe_wait_copies(n_big, n_small, ys_ref, loc_scr, sem, to_hbm=False)
    filled = lax.broadcasted_iota(jnp.int32, (loc_scr.shape[0], 1), 0) < n_rows
    y_loc = jnp.where(filled, loc_scr[...], jnp.zeros_like(loc_scr))
    y = _dot(w_hi, y_loc) + _dot(w_lo, y_loc)
    o_ref[...] = h_ref[...] + mod_ref[5:6, :] * (y * _rms_scale(y) * gpost_ref[...])


def _moe_call(h, mod_l, g_pre, g_post, router_pad, wg, wu, wd, *, n_b, lx, tm, tf):
    rows, d = h.shape
    ne, _, fe = wg.shape
    ts, tg = tm, MOE_GROUP_TILE
    n_tiles = rows // ts
    lrows = _moe_local_rows(ts)
    nt_max = -(-(TOP_K * rows + n_tiles * ne * (MOE_ALIGN - 1)) // tg) + ne
    n_pad = nt_max * tg
    nxt, per_b = n_b * lx // ts, lx // ts
    mod_row = lambda i: jnp.where(i < nxt, i // per_b, n_b)

    u, dest, gate, cnt = pl.pallas_call(
        _moe_route_kernel,
        grid=(n_tiles,),
        in_specs=[pl.BlockSpec((ts, d), lambda i: (i, 0)),
                  pl.BlockSpec((None, 6, d), lambda i: (mod_row(i), 0, 0)),
                  pl.BlockSpec((1, d), lambda i: (0, 0)),
                  pl.BlockSpec((2, d, 128), lambda i: (0, 0, 0))],
        out_specs=[pl.BlockSpec((ts, d), lambda i: (i, 0)),
                   pl.BlockSpec((ts, 128), lambda i: (i, 0)),
                   pl.BlockSpec((ts, 128), lambda i: (i, 0)),
                   pl.BlockSpec((None, 8, 128), lambda i: (i, 0, 0))],
        out_shape=[jax.ShapeDtypeStruct((rows, d), BF16),
                   jax.ShapeDtypeStruct((rows, 128), F32),
                   jax.ShapeDtypeStruct((rows, 128), F32),
                   jax.ShapeDtypeStruct((n_tiles, 8, 128), F32)],
        compiler_params=_cp(("parallel",), VMEM_STREAM_MB),
        name="moe_route",
    )(h, mod_l, g_pre.reshape(1, d), router_pad)

    cnt_al = cnt[:, 0, :ne].astype(jnp.int32)
    group = -(-jnp.sum(cnt_al, axis=0) // tg) * tg
    group_end = jnp.cumsum(group)
    seg_off = (group_end - group)[None, :] + jnp.cumsum(cnt_al, axis=0) - cnt_al
    n_used = group_end[-1] // tg
    tile_id = jnp.arange(nt_max, dtype=jnp.int32)
    tile_valid = (tile_id < n_used).astype(jnp.int32)
    tile_expert = jnp.searchsorted(group_end // tg, jnp.minimum(tile_id, n_used - 1), side='right')
    tile_expert = jnp.minimum(tile_expert, ne - 1).astype(jnp.int32)
    seg_off = seg_off.reshape(-1).astype(jnp.int32)
    seg_chunks = (cnt_al // MOE_ALIGN).reshape(-1)

    u_sorted = pl.pallas_call(
        _moe_sort_kernel,
        grid_spec=pltpu.PrefetchScalarGridSpec(
            num_scalar_prefetch=2,
            grid=(n_tiles,),
            in_specs=[pl.BlockSpec((ts, d), lambda i, o_r, n_r: (i, 0)),
                      pl.BlockSpec((ts, 128), lambda i, o_r, n_r: (i, 0)),
                      pl.BlockSpec(memory_space=pl.ANY)],
            out_specs=pl.BlockSpec(memory_space=pl.ANY),
            scratch_shapes=[pltpu.VMEM((lrows, d), BF16), pltpu.SemaphoreType.DMA((2,))]),
        out_shape=jax.ShapeDtypeStruct((n_pad, d), BF16),
        input_output_aliases={4: 0},
        compiler_params=_cp(("arbitrary",), VMEM_STREAM_MB),
        name="moe_sort",
    )(seg_off, seg_chunks, u, dest, jnp.zeros((n_pad, d), BF16))

    nj = fe // tf
    w_col = lambda k, j, te_r, tv_r: jnp.where(tv_r[k] > 0, j, nj - 1)
    y_sorted = pl.pallas_call(
        _moe_expert_kernel,
        grid_spec=pltpu.PrefetchScalarGridSpec(
            num_scalar_prefetch=2,
            grid=(nt_max, nj),
            in_specs=[pl.BlockSpec((tg, d), lambda k, j, te_r, tv_r: (k, 0)),
                      pl.BlockSpec((None, d, tf), lambda k, j, te_r, tv_r: (te_r[k], 0, w_col(k, j, te_r, tv_r))),
                      pl.BlockSpec((None, d, tf), lambda k, j, te_r, tv_r: (te_r[k], 0, w_col(k, j, te_r, tv_r))),
                      pl.BlockSpec((None, tf, d), lambda k, j, te_r, tv_r: (te_r[k], w_col(k, j, te_r, tv_r), 0))],
            out_specs=pl.BlockSpec((tg, d), lambda k, j, te_r, tv_r: (k, 0)),
            scratch_shapes=[pltpu.VMEM((tg, d), F32)]),
        out_shape=jax.ShapeDtypeStruct((n_pad, d), BF16),
        compiler_params=_cp(("parallel", "arbitrary"), VMEM_ATTN_MB),
        name="moe_experts",
    )(tile_expert, tile_valid, u_sorted, wg, wu, wd)

    return pl.pallas_call(
        _moe_combine_kernel,
        grid_spec=pltpu.PrefetchScalarGridSpec(
            num_scalar_prefetch=2,
            grid=(n_tiles,),
            in_specs=[pl.BlockSpec((ts, d), lambda i, o_r, n_r: (i, 0)),
                      pl.BlockSpec((None, 6, d), lambda i, o_r, n_r: (mod_row(i), 0, 0)),
                      pl.BlockSpec((1, d), lambda i, o_r, n_r: (0, 0)),
                      pl.BlockSpec((ts, 128), lambda i, o_r, n_r: (i, 0)),
                      pl.BlockSpec((ts, 128), lambda i, o_r, n_r: (i, 0)),
                      pl.BlockSpec(memory_space=pl.ANY)],
            out_specs=pl.BlockSpec((ts, d), lambda i, o_r, n_r: (i, 0)),
            scratch_shapes=[pltpu.VMEM((lrows, d), BF16), pltpu.SemaphoreType.DMA((2,))]),
        out_shape=jax.ShapeDtypeStruct((rows, d), F32),
        compiler_params=_cp(("arbitrary",), VMEM_ATTN_MB),
        name="moe_combine",
    )(seg_off, seg_chunks, h, mod_l, g_post.reshape(1, d), dest, gate, y_sorted)


def _rope_tables(lx, tm):
    half = MLA_ROPE // 2
    n_axis = half // 2
    inv_freq = ROPE_THETA ** (-jnp.arange(n_axis, dtype=F32) / n_axis)
    pos = jnp.arange(lx)
    rows = (pos // GRID_W).astype(F32)
    cols = (pos % GRID_W).astype(F32)
    ang = jnp.concatenate([rows[:, None] * inv_freq, cols[:, None] * inv_freq], axis=-1)
    cos, sin = jnp.cos(ang), jnp.sin(ang)
    cos_t = jnp.concatenate([cos, cos], axis=-1)
    sin_t = jnp.concatenate([-sin, sin], axis=-1)
    cos_t = jnp.concatenate([cos_t, jnp.ones((tm, MLA_ROPE), F32)], axis=0)
    sin_t = jnp.concatenate([sin_t, jnp.zeros((tm, MLA_ROPE), F32)], axis=0)
    return cos_t, sin_t


def _mla_weights(q_norm, w_uq, kv_norm, w_ukv):
    dq = MLA_NOPE + MLA_ROPE
    half = MLA_ROPE // 2
    cols = lambda a, b: w_uq[:, a:b]
    nope = [cols(h * dq, h * dq + MLA_NOPE) for h in range(MLA_HEADS)]
    rope = [cols(h * dq + MLA_NOPE, (h + 1) * dq) for h in range(MLA_HEADS)]
    rope_sw = [cols(h * dq + MLA_NOPE + s * half, h * dq + MLA_NOPE + (s + 1) * half)
               for h in range(MLA_HEADS) for s in (1, 0)]
    pad = ((0, MLA_ROPE), (0, 0))
    wq = jnp.pad(jnp.concatenate(nope + rope, axis=1), pad).astype(BF16)
    wqs = jnp.pad(jnp.concatenate(rope_sw, axis=1), pad).astype(BF16)
    qn_ext = jnp.pad(q_norm, (0, MLA_ROPE)).reshape(1, -1)
    perm = np.zeros((MLA_ROPE, MLA_ROPE), np.float32)
    perm[(np.arange(MLA_ROPE) + half) % MLA_ROPE, np.arange(MLA_ROPE)] = 1.0
    return qn_ext, wq, wqs, kv_norm.reshape(1, -1), w_ukv.astype(BF16), jnp.asarray(perm, BF16)


def _pick_tile(n, cands):
    for t in cands:
        if n % t == 0:
            return t
    raise ValueError(f"no tile for {n}")


def _tile_plan(n_b, lx, lc, d_ff, d_expert):
    seq = math.gcd(lx, n_b * lc)
    return dict(
        tm=_pick_tile(seq, (512, 256, 128)),
        tm_in=_pick_tile(seq, (1024, 512, 256, 128)),
        tq=_pick_tile(lc, (256, 128)),
        tr=_pick_tile(lc, (256, 128)),
        tf_ffn=_pick_tile(d_ff, (512, 256, 128)),
        tf_moe=_pick_tile(d_expert, (256, 128)),
    )


def kernel(x, c, ctx, c_ctx, w_ada, b_ada, g_pre_mix, g_post_mix, g_pre_ffn, g_post_ffn, w_in, w_out, mla_q_norm, mla_w_uq, mla_kv_norm, mla_w_ukv, ssd_conv_w, ssd_conv_b, ssd_a_log, ssd_dt_bias, ssd_d, ssd_norm, na_rpb, gdn_conv_w, gdn_a_log, gdn_dt_bias, gdn_norm, ffn_w_gate, ffn_w_up, ffn_w_down, moe_router, moe_w_gate, moe_w_up, moe_w_down):
    n_b, lx, d = x.shape
    lc = ctx.shape[1]
    depth = w_ada.shape[0]
    rows_x, rows_c = n_b * lx, n_b * lc
    assert n_b + 1 <= 8 and lx % GRID_W == 0 and lx % lc == 0 and lc % CHUNK == 0
    plan = _tile_plan(n_b, lx, lc, ffn_w_gate.shape[-1], moe_w_gate.shape[-1])
    tm, tm_in, tq, tr = plan["tm"], plan["tm_in"], plan["tq"], plan["tr"]

    cvec = jnp.concatenate([c, c_ctx[None, :], jnp.zeros((8 - n_b - 1, d), F32)], axis=0)
    mod = _ada_call(cvec, w_ada, b_ada).reshape(depth, 8, 6, d)
    cos_t, sin_t = _rope_tables(lx, tm)
    na_bias = _na_bias_table(na_rpb.reshape((-1,) + na_rpb.shape[2:]), lx // GRID_W, lc)
    h_all = jnp.concatenate([x.reshape(rows_x, d), ctx.reshape(rows_c, d)], axis=0)

    for i in range(depth):
        need_ctx = i < depth - 1
        w_main, w_small = _regroup_w_in(w_in[i])
        proj, small = _inproj_call(h_all, mod[i], g_pre_mix[i], w_main, w_small, n_b=n_b, lx=lx, tm=tm_in)

        mla_w = _mla_weights(mla_q_norm[i], mla_w_uq[i], mla_kv_norm[i], mla_w_ukv[i])
        q_a, k_a, v_a = _mla_prep_call(proj, *mla_w[:5], cos_t, sin_t, mla_w[5], n_b=n_b, lx=lx, tm=tm)
        ya = _mla_attn_call(q_a, k_a, v_a, n_b=n_b, lx=lx, lc=lc, tq=tq, need_ctx=need_ctx)

        xbc = _conv_call(proj, ssd_conv_w[i], ssd_conv_b[i], col_off=P_XBC, n_b=n_b, lx=lx, lc=lc, tr=tr)
        ys = _ssd_call(xbc, small, ssd_a_log[i], ssd_dt_bias[i], ssd_d[i], n_b=n_b, lx=lx, lc=lc)

        yn = _na_call(proj, na_bias, layer=i, n_b=n_b, lx=lx, lc=lc, need_ctx=need_ctx)

        qkv = _conv_call(proj, gdn_conv_w[i], jnp.zeros((3 * GDN_DIM,), F32), col_off=P_GQKV,
                         n_b=n_b, lx=lx, lc=lc, tr=tr)
        og = _gdn_call(qkv, small, gdn_a_log[i], gdn_dt_bias[i], n_b=n_b, lx=lx, lc=lc)

        n_rows = rows_x + rows_c if need_ctx else rows_x
        h_mid = _mixout_call(h_all, mod[i], g_post_mix[i], ya, ys, proj, ssd_norm[i], yn, og, gdn_norm[i],
                             w_out[i].astype(BF16), n_b=n_b, lx=lx, n_rows=n_rows, tm=tm)
        j = i // 2
        if i % 2 == 0:
            h_all = _ffn_call(h_mid, mod[i], g_pre_ffn[i], g_post_ffn[i], ffn_w_gate[j].astype(BF16),
                              ffn_w_up[j].astype(BF16), ffn_w_down[j].astype(BF16), n_b=n_b, lx=lx, tm=tm,
                              tf=plan["tf_ffn"])
        else:
            router_f32 = jnp.pad(moe_router[j], ((0, 0), (0, 128 - N_EXPERTS)))
            router_hi = router_f32.astype(BF16)
            router_pad = jnp.stack([router_hi, (router_f32 - router_hi.astype(F32)).astype(BF16)])
            h_all = _moe_call(h_mid, mod[i], g_pre_ffn[i], g_post_ffn[i], router_pad, moe_w_gate[j].astype(BF16),
                              moe_w_up[j].astype(BF16), moe_w_down[j].astype(BF16), n_b=n_b, lx=lx, tm=tm,
                              tf=plan["tf_moe"])
    return h_all[:rows_x].reshape(n_b, lx, d)
```

```python
import functools
import math

import numpy as np
import jax
import jax.numpy as jnp
from jax import lax
from jax.experimental import pallas as pl
from jax.experimental.pallas import tpu as pltpu

F32 = jnp.float32
BF16 = jnp.bfloat16
HIGHEST = lax.Precision.HIGHEST

GRID_W = 64
EPS = 1e-6
ROPE_THETA = 10000.0
CHUNK = 128
CONV_W = 5
MLA_HEADS, MLA_NOPE, MLA_ROPE, MLA_V = 4, 128, 64, 128
MLA_Q_LORA, MLA_KV_LORA = 448, 128
SSD_HEADS, SSD_HEAD_DIM, SSD_STATE, SSD_GROUPS = 8, 64, 128, 2
SSD_D_INNER = SSD_HEADS * SSD_HEAD_DIM
SSD_CONV_DIM = SSD_D_INNER + 2 * SSD_GROUPS * SSD_STATE
NA_HEADS, NA_HEAD_DIM = 4, 128
NA_DIM = NA_HEADS * NA_HEAD_DIM
NA_WIN_ROWS, NA_WIN_COLS = 8, 16
GDN_HEADS, GDN_HEAD_DIM = 4, 128
GDN_DIM = GDN_HEADS * GDN_HEAD_DIM
N_EXPERTS, TOP_K = 8, 2
MLA_COLS = MLA_Q_LORA + MLA_KV_LORA + MLA_ROPE
SSD_COLS = SSD_D_INNER + SSD_CONV_DIM + 2 * SSD_HEADS
NA_COLS = 3 * NA_DIM
GDN_COLS = 4 * GDN_DIM + 4 * GDN_HEADS

P_XBC, P_ZSSD, P_NAQ, P_NAK, P_NAV = 0, 1024, 1536, 2048, 2560
P_GQKV, P_GZ, P_MLA = 3072, 4608, 5120
P_MAIN = 5760
P_SMALL = 128
NEG = -1e30
VMEM_MB = 1024 * 1024
VMEM_STREAM_MB, VMEM_ATTN_MB, VMEM_WEIGHTS_MB = 40, 48, 56


def _cp(sem, mb):
    return pltpu.CompilerParams(dimension_semantics=sem, vmem_limit_bytes=mb * VMEM_MB)


def _dot(a, b):
    return jnp.dot(a, b, preferred_element_type=F32)


def _dot_nt(a, b, precision=None):
    return lax.dot_general(a, b, (((1,), (1,)), ((), ())), preferred_element_type=F32, precision=precision)


def _dot_tn(a, b):
    return lax.dot_general(a, b, (((0,), (0,)), ((), ())), preferred_element_type=F32)


def _sigmoid(x):
    return 1.0 / (1.0 + jnp.exp(-x))


def _silu(x):
    return x * _sigmoid(x)


def _softplus(x):
    return jnp.maximum(x, 0.0) + jnp.log(1.0 + jnp.exp(-jnp.abs(x)))


def _rms_scale(x):
    return lax.rsqrt(jnp.mean(x * x, axis=-1, keepdims=True) + EPS)


def _regroup_w_in(w):
    o_mla, o_ssd = 0, MLA_COLS
    o_na, o_gdn = o_ssd + SSD_COLS, o_ssd + SSD_COLS + NA_COLS
    main_segs = [
        (o_ssd + SSD_D_INNER, SSD_CONV_DIM),
        (o_ssd, SSD_D_INNER),
        (o_na, NA_COLS),
        (o_gdn, 4 * GDN_DIM),
        (o_mla + MLA_Q_LORA, MLA_KV_LORA),
        (o_mla, MLA_Q_LORA),
        (o_mla + MLA_Q_LORA + MLA_KV_LORA, MLA_ROPE),
    ]
    small_segs = [(o_ssd + SSD_D_INNER + SSD_CONV_DIM, 2 * SSD_HEADS), (o_gdn + 4 * GDN_DIM, 4 * GDN_HEADS)]
    assert sum(n for _, n in main_segs) == P_MAIN
    main = jnp.concatenate([w[:, a:a + n] for a, n in main_segs], axis=1).astype(BF16)
    n_small = sum(n for _, n in small_segs)
    small = jnp.concatenate([w[:, a:a + n] for a, n in small_segs]
                            + [jnp.zeros((w.shape[0], P_SMALL - n_small), w.dtype)], axis=1).astype(BF16)
    return main, small


def _ada_kernel(c_ref, w_ref, b_ref, o_ref):
    s = _silu(c_ref[...]).astype(BF16)
    o_ref[...] = _dot(s, w_ref[...].astype(BF16)) + b_ref[...]


def _ada_call(cvec, w_ada, b_ada):
    depth, d, n = w_ada.shape
    tn = 1024
    return pl.pallas_call(
        _ada_kernel,
        grid=(depth, n // tn),
        in_specs=[pl.BlockSpec((8, d), lambda l, j: (0, 0)),
                  pl.BlockSpec((None, d, tn), lambda l, j: (l, 0, j)),
                  pl.BlockSpec((None, 1, tn), lambda l, j: (l, 0, j))],
        out_specs=pl.BlockSpec((None, 8, tn), lambda l, j: (l, 0, j)),
        out_shape=jax.ShapeDtypeStruct((depth, 8, n), F32),
        compiler_params=_cp(("parallel", "parallel"), VMEM_STREAM_MB),
        name="adaln",
    )(cvec, w_ada, b_ada.reshape(depth, 1, n))


def _inproj_kernel(h_ref, mod_ref, g_ref, w_ref, ws_ref, o_ref, os_ref, u_scr):
    @pl.when(pl.program_id(1) == 0)
    def _():
        x = h_ref[...]
        y = x * _rms_scale(x) * g_ref[...]
        u = (y * (1.0 + mod_ref[1:2, :]) + mod_ref[0:1, :]).astype(BF16)
        u_scr[...] = u
        os_ref[...] = _dot(u, ws_ref[...])

    o_ref[...] = _dot(u_scr[...], w_ref[...]).astype(BF16)


def _inproj_call(h, mod_l, g_pre, w_main, w_small, *, n_b, lx, tm):
    rows, d = h.shape
    tn = _pick_tile(P_MAIN, (1920, 640, 128))
    nxt, per_b = n_b * lx // tm, lx // tm

    def mod_idx(i, j):
        return (jnp.where(i < nxt, i // per_b, n_b), 0, 0)

    return pl.pallas_call(
        _inproj_kernel,
        grid=(rows // tm, P_MAIN // tn),
        in_specs=[pl.BlockSpec((tm, d), lambda i, j: (i, 0)),
                  pl.BlockSpec((None, 6, d), mod_idx),
                  pl.BlockSpec((1, d), lambda i, j: (0, 0)),
                  pl.BlockSpec((d, tn), lambda i, j: (0, j)),
                  pl.BlockSpec((d, P_SMALL), lambda i, j: (0, 0))],
        out_specs=[pl.BlockSpec((tm, tn), lambda i, j: (i, j)),
                   pl.BlockSpec((tm, P_SMALL), lambda i, j: (i, 0))],
        out_shape=[jax.ShapeDtypeStruct((rows, P_MAIN), BF16),
                   jax.ShapeDtypeStruct((rows, P_SMALL), F32)],
        scratch_shapes=[pltpu.VMEM((tm, d), BF16)],
        compiler_params=_cp(("parallel", "arbitrary"), VMEM_WEIGHTS_MB),
        name="in_proj",
    )(h, mod_l, g_pre.reshape(1, d), w_main, w_small)


HALO = 16


def _conv_kernel(prev_ref, cur_ref, next_ref, w_ref, b_ref, o_ref, ext_scr, *, tr, blocks_x, seq_x, seq_c):
    i = pl.program_id(0)
    in_x = i < blocks_x
    pos = jnp.where(in_x, i % seq_x, (i - blocks_x) % seq_c)
    last_pos = jnp.where(in_x, seq_x - 1, seq_c - 1)
    ext_scr[0:HALO, :] = jnp.where(pos == 0, 0.0, prev_ref[...].astype(F32))
    ext_scr[HALO:HALO + tr, :] = cur_ref[...].astype(F32)
    ext_scr[HALO + tr:2 * HALO + tr, :] = jnp.where(pos == last_pos, 0.0, next_ref[...].astype(F32))
    acc = b_ref[...] + w_ref[0:1, :] * ext_scr[HALO - 2:HALO - 2 + tr, :]
    for k in range(1, CONV_W):
        acc = acc + w_ref[k:k + 1, :] * ext_scr[HALO - 2 + k:HALO - 2 + k + tr, :]
    o_ref[...] = _silu(acc).astype(BF16)


def _conv_call(proj, w, b, *, col_off, n_b, lx, lc, tr):
    rows = proj.shape[0]
    c = w.shape[1]
    cb = col_off // c
    assert cb * c == col_off
    hb = tr // HALO
    n_halo = rows // HALO
    kern = functools.partial(_conv_kernel, tr=tr, blocks_x=n_b * lx // tr, seq_x=lx // tr, seq_c=lc // tr)
    return pl.pallas_call(
        kern,
        grid=(rows // tr,),
        in_specs=[pl.BlockSpec((HALO, c), lambda i: (jnp.maximum(i * hb - 1, 0), cb)),
                  pl.BlockSpec((tr, c), lambda i: (i, cb)),
                  pl.BlockSpec((HALO, c), lambda i: (jnp.minimum((i + 1) * hb, n_halo - 1), cb)),
                  pl.BlockSpec((CONV_W, c), lambda i: (0, 0)),
                  pl.BlockSpec((1, c), lambda i: (0, 0))],
        out_specs=pl.BlockSpec((tr, c), lambda i: (i, 0)),
        out_shape=jax.ShapeDtypeStruct((rows, c), BF16),
        scratch_shapes=[pltpu.VMEM((tr + 2 * HALO, c), F32)],
        compiler_params=_cp(("parallel",), VMEM_STREAM_MB),
        name="dwconv_silu",
    )(proj, proj, proj, w, b.reshape(1, c))


def _mla_prep_kernel(p_ref, qn_ref, wq_ref, wqs_ref, kvn_ref, wkv_ref, cos_ref, sin_ref, perm_ref,
                     q_ref, k_ref, v_ref):
    p = p_ref[...].astype(F32)
    ckv = p[:, 0:MLA_KV_LORA]
    ce = p[:, MLA_KV_LORA:]
    lane = lax.broadcasted_iota(jnp.int32, ce.shape, 1)
    ssq = jnp.sum(jnp.where(lane < MLA_Q_LORA, ce * ce, 0.0), axis=-1, keepdims=True)
    cqn = (ce * lax.rsqrt(ssq / MLA_Q_LORA + EPS) * qn_ref[...]).astype(BF16)
    ckvn = (ckv * _rms_scale(ckv) * kvn_ref[...]).astype(BF16)
    q = _dot(cqn, wq_ref[...])
    qs = _dot(cqn, wqs_ref[...])
    kv = _dot(ckvn, wkv_ref[...])
    cos, sin = cos_ref[...], sin_ref[...]
    kr = p_ref[:, MLA_KV_LORA + MLA_Q_LORA:]
    kr_rot = kr.astype(F32) * cos + _dot(kr, perm_ref[...]) * sin
    nr = MLA_HEADS * MLA_NOPE
    scale = (MLA_NOPE + MLA_ROPE) ** -0.5
    ones_col = jnp.where(lax.broadcasted_iota(jnp.int32, (p.shape[0], MLA_V), 1) == 0, 1.0, 0.0).astype(BF16)
    for h in range(MLA_HEADS):
        q_ref[h, :, 0:MLA_NOPE] = (q[:, h * MLA_NOPE:(h + 1) * MLA_NOPE] * scale).astype(BF16)
        qr = q[:, nr + h * MLA_ROPE:nr + (h + 1) * MLA_ROPE] * cos + qs[:, h * MLA_ROPE:(h + 1) * MLA_ROPE] * sin
        q_ref[h, :, MLA_NOPE:] = (qr * scale).astype(BF16)
        hv = h * (MLA_NOPE + MLA_V)
        k_ref[h, :, 0:MLA_NOPE] = kv[:, hv:hv + MLA_NOPE].astype(BF16)
        k_ref[h, :, MLA_NOPE:] = kr_rot.astype(BF16)
        v_ref[h, :, 0:MLA_V] = kv[:, hv + MLA_NOPE:hv + MLA_NOPE + MLA_V].astype(BF16)
        v_ref[h, :, MLA_V:] = ones_col


def _mla_prep_call(proj, qn_ext, wq, wqs, kvn, wkv, cos_t, sin_t, perm, *, n_b, lx, tm):
    rows = proj.shape[0]
    nxt, per_b = n_b * lx // tm, lx // tm
    dk = MLA_NOPE + MLA_ROPE
    rope_idx = lambda i: (jnp.where(i < nxt, i % per_b, per_b), 0)
    full = lambda a: pl.BlockSpec(a.shape, lambda i: (0,) * a.ndim)
    return pl.pallas_call(
        _mla_prep_kernel,
        grid=(rows // tm,),
        in_specs=[pl.BlockSpec((tm, MLA_COLS), lambda i: (i, P_MLA // MLA_COLS)),
                  full(qn_ext), full(wq), full(wqs), full(kvn), full(wkv),
                  pl.BlockSpec((tm, MLA_ROPE), rope_idx), pl.BlockSpec((tm, MLA_ROPE), rope_idx),
                  full(perm)],
        out_specs=[pl.BlockSpec((MLA_HEADS, tm, dk), lambda i: (0, i, 0)),
                   pl.BlockSpec((MLA_HEADS, tm, dk), lambda i: (0, i, 0)),
                   pl.BlockSpec((MLA_HEADS, tm, 2 * MLA_V), lambda i: (0, i, 0))],
        out_shape=[jax.ShapeDtypeStruct((MLA_HEADS, rows, dk), BF16),
                   jax.ShapeDtypeStruct((MLA_HEADS, rows, dk), BF16),
                   jax.ShapeDtypeStruct((MLA_HEADS, rows, 2 * MLA_V), BF16)],
        compiler_params=_cp(("parallel",), VMEM_STREAM_MB),
        name="mla_prep",
    )(proj, qn_ext, wq, wqs, kvn, wkv, cos_t, sin_t, perm)


def _softmax_pv(scores, values):
    m = functools.reduce(jnp.maximum, [jnp.max(s, axis=-1, keepdims=True) for s in scores])
    ps = [jnp.exp(s - m) for s in scores]
    den = functools.reduce(lambda a, b: a + b, [jnp.sum(p, axis=-1, keepdims=True) for p in ps])
    num = functools.reduce(lambda a, b: a + b, [_dot(p.astype(BF16), v) for p, v in zip(ps, values)])
    return num / den


def _softmax_pv_aug(scores, values_aug):
    m = functools.reduce(jnp.maximum, [jnp.max(s, axis=-1, keepdims=True) for s in scores])
    acc = functools.reduce(lambda a, b: a + b,
                           [_dot(jnp.exp((s - m).astype(BF16)), v) for s, v in zip(scores, values_aug)])
    return acc[:, 0:MLA_V] / acc[:, MLA_V:MLA_V + 1]


MLA_HEADS_PER_STEP = 4


def _mla_attn_kernel(q_ref, kx_ref, vx_ref, kc_ref, vc_ref, o_ref, *, nqx):
    qi = pl.program_id(2)
    heads = range(q_ref.shape[0])

    @pl.when(qi < nqx)
    def _():
        outs = [_softmax_pv_aug([_dot_nt(q_ref[h], kx_ref[h]), _dot_nt(q_ref[h], kc_ref[h])], [vx_ref[h], vc_ref[h]])
                for h in heads]
        o_ref[...] = jnp.concatenate(outs, axis=1).astype(BF16)

    @pl.when(qi >= nqx)
    def _():
        outs = [_softmax_pv_aug([_dot_nt(q_ref[h], kc_ref[h])], [vc_ref[h]]) for h in heads]
        o_ref[...] = jnp.concatenate(outs, axis=1).astype(BF16)


def _mla_attn_call(q, k, v, *, n_b, lx, lc, tq, need_ctx):
    rows = q.shape[1]
    dk = q.shape[2]
    nqx, nqc = lx // tq, lc // tq
    nq = nqx + (nqc if need_ctx else 0)
    nbx = n_b * lx // lc

    def q_row(b, qi):
        return jnp.where(qi < nqx, b * nqx + qi, n_b * nqx + b * nqc + (qi - nqx))

    kern = functools.partial(_mla_attn_kernel, nqx=nqx)
    out_rows = rows if need_ctx else n_b * lx
    hp = MLA_HEADS_PER_STEP
    return pl.pallas_call(
        kern,
        grid=(n_b, MLA_HEADS // hp, nq),
        in_specs=[pl.BlockSpec((hp, tq, dk), lambda b, h, qi: (h, q_row(b, qi), 0)),
                  pl.BlockSpec((hp, lx, dk), lambda b, h, qi: (h, b, 0)),
                  pl.BlockSpec((hp, lx, 2 * MLA_V), lambda b, h, qi: (h, b, 0)),
                  pl.BlockSpec((hp, lc, dk), lambda b, h, qi: (h, nbx + b, 0)),
                  pl.BlockSpec((hp, lc, 2 * MLA_V), lambda b, h, qi: (h, nbx + b, 0))],
        out_specs=pl.BlockSpec((tq, hp * MLA_V), lambda b, h, qi: (q_row(b, qi), h)),
        out_shape=jax.ShapeDtypeStruct((out_rows, MLA_HEADS * MLA_V), BF16),
        compiler_params=_cp(("parallel", "parallel", "arbitrary"), VMEM_ATTN_MB),
        name="mla_attn",
    )(q, k, v, k, v)


def _na_plan(g_rows, lc):
    wr = min(NA_WIN_ROWS, g_rows)
    rg = next(r for r in (4, 2, 1) if g_rows % r == 0 and lc % (r * GRID_W) == 0)
    wk = min(rg + wr - 1, g_rows)
    n_groups = g_rows // rg
    ks = np.clip(np.arange(n_groups) * rg - wr // 2, 0, g_rows - wk)
    r = np.arange(g_rows)
    rs = np.clip(r - wr // 2, 0, g_rows - wr)
    q_off = (r - np.repeat(ks, rg)).reshape(n_groups, rg)
    rel = (rs - np.repeat(ks, rg)).reshape(n_groups, rg)
    assert (rel >= 0).all() and (rel + wr <= wk).all()
    pats = [tuple(q_off[g]) + tuple(rel[g]) for g in range(n_groups)]
    uniq = sorted(set(pats))
    var = np.array([uniq.index(p) for p in pats], np.int32)
    q_off_v = np.array([p[:rg] for p in uniq])
    rel_v = np.array([p[rg:] for p in uniq])
    return wr, rg, wk, ks.astype(np.int32), var, q_off_v, rel_v


def _na_bias_table(rpb, g_rows, lc):
    wr, rg, wk, _, _, q_off_v, rel_v = _na_plan(g_rows, lc)
    col_start = np.clip(np.arange(GRID_W) - NA_WIN_COLS // 2, 0, GRID_W - NA_WIN_COLS)
    cc = np.arange(GRID_W)
    col_ok = (cc[None, :] >= col_start[:, None]) & (cc[None, :] < col_start[:, None] + NA_WIN_COLS)
    dc = np.clip(cc[None, :] - cc[:, None] + NA_WIN_COLS - 1, 0, 2 * NA_WIN_COLS - 2)
    n_dc = 2 * NA_WIN_COLS - 1
    nh = rpb.shape[0]
    onehot = (dc.reshape(-1)[:, None] == np.arange(n_dc)[None, :]).astype(np.float32)
    g = jnp.einsum('hab,yb->hay', rpb.astype(F32), jnp.asarray(onehot), precision=HIGHEST)
    g = jnp.where(col_ok[None, None], g.reshape(nh, -1, GRID_W, GRID_W), NEG)
    g = jnp.pad(g, ((0, 0), (wk, wk), (0, 0), (0, 0)), constant_values=NEG)
    w = np.arange(wk)
    tabs = []
    for v in range(q_off_v.shape[0]):
        rows_v = []
        for j in range(rg):
            a0 = wk - q_off_v[v, j] + NA_WIN_ROWS - 1
            own = (w >= rel_v[v, j]) & (w < rel_v[v, j] + wr)
            blk = jnp.where(jnp.asarray(own)[None, :, None, None], g[:, a0:a0 + wk], NEG)
            rows_v.append(jnp.transpose(blk, (0, 2, 1, 3)).reshape(nh, GRID_W, wk * GRID_W))
        tabs.append(jnp.concatenate(rows_v, axis=-2))
    return jnp.stack(tabs, axis=0)


def _na_kernel(var_ref, ks_ref, q_ref, kx_ref, vx_ref, kc_ref, vc_ref, bias_ref, o_ref, *, n_groups, wk, scale):
    g = pl.program_id(1)
    hd = NA_HEAD_DIM
    cols = [slice(h * hd, (h + 1) * hd) for h in range(NA_HEADS)]

    @pl.when(g < n_groups)
    def _():
        start = pl.multiple_of(ks_ref[g] * GRID_W, GRID_W)
        rows = pl.ds(start, wk * GRID_W)
        outs = []
        for h, c in enumerate(cols):
            q = q_ref[:, c]
            sl = _dot_nt(q, kx_ref[rows, c]) * scale + bias_ref[h]
            sc = _dot_nt(q, kc_ref[:, c]) * scale
            outs.append(_softmax_pv([sl, sc], [vx_ref[rows, c], vc_ref[:, c]]))
        o_ref[...] = jnp.concatenate(outs, axis=1).astype(BF16)

    @pl.when(g >= n_groups)
    def _():
        outs = [_softmax_pv([_dot_nt(q_ref[:, c], kc_ref[:, c]) * scale], [vc_ref[:, c]]) for c in cols]
        o_ref[...] = jnp.concatenate(outs, axis=1).astype(BF16)


def _na_call(proj, bias_tab, *, layer, n_b, lx, lc, need_ctx):
    rows = proj.shape[0]
    g_rows = lx // GRID_W
    _, rg, wk, ks, var, _, _ = _na_plan(g_rows, lc)
    n_groups = g_rows // rg
    tq = rg * GRID_W
    nqc = lc // tq
    nq = n_groups + (nqc if need_ctx else 0)
    nbx = n_b * lx // lc
    hd, nd = NA_HEAD_DIM, NA_DIM
    cq, ck, cv = P_NAQ // nd, P_NAK // nd, P_NAV // nd

    def q_row(b, g):
        return jnp.where(g < n_groups, b * n_groups + g, n_b * n_groups + b * nqc + (g - n_groups))

    kern = functools.partial(_na_kernel, n_groups=n_groups, wk=wk, scale=hd ** -0.5)
    grid_spec = pltpu.PrefetchScalarGridSpec(
        num_scalar_prefetch=2,
        grid=(n_b, nq),
        in_specs=[pl.BlockSpec((tq, nd), lambda b, g, var_r, ks_r: (q_row(b, g), cq)),
                  pl.BlockSpec((lx, nd), lambda b, g, var_r, ks_r: (b, ck)),
                  pl.BlockSpec((lx, nd), lambda b, g, var_r, ks_r: (b, cv)),
                  pl.BlockSpec((lc, nd), lambda b, g, var_r, ks_r: (nbx + b, ck)),
                  pl.BlockSpec((lc, nd), lambda b, g, var_r, ks_r: (nbx + b, cv)),
                  pl.BlockSpec((None, NA_HEADS, tq, wk * GRID_W),
                               lambda b, g, var_r, ks_r: (var_r[jnp.minimum(g, n_groups - 1)], layer, 0, 0))],
        out_specs=pl.BlockSpec((tq, nd), lambda b, g, var_r, ks_r: (q_row(b, g), 0)),
    )
    return pl.pallas_call(
        kern,
        grid_spec=grid_spec,
        out_shape=jax.ShapeDtypeStruct((rows if need_ctx else n_b * lx, NA_DIM), BF16),
        compiler_params=_cp(("parallel", "arbitrary"), VMEM_STREAM_MB),
        name="na_attn",
    )(jnp.asarray(var), jnp.asarray(ks), proj, proj, proj, proj, proj, bias_tab)


def _chunk_block(n_b, nxc, ncc):
    def f(b, d, c):
        cc = jnp.where(d == 0, c, ncc - 1 - c)
        cx = jnp.where(d == 0, c - ncc, nxc - 1 - (c - ncc))
        return jnp.where(c < ncc, n_b * nxc + b * ncc + cc, b * nxc + cx)
    return f


def _dir_masks(d):
    row = lax.broadcasted_iota(jnp.int32, (CHUNK, CHUNK), 0)
    col = lax.broadcasted_iota(jnp.int32, (CHUNK, CHUNK), 1)
    diff = (row - col) * jnp.where(d == 0, 1, -1)
    return diff >= 0, diff > 0


def _cumsum_lanes(x, incl):
    cs = jnp.dot(incl.astype(F32), x, preferred_element_type=F32, precision=HIGHEST)
    return cs, cs.T


def _lane_vec(vals, offset):
    flat = vals.reshape(-1).astype(F32)
    return jnp.pad(flat, (offset, P_SMALL - offset - flat.shape[0])).reshape(1, P_SMALL)


def _ssd_prep_kernel(xbc_ref, sm_ref, alog_ref, dtb_ref, dsk_ref, yp_ref, xw_ref, ea_ref):
    nh, hp, ns = SSD_HEADS, SSD_HEAD_DIM, SSD_STATE
    gh = nh // SSD_GROUPS
    dt2 = _softplus(sm_ref[...] + dtb_ref[...])
    dta2 = dt2 * (-jnp.exp(alog_ref[...]))
    row = lax.broadcasted_iota(jnp.int32, (CHUNK, CHUNK), 0)
    col = lax.broadcasted_iota(jnp.int32, (CHUNK, CHUNK), 1)
    incl = [row >= col, row <= col]
    cs = [jnp.dot(m.astype(F32), dta2, preferred_element_type=F32, precision=HIGHEST) for m in incl]
    cs_t = [x.T for x in cs]
    tot2 = jnp.sum(dta2, axis=0, keepdims=True)
    dsk = dsk_ref[...]
    bo, co = SSD_D_INNER, SSD_D_INNER + SSD_GROUPS * ns
    scores = [_dot_nt(xbc_ref[:, co + g * ns:co + (g + 1) * ns], xbc_ref[:, bo + g * ns:bo + (g + 1) * ns])
              for g in range(SSD_GROUPS)]
    xs = [xbc_ref[:, h * hp:(h + 1) * hp].astype(F32) for h in range(nh)]
    yp, xw = [], []
    for d in range(2):
        for h in range(nh):
            ln = d * nh + h
            a_c, a_r = cs[d][:, ln:ln + 1], cs_t[d][ln:ln + 1, :]
            dec = jnp.where(incl[d], jnp.exp(jnp.where(incl[d], a_c - a_r, 0.0)), 0.0)
            m = (scores[h // gh] * dec).astype(BF16)
            y = _dot(m, (xs[h] * dt2[:, ln:ln + 1]).astype(BF16))
            yp.append(y + dsk[:, h:h + 1] * xs[h] if d == 0 else y)
            xw.append((xs[h] * (jnp.exp(tot2[:, ln:ln + 1] - a_c) * dt2[:, ln:ln + 1])).astype(BF16))
    yp_ref[...] = jnp.concatenate(yp, axis=1)
    xw_ref[...] = jnp.concatenate(xw, axis=1)
    ea_ref[...] = jnp.concatenate([jnp.exp(cs[0]), jnp.exp(cs[1])], axis=1)


def _ssd_scan_kernel(*refs):
    (bc0, yp0, xw0, ea0, sm0, bc1, yp1, xw1, ea1, sm1, alog_ref, dtb_ref, y0_ref, y1_ref, s_scr) = refs
    nh, hp, ns = SSD_HEADS, SSD_HEAD_DIM, SSD_STATE
    gh = nh // SSD_GROUPS

    @pl.when(pl.program_id(1) == 0)
    def _():
        s_scr[...] = jnp.zeros_like(s_scr)

    neg_a = -jnp.exp(alog_ref[...])
    for d, (bc, yp, xw, ea, sm, y_ref) in enumerate([(bc0, yp0, xw0, ea0, sm0, y0_ref),
                                                      (bc1, yp1, xw1, ea1, sm1, y1_ref)]):
        c_dec = jnp.exp(jnp.sum(_softplus(sm[...] + dtb_ref[...]) * neg_a, axis=0, keepdims=True))
        e_acum = ea[...]
        ys, states = [], []
        for g in range(SSD_GROUPS):
            bg = bc[:, g * ns:(g + 1) * ns]
            cg = bc[:, SSD_GROUPS * ns + g * ns:SSD_GROUPS * ns + (g + 1) * ns]
            s_g = s_scr[d, :, g * gh * hp:(g + 1) * gh * hp]
            y_int = _dot(cg, s_g.astype(BF16))
            upd = _dot_tn(bg, xw[:, g * gh * hp:(g + 1) * gh * hp])
            for hh in range(gh):
                h = g * gh + hh
                ln = d * nh + h
                sl = slice(hh * hp, (hh + 1) * hp)
                ys.append(yp[:, h * hp:(h + 1) * hp] + y_int[:, sl] * e_acum[:, ln:ln + 1])
                states.append(s_g[:, sl] * c_dec[:, ln:ln + 1] + upd[:, sl])
        y_ref[...] = jnp.concatenate(ys, axis=1)
        s_scr[d] = jnp.concatenate(states, axis=1)


def _ssd_call(xbc, small, a_log, dt_bias, d_skip, *, n_b, lx, lc):
    rows = xbc.shape[0]
    nxc, ncc = lx // CHUNK, lc // CHUNK
    blk = _chunk_block(n_b, nxc, ncc)
    nh, di = SSD_HEADS, SSD_D_INNER
    alog_v, dtb_v = _lane_vec(a_log, 0), _lane_vec(dt_bias, 0)
    ypart, xw, ea = pl.pallas_call(
        _ssd_prep_kernel,
        grid=(rows // CHUNK,),
        in_specs=[pl.BlockSpec((CHUNK, SSD_CONV_DIM), lambda i: (i, 0)),
                  pl.BlockSpec((CHUNK, P_SMALL), lambda i: (i, 0)),
                  pl.BlockSpec((1, P_SMALL), lambda i: (0, 0)),
                  pl.BlockSpec((1, P_SMALL), lambda i: (0, 0)),
                  pl.BlockSpec((1, nh), lambda i: (0, 0))],
        out_specs=[pl.BlockSpec((CHUNK, 2 * di), lambda i: (i, 0)),
                   pl.BlockSpec((CHUNK, 2 * di), lambda i: (i, 0)),
                   pl.BlockSpec((CHUNK, 2 * P_SMALL), lambda i: (i, 0))],
        out_shape=[jax.ShapeDtypeStruct((rows, 2 * di), F32),
                   jax.ShapeDtypeStruct((rows, 2 * di), BF16),
                   jax.ShapeDtypeStruct((rows, 2 * P_SMALL), F32)],
        compiler_params=_cp(("parallel",), VMEM_STREAM_MB),
        name="ssd_prep",
    )(xbc, small, alog_v, dtb_v, d_skip.reshape(1, nh))

    def dir_specs(d):
        at = lambda b, c: blk(b, d, c)
        return [pl.BlockSpec((CHUNK, SSD_CONV_DIM - di), lambda b, c: (at(b, c), 1)),
                pl.BlockSpec((CHUNK, di), lambda b, c: (at(b, c), d)),
                pl.BlockSpec((CHUNK, di), lambda b, c: (at(b, c), d)),
                pl.BlockSpec((CHUNK, P_SMALL), lambda b, c: (at(b, c), d)),
                pl.BlockSpec((CHUNK, P_SMALL), lambda b, c: (at(b, c), 0))]

    const = pl.BlockSpec((1, P_SMALL), lambda b, c: (0, 0))
    return pl.pallas_call(
        _ssd_scan_kernel,
        grid=(n_b, ncc + nxc),
        in_specs=dir_specs(0) + dir_specs(1) + [const, const],
        out_specs=[pl.BlockSpec((CHUNK, di), lambda b, c: (blk(b, 0, c), 0)),
                   pl.BlockSpec((CHUNK, di), lambda b, c: (blk(b, 1, c), 0))],
        out_shape=[jax.ShapeDtypeStruct((rows, di), F32), jax.ShapeDtypeStruct((rows, di), F32)],
        scratch_shapes=[pltpu.VMEM((2, SSD_STATE, di), F32)],
        compiler_params=_cp(("parallel", "arbitrary"), VMEM_STREAM_MB),
        name="ssd_scan",
    )(xbc, ypart, xw, ea, small, xbc, ypart, xw, ea, small, alog_v, dtb_v)


SOLVE_BLOCK = 16


def _unit_tri_solve_many(n_mats, rhss):
    ln = n_mats[0].shape[0]
    row = lax.broadcasted_iota(jnp.int32, (ln, ln), 0)
    col = lax.broadcasted_iota(jnp.int32, (ln, ln), 1)
    same_block = lambda s: (row // s) == (col // s)
    eye = jnp.where(row == col, 1.0, 0.0)
    mm = lambda a, b: _dot(a.astype(BF16), b.astype(BF16))
    ms = [jnp.where(same_block(SOLVE_BLOCK), -n, 0.0) for n in n_mats]
    xs = [eye + m for m in ms]
    mps = ms
    k = 1
    while 2 * k < SOLVE_BLOCK:
        mps = [mm(m, m) for m in mps]
        xs = [x + mm(x, m) for x, m in zip(xs, mps)]
        k *= 2
    s = SOLVE_BLOCK
    while s < ln:
        joins = [jnp.where(same_block(2 * s), jnp.where(same_block(s), 0.0, n), 0.0) for n in n_mats]
        xs = [x - mm(mm(x, e), x) for x, e in zip(xs, joins)]
        s *= 2
    return [mm(x, r) for x, r in zip(xs, rhss)]


GDN_PACK = 5 * GDN_DIM
GDN_G_LANE = 2 * SSD_HEADS
GDN_B_LANE = 2 * SSD_HEADS + 2 * GDN_HEADS


def _gdn_prep_kernel(qkv_ref, sm_ref, alog_ref, dtb_ref, o_ref):
    nh, hd = GDN_HEADS, GDN_HEAD_DIM
    sm = sm_ref[...]
    g2 = -jnp.exp(alog_ref[...]) * _softplus(sm + dtb_ref[...])
    beta2 = _sigmoid(sm)
    row = lax.broadcasted_iota(jnp.int32, (CHUNK, CHUNK), 0)
    col = lax.broadcasted_iota(jnp.int32, (CHUNK, CHUNK), 1)
    incl = [row >= col, row <= col]
    strict = [row > col, row < col]
    cs = [jnp.dot(m.astype(F32), g2, preferred_element_type=F32, precision=HIGHEST) for m in incl]
    cs_t = [x.T for x in cs]
    gtot2 = jnp.sum(g2, axis=0, keepdims=True)
    qn, kn, kn_b, vv, qk_raw = [], [], [], [], []
    for h in range(nh):
        qh = qkv_ref[:, h * hd:(h + 1) * hd].astype(F32)
        kh = qkv_ref[:, GDN_DIM + h * hd:GDN_DIM + (h + 1) * hd].astype(F32)
        vv.append(qkv_ref[:, 2 * GDN_DIM + h * hd:2 * GDN_DIM + (h + 1) * hd].astype(F32))
        qn.append(qh * (lax.rsqrt(jnp.sum(qh * qh, axis=-1, keepdims=True) + EPS) * hd ** -0.5))
        kn.append(kh * lax.rsqrt(jnp.sum(kh * kh, axis=-1, keepdims=True) + EPS))
        kn_b.append(kn[h].astype(BF16))
        qk_raw.append(_dot_nt(qn[h].astype(BF16), kn_b[h]))
    n_mats, rhss, qks, qds, kds = [], [], [], [], []
    for d in range(2):
        for h in range(nh):
            lg, lb = GDN_G_LANE + d * nh + h, GDN_B_LANE + d * nh + h
            gcc, gcr = cs[d][:, lg:lg + 1], cs_t[d][lg:lg + 1, :]
            beta, gtot = beta2[:, lb:lb + 1], gtot2[:, lg:lg + 1]
            dec = jnp.where(incl[d], jnp.exp(jnp.where(incl[d], gcc - gcr, 0.0)), 0.0)
            kb = kn[h] * beta
            n_mats.append(jnp.where(strict[d], _dot_nt(kb.astype(BF16), kn_b[h]) * dec, 0.0))
            e_gc = jnp.exp(gcc)
            rhss.append(jnp.concatenate([vv[h] * beta, kb * e_gc], axis=1))
            qks.append(qk_raw[h] * dec)
            qds.append(qn[h] * e_gc)
            kds.append(kn[h] * jnp.exp(gtot - gcc))
    sols = _unit_tri_solve_many(n_mats, rhss)
    pieces = []
    for d in range(2):
        js = range(d * nh, (d + 1) * nh)
        pieces += [sols[j][:, 0:hd] for j in js] + [sols[j][:, hd:2 * hd] for j in js]
        pieces += [qks[j] for j in js] + [qds[j] for j in js] + [kds[j] for j in js]
    o_ref[...] = jnp.concatenate([p.astype(BF16) for p in pieces], axis=1)


def _gdn_scan_kernel(pk0, sm0, pk1, sm1, alog_ref, dtb_ref, o0_ref, o1_ref, s_scr):
    nh, hd = GDN_HEADS, GDN_HEAD_DIM

    @pl.when(pl.program_id(1) == 0)
    def _():
        s_scr[...] = jnp.zeros_like(s_scr)

    neg_a = -jnp.exp(alog_ref[...])
    for d, (pk_ref, sm_ref, o_ref) in enumerate([(pk0, sm0, o0_ref), (pk1, sm1, o1_ref)]):
        g_end2 = jnp.exp(jnp.sum(neg_a * _softplus(sm_ref[...] + dtb_ref[...]), axis=0, keepdims=True))
        outs, states = [], []
        for h in range(nh):
            lg = GDN_G_LANE + d * nh + h
            part = lambda j: pk_ref[:, (j * nh + h) * hd:(j * nh + h + 1) * hd]
            u, w, qk, qd, kd = part(0), part(1), part(2), part(3), part(4)
            s_h = s_scr[d, :, h * hd:(h + 1) * hd]
            s_b = s_h.astype(BF16)
            v_new = (u.astype(F32) - _dot(w, s_b)).astype(BF16)
            outs.append(_dot(qd, s_b) + _dot(qk, v_new))
            states.append(s_h * g_end2[:, lg:lg + 1] + _dot_tn(kd, v_new))
        o_ref[...] = jnp.concatenate(outs, axis=1)
        s_scr[d] = jnp.concatenate(states, axis=1)


def _gdn_call(qkv, small, a_log, dt_bias, *, n_b, lx, lc):
    rows = qkv.shape[0]
    nxc, ncc = lx // CHUNK, lc // CHUNK
    blk = _chunk_block(n_b, nxc, ncc)
    alog_v, dtb_v = _lane_vec(a_log, GDN_G_LANE), _lane_vec(dt_bias, GDN_G_LANE)
    packed = pl.pallas_call(
        _gdn_prep_kernel,
        grid=(rows // CHUNK,),
        in_specs=[pl.BlockSpec((CHUNK, 3 * GDN_DIM), lambda i: (i, 0)),
                  pl.BlockSpec((CHUNK, P_SMALL), lambda i: (i, 0)),
                  pl.BlockSpec((1, P_SMALL), lambda i: (0, 0)),
                  pl.BlockSpec((1, P_SMALL), lambda i: (0, 0))],
        out_specs=pl.BlockSpec((CHUNK, 2 * GDN_PACK), lambda i: (i, 0)),
        out_shape=jax.ShapeDtypeStruct((rows, 2 * GDN_PACK), BF16),
        compiler_params=_cp(("parallel",), VMEM_STREAM_MB),
        name="gdn_prep",
    )(qkv, small, alog_v, dtb_v)
    def dir_specs(d):
        return [pl.BlockSpec((CHUNK, GDN_PACK), lambda b, c: (blk(b, d, c), d)),
                pl.BlockSpec((CHUNK, P_SMALL), lambda b, c: (blk(b, d, c), 0))]

    const = pl.BlockSpec((1, P_SMALL), lambda b, c: (0, 0))
    return pl.pallas_call(
        _gdn_scan_kernel,
        grid=(n_b, ncc + nxc),
        in_specs=dir_specs(0) + dir_specs(1) + [const, const],
        out_specs=[pl.BlockSpec((CHUNK, GDN_DIM), lambda b, c: (blk(b, 0, c), 0)),
                   pl.BlockSpec((CHUNK, GDN_DIM), lambda b, c: (blk(b, 1, c), 0))],
        out_shape=[jax.ShapeDtypeStruct((rows, GDN_DIM), F32), jax.ShapeDtypeStruct((rows, GDN_DIM), F32)],
        scratch_shapes=[pltpu.VMEM((2, GDN_HEAD_DIM, GDN_DIM), F32)],
        compiler_params=_cp(("parallel", "arbitrary"), VMEM_STREAM_MB),
        name="gdn_scan",
    )(packed, small, packed, small, alog_v, dtb_v)


MIXOUT_SUB_ROWS = 256


def _mixout_kernel(h_ref, mod_ref, gpost_ref, ya_ref, ys0_ref, ys1_ref, zs_ref, sn_ref, yn_ref, og0_ref, og1_ref,
                   zg_ref, gn_ref, w_ref, o_ref):
    hd = GDN_HEAD_DIM
    tm = h_ref.shape[0]
    sub = min(tm, MIXOUT_SUB_ROWS)
    outs = []
    for s in range(0, tm, sub):
        r = slice(s, s + sub)
        ssd = (ys0_ref[r, :] + ys1_ref[r, :]) * _silu(zs_ref[r, :].astype(F32))
        yb = (ssd * _rms_scale(ssd) * sn_ref[...]).astype(BF16)
        gd = og0_ref[r, :] + og1_ref[r, :]
        zg = _silu(zg_ref[r, :].astype(F32))
        yd = []
        for h in range(GDN_HEADS):
            oh = gd[:, h * hd:(h + 1) * hd]
            yd.append((oh * _rms_scale(oh) * gn_ref[...] * zg[:, h * hd:(h + 1) * hd]).astype(BF16))
        parts = [ya_ref[r, :], yb, yn_ref[r, :]] + yd
        widths = [512, 512, 512] + [hd] * GDN_HEADS
        y = None
        off = 0
        for part, wd in zip(parts, widths):
            t = _dot(part, w_ref[off:off + wd, :])
            y = t if y is None else y + t
            off += wd
        outs.append(h_ref[r, :] + mod_ref[2:3, :] * (y * _rms_scale(y) * gpost_ref[...]))
    o_ref[...] = jnp.concatenate(outs, axis=0)


def _mixout_call(h, mod_l, g_post, ya, ys, proj, ssd_norm, yn, og, gdn_norm, w_out, *, n_b, lx, n_rows, tm):
    d = h.shape[1]
    nxt, per_b = n_b * lx // tm, lx // tm
    mod_idx = lambda i: (jnp.where(i < nxt, i // per_b, n_b), 0, 0)
    row = lambda i: (i, 0)
    const = lambda i: (0, 0)
    return pl.pallas_call(
        _mixout_kernel,
        grid=(n_rows // tm,),
        in_specs=[pl.BlockSpec((tm, d), row),
                  pl.BlockSpec((None, 6, d), mod_idx),
                  pl.BlockSpec((1, d), const),
                  pl.BlockSpec((tm, 512), row),
                  pl.BlockSpec((tm, 512), row),
                  pl.BlockSpec((tm, 512), row),
                  pl.BlockSpec((tm, 512), lambda i: (i, P_ZSSD // 512)),
                  pl.BlockSpec((1, 512), const),
                  pl.BlockSpec((tm, 512), row),
                  pl.BlockSpec((tm, 512), row),
                  pl.BlockSpec((tm, 512), row),
                  pl.BlockSpec((tm, 512), lambda i: (i, P_GZ // 512)),
                  pl.BlockSpec((1, GDN_HEAD_DIM), const),
                  pl.BlockSpec((d, d), const)],
        out_specs=pl.BlockSpec((tm, d), row),
        out_shape=jax.ShapeDtypeStruct((n_rows, d), F32),
        compiler_params=_cp(("parallel",), VMEM_WEIGHTS_MB),
        name="mix_out",
    )(h, mod_l, g_post.reshape(1, d), ya, ys[0], ys[1], proj, ssd_norm.reshape(1, 512), yn, og[0], og[1], proj,
      gdn_norm.reshape(1, GDN_HEAD_DIM), w_out)


def _ffn_kernel(h_ref, mod_ref, gpre_ref, gpost_ref, wg_ref, wu_ref, wd_ref, o_ref, u_scr, acc_scr):
    j = pl.program_id(1)

    @pl.when(j == 0)
    def _():
        x = h_ref[...]
        y = x * _rms_scale(x) * gpre_ref[...]
        u_scr[...] = (y * (1.0 + mod_ref[4:5, :]) + mod_ref[3:4, :]).astype(BF16)
        acc_scr[...] = jnp.zeros_like(acc_scr)

    u = u_scr[...]
    mid = (_silu(_dot(u, wg_ref[...])) * _dot(u, wu_ref[...])).astype(BF16)
    acc_scr[...] += _dot(mid, wd_ref[...])

    @pl.when(j == pl.num_programs(1) - 1)
    def _():
        y = acc_scr[...]
        o_ref[...] = h_ref[...] + mod_ref[5:6, :] * (y * _rms_scale(y) * gpost_ref[...])


def _ffn_call(h, mod_l, g_pre, g_post, wg, wu, wd, *, n_b, lx, tm, tf):
    rows, d = h.shape
    ff = wg.shape[1]
    nxt, per_b = n_b * lx // tm, lx // tm
    mod_idx = lambda i, j: (jnp.where(i < nxt, i // per_b, n_b), 0, 0)
    return pl.pallas_call(
        _ffn_kernel,
        grid=(rows // tm, ff // tf),
        in_specs=[pl.BlockSpec((tm, d), lambda i, j: (i, 0)),
                  pl.BlockSpec((None, 6, d), mod_idx),
                  pl.BlockSpec((1, d), lambda i, j: (0, 0)),
                  pl.BlockSpec((1, d), lambda i, j: (0, 0)),
                  pl.BlockSpec((d, tf), lambda i, j: (0, j)),
                  pl.BlockSpec((d, tf), lambda i, j: (0, j)),
                  pl.BlockSpec((tf, d), lambda i, j: (j, 0))],
        out_specs=pl.BlockSpec((tm, d), lambda i, j: (i, 0)),
        out_shape=jax.ShapeDtypeStruct((rows, d), F32),
        scratch_shapes=[pltpu.VMEM((tm, d), BF16), pltpu.VMEM((tm, d), F32)],
        compiler_params=_cp(("parallel", "arbitrary"), VMEM_WEIGHTS_MB),
        name="ffn_swiglu",
    )(h, mod_l, g_pre.reshape(1, d), g_post.reshape(1, d), wg, wu, wd)


MOE_ALIGN = 16
MOE_GROUP_TILE = 512


def _moe_local_rows(ts):
    return -(-(TOP_K * ts + N_EXPERTS * MOE_ALIGN) // 128) * 128


def _moe_route_kernel(h_ref, mod_ref, gpre_ref, wr_ref, u_ref, dest_ref, gate_ref, cnt_ref):
    ts, lanes = dest_ref.shape
    x = h_ref[...]
    y = x * _rms_scale(x) * gpre_ref[...]
    u = y * (1.0 + mod_ref[4:5, :]) + mod_ref[3:4, :]
    u_hi = u.astype(BF16)
    u_ref[...] = u_hi
    u_lo = (u - u_hi.astype(F32)).astype(BF16)
    logits = _dot(u_hi, wr_ref[0]) + (_dot(u_hi, wr_ref[1]) + _dot(u_lo, wr_ref[0]))
    lane = lax.broadcasted_iota(jnp.int32, logits.shape, 1).astype(F32)
    lg = jnp.where(lane < N_EXPERTS, logits, NEG)
    m1 = jnp.max(lg, axis=-1, keepdims=True)
    i1 = jnp.min(jnp.where(lg == m1, lane, float(lanes)), axis=-1, keepdims=True)
    lg2 = jnp.where(lane == i1, NEG, lg)
    m2 = jnp.max(lg2, axis=-1, keepdims=True)
    i2 = jnp.min(jnp.where(lg2 == m2, lane, float(lanes)), axis=-1, keepdims=True)
    e2 = jnp.exp(m2 - m1)
    gate_ref[...] = jnp.where(lane == 0.0, 1.0 / (1.0 + e2), jnp.where(lane == 1.0, e2 / (1.0 + e2), 0.0))
    sel = jnp.where(lane == i1, 1.0, 0.0) + jnp.where(lane == i2, 1.0, 0.0)
    cnt = jnp.sum(sel, axis=0, keepdims=True)
    cnt_al = jnp.floor((cnt + (MOE_ALIGN - 1)) / MOE_ALIGN) * MOE_ALIGN
    cnt_ref[...] = jnp.broadcast_to(cnt_al, cnt_ref.shape)
    before = lax.broadcasted_iota(jnp.int32, (lanes, lanes), 0) < lax.broadcasted_iota(jnp.int32, (lanes, lanes), 1)
    seg_lo = jnp.dot(jnp.broadcast_to(cnt_al, (8, lanes)), before.astype(F32),
                     preferred_element_type=F32, precision=HIGHEST)[0:1]
    earlier = lax.broadcasted_iota(jnp.int32, (ts, ts), 1) < lax.broadcasted_iota(jnp.int32, (ts, ts), 0)
    rank = _dot(earlier.astype(BF16), sel.astype(BF16))
    slot = seg_lo + rank
    slot_1 = jnp.sum(jnp.where(lane == i1, slot, 0.0), axis=-1, keepdims=True)
    slot_2 = jnp.sum(jnp.where(lane == i2, slot, 0.0), axis=-1, keepdims=True)
    dest_ref[...] = jnp.where(lane == 0.0, slot_1, jnp.where(lane == 1.0, slot_2, -1.0))


MOE_BIG = 4


def _moe_copy(hbm_ref, loc_ref, sems, far_row, loc_row, size_idx, *, to_hbm):
    rows = (MOE_BIG * MOE_ALIGN, MOE_ALIGN)[size_idx]
    loc = loc_ref.at[pl.ds(pl.multiple_of(loc_row, MOE_ALIGN), rows), :]
    far = hbm_ref.at[pl.ds(pl.multiple_of(far_row, MOE_ALIGN), rows), :]
    src, dst = (loc, far) if to_hbm else (far, loc)
    return pltpu.make_async_copy(src, dst, sems.at[size_idx])


def _moe_segment_copies(i, off_ref, n_ref, hbm_ref, loc_ref, sems, *, to_hbm):
    lo = jnp.int32(0)
    n_big_all, n_small_all = jnp.int32(0), jnp.int32(0)
    for e in range(N_EXPERTS):
        n_chunks = n_ref[i * N_EXPERTS + e]
        go = off_ref[i * N_EXPERTS + e]
        n_big = n_chunks // MOE_BIG
        n_small = n_chunks - n_big * MOE_BIG
        done = n_big * (MOE_BIG * MOE_ALIGN)

        def big(k, carry, lo=lo, go=go):
            step = k * (MOE_BIG * MOE_ALIGN)
            _moe_copy(hbm_ref, loc_ref, sems, go + step, lo + step, 0, to_hbm=to_hbm).start()
            return carry

        def small(k, carry, lo=lo, go=go, done=done):
            step = done + k * MOE_ALIGN
            _moe_copy(hbm_ref, loc_ref, sems, go + step, lo + step, 1, to_hbm=to_hbm).start()
            return carry

        lax.fori_loop(0, n_big, big, 0)
        lax.fori_loop(0, n_small, small, 0)
        lo = lo + n_chunks * MOE_ALIGN
        n_big_all, n_small_all = n_big_all + n_big, n_small_all + n_small
    return lo, n_big_all, n_small_all


def _moe_wait_copies(n_big, n_small, hbm_ref, loc_ref, sems, *, to_hbm):
    for size_idx, n in ((0, n_big), (1, n_small)):
        def body(k, carry, size_idx=size_idx):
            _moe_copy(hbm_ref, loc_ref, sems, 0, 0, size_idx, to_hbm=to_hbm).wait()
            return carry

        lax.fori_loop(0, n, body, 0)


def _moe_segment_counts(i, n_ref):
    n_chunks = [n_ref[i * N_EXPERTS + e] for e in range(N_EXPERTS)]
    n_big = sum(n // MOE_BIG for n in n_chunks)
    return sum(n_chunks) * MOE_ALIGN, n_big, sum(n_chunks) - n_big * MOE_BIG


def _moe_sort_kernel(off_ref, n_ref, u_ref, dest_ref, init_ref, us_ref, loc_scr, sems):
    del init_ref
    i = pl.program_id(0)
    cur = lax.rem(i, 2)
    dest = dest_ref[...]
    slot = lax.broadcasted_iota(jnp.int32, (dest.shape[0], loc_scr.shape[1]), 1).astype(F32)
    onehot = jnp.where(slot == dest[:, 0:1], 1.0, jnp.where(slot == dest[:, 1:2], 1.0, 0.0))
    loc_scr[cur] = _dot_tn(onehot.astype(BF16), u_ref[...]).astype(BF16)
    _, n_big, n_small = _moe_segment_copies(i, off_ref, n_ref, us_ref, loc_scr.at[cur], sems.at[cur], to_hbm=True)

    @pl.when(i > 0)
    def _():
        _, p_big, p_small = _moe_segment_counts(i - 1, n_ref)
        _moe_wait_copies(p_big, p_small, us_ref, loc_scr.at[1 - cur], sems.at[1 - cur], to_hbm=True)

    @pl.when(i == pl.num_programs(0) - 1)
    def _():
        _moe_wait_copies(n_big, n_small, us_ref, loc_scr.at[cur], sems.at[cur], to_hbm=True)


def _moe_expert_kernel(te_ref, tv_ref, u_ref, wg_ref, wu_ref, wd_ref, y_ref, acc_scr):
    k, j = pl.program_id(0), pl.program_id(1)
    last = pl.num_programs(1) - 1
    valid = tv_ref[k] > 0

    @pl.when(valid)
    def _():
        @pl.when(j == 0)
        def _():
            acc_scr[...] = jnp.zeros_like(acc_scr)

        u = u_ref[...]
        mid = (_silu(_dot(u, wg_ref[...])) * _dot(u, wu_ref[...])).astype(BF16)
        acc_scr[...] += _dot(mid, wd_ref[...])

        @pl.when(j == last)
        def _():
            y_ref[...] = acc_scr[...].astype(BF16)

    @pl.when(jnp.logical_not(valid) & (j == last))
    def _():
        y_ref[...] = jnp.zeros_like(y_ref)


def _moe_combine_kernel(off_ref, n_ref, h_ref, mod_ref, gpost_ref, dest_ref, gate_ref, ys_ref, o_ref, loc_scr, sems):
    i = pl.program_id(0)
    cur = lax.rem(i, 2)

    @pl.when(i == 0)
    def _():
        _moe_segment_copies(i, off_ref, n_ref, ys_ref, loc_scr.at[cur], sems.at[cur], to_hbm=False)

    @pl.when(i < pl.num_programs(0) - 1)
    def _():
        _moe_segment_copies(i + 1, off_ref, n_ref, ys_ref, loc_scr.at[1 - cur], sems.at[1 - cur], to_hbm=False)

    dest, gate = dest_ref[...], gate_ref[...]
    slot = lax.broadcasted_iota(jnp.int32, (dest.shape[0], loc_scr.shape[1]), 1).astype(F32)
    w = jnp.where(slot == dest[:, 0:1], gate[:, 0:1], jnp.where(slot == dest[:, 1:2], gate[:, 1:2], 0.0))
    w_hi = w.astype(BF16)
    w_lo = (w - w_hi.astype(F32)).astype(BF16)
    n_rows, n_big, n_small = _moe_segment_counts(i, n_ref)
    _moe_wait_copies(n_big, n_small, ys_ref, loc_scr.at[cur], sems.at[cur], to_hbm=False)
    filled = lax.broadcasted_iota(jnp.int32, (loc_scr.shape[1], 1), 0) < n_rows
    y_cur = loc_scr[cur]
    y_loc = jnp.where(filled, y_cur, jnp.zeros_like(y_cur))
    y = _dot(w_hi, y_loc) + _dot(w_lo, y_loc)
    o_ref[...] = h_ref[...] + mod_ref[5:6, :] * (y * _rms_scale(y) * gpost_ref[...])


def _moe_call(h, mod_l, g_pre, g_post, router_pad, wg, wu, wd, *, n_b, lx, tm, tf):
    rows, d = h.shape
    ne, _, fe = wg.shape
    ts, tg = tm, MOE_GROUP_TILE
    n_tiles = rows // ts
    lrows = _moe_local_rows(ts)
    nt_max = -(-(TOP_K * rows + n_tiles * ne * (MOE_ALIGN - 1)) // tg) + ne
    n_pad = nt_max * tg
    nxt, per_b = n_b * lx // ts, lx // ts
    mod_row = lambda i: jnp.where(i < nxt, i // per_b, n_b)

    u, dest, gate, cnt = pl.pallas_call(
        _moe_route_kernel,
        grid=(n_tiles,),
        in_specs=[pl.BlockSpec((ts, d), lambda i: (i, 0)),
                  pl.BlockSpec((None, 6, d), lambda i: (mod_row(i), 0, 0)),
                  pl.BlockSpec((1, d), lambda i: (0, 0)),
                  pl.BlockSpec((2, d, 128), lambda i: (0, 0, 0))],
        out_specs=[pl.BlockSpec((ts, d), lambda i: (i, 0)),
                   pl.BlockSpec((ts, 128), lambda i: (i, 0)),
                   pl.BlockSpec((ts, 128), lambda i: (i, 0)),
                   pl.BlockSpec((None, 8, 128), lambda i: (i, 0, 0))],
        out_shape=[jax.ShapeDtypeStruct((rows, d), BF16),
                   jax.ShapeDtypeStruct((rows, 128), F32),
                   jax.ShapeDtypeStruct((rows, 128), F32),
                   jax.ShapeDtypeStruct((n_tiles, 8, 128), F32)],
        compiler_params=_cp(("parallel",), VMEM_STREAM_MB),
        name="moe_route",
    )(h, mod_l, g_pre.reshape(1, d), router_pad)

    cnt_al = cnt[:, 0, :ne].astype(jnp.int32)
    group = -(-jnp.sum(cnt_al, axis=0) // tg) * tg
    group_end = jnp.cumsum(group)
    seg_off = (group_end - group)[None, :] + jnp.cumsum(cnt_al, axis=0) - cnt_al
    n_used = group_end[-1] // tg
    tile_id = jnp.arange(nt_max, dtype=jnp.int32)
    tile_valid = (tile_id < n_used).astype(jnp.int32)
    tile_expert = jnp.searchsorted(group_end // tg, jnp.minimum(tile_id, n_used - 1), side='right')
    tile_expert = jnp.minimum(tile_expert, ne - 1).astype(jnp.int32)
    seg_off = seg_off.reshape(-1).astype(jnp.int32)
    seg_chunks = (cnt_al // MOE_ALIGN).reshape(-1)

    u_sorted = pl.pallas_call(
        _moe_sort_kernel,
        grid_spec=pltpu.PrefetchScalarGridSpec(
            num_scalar_prefetch=2,
            grid=(n_tiles,),
            in_specs=[pl.BlockSpec((ts, d), lambda i, o_r, n_r: (i, 0)),
                      pl.BlockSpec((ts, 128), lambda i, o_r, n_r: (i, 0)),
                      pl.BlockSpec(memory_space=pl.ANY)],
            out_specs=pl.BlockSpec(memory_space=pl.ANY),
            scratch_shapes=[pltpu.VMEM((2, lrows, d), BF16), pltpu.SemaphoreType.DMA((2, 2))]),
        out_shape=jax.ShapeDtypeStruct((n_pad, d), BF16),
        input_output_aliases={4: 0},
        compiler_params=_cp(("arbitrary",), VMEM_STREAM_MB),
        name="moe_sort",
    )(seg_off, seg_chunks, u, dest, jnp.zeros((n_pad, d), BF16))

    nj = fe // tf
    w_col = lambda k, j, te_r, tv_r: jnp.where(tv_r[k] > 0, j, nj - 1)
    y_sorted = pl.pallas_call(
        _moe_expert_kernel,
        grid_spec=pltpu.PrefetchScalarGridSpec(
            num_scalar_prefetch=2,
            grid=(nt_max, nj),
            in_specs=[pl.BlockSpec((tg, d), lambda k, j, te_r, tv_r: (k, 0)),
                      pl.BlockSpec((None, d, tf), lambda k, j, te_r, tv_r: (te_r[k], 0, w_col(k, j, te_r, tv_r))),
                      pl.BlockSpec((None, d, tf), lambda k, j, te_r, tv_r: (te_r[k], 0, w_col(k, j, te_r, tv_r))),
                      pl.BlockSpec((None, tf, d), lambda k, j, te_r, tv_r: (te_r[k], w_col(k, j, te_r, tv_r), 0))],
            out_specs=pl.BlockSpec((tg, d), lambda k, j, te_r, tv_r: (k, 0)),
            scratch_shapes=[pltpu.VMEM((tg, d), F32)]),
        out_shape=jax.ShapeDtypeStruct((n_pad, d), BF16),
        compiler_params=_cp(("parallel", "arbitrary"), VMEM_ATTN_MB),
        name="moe_experts",
    )(tile_expert, tile_valid, u_sorted, wg, wu, wd)

    return pl.pallas_call(
        _moe_combine_kernel,
        grid_spec=pltpu.PrefetchScalarGridSpec(
            num_scalar_prefetch=2,
            grid=(n_tiles,),
            in_specs=[pl.BlockSpec((ts, d), lambda i, o_r, n_r: (i, 0)),
                      pl.BlockSpec((None, 6, d), lambda i, o_r, n_r: (mod_row(i), 0, 0)),
                      pl.BlockSpec((1, d), lambda i, o_r, n_r: (0, 0)),
                      pl.BlockSpec((ts, 128), lambda i, o_r, n_r: (i, 0)),
                      pl.BlockSpec((ts, 128), lambda i, o_r, n_r: (i, 0)),
                      pl.BlockSpec(memory_space=pl.ANY)],
            out_specs=pl.BlockSpec((ts, d), lambda i, o_r, n_r: (i, 0)),
            scratch_shapes=[pltpu.VMEM((2, lrows, d), BF16), pltpu.SemaphoreType.DMA((2, 2))]),
        out_shape=jax.ShapeDtypeStruct((rows, d), F32),
        compiler_params=_cp(("arbitrary",), VMEM_ATTN_MB),
        name="moe_combine",
    )(seg_off, seg_chunks, h, mod_l, g_post.reshape(1, d), dest, gate, y_sorted)


def _rope_tables(lx, tm):
    half = MLA_ROPE // 2
    n_axis = half // 2
    inv_freq = ROPE_THETA ** (-jnp.arange(n_axis, dtype=F32) / n_axis)
    pos = jnp.arange(lx)
    rows = (pos // GRID_W).astype(F32)
    cols = (pos % GRID_W).astype(F32)
    ang = jnp.concatenate([rows[:, None] * inv_freq, cols[:, None] * inv_freq], axis=-1)
    cos, sin = jnp.cos(ang), jnp.sin(ang)
    cos_t = jnp.concatenate([cos, cos], axis=-1)
    sin_t = jnp.concatenate([-sin, sin], axis=-1)
    cos_t = jnp.concatenate([cos_t, jnp.ones((tm, MLA_ROPE), F32)], axis=0)
    sin_t = jnp.concatenate([sin_t, jnp.zeros((tm, MLA_ROPE), F32)], axis=0)
    return cos_t, sin_t


def _mla_weights(q_norm, w_uq, kv_norm, w_ukv):
    dq = MLA_NOPE + MLA_ROPE
    half = MLA_ROPE // 2
    cols = lambda a, b: w_uq[:, a:b]
    nope = [cols(h * dq, h * dq + MLA_NOPE) for h in range(MLA_HEADS)]
    rope = [cols(h * dq + MLA_NOPE, (h + 1) * dq) for h in range(MLA_HEADS)]
    rope_sw = [cols(h * dq + MLA_NOPE + s * half, h * dq + MLA_NOPE + (s + 1) * half)
               for h in range(MLA_HEADS) for s in (1, 0)]
    pad = ((0, MLA_ROPE), (0, 0))
    wq = jnp.pad(jnp.concatenate(nope + rope, axis=1), pad).astype(BF16)
    wqs = jnp.pad(jnp.concatenate(rope_sw, axis=1), pad).astype(BF16)
    qn_ext = jnp.pad(q_norm, (0, MLA_ROPE)).reshape(1, -1)
    perm = np.zeros((MLA_ROPE, MLA_ROPE), np.float32)
    perm[(np.arange(MLA_ROPE) + half) % MLA_ROPE, np.arange(MLA_ROPE)] = 1.0
    return qn_ext, wq, wqs, kv_norm.reshape(1, -1), w_ukv.astype(BF16), jnp.asarray(perm, BF16)


def _pick_tile(n, cands):
    for t in cands:
        if n % t == 0:
            return t
    raise ValueError(f"no tile for {n}")


def _tile_plan(n_b, lx, lc, d_ff, d_expert):
    seq = math.gcd(lx, n_b * lc)
    return dict(
        tm=_pick_tile(seq, (512, 256, 128)),
        tm_in=_pick_tile(seq, (1024, 512, 256, 128)),
        tq=_pick_tile(lc, (256, 128)),
        tr=_pick_tile(lc, (256, 128)),
        tf_ffn=_pick_tile(d_ff, (512, 256, 128)),
        tf_moe=_pick_tile(d_expert, (256, 128)),
    )


def kernel(x, c, ctx, c_ctx, w_ada, b_ada, g_pre_mix, g_post_mix, g_pre_ffn, g_post_ffn, w_in, w_out, mla_q_norm, mla_w_uq, mla_kv_norm, mla_w_ukv, ssd_conv_w, ssd_conv_b, ssd_a_log, ssd_dt_bias, ssd_d, ssd_norm, na_rpb, gdn_conv_w, gdn_a_log, gdn_dt_bias, gdn_norm, ffn_w_gate, ffn_w_up, ffn_w_down, moe_router, moe_w_gate, moe_w_up, moe_w_down):
    n_b, lx, d = x.shape
    lc = ctx.shape[1]
    depth = w_ada.shape[0]
    rows_x, rows_c = n_b * lx, n_b * lc
    assert n_b + 1 <= 8 and lx % GRID_W == 0 and lx % lc == 0 and lc % CHUNK == 0
    plan = _tile_plan(n_b, lx, lc, ffn_w_gate.shape[-1], moe_w_gate.shape[-1])
    tm, tm_in, tq, tr = plan["tm"], plan["tm_in"], plan["tq"], plan["tr"]

    cvec = jnp.concatenate([c, c_ctx[None, :], jnp.zeros((8 - n_b - 1, d), F32)], axis=0)
    mod = _ada_call(cvec, w_ada, b_ada).reshape(depth, 8, 6, d)
    cos_t, sin_t = _rope_tables(lx, tm)
    na_bias = _na_bias_table(na_rpb.reshape((-1,) + na_rpb.shape[2:]), lx // GRID_W, lc)
    h_all = jnp.concatenate([x.reshape(rows_x, d), ctx.reshape(rows_c, d)], axis=0)

    for i in range(depth):
        need_ctx = i < depth - 1
        w_main, w_small = _regroup_w_in(w_in[i])
        proj, small = _inproj_call(h_all, mod[i], g_pre_mix[i], w_main, w_small, n_b=n_b, lx=lx, tm=tm_in)

        mla_w = _mla_weights(mla_q_norm[i], mla_w_uq[i], mla_kv_norm[i], mla_w_ukv[i])
        q_a, k_a, v_a = _mla_prep_call(proj, *mla_w[:5], cos_t, sin_t, mla_w[5], n_b=n_b, lx=lx, tm=tm)
        ya = _mla_attn_call(q_a, k_a, v_a, n_b=n_b, lx=lx, lc=lc, tq=tq, need_ctx=need_ctx)

        xbc = _conv_call(proj, ssd_conv_w[i], ssd_conv_b[i], col_off=P_XBC, n_b=n_b, lx=lx, lc=lc, tr=tr)
        ys = _ssd_call(xbc, small, ssd_a_log[i], ssd_dt_bias[i], ssd_d[i], n_b=n_b, lx=lx, lc=lc)

        yn = _na_call(proj, na_bias, layer=i, n_b=n_b, lx=lx, lc=lc, need_ctx=need_ctx)

        qkv = _conv_call(proj, gdn_conv_w[i], jnp.zeros((3 * GDN_DIM,), F32), col_off=P_GQKV,
                         n_b=n_b, lx=lx, lc=lc, tr=tr)
        og = _gdn_call(qkv, small, gdn_a_log[i], gdn_dt_bias[i], n_b=n_b, lx=lx, lc=lc)

        n_rows = rows_x + rows_c if need_ctx else rows_x
        h_mid = _mixout_call(h_all, mod[i], g_post_mix[i], ya, ys, proj, ssd_norm[i], yn, og, gdn_norm[i],
                             w_out[i].astype(BF16), n_b=n_b, lx=lx, n_rows=n_rows, tm=tm)
        j = i // 2
        if i % 2 == 0:
            h_all = _ffn_call(h_mid, mod[i], g_pre_ffn[i], g_post_ffn[i], ffn_w_gate[j].astype(BF16),
                              ffn_w_up[j].astype(BF16), ffn_w_down[j].astype(BF16), n_b=n_b, lx=lx, tm=tm,
                              tf=plan["tf_ffn"])
        else:
            router_f32 = jnp.pad(moe_router[j], ((0, 0), (0, 128 - N_EXPERTS)))
            router_hi = router_f32.astype(BF16)
            router_pad = jnp.stack([router_hi, (router_f32 - router_hi.astype(F32)).astype(BF16)])
            h_all = _moe_call(h_mid, mod[i], g_pre_ffn[i], g_post_ffn[i], router_pad, moe_w_gate[j].astype(BF16),
                              moe_w_up[j].astype(BF16), moe_w_down[j].astype(BF16), n_b=n_b, lx=lx, tm=tm,
                              tf=plan["tf_moe"])
    return h_all[:rows_x].reshape(n_b, lx, d)
```

```python
import functools
import math

import numpy as np
import jax
import jax.numpy as jnp
from jax import lax
from jax.experimental import pallas as pl
from jax.experimental.pallas import tpu as pltpu

F32 = jnp.float32
BF16 = jnp.bfloat16
HIGHEST = lax.Precision.HIGHEST

GRID_W = 64
EPS = 1e-6
ROPE_THETA = 10000.0
CHUNK = 128
CONV_W = 5
MLA_HEADS, MLA_NOPE, MLA_ROPE, MLA_V = 4, 128, 64, 128
MLA_Q_LORA, MLA_KV_LORA = 448, 128
SSD_HEADS, SSD_HEAD_DIM, SSD_STATE, SSD_GROUPS = 8, 64, 128, 2
SSD_D_INNER = SSD_HEADS * SSD_HEAD_DIM
SSD_CONV_DIM = SSD_D_INNER + 2 * SSD_GROUPS * SSD_STATE
NA_HEADS, NA_HEAD_DIM = 4, 128
NA_DIM = NA_HEADS * NA_HEAD_DIM
NA_WIN_ROWS, NA_WIN_COLS = 8, 16
GDN_HEADS, GDN_HEAD_DIM = 4, 128
GDN_DIM = GDN_HEADS * GDN_HEAD_DIM
N_EXPERTS, TOP_K = 8, 2
MLA_COLS = MLA_Q_LORA + MLA_KV_LORA + MLA_ROPE
SSD_COLS = SSD_D_INNER + SSD_CONV_DIM + 2 * SSD_HEADS
NA_COLS = 3 * NA_DIM
GDN_COLS = 4 * GDN_DIM + 4 * GDN_HEADS

P_XBC, P_ZSSD, P_NAQ, P_NAK, P_NAV = 0, 1024, 1536, 2048, 2560
P_GQKV, P_GZ, P_MLA = 3072, 4608, 5120
P_MAIN = 5760
P_SMALL = 128
NEG = -1e30
VMEM_MB = 1024 * 1024
VMEM_STREAM_MB, VMEM_ATTN_MB, VMEM_WEIGHTS_MB = 40, 48, 56


def _cp(sem, mb):
    return pltpu.CompilerParams(dimension_semantics=sem, vmem_limit_bytes=mb * VMEM_MB)


def _dot(a, b):
    return jnp.dot(a, b, preferred_element_type=F32)


def _dot_nt(a, b, precision=None):
    return lax.dot_general(a, b, (((1,), (1,)), ((), ())), preferred_element_type=F32, precision=precision)


def _dot_tn(a, b):
    return lax.dot_general(a, b, (((0,), (0,)), ((), ())), preferred_element_type=F32)


def _sigmoid(x):
    return 1.0 / (1.0 + jnp.exp(-x))


def _silu(x):
    return x * _sigmoid(x)


def _softplus(x):
    return jnp.maximum(x, 0.0) + jnp.log(1.0 + jnp.exp(-jnp.abs(x)))


def _rms_scale(x):
    return lax.rsqrt(jnp.mean(x * x, axis=-1, keepdims=True) + EPS)


def _regroup_w_in(w):
    o_mla, o_ssd = 0, MLA_COLS
    o_na, o_gdn = o_ssd + SSD_COLS, o_ssd + SSD_COLS + NA_COLS
    main_segs = [
        (o_ssd + SSD_D_INNER, SSD_CONV_DIM),
        (o_ssd, SSD_D_INNER),
        (o_na, NA_COLS),
        (o_gdn, 4 * GDN_DIM),
        (o_mla + MLA_Q_LORA, MLA_KV_LORA),
        (o_mla, MLA_Q_LORA),
        (o_mla + MLA_Q_LORA + MLA_KV_LORA, MLA_ROPE),
    ]
    small_segs = [(o_ssd + SSD_D_INNER + SSD_CONV_DIM, 2 * SSD_HEADS), (o_gdn + 4 * GDN_DIM, 4 * GDN_HEADS)]
    assert sum(n for _, n in main_segs) == P_MAIN
    main = jnp.concatenate([w[:, a:a + n] for a, n in main_segs], axis=1).astype(BF16)
    n_small = sum(n for _, n in small_segs)
    small = jnp.concatenate([w[:, a:a + n] for a, n in small_segs]
                            + [jnp.zeros((w.shape[0], P_SMALL - n_small), w.dtype)], axis=1).astype(BF16)
    return main, small


def _ada_kernel(c_ref, w_ref, b_ref, o_ref):
    s = _silu(c_ref[...]).astype(BF16)
    o_ref[...] = _dot(s, w_ref[...].astype(BF16)) + b_ref[...]


def _ada_call(cvec, w_ada, b_ada):
    depth, d, n = w_ada.shape
    tn = 1024
    return pl.pallas_call(
        _ada_kernel,
        grid=(depth, n // tn),
        in_specs=[pl.BlockSpec((8, d), lambda l, j: (0, 0)),
                  pl.BlockSpec((None, d, tn), lambda l, j: (l, 0, j)),
                  pl.BlockSpec((None, 1, tn), lambda l, j: (l, 0, j))],
        out_specs=pl.BlockSpec((None, 8, tn), lambda l, j: (l, 0, j)),
        out_shape=jax.ShapeDtypeStruct((depth, 8, n), F32),
        compiler_params=_cp(("parallel", "parallel"), VMEM_STREAM_MB),
        name="adaln",
    )(cvec, w_ada, b_ada.reshape(depth, 1, n))


def _inproj_kernel(h_ref, mod_ref, g_ref, w_ref, ws_ref, o_ref, os_ref, u_scr):
    @pl.when(pl.program_id(1) == 0)
    def _():
        x = h_ref[...]
        y = x * _rms_scale(x) * g_ref[...]
        u = (y * (1.0 + mod_ref[1:2, :]) + mod_ref[0:1, :]).astype(BF16)
        u_scr[...] = u
        os_ref[...] = _dot(u, ws_ref[...])

    o_ref[...] = _dot(u_scr[...], w_ref[...]).astype(BF16)


def _inproj_call(h, mod_l, g_pre, w_main, w_small, *, n_b, lx, tm):
    rows, d = h.shape
    tn = _pick_tile(P_MAIN, (1920, 640, 128))
    nxt, per_b = n_b * lx // tm, lx // tm

    def mod_idx(i, j):
        return (jnp.where(i < nxt, i // per_b, n_b), 0, 0)

    return pl.pallas_call(
        _inproj_kernel,
        grid=(rows // tm, P_MAIN // tn),
        in_specs=[pl.BlockSpec((tm, d), lambda i, j: (i, 0)),
                  pl.BlockSpec((None, 6, d), mod_idx),
                  pl.BlockSpec((1, d), lambda i, j: (0, 0)),
                  pl.BlockSpec((d, tn), lambda i, j: (0, j)),
                  pl.BlockSpec((d, P_SMALL), lambda i, j: (0, 0))],
        out_specs=[pl.BlockSpec((tm, tn), lambda i, j: (i, j)),
                   pl.BlockSpec((tm, P_SMALL), lambda i, j: (i, 0))],
        out_shape=[jax.ShapeDtypeStruct((rows, P_MAIN), BF16),
                   jax.ShapeDtypeStruct((rows, P_SMALL), F32)],
        scratch_shapes=[pltpu.VMEM((tm, d), BF16)],
        compiler_params=_cp(("parallel", "arbitrary"), VMEM_WEIGHTS_MB),
        name="in_proj",
    )(h, mod_l, g_pre.reshape(1, d), w_main, w_small)


HALO = 16


def _conv_kernel(prev_ref, cur_ref, next_ref, w_ref, b_ref, o_ref, ext_scr, *, tr, blocks_x, seq_x, seq_c):
    i = pl.program_id(0)
    in_x = i < blocks_x
    pos = jnp.where(in_x, i % seq_x, (i - blocks_x) % seq_c)
    last_pos = jnp.where(in_x, seq_x - 1, seq_c - 1)
    ext_scr[0:HALO, :] = jnp.where(pos == 0, 0.0, prev_ref[...].astype(F32))
    ext_scr[HALO:HALO + tr, :] = cur_ref[...].astype(F32)
    ext_scr[HALO + tr:2 * HALO + tr, :] = jnp.where(pos == last_pos, 0.0, next_ref[...].astype(F32))
    ext = ext_scr[...]
    n_ext = ext.shape[0]
    acc = b_ref[...]
    for k in range(CONV_W):
        sh = (CONV_W // 2 - k) % n_ext
        tap = ext if sh == 0 else pltpu.roll(ext, sh, 0)
        acc = acc + w_ref[k:k + 1, :] * tap[HALO:HALO + tr, :]
    o_ref[...] = _silu(acc).astype(BF16)


def _conv_call(proj, w, b, *, col_off, n_b, lx, lc, tr):
    rows = proj.shape[0]
    c = w.shape[1]
    cb = col_off // c
    assert cb * c == col_off
    hb = tr // HALO
    n_halo = rows // HALO
    kern = functools.partial(_conv_kernel, tr=tr, blocks_x=n_b * lx // tr, seq_x=lx // tr, seq_c=lc // tr)
    return pl.pallas_call(
        kern,
        grid=(rows // tr,),
        in_specs=[pl.BlockSpec((HALO, c), lambda i: (jnp.maximum(i * hb - 1, 0), cb)),
                  pl.BlockSpec((tr, c), lambda i: (i, cb)),
                  pl.BlockSpec((HALO, c), lambda i: (jnp.minimum((i + 1) * hb, n_halo - 1), cb)),
                  pl.BlockSpec((CONV_W, c), lambda i: (0, 0)),
                  pl.BlockSpec((1, c), lambda i: (0, 0))],
        out_specs=pl.BlockSpec((tr, c), lambda i: (i, 0)),
        out_shape=jax.ShapeDtypeStruct((rows, c), BF16),
        scratch_shapes=[pltpu.VMEM((tr + 2 * HALO, c), F32)],
        compiler_params=_cp(("parallel",), VMEM_STREAM_MB),
        name="dwconv_silu",
    )(proj, proj, proj, w, b.reshape(1, c))


def _mla_prep_kernel(p_ref, qn_ref, wq_ref, wqs_ref, kvn_ref, wkv_ref, cos_ref, sin_ref, perm_ref,
                     q_ref, k_ref, v_ref):
    p = p_ref[...].astype(F32)
    ckv = p[:, 0:MLA_KV_LORA]
    ce = p[:, MLA_KV_LORA:]
    lane = lax.broadcasted_iota(jnp.int32, ce.shape, 1)
    ssq = jnp.sum(jnp.where(lane < MLA_Q_LORA, ce * ce, 0.0), axis=-1, keepdims=True)
    cqn = (ce * lax.rsqrt(ssq / MLA_Q_LORA + EPS) * qn_ref[...]).astype(BF16)
    ckvn = (ckv * _rms_scale(ckv) * kvn_ref[...]).astype(BF16)
    q = _dot(cqn, wq_ref[...])
    qs = _dot(cqn, wqs_ref[...])
    kv = _dot(ckvn, wkv_ref[...])
    cos, sin = cos_ref[...], sin_ref[...]
    kr = p_ref[:, MLA_KV_LORA + MLA_Q_LORA:]
    kr_rot = kr.astype(F32) * cos + _dot(kr, perm_ref[...]) * sin
    nr = MLA_HEADS * MLA_NOPE
    scale = (MLA_NOPE + MLA_ROPE) ** -0.5
    ones_col = jnp.where(lax.broadcasted_iota(jnp.int32, (p.shape[0], MLA_V), 1) == 0, 1.0, 0.0).astype(BF16)
    for h in range(MLA_HEADS):
        q_ref[h, :, 0:MLA_NOPE] = (q[:, h * MLA_NOPE:(h + 1) * MLA_NOPE] * scale).astype(BF16)
        qr = q[:, nr + h * MLA_ROPE:nr + (h + 1) * MLA_ROPE] * cos + qs[:, h * MLA_ROPE:(h + 1) * MLA_ROPE] * sin
        q_ref[h, :, MLA_NOPE:] = (qr * scale).astype(BF16)
        hv = h * (MLA_NOPE + MLA_V)
        k_ref[h, :, 0:MLA_NOPE] = kv[:, hv:hv + MLA_NOPE].astype(BF16)
        k_ref[h, :, MLA_NOPE:] = kr_rot.astype(BF16)
        v_ref[h, :, 0:MLA_V] = kv[:, hv + MLA_NOPE:hv + MLA_NOPE + MLA_V].astype(BF16)
        v_ref[h, :, MLA_V:] = ones_col


def _mla_prep_call(proj, qn_ext, wq, wqs, kvn, wkv, cos_t, sin_t, perm, *, n_b, lx, tm):
    rows = proj.shape[0]
    nxt, per_b = n_b * lx // tm, lx // tm
    dk = MLA_NOPE + MLA_ROPE
    rope_idx = lambda i: (jnp.where(i < nxt, i % per_b, per_b), 0)
    full = lambda a: pl.BlockSpec(a.shape, lambda i: (0,) * a.ndim)
    return pl.pallas_call(
        _mla_prep_kernel,
        grid=(rows // tm,),
        in_specs=[pl.BlockSpec((tm, MLA_COLS), lambda i: (i, P_MLA // MLA_COLS)),
                  full(qn_ext), full(wq), full(wqs), full(kvn), full(wkv),
                  pl.BlockSpec((tm, MLA_ROPE), rope_idx), pl.BlockSpec((tm, MLA_ROPE), rope_idx),
                  full(perm)],
        out_specs=[pl.BlockSpec((MLA_HEADS, tm, dk), lambda i: (0, i, 0)),
                   pl.BlockSpec((MLA_HEADS, tm, dk), lambda i: (0, i, 0)),
                   pl.BlockSpec((MLA_HEADS, tm, 2 * MLA_V), lambda i: (0, i, 0))],
        out_shape=[jax.ShapeDtypeStruct((MLA_HEADS, rows, dk), BF16),
                   jax.ShapeDtypeStruct((MLA_HEADS, rows, dk), BF16),
                   jax.ShapeDtypeStruct((MLA_HEADS, rows, 2 * MLA_V), BF16)],
        compiler_params=_cp(("parallel",), VMEM_STREAM_MB),
        name="mla_prep",
    )(proj, qn_ext, wq, wqs, kvn, wkv, cos_t, sin_t, perm)


def _softmax_pv(scores, values):
    m = functools.reduce(jnp.maximum, [jnp.max(s, axis=-1, keepdims=True) for s in scores])
    ps = [jnp.exp(s - m) for s in scores]
    den = functools.reduce(lambda a, b: a + b, [jnp.sum(p, axis=-1, keepdims=True) for p in ps])
    num = functools.reduce(lambda a, b: a + b, [_dot(p.astype(BF16), v) for p, v in zip(ps, values)])
    return num / den


def _softmax_pv_aug(scores, values_aug):
    m = functools.reduce(jnp.maximum, [jnp.max(s, axis=-1, keepdims=True) for s in scores])
    acc = functools.reduce(lambda a, b: a + b,
                           [_dot(jnp.exp((s - m).astype(BF16)), v) for s, v in zip(scores, values_aug)])
    return acc[:, 0:MLA_V] / acc[:, MLA_V:MLA_V + 1]


MLA_HEADS_PER_STEP = 4


def _mla_attn_kernel(q_ref, kx_ref, vx_ref, kc_ref, vc_ref, o_ref, *, nqx):
    qi = pl.program_id(2)
    heads = range(q_ref.shape[0])

    @pl.when(qi < nqx)
    def _():
        outs = [_softmax_pv_aug([_dot_nt(q_ref[h], kx_ref[h]), _dot_nt(q_ref[h], kc_ref[h])], [vx_ref[h], vc_ref[h]])
                for h in heads]
        o_ref[...] = jnp.concatenate(outs, axis=1).astype(BF16)

    @pl.when(qi >= nqx)
    def _():
        outs = [_softmax_pv_aug([_dot_nt(q_ref[h], kc_ref[h])], [vc_ref[h]]) for h in heads]
        o_ref[...] = jnp.concatenate(outs, axis=1).astype(BF16)


def _mla_attn_call(q, k, v, *, n_b, lx, lc, tq, need_ctx):
    rows = q.shape[1]
    dk = q.shape[2]
    nqx, nqc = lx // tq, lc // tq
    nq = nqx + (nqc if need_ctx else 0)
    nbx = n_b * lx // lc

    def q_row(b, qi):
        return jnp.where(qi < nqx, b * nqx + qi, n_b * nqx + b * nqc + (qi - nqx))

    kern = functools.partial(_mla_attn_kernel, nqx=nqx)
    out_rows = rows if need_ctx else n_b * lx
    hp = MLA_HEADS_PER_STEP
    return pl.pallas_call(
        kern,
        grid=(n_b, MLA_HEADS // hp, nq),
        in_specs=[pl.BlockSpec((hp, tq, dk), lambda b, h, qi: (h, q_row(b, qi), 0)),
                  pl.BlockSpec((hp, lx, dk), lambda b, h, qi: (h, b, 0)),
                  pl.BlockSpec((hp, lx, 2 * MLA_V), lambda b, h, qi: (h, b, 0)),
                  pl.BlockSpec((hp, lc, dk), lambda b, h, qi: (h, nbx + b, 0)),
                  pl.BlockSpec((hp, lc, 2 * MLA_V), lambda b, h, qi: (h, nbx + b, 0))],
        out_specs=pl.BlockSpec((tq, hp * MLA_V), lambda b, h, qi: (q_row(b, qi), h)),
        out_shape=jax.ShapeDtypeStruct((out_rows, MLA_HEADS * MLA_V), BF16),
        compiler_params=_cp(("parallel", "parallel", "arbitrary"), VMEM_ATTN_MB),
        name="mla_attn",
    )(q, k, v, k, v)


def _na_plan(g_rows, lc):
    wr = min(NA_WIN_ROWS, g_rows)
    rg = next(r for r in (4, 2, 1) if g_rows % r == 0 and lc % (r * GRID_W) == 0)
    wk = min(rg + wr - 1, g_rows)
    n_groups = g_rows // rg
    ks = np.clip(np.arange(n_groups) * rg - wr // 2, 0, g_rows - wk)
    r = np.arange(g_rows)
    rs = np.clip(r - wr // 2, 0, g_rows - wr)
    q_off = (r - np.repeat(ks, rg)).reshape(n_groups, rg)
    rel = (rs - np.repeat(ks, rg)).reshape(n_groups, rg)
    assert (rel >= 0).all() and (rel + wr <= wk).all()
    pats = [tuple(q_off[g]) + tuple(rel[g]) for g in range(n_groups)]
    uniq = sorted(set(pats))
    var = np.array([uniq.index(p) for p in pats], np.int32)
    q_off_v = np.array([p[:rg] for p in uniq])
    rel_v = np.array([p[rg:] for p in uniq])
    return wr, rg, wk, ks.astype(np.int32), var, q_off_v, rel_v


def _na_bias_table(rpb, g_rows, lc):
    wr, rg, wk, _, _, q_off_v, rel_v = _na_plan(g_rows, lc)
    col_start = np.clip(np.arange(GRID_W) - NA_WIN_COLS // 2, 0, GRID_W - NA_WIN_COLS)
    cc = np.arange(GRID_W)
    col_ok = (cc[None, :] >= col_start[:, None]) & (cc[None, :] < col_start[:, None] + NA_WIN_COLS)
    dc = np.clip(cc[None, :] - cc[:, None] + NA_WIN_COLS - 1, 0, 2 * NA_WIN_COLS - 2)
    n_dc = 2 * NA_WIN_COLS - 1
    nh = rpb.shape[0]
    onehot = (dc.reshape(-1)[:, None] == np.arange(n_dc)[None, :]).astype(np.float32)
    g = jnp.einsum('hab,yb->hay', rpb.astype(F32), jnp.asarray(onehot), precision=HIGHEST)
    g = jnp.where(col_ok[None, None], g.reshape(nh, -1, GRID_W, GRID_W), NEG)
    g = jnp.pad(g, ((0, 0), (wk, wk), (0, 0), (0, 0)), constant_values=NEG)
    w = np.arange(wk)
    tabs = []
    for v in range(q_off_v.shape[0]):
        rows_v = []
        for j in range(rg):
            a0 = wk - q_off_v[v, j] + NA_WIN_ROWS - 1
            own = (w >= rel_v[v, j]) & (w < rel_v[v, j] + wr)
            blk = jnp.where(jnp.asarray(own)[None, :, None, None], g[:, a0:a0 + wk], NEG)
            rows_v.append(jnp.transpose(blk, (0, 2, 1, 3)).reshape(nh, GRID_W, wk * GRID_W))
        tabs.append(jnp.concatenate(rows_v, axis=-2))
    return jnp.stack(tabs, axis=0)


def _na_kernel(var_ref, ks_ref, q_ref, kx_ref, vx_ref, kc_ref, vc_ref, bias_ref, o_ref, *, n_groups, wk, scale):
    g = pl.program_id(1)
    hd = NA_HEAD_DIM
    cols = [slice(h * hd, (h + 1) * hd) for h in range(NA_HEADS)]

    @pl.when(g < n_groups)
    def _():
        start = pl.multiple_of(ks_ref[g] * GRID_W, GRID_W)
        rows = pl.ds(start, wk * GRID_W)
        outs = []
        for h, c in enumerate(cols):
            q = q_ref[:, c]
            sl = _dot_nt(q, kx_ref[rows, c]) * scale + bias_ref[h]
            sc = _dot_nt(q, kc_ref[:, c]) * scale
            outs.append(_softmax_pv([sl, sc], [vx_ref[rows, c], vc_ref[:, c]]))
        o_ref[...] = jnp.concatenate(outs, axis=1).astype(BF16)

    @pl.when(g >= n_groups)
    def _():
        outs = [_softmax_pv([_dot_nt(q_ref[:, c], kc_ref[:, c]) * scale], [vc_ref[:, c]]) for c in cols]
        o_ref[...] = jnp.concatenate(outs, axis=1).astype(BF16)


def _na_call(proj, bias_tab, *, layer, n_b, lx, lc, need_ctx):
    rows = proj.shape[0]
    g_rows = lx // GRID_W
    _, rg, wk, ks, var, _, _ = _na_plan(g_rows, lc)
    n_groups = g_rows // rg
    tq = rg * GRID_W
    nqc = lc // tq
    nq = n_groups + (nqc if need_ctx else 0)
    nbx = n_b * lx // lc
    hd, nd = NA_HEAD_DIM, NA_DIM
    cq, ck, cv = P_NAQ // nd, P_NAK // nd, P_NAV // nd

    def q_row(b, g):
        return jnp.where(g < n_groups, b * n_groups + g, n_b * n_groups + b * nqc + (g - n_groups))

    kern = functools.partial(_na_kernel, n_groups=n_groups, wk=wk, scale=hd ** -0.5)
    grid_spec = pltpu.PrefetchScalarGridSpec(
        num_scalar_prefetch=2,
        grid=(n_b, nq),
        in_specs=[pl.BlockSpec((tq, nd), lambda b, g, var_r, ks_r: (q_row(b, g), cq)),
                  pl.BlockSpec((lx, nd), lambda b, g, var_r, ks_r: (b, ck)),
                  pl.BlockSpec((lx, nd), lambda b, g, var_r, ks_r: (b, cv)),
                  pl.BlockSpec((lc, nd), lambda b, g, var_r, ks_r: (nbx + b, ck)),
                  pl.BlockSpec((lc, nd), lambda b, g, var_r, ks_r: (nbx + b, cv)),
                  pl.BlockSpec((None, NA_HEADS, tq, wk * GRID_W),
                               lambda b, g, var_r, ks_r: (var_r[jnp.minimum(g, n_groups - 1)], layer, 0, 0))],
        out_specs=pl.BlockSpec((tq, nd), lambda b, g, var_r, ks_r: (q_row(b, g), 0)),
    )
    return pl.pallas_call(
        kern,
        grid_spec=grid_spec,
        out_shape=jax.ShapeDtypeStruct((rows if need_ctx else n_b * lx, NA_DIM), BF16),
        compiler_params=_cp(("parallel", "arbitrary"), VMEM_STREAM_MB),
        name="na_attn",
    )(jnp.asarray(var), jnp.asarray(ks), proj, proj, proj, proj, proj, bias_tab)


def _chunk_block(n_b, nxc, ncc):
    def f(b, d, c):
        cc = jnp.where(d == 0, c, ncc - 1 - c)
        cx = jnp.where(d == 0, c - ncc, nxc - 1 - (c - ncc))
        return jnp.where(c < ncc, n_b * nxc + b * ncc + cc, b * nxc + cx)
    return f


def _lane_vec(vals, offset):
    flat = vals.reshape(-1).astype(F32)
    return jnp.pad(flat, (offset, P_SMALL - offset - flat.shape[0])).reshape(1, P_SMALL)


def _ssd_prep_kernel(xbc_ref, sm_ref, alog_ref, dtb_ref, dsk_ref, yp_ref, xw_ref, ea_ref):
    nh, hp, ns = SSD_HEADS, SSD_HEAD_DIM, SSD_STATE
    gh = nh // SSD_GROUPS
    dt2 = _softplus(sm_ref[...] + dtb_ref[...])
    dta2 = dt2 * (-jnp.exp(alog_ref[...]))
    row = lax.broadcasted_iota(jnp.int32, (CHUNK, CHUNK), 0)
    col = lax.broadcasted_iota(jnp.int32, (CHUNK, CHUNK), 1)
    incl = [row >= col, row <= col]
    cs = [jnp.dot(m.astype(F32), dta2, preferred_element_type=F32, precision=HIGHEST) for m in incl]
    cs_t = [x.T for x in cs]
    tot2 = jnp.sum(dta2, axis=0, keepdims=True)
    dsk = dsk_ref[...]
    bo, co = SSD_D_INNER, SSD_D_INNER + SSD_GROUPS * ns
    scores = [_dot_nt(xbc_ref[:, co + g * ns:co + (g + 1) * ns], xbc_ref[:, bo + g * ns:bo + (g + 1) * ns])
              for g in range(SSD_GROUPS)]
    xs = [xbc_ref[:, h * hp:(h + 1) * hp].astype(F32) for h in range(nh)]
    yp, xw = [], []
    for d in range(2):
        for h in range(nh):
            ln = d * nh + h
            a_c, a_r = cs[d][:, ln:ln + 1], cs_t[d][ln:ln + 1, :]
            dec = jnp.exp(jnp.where(incl[d], a_c - a_r, NEG))
            m = (scores[h // gh] * dec).astype(BF16)
            y = _dot(m, (xs[h] * dt2[:, ln:ln + 1]).astype(BF16))
            yp.append(y + dsk[:, h:h + 1] * xs[h] if d == 0 else y)
            xw.append((xs[h] * (jnp.exp(tot2[:, ln:ln + 1] - a_c) * dt2[:, ln:ln + 1])).astype(BF16))
    yp_ref[...] = jnp.concatenate(yp, axis=1)
    xw_ref[...] = jnp.concatenate(xw, axis=1)
    ea_ref[...] = jnp.concatenate([jnp.exp(cs[0]), jnp.exp(cs[1])], axis=1)


def _ssd_scan_kernel(*refs):
    (bc0, yp0, xw0, ea0, sm0, bc1, yp1, xw1, ea1, sm1, alog_ref, dtb_ref, y0_ref, y1_ref, s_scr) = refs
    nh, hp, ns = SSD_HEADS, SSD_HEAD_DIM, SSD_STATE
    gh = nh // SSD_GROUPS

    @pl.when(pl.program_id(1) == 0)
    def _():
        s_scr[...] = jnp.zeros_like(s_scr)

    neg_a = -jnp.exp(alog_ref[...])
    for d, (bc, yp, xw, ea, sm, y_ref) in enumerate([(bc0, yp0, xw0, ea0, sm0, y0_ref),
                                                      (bc1, yp1, xw1, ea1, sm1, y1_ref)]):
        c_dec = jnp.exp(jnp.sum(_softplus(sm[...] + dtb_ref[...]) * neg_a, axis=0, keepdims=True))
        e_acum = ea[...]
        ys, states = [], []
        for g in range(SSD_GROUPS):
            bg = bc[:, g * ns:(g + 1) * ns]
            cg = bc[:, SSD_GROUPS * ns + g * ns:SSD_GROUPS * ns + (g + 1) * ns]
            s_g = s_scr[d, :, g * gh * hp:(g + 1) * gh * hp]
            y_int = _dot(cg, s_g.astype(BF16))
            upd = _dot_tn(bg, xw[:, g * gh * hp:(g + 1) * gh * hp])
            for hh in range(gh):
                h = g * gh + hh
                ln = d * nh + h
                sl = slice(hh * hp, (hh + 1) * hp)
                ys.append(yp[:, h * hp:(h + 1) * hp] + y_int[:, sl] * e_acum[:, ln:ln + 1])
                states.append(s_g[:, sl] * c_dec[:, ln:ln + 1] + upd[:, sl])
        y_ref[...] = jnp.concatenate(ys, axis=1)
        s_scr[d] = jnp.concatenate(states, axis=1)


def _ssd_call(xbc, small, a_log, dt_bias, d_skip, *, n_b, lx, lc):
    rows = xbc.shape[0]
    nxc, ncc = lx // CHUNK, lc // CHUNK
    blk = _chunk_block(n_b, nxc, ncc)
    nh, di = SSD_HEADS, SSD_D_INNER
    alog_v, dtb_v = _lane_vec(a_log, 0), _lane_vec(dt_bias, 0)
    ypart, xw, ea = pl.pallas_call(
        _ssd_prep_kernel,
        grid=(rows // CHUNK,),
        in_specs=[pl.BlockSpec((CHUNK, SSD_CONV_DIM), lambda i: (i, 0)),
                  pl.BlockSpec((CHUNK, P_SMALL), lambda i: (i, 0)),
                  pl.BlockSpec((1, P_SMALL), lambda i: (0, 0)),
                  pl.BlockSpec((1, P_SMALL), lambda i: (0, 0)),
                  pl.BlockSpec((1, nh), lambda i: (0, 0))],
        out_specs=[pl.BlockSpec((CHUNK, 2 * di), lambda i: (i, 0)),
                   pl.BlockSpec((CHUNK, 2 * di), lambda i: (i, 0)),
                   pl.BlockSpec((CHUNK, 2 * P_SMALL), lambda i: (i, 0))],
        out_shape=[jax.ShapeDtypeStruct((rows, 2 * di), F32),
                   jax.ShapeDtypeStruct((rows, 2 * di), BF16),
                   jax.ShapeDtypeStruct((rows, 2 * P_SMALL), F32)],
        compiler_params=_cp(("parallel",), VMEM_STREAM_MB),
        name="ssd_prep",
    )(xbc, small, alog_v, dtb_v, d_skip.reshape(1, nh))

    def dir_specs(d):
        at = lambda b, c: blk(b, d, c)
        return [pl.BlockSpec((CHUNK, SSD_CONV_DIM - di), lambda b, c: (at(b, c), 1)),
                pl.BlockSpec((CHUNK, di), lambda b, c: (at(b, c), d)),
                pl.BlockSpec((CHUNK, di), lambda b, c: (at(b, c), d)),
                pl.BlockSpec((CHUNK, P_SMALL), lambda b, c: (at(b, c), d)),
                pl.BlockSpec((CHUNK, P_SMALL), lambda b, c: (at(b, c), 0))]

    const = pl.BlockSpec((1, P_SMALL), lambda b, c: (0, 0))
    return pl.pallas_call(
        _ssd_scan_kernel,
        grid=(n_b, ncc + nxc),
        in_specs=dir_specs(0) + dir_specs(1) + [const, const],
        out_specs=[pl.BlockSpec((CHUNK, di), lambda b, c: (blk(b, 0, c), 0)),
                   pl.BlockSpec((CHUNK, di), lambda b, c: (blk(b, 1, c), 0))],
        out_shape=[jax.ShapeDtypeStruct((rows, di), F32), jax.ShapeDtypeStruct((rows, di), F32)],
        scratch_shapes=[pltpu.VMEM((2, SSD_STATE, di), F32)],
        compiler_params=_cp(("parallel", "arbitrary"), VMEM_STREAM_MB),
        name="ssd_scan",
    )(xbc, ypart, xw, ea, small, xbc, ypart, xw, ea, small, alog_v, dtb_v)


SOLVE_BLOCK = 16


def _unit_tri_solve_many(n_mats, rhss):
    ln = n_mats[0].shape[0]
    row = lax.broadcasted_iota(jnp.int32, (ln, ln), 0)
    col = lax.broadcasted_iota(jnp.int32, (ln, ln), 1)
    same_block = lambda s: (row // s) == (col // s)
    eye = jnp.where(row == col, 1.0, 0.0)
    mm = lambda a, b: _dot(a.astype(BF16), b.astype(BF16))
    ms = [jnp.where(same_block(SOLVE_BLOCK), -n, 0.0) for n in n_mats]
    xs = [eye + m for m in ms]
    mps = ms
    k = 1
    while 2 * k < SOLVE_BLOCK:
        mps = [mm(m, m) for m in mps]
        xs = [x + mm(x, m) for x, m in zip(xs, mps)]
        k *= 2
    s = SOLVE_BLOCK
    while s < ln:
        joins = [jnp.where(same_block(2 * s), jnp.where(same_block(s), 0.0, n), 0.0) for n in n_mats]
        xs = [x - mm(mm(x, e), x) for x, e in zip(xs, joins)]
        s *= 2
    return [mm(x, r) for x, r in zip(xs, rhss)]


GDN_PACK = 5 * GDN_DIM
GDN_G_LANE = 2 * SSD_HEADS
GDN_B_LANE = 2 * SSD_HEADS + 2 * GDN_HEADS


def _gdn_prep_kernel(qkv_ref, sm_ref, alog_ref, dtb_ref, o_ref):
    nh, hd = GDN_HEADS, GDN_HEAD_DIM
    sm = sm_ref[...]
    g2 = -jnp.exp(alog_ref[...]) * _softplus(sm + dtb_ref[...])
    beta2 = _sigmoid(sm)
    row = lax.broadcasted_iota(jnp.int32, (CHUNK, CHUNK), 0)
    col = lax.broadcasted_iota(jnp.int32, (CHUNK, CHUNK), 1)
    incl = [row >= col, row <= col]
    strict = [row > col, row < col]
    cs = [jnp.dot(m.astype(F32), g2, preferred_element_type=F32, precision=HIGHEST) for m in incl]
    cs_t = [x.T for x in cs]
    gtot2 = jnp.sum(g2, axis=0, keepdims=True)
    qn, kn, kn_b, vv, qk_raw = [], [], [], [], []
    for h in range(nh):
        qh = qkv_ref[:, h * hd:(h + 1) * hd].astype(F32)
        kh = qkv_ref[:, GDN_DIM + h * hd:GDN_DIM + (h + 1) * hd].astype(F32)
        vv.append(qkv_ref[:, 2 * GDN_DIM + h * hd:2 * GDN_DIM + (h + 1) * hd].astype(F32))
        qn.append(qh * (lax.rsqrt(jnp.sum(qh * qh, axis=-1, keepdims=True) + EPS) * hd ** -0.5))
        kn.append(kh * lax.rsqrt(jnp.sum(kh * kh, axis=-1, keepdims=True) + EPS))
        kn_b.append(kn[h].astype(BF16))
        qk_raw.append(_dot_nt(qn[h].astype(BF16), kn_b[h]))
    n_mats, rhss, qks, qds, kds = [], [], [], [], []
    for d in range(2):
        for h in range(nh):
            lg, lb = GDN_G_LANE + d * nh + h, GDN_B_LANE + d * nh + h
            gcc, gcr = cs[d][:, lg:lg + 1], cs_t[d][lg:lg + 1, :]
            beta, gtot = beta2[:, lb:lb + 1], gtot2[:, lg:lg + 1]
            dec = jnp.exp(jnp.where(incl[d], gcc - gcr, NEG))
            kb = kn[h] * beta
            n_mats.append(jnp.where(strict[d], _dot_nt(kb.astype(BF16), kn_b[h]) * dec, 0.0))
            e_gc = jnp.exp(gcc)
            rhss.append(jnp.concatenate([vv[h] * beta, kb * e_gc], axis=1))
            qks.append(qk_raw[h] * dec)
            qds.append(qn[h] * e_gc)
            kds.append(kn[h] * jnp.exp(gtot - gcc))
    sols = _unit_tri_solve_many(n_mats, rhss)
    pieces = []
    for d in range(2):
        js = range(d * nh, (d + 1) * nh)
        pieces += [sols[j][:, 0:hd] for j in js] + [sols[j][:, hd:2 * hd] for j in js]
        pieces += [qks[j] for j in js] + [qds[j] for j in js] + [kds[j] for j in js]
    o_ref[...] = jnp.concatenate([p.astype(BF16) for p in pieces], axis=1)


def _gdn_scan_kernel(pk0, sm0, pk1, sm1, alog_ref, dtb_ref, o0_ref, o1_ref, s_scr):
    nh, hd = GDN_HEADS, GDN_HEAD_DIM

    @pl.when(pl.program_id(1) == 0)
    def _():
        s_scr[...] = jnp.zeros_like(s_scr)

    neg_a = -jnp.exp(alog_ref[...])
    for d, (pk_ref, sm_ref, o_ref) in enumerate([(pk0, sm0, o0_ref), (pk1, sm1, o1_ref)]):
        g_end2 = jnp.exp(jnp.sum(neg_a * _softplus(sm_ref[...] + dtb_ref[...]), axis=0, keepdims=True))
        outs, states = [], []
        for h in range(nh):
            lg = GDN_G_LANE + d * nh + h
            part = lambda j: pk_ref[:, (j * nh + h) * hd:(j * nh + h + 1) * hd]
            u, w, qk, qd, kd = part(0), part(1), part(2), part(3), part(4)
            s_h = s_scr[d, :, h * hd:(h + 1) * hd]
            s_b = s_h.astype(BF16)
            v_new = (u.astype(F32) - _dot(w, s_b)).astype(BF16)
            outs.append(_dot(qd, s_b) + _dot(qk, v_new))
            states.append(s_h * g_end2[:, lg:lg + 1] + _dot_tn(kd, v_new))
        o_ref[...] = jnp.concatenate(outs, axis=1)
        s_scr[d] = jnp.concatenate(states, axis=1)


def _gdn_call(qkv, small, a_log, dt_bias, *, n_b, lx, lc):
    rows = qkv.shape[0]
    nxc, ncc = lx // CHUNK, lc // CHUNK
    blk = _chunk_block(n_b, nxc, ncc)
    alog_v, dtb_v = _lane_vec(a_log, GDN_G_LANE), _lane_vec(dt_bias, GDN_G_LANE)
    packed = pl.pallas_call(
        _gdn_prep_kernel,
        grid=(rows // CHUNK,),
        in_specs=[pl.BlockSpec((CHUNK, 3 * GDN_DIM), lambda i: (i, 0)),
                  pl.BlockSpec((CHUNK, P_SMALL), lambda i: (i, 0)),
                  pl.BlockSpec((1, P_SMALL), lambda i: (0, 0)),
                  pl.BlockSpec((1, P_SMALL), lambda i: (0, 0))],
        out_specs=pl.BlockSpec((CHUNK, 2 * GDN_PACK), lambda i: (i, 0)),
        out_shape=jax.ShapeDtypeStruct((rows, 2 * GDN_PACK), BF16),
        compiler_params=_cp(("parallel",), VMEM_STREAM_MB),
        name="gdn_prep",
    )(qkv, small, alog_v, dtb_v)
    def dir_specs(d):
        return [pl.BlockSpec((CHUNK, GDN_PACK), lambda b, c: (blk(b, d, c), d)),
                pl.BlockSpec((CHUNK, P_SMALL), lambda b, c: (blk(b, d, c), 0))]

    const = pl.BlockSpec((1, P_SMALL), lambda b, c: (0, 0))
    return pl.pallas_call(
        _gdn_scan_kernel,
        grid=(n_b, ncc + nxc),
        in_specs=dir_specs(0) + dir_specs(1) + [const, const],
        out_specs=[pl.BlockSpec((CHUNK, GDN_DIM), lambda b, c: (blk(b, 0, c), 0)),
                   pl.BlockSpec((CHUNK, GDN_DIM), lambda b, c: (blk(b, 1, c), 0))],
        out_shape=[jax.ShapeDtypeStruct((rows, GDN_DIM), F32), jax.ShapeDtypeStruct((rows, GDN_DIM), F32)],
        scratch_shapes=[pltpu.VMEM((2, GDN_HEAD_DIM, GDN_DIM), F32)],
        compiler_params=_cp(("parallel", "arbitrary"), VMEM_STREAM_MB),
        name="gdn_scan",
    )(packed, small, packed, small, alog_v, dtb_v)


MIXOUT_SUB_ROWS = 256


def _mixout_kernel(h_ref, mod_ref, gpost_ref, ya_ref, ys0_ref, ys1_ref, zs_ref, sn_ref, yn_ref, og0_ref, og1_ref,
                   zg_ref, gn_ref, w_ref, o_ref):
    hd = GDN_HEAD_DIM
    tm = h_ref.shape[0]
    sub = min(tm, MIXOUT_SUB_ROWS)
    outs = []
    for s in range(0, tm, sub):
        r = slice(s, s + sub)
        ssd = (ys0_ref[r, :] + ys1_ref[r, :]) * _silu(zs_ref[r, :].astype(F32))
        yb = (ssd * _rms_scale(ssd) * sn_ref[...]).astype(BF16)
        gd = og0_ref[r, :] + og1_ref[r, :]
        zg = _silu(zg_ref[r, :].astype(F32))
        yd = []
        for h in range(GDN_HEADS):
            oh = gd[:, h * hd:(h + 1) * hd]
            yd.append((oh * _rms_scale(oh) * gn_ref[...] * zg[:, h * hd:(h + 1) * hd]).astype(BF16))
        parts = [ya_ref[r, :], yb, yn_ref[r, :]] + yd
        widths = [512, 512, 512] + [hd] * GDN_HEADS
        y = None
        off = 0
        for part, wd in zip(parts, widths):
            t = _dot(part, w_ref[off:off + wd, :])
            y = t if y is None else y + t
            off += wd
        outs.append(h_ref[r, :] + mod_ref[2:3, :] * (y * _rms_scale(y) * gpost_ref[...]))
    o_ref[...] = jnp.concatenate(outs, axis=0)


def _mixout_call(h, mod_l, g_post, ya, ys, proj, ssd_norm, yn, og, gdn_norm, w_out, *, n_b, lx, n_rows, tm):
    d = h.shape[1]
    nxt, per_b = n_b * lx // tm, lx // tm
    mod_idx = lambda i: (jnp.where(i < nxt, i // per_b, n_b), 0, 0)
    row = lambda i: (i, 0)
    const = lambda i: (0, 0)
    return pl.pallas_call(
        _mixout_kernel,
        grid=(n_rows // tm,),
        in_specs=[pl.BlockSpec((tm, d), row),
                  pl.BlockSpec((None, 6, d), mod_idx),
                  pl.BlockSpec((1, d), const),
                  pl.BlockSpec((tm, 512), row),
                  pl.BlockSpec((tm, 512), row),
                  pl.BlockSpec((tm, 512), row),
                  pl.BlockSpec((tm, 512), lambda i: (i, P_ZSSD // 512)),
                  pl.BlockSpec((1, 512), const),
                  pl.BlockSpec((tm, 512), row),
                  pl.BlockSpec((tm, 512), row),
                  pl.BlockSpec((tm, 512), row),
                  pl.BlockSpec((tm, 512), lambda i: (i, P_GZ // 512)),
                  pl.BlockSpec((1, GDN_HEAD_DIM), const),
                  pl.BlockSpec((d, d), const)],
        out_specs=pl.BlockSpec((tm, d), row),
        out_shape=jax.ShapeDtypeStruct((n_rows, d), F32),
        compiler_params=_cp(("parallel",), VMEM_WEIGHTS_MB),
        name="mix_out",
    )(h, mod_l, g_post.reshape(1, d), ya, ys[0], ys[1], proj, ssd_norm.reshape(1, 512), yn, og[0], og[1], proj,
      gdn_norm.reshape(1, GDN_HEAD_DIM), w_out)


def _ffn_kernel(h_ref, mod_ref, gpre_ref, gpost_ref, wg_ref, wu_ref, wd_ref, o_ref, u_scr, acc_scr):
    j = pl.program_id(1)

    @pl.when(j == 0)
    def _():
        x = h_ref[...]
        y = x * _rms_scale(x) * gpre_ref[...]
        u_scr[...] = (y * (1.0 + mod_ref[4:5, :]) + mod_ref[3:4, :]).astype(BF16)
        acc_scr[...] = jnp.zeros_like(acc_scr)

    u = u_scr[...]
    mid = (_silu(_dot(u, wg_ref[...])) * _dot(u, wu_ref[...])).astype(BF16)
    acc_scr[...] += _dot(mid, wd_ref[...])

    @pl.when(j == pl.num_programs(1) - 1)
    def _():
        y = acc_scr[...]
        o_ref[...] = h_ref[...] + mod_ref[5:6, :] * (y * _rms_scale(y) * gpost_ref[...])


def _ffn_call(h, mod_l, g_pre, g_post, wg, wu, wd, *, n_b, lx, tm, tf):
    rows, d = h.shape
    ff = wg.shape[1]
    nxt, per_b = n_b * lx // tm, lx // tm
    mod_idx = lambda i, j: (jnp.where(i < nxt, i // per_b, n_b), 0, 0)
    return pl.pallas_call(
        _ffn_kernel,
        grid=(rows // tm, ff // tf),
        in_specs=[pl.BlockSpec((tm, d), lambda i, j: (i, 0)),
                  pl.BlockSpec((None, 6, d), mod_idx),
                  pl.BlockSpec((1, d), lambda i, j: (0, 0)),
                  pl.BlockSpec((1, d), lambda i, j: (0, 0)),
                  pl.BlockSpec((d, tf), lambda i, j: (0, j)),
                  pl.BlockSpec((d, tf), lambda i, j: (0, j)),
                  pl.BlockSpec((tf, d), lambda i, j: (j, 0))],
        out_specs=pl.BlockSpec((tm, d), lambda i, j: (i, 0)),
        out_shape=jax.ShapeDtypeStruct((rows, d), F32),
        scratch_shapes=[pltpu.VMEM((tm, d), BF16), pltpu.VMEM((tm, d), F32)],
        compiler_params=_cp(("parallel", "arbitrary"), VMEM_WEIGHTS_MB),
        name="ffn_swiglu",
    )(h, mod_l, g_pre.reshape(1, d), g_post.reshape(1, d), wg, wu, wd)


MOE_ALIGN = 16
MOE_GROUP_TILE = 512


def _moe_local_rows(ts):
    return -(-(TOP_K * ts + N_EXPERTS * MOE_ALIGN) // 128) * 128


def _moe_route_kernel(h_ref, mod_ref, gpre_ref, wr_ref, u_ref, dest_ref, gate_ref, cnt_ref):
    ts, lanes = dest_ref.shape
    x = h_ref[...]
    y = x * _rms_scale(x) * gpre_ref[...]
    u = y * (1.0 + mod_ref[4:5, :]) + mod_ref[3:4, :]
    u_hi = u.astype(BF16)
    u_ref[...] = u_hi
    u_lo = (u - u_hi.astype(F32)).astype(BF16)
    logits = _dot(u_hi, wr_ref[0]) + (_dot(u_hi, wr_ref[1]) + _dot(u_lo, wr_ref[0]))
    lane = lax.broadcasted_iota(jnp.int32, logits.shape, 1).astype(F32)
    lg = jnp.where(lane < N_EXPERTS, logits, NEG)
    m1 = jnp.max(lg, axis=-1, keepdims=True)
    i1 = jnp.min(jnp.where(lg == m1, lane, float(lanes)), axis=-1, keepdims=True)
    lg2 = jnp.where(lane == i1, NEG, lg)
    m2 = jnp.max(lg2, axis=-1, keepdims=True)
    i2 = jnp.min(jnp.where(lg2 == m2, lane, float(lanes)), axis=-1, keepdims=True)
    e2 = jnp.exp(m2 - m1)
    gate_ref[...] = jnp.where(lane == 0.0, 1.0 / (1.0 + e2), jnp.where(lane == 1.0, e2 / (1.0 + e2), 0.0))
    sel = jnp.where(lane == i1, 1.0, 0.0) + jnp.where(lane == i2, 1.0, 0.0)
    cnt = jnp.sum(sel, axis=0, keepdims=True)
    cnt_al = jnp.floor((cnt + (MOE_ALIGN - 1)) / MOE_ALIGN) * MOE_ALIGN
    cnt_ref[...] = jnp.broadcast_to(cnt_al, cnt_ref.shape)
    before = lax.broadcasted_iota(jnp.int32, (lanes, lanes), 0) < lax.broadcasted_iota(jnp.int32, (lanes, lanes), 1)
    seg_lo = jnp.dot(jnp.broadcast_to(cnt_al, (8, lanes)), before.astype(F32),
                     preferred_element_type=F32, precision=HIGHEST)[0:1]
    earlier = lax.broadcasted_iota(jnp.int32, (ts, ts), 1) < lax.broadcasted_iota(jnp.int32, (ts, ts), 0)
    rank = _dot(earlier.astype(BF16), sel.astype(BF16))
    slot = seg_lo + rank
    slot_1 = jnp.sum(jnp.where(lane == i1, slot, 0.0), axis=-1, keepdims=True)
    slot_2 = jnp.sum(jnp.where(lane == i2, slot, 0.0), axis=-1, keepdims=True)
    dest_ref[...] = jnp.where(lane == 0.0, slot_1, jnp.where(lane == 1.0, slot_2, -1.0))


MOE_BIG = 4


def _moe_copy(hbm_ref, loc_ref, sems, far_row, loc_row, size_idx, *, to_hbm):
    rows = (MOE_BIG * MOE_ALIGN, MOE_ALIGN)[size_idx]
    loc = loc_ref.at[pl.ds(pl.multiple_of(loc_row, MOE_ALIGN), rows), :]
    far = hbm_ref.at[pl.ds(pl.multiple_of(far_row, MOE_ALIGN), rows), :]
    src, dst = (loc, far) if to_hbm else (far, loc)
    return pltpu.make_async_copy(src, dst, sems.at[size_idx])


def _moe_segment_copies(i, off_ref, n_ref, hbm_ref, loc_ref, sems, *, to_hbm):
    lo = jnp.int32(0)
    n_big_all, n_small_all = jnp.int32(0), jnp.int32(0)
    for e in range(N_EXPERTS):
        n_chunks = n_ref[i * N_EXPERTS + e]
        go = off_ref[i * N_EXPERTS + e]
        n_big = n_chunks // MOE_BIG
        n_small = n_chunks - n_big * MOE_BIG
        done = n_big * (MOE_BIG * MOE_ALIGN)

        def big(k, carry, lo=lo, go=go):
            step = k * (MOE_BIG * MOE_ALIGN)
            _moe_copy(hbm_ref, loc_ref, sems, go + step, lo + step, 0, to_hbm=to_hbm).start()
            return carry

        def small(k, carry, lo=lo, go=go, done=done):
            step = done + k * MOE_ALIGN
            _moe_copy(hbm_ref, loc_ref, sems, go + step, lo + step, 1, to_hbm=to_hbm).start()
            return carry

        lax.fori_loop(0, n_big, big, 0)
        lax.fori_loop(0, n_small, small, 0)
        lo = lo + n_chunks * MOE_ALIGN
        n_big_all, n_small_all = n_big_all + n_big, n_small_all + n_small
    return lo, n_big_all, n_small_all


def _moe_wait_copies(n_big, n_small, hbm_ref, loc_ref, sems, *, to_hbm):
    for size_idx, n in ((0, n_big), (1, n_small)):
        def body(k, carry, size_idx=size_idx):
            _moe_copy(hbm_ref, loc_ref, sems, 0, 0, size_idx, to_hbm=to_hbm).wait()
            return carry

        lax.fori_loop(0, n, body, 0)


def _moe_segment_counts(i, n_ref):
    n_chunks = [n_ref[i * N_EXPERTS + e] for e in range(N_EXPERTS)]
    n_big = sum(n // MOE_BIG for n in n_chunks)
    return sum(n_chunks) * MOE_ALIGN, n_big, sum(n_chunks) - n_big * MOE_BIG


def _moe_sort_kernel(off_ref, n_ref, u_ref, dest_ref, init_ref, us_ref, loc_scr, sems):
    del init_ref
    i = pl.program_id(0)
    cur = lax.rem(i, 2)
    dest = dest_ref[...]
    slot = lax.broadcasted_iota(jnp.int32, (dest.shape[0], loc_scr.shape[1]), 1).astype(F32)
    onehot = jnp.where(slot == dest[:, 0:1], 1.0, jnp.where(slot == dest[:, 1:2], 1.0, 0.0))
    loc_scr[cur] = _dot_tn(onehot.astype(BF16), u_ref[...]).astype(BF16)
    _, n_big, n_small = _moe_segment_copies(i, off_ref, n_ref, us_ref, loc_scr.at[cur], sems.at[cur], to_hbm=True)

    @pl.when(i > 0)
    def _():
        _, p_big, p_small = _moe_segment_counts(i - 1, n_ref)
        _moe_wait_copies(p_big, p_small, us_ref, loc_scr.at[1 - cur], sems.at[1 - cur], to_hbm=True)

    @pl.when(i == pl.num_programs(0) - 1)
    def _():
        _moe_wait_copies(n_big, n_small, us_ref, loc_scr.at[cur], sems.at[cur], to_hbm=True)


def _moe_expert_kernel(te_ref, tv_ref, u_ref, wg_ref, wu_ref, wd_ref, y_ref, acc_scr):
    k, j = pl.program_id(0), pl.program_id(1)
    last = pl.num_programs(1) - 1
    valid = tv_ref[k] > 0

    @pl.when(valid)
    def _():
        @pl.when(j == 0)
        def _():
            acc_scr[...] = jnp.zeros_like(acc_scr)

        u = u_ref[...]
        mid = (_silu(_dot(u, wg_ref[...])) * _dot(u, wu_ref[...])).astype(BF16)
        acc_scr[...] += _dot(mid, wd_ref[...])

        @pl.when(j == last)
        def _():
            y_ref[...] = acc_scr[...].astype(BF16)

    @pl.when(jnp.logical_not(valid) & (j == last))
    def _():
        y_ref[...] = jnp.zeros_like(y_ref)


def _moe_combine_kernel(off_ref, n_ref, h_ref, mod_ref, gpost_ref, dest_ref, gate_ref, ys_ref, o_ref, loc_scr, sems):
    i = pl.program_id(0)
    cur = lax.rem(i, 2)

    @pl.when(i == 0)
    def _():
        _moe_segment_copies(i, off_ref, n_ref, ys_ref, loc_scr.at[cur], sems.at[cur], to_hbm=False)

    @pl.when(i < pl.num_programs(0) - 1)
    def _():
        _moe_segment_copies(i + 1, off_ref, n_ref, ys_ref, loc_scr.at[1 - cur], sems.at[1 - cur], to_hbm=False)

    dest, gate = dest_ref[...], gate_ref[...]
    slot = lax.broadcasted_iota(jnp.int32, (dest.shape[0], loc_scr.shape[1]), 1).astype(F32)
    w = jnp.where(slot == dest[:, 0:1], gate[:, 0:1], jnp.where(slot == dest[:, 1:2], gate[:, 1:2], 0.0))
    w_hi = w.astype(BF16)
    w_lo = (w - w_hi.astype(F32)).astype(BF16)
    n_rows, n_big, n_small = _moe_segment_counts(i, n_ref)
    _moe_wait_copies(n_big, n_small, ys_ref, loc_scr.at[cur], sems.at[cur], to_hbm=False)
    filled = lax.broadcasted_iota(jnp.int32, (loc_scr.shape[1], 1), 0) < n_rows
    y_cur = loc_scr[cur]
    y_loc = jnp.where(filled, y_cur, jnp.zeros_like(y_cur))
    y = _dot(w_hi, y_loc) + _dot(w_lo, y_loc)
    o_ref[...] = h_ref[...] + mod_ref[5:6, :] * (y * _rms_scale(y) * gpost_ref[...])


def _moe_call(h, mod_l, g_pre, g_post, router_pad, wg, wu, wd, *, n_b, lx, tm, tf):
    rows, d = h.shape
    ne, _, fe = wg.shape
    ts, tg = tm, MOE_GROUP_TILE
    n_tiles = rows // ts
    lrows = _moe_local_rows(ts)
    nt_max = -(-(TOP_K * rows + n_tiles * ne * (MOE_ALIGN - 1)) // tg) + ne
    n_pad = nt_max * tg
    nxt, per_b = n_b * lx // ts, lx // ts
    mod_row = lambda i: jnp.where(i < nxt, i // per_b, n_b)

    u, dest, gate, cnt = pl.pallas_call(
        _moe_route_kernel,
        grid=(n_tiles,),
        in_specs=[pl.BlockSpec((ts, d), lambda i: (i, 0)),
                  pl.BlockSpec((None, 6, d), lambda i: (mod_row(i), 0, 0)),
                  pl.BlockSpec((1, d), lambda i: (0, 0)),
                  pl.BlockSpec((2, d, 128), lambda i: (0, 0, 0))],
        out_specs=[pl.BlockSpec((ts, d), lambda i: (i, 0)),
                   pl.BlockSpec((ts, 128), lambda i: (i, 0)),
                   pl.BlockSpec((ts, 128), lambda i: (i, 0)),
                   pl.BlockSpec((None, 8, 128), lambda i: (i, 0, 0))],
        out_shape=[jax.ShapeDtypeStruct((rows, d), BF16),
                   jax.ShapeDtypeStruct((rows, 128), F32),
                   jax.ShapeDtypeStruct((rows, 128), F32),
                   jax.ShapeDtypeStruct((n_tiles, 8, 128), F32)],
        compiler_params=_cp(("parallel",), VMEM_STREAM_MB),
        name="moe_route",
    )(h, mod_l, g_pre.reshape(1, d), router_pad)

    cnt_al = cnt[:, 0, :ne].astype(jnp.int32)
    group = -(-jnp.sum(cnt_al, axis=0) // tg) * tg
    group_end = jnp.cumsum(group)
    seg_off = (group_end - group)[None, :] + jnp.cumsum(cnt_al, axis=0) - cnt_al
    n_used = group_end[-1] // tg
    tile_id = jnp.arange(nt_max, dtype=jnp.int32)
    tile_valid = (tile_id < n_used).astype(jnp.int32)
    tile_expert = jnp.searchsorted(group_end // tg, jnp.minimum(tile_id, n_used - 1), side='right')
    tile_expert = jnp.minimum(tile_expert, ne - 1).astype(jnp.int32)
    seg_off = seg_off.reshape(-1).astype(jnp.int32)
    seg_chunks = (cnt_al // MOE_ALIGN).reshape(-1)

    u_sorted = pl.pallas_call(
        _moe_sort_kernel,
        grid_spec=pltpu.PrefetchScalarGridSpec(
            num_scalar_prefetch=2,
            grid=(n_tiles,),
            in_specs=[pl.BlockSpec((ts, d), lambda i, o_r, n_r: (i, 0)),
                      pl.BlockSpec((ts, 128), lambda i, o_r, n_r: (i, 0)),
                      pl.BlockSpec(memory_space=pl.ANY)],
            out_specs=pl.BlockSpec(memory_space=pl.ANY),
            scratch_shapes=[pltpu.VMEM((2, lrows, d), BF16), pltpu.SemaphoreType.DMA((2, 2))]),
        out_shape=jax.ShapeDtypeStruct((n_pad, d), BF16),
        input_output_aliases={4: 0},
        compiler_params=_cp(("arbitrary",), VMEM_STREAM_MB),
        name="moe_sort",
    )(seg_off, seg_chunks, u, dest, jnp.zeros((n_pad, d), BF16))

    nj = fe // tf
    w_col = lambda k, j, te_r, tv_r: jnp.where(tv_r[k] > 0, j, nj - 1)
    y_sorted = pl.pallas_call(
        _moe_expert_kernel,
        grid_spec=pltpu.PrefetchScalarGridSpec(
            num_scalar_prefetch=2,
            grid=(nt_max, nj),
            in_specs=[pl.BlockSpec((tg, d), lambda k, j, te_r, tv_r: (k, 0)),
                      pl.BlockSpec((None, d, tf), lambda k, j, te_r, tv_r: (te_r[k], 0, w_col(k, j, te_r, tv_r))),
                      pl.BlockSpec((None, d, tf), lambda k, j, te_r, tv_r: (te_r[k], 0, w_col(k, j, te_r, tv_r))),
                      pl.BlockSpec((None, tf, d), lambda k, j, te_r, tv_r: (te_r[k], w_col(k, j, te_r, tv_r), 0))],
            out_specs=pl.BlockSpec((tg, d), lambda k, j, te_r, tv_r: (k, 0)),
            scratch_shapes=[pltpu.VMEM((tg, d), F32)]),
        out_shape=jax.ShapeDtypeStruct((n_pad, d), BF16),
        compiler_params=_cp(("parallel", "arbitrary"), VMEM_ATTN_MB),
        name="moe_experts",
    )(tile_expert, tile_valid, u_sorted, wg, wu, wd)

    return pl.pallas_call(
        _moe_combine_kernel,
        grid_spec=pltpu.PrefetchScalarGridSpec(
            num_scalar_prefetch=2,
            grid=(n_tiles,),
            in_specs=[pl.BlockSpec((ts, d), lambda i, o_r, n_r: (i, 0)),
                      pl.BlockSpec((None, 6, d), lambda i, o_r, n_r: (mod_row(i), 0, 0)),
                      pl.BlockSpec((1, d), lambda i, o_r, n_r: (0, 0)),
                      pl.BlockSpec((ts, 128), lambda i, o_r, n_r: (i, 0)),
                      pl.BlockSpec((ts, 128), lambda i, o_r, n_r: (i, 0)),
                      pl.BlockSpec(memory_space=pl.ANY)],
            out_specs=pl.BlockSpec((ts, d), lambda i, o_r, n_r: (i, 0)),
            scratch_shapes=[pltpu.VMEM((2, lrows, d), BF16), pltpu.SemaphoreType.DMA((2, 2))]),
        out_shape=jax.ShapeDtypeStruct((rows, d), F32),
        compiler_params=_cp(("arbitrary",), VMEM_ATTN_MB),
        name="moe_combine",
    )(seg_off, seg_chunks, h, mod_l, g_post.reshape(1, d), dest, gate, y_sorted)


def _rope_tables(lx, tm):
    half = MLA_ROPE // 2
    n_axis = half // 2
    inv_freq = ROPE_THETA ** (-jnp.arange(n_axis, dtype=F32) / n_axis)
    pos = jnp.arange(lx)
    rows = (pos // GRID_W).astype(F32)
    cols = (pos % GRID_W).astype(F32)
    ang = jnp.concatenate([rows[:, None] * inv_freq, cols[:, None] * inv_freq], axis=-1)
    cos, sin = jnp.cos(ang), jnp.sin(ang)
    cos_t = jnp.concatenate([cos, cos], axis=-1)
    sin_t = jnp.concatenate([-sin, sin], axis=-1)
    cos_t = jnp.concatenate([cos_t, jnp.ones((tm, MLA_ROPE), F32)], axis=0)
    sin_t = jnp.concatenate([sin_t, jnp.zeros((tm, MLA_ROPE), F32)], axis=0)
    return cos_t, sin_t


def _mla_weights(q_norm, w_uq, kv_norm, w_ukv):
    dq = MLA_NOPE + MLA_ROPE
    half = MLA_ROPE // 2
    cols = lambda a, b: w_uq[:, a:b]
    nope = [cols(h * dq, h * dq + MLA_NOPE) for h in range(MLA_HEADS)]
    rope = [cols(h * dq + MLA_NOPE, (h + 1) * dq) for h in range(MLA_HEADS)]
    rope_sw = [cols(h * dq + MLA_NOPE + s * half, h * dq + MLA_NOPE + (s + 1) * half)
               for h in range(MLA_HEADS) for s in (1, 0)]
    pad = ((0, MLA_ROPE), (0, 0))
    wq = jnp.pad(jnp.concatenate(nope + rope, axis=1), pad).astype(BF16)
    wqs = jnp.pad(jnp.concatenate(rope_sw, axis=1), pad).astype(BF16)
    qn_ext = jnp.pad(q_norm, (0, MLA_ROPE)).reshape(1, -1)
    perm = np.zeros((MLA_ROPE, MLA_ROPE), np.float32)
    perm[(np.arange(MLA_ROPE) + half) % MLA_ROPE, np.arange(MLA_ROPE)] = 1.0
    return qn_ext, wq, wqs, kv_norm.reshape(1, -1), w_ukv.astype(BF16), jnp.asarray(perm, BF16)


def _pick_tile(n, cands):
    for t in cands:
        if n % t == 0:
            return t
    raise ValueError(f"no tile for {n}")


def _tile_plan(n_b, lx, lc, d_ff, d_expert):
    seq = math.gcd(lx, n_b * lc)
    return dict(
        tm=_pick_tile(seq, (512, 256, 128)),
        tm_in=_pick_tile(seq, (1024, 512, 256, 128)),
        tq=_pick_tile(lc, (256, 128)),
        tr=_pick_tile(lc, (256, 128)),
        tf_ffn=_pick_tile(d_ff, (512, 256, 128)),
        tf_moe=_pick_tile(d_expert, (256, 128)),
    )


def kernel(x, c, ctx, c_ctx, w_ada, b_ada, g_pre_mix, g_post_mix, g_pre_ffn, g_post_ffn, w_in, w_out, mla_q_norm, mla_w_uq, mla_kv_norm, mla_w_ukv, ssd_conv_w, ssd_conv_b, ssd_a_log, ssd_dt_bias, ssd_d, ssd_norm, na_rpb, gdn_conv_w, gdn_a_log, gdn_dt_bias, gdn_norm, ffn_w_gate, ffn_w_up, ffn_w_down, moe_router, moe_w_gate, moe_w_up, moe_w_down):
    n_b, lx, d = x.shape
    lc = ctx.shape[1]
    depth = w_ada.shape[0]
    rows_x, rows_c = n_b * lx, n_b * lc
    assert n_b + 1 <= 8 and lx % GRID_W == 0 and lx % lc == 0 and lc % CHUNK == 0
    plan = _tile_plan(n_b, lx, lc, ffn_w_gate.shape[-1], moe_w_gate.shape[-1])
    tm, tm_in, tq, tr = plan["tm"], plan["tm_in"], plan["tq"], plan["tr"]

    cvec = jnp.concatenate([c, c_ctx[None, :], jnp.zeros((8 - n_b - 1, d), F32)], axis=0)
    mod = _ada_call(cvec, w_ada, b_ada).reshape(depth, 8, 6, d)
    cos_t, sin_t = _rope_tables(lx, tm)
    na_bias = _na_bias_table(na_rpb.reshape((-1,) + na_rpb.shape[2:]), lx // GRID_W, lc)
    h_all = jnp.concatenate([x.reshape(rows_x, d), ctx.reshape(rows_c, d)], axis=0)

    for i in range(depth):
        need_ctx = i < depth - 1
        w_main, w_small = _regroup_w_in(w_in[i])
        proj, small = _inproj_call(h_all, mod[i], g_pre_mix[i], w_main, w_small, n_b=n_b, lx=lx, tm=tm_in)

        mla_w = _mla_weights(mla_q_norm[i], mla_w_uq[i], mla_kv_norm[i], mla_w_ukv[i])
        q_a, k_a, v_a = _mla_prep_call(proj, *mla_w[:5], cos_t, sin_t, mla_w[5], n_b=n_b, lx=lx, tm=tm)
        ya = _mla_attn_call(q_a, k_a, v_a, n_b=n_b, lx=lx, lc=lc, tq=tq, need_ctx=need_ctx)

        xbc = _conv_call(proj, ssd_conv_w[i], ssd_conv_b[i], col_off=P_XBC, n_b=n_b, lx=lx, lc=lc, tr=tr)
        ys = _ssd_call(xbc, small, ssd_a_log[i], ssd_dt_bias[i], ssd_d[i], n_b=n_b, lx=lx, lc=lc)

        yn = _na_call(proj, na_bias, layer=i, n_b=n_b, lx=lx, lc=lc, need_ctx=need_ctx)

        qkv = _conv_call(proj, gdn_conv_w[i], jnp.zeros((3 * GDN_DIM,), F32), col_off=P_GQKV,
                         n_b=n_b, lx=lx, lc=lc, tr=tr)
        og = _gdn_call(qkv, small, gdn_a_log[i], gdn_dt_bias[i], n_b=n_b, lx=lx, lc=lc)

        n_rows = rows_x + rows_c if need_ctx else rows_x
        h_mid = _mixout_call(h_all, mod[i], g_post_mix[i], ya, ys, proj, ssd_norm[i], yn, og, gdn_norm[i],
                             w_out[i].astype(BF16), n_b=n_b, lx=lx, n_rows=n_rows, tm=tm)
        j = i // 2
        if i % 2 == 0:
            h_all = _ffn_call(h_mid, mod[i], g_pre_ffn[i], g_post_ffn[i], ffn_w_gate[j].astype(BF16),
                              ffn_w_up[j].astype(BF16), ffn_w_down[j].astype(BF16), n_b=n_b, lx=lx, tm=tm,
                              tf=plan["tf_ffn"])
        else:
            router_f32 = jnp.pad(moe_router[j], ((0, 0), (0, 128 - N_EXPERTS)))
            router_hi = router_f32.astype(BF16)
            router_pad = jnp.stack([router_hi, (router_f32 - router_hi.astype(F32)).astype(BF16)])
            h_all = _moe_call(h_mid, mod[i], g_pre_ffn[i], g_post_ffn[i], router_pad, moe_w_gate[j].astype(BF16),
                              moe_w_up[j].astype(BF16), moe_w_down[j].astype(BF16), n_b=n_b, lx=lx, tm=tm,
                              tf=plan["tf_moe"])
    return h_all[:rows_x].reshape(n_b, lx, d)
```

```python
import functools
import math

import numpy as np
import jax
import jax.numpy as jnp
from jax import lax
from jax.experimental import pallas as pl
from jax.experimental.pallas import tpu as pltpu

F32 = jnp.float32
BF16 = jnp.bfloat16
HIGHEST = lax.Precision.HIGHEST

GRID_W = 64
EPS = 1e-6
ROPE_THETA = 10000.0
CHUNK = 128
CONV_W = 5
MLA_HEADS, MLA_NOPE, MLA_ROPE, MLA_V = 4, 128, 64, 128
MLA_Q_LORA, MLA_KV_LORA = 448, 128
SSD_HEADS, SSD_HEAD_DIM, SSD_STATE, SSD_GROUPS = 8, 64, 128, 2
SSD_D_INNER = SSD_HEADS * SSD_HEAD_DIM
SSD_CONV_DIM = SSD_D_INNER + 2 * SSD_GROUPS * SSD_STATE
NA_HEADS, NA_HEAD_DIM = 4, 128
NA_DIM = NA_HEADS * NA_HEAD_DIM
NA_WIN_ROWS, NA_WIN_COLS = 8, 16
GDN_HEADS, GDN_HEAD_DIM = 4, 128
GDN_DIM = GDN_HEADS * GDN_HEAD_DIM
N_EXPERTS, TOP_K = 8, 2
MLA_COLS = MLA_Q_LORA + MLA_KV_LORA + MLA_ROPE
SSD_COLS = SSD_D_INNER + SSD_CONV_DIM + 2 * SSD_HEADS
NA_COLS = 3 * NA_DIM
GDN_COLS = 4 * GDN_DIM + 4 * GDN_HEADS

P_XBC, P_ZSSD, P_NAQ, P_NAK, P_NAV = 0, 1024, 1536, 2048, 2560
P_GQKV, P_GZ, P_MLA = 3072, 4608, 5120
P_MAIN = 5760
P_SMALL = 128
NEG = -1e30
VMEM_MB = 1024 * 1024
VMEM_STREAM_MB, VMEM_ATTN_MB, VMEM_WEIGHTS_MB = 40, 48, 56


def _cp(sem, mb):
    return pltpu.CompilerParams(dimension_semantics=sem, vmem_limit_bytes=mb * VMEM_MB)


def _dot(a, b):
    return jnp.dot(a, b, preferred_element_type=F32)


def _dot_nt(a, b, precision=None):
    return lax.dot_general(a, b, (((1,), (1,)), ((), ())), preferred_element_type=F32, precision=precision)


def _dot_tn(a, b):
    return lax.dot_general(a, b, (((0,), (0,)), ((), ())), preferred_element_type=F32)


def _sigmoid(x):
    return 1.0 / (1.0 + jnp.exp(-x))


def _silu(x):
    return x * _sigmoid(x)


def _softplus(x):
    return jnp.maximum(x, 0.0) + jnp.log(1.0 + jnp.exp(-jnp.abs(x)))


def _select_sum(mask, x, mask_on_right=False):
    sel = jnp.where(mask, 1.0, 0.0).astype(BF16)
    x1 = x.astype(BF16)
    r1 = x - x1.astype(F32)
    x2 = r1.astype(BF16)
    x3 = (r1 - x2.astype(F32)).astype(BF16)
    mm = (lambda p: _dot(p, sel)) if mask_on_right else (lambda p: _dot(sel, p))
    return mm(x1) + (mm(x2) + mm(x3))


def _rms_scale(x):
    return lax.rsqrt(jnp.mean(x * x, axis=-1, keepdims=True) + EPS)


def _regroup_w_in(w):
    o_mla, o_ssd = 0, MLA_COLS
    o_na, o_gdn = o_ssd + SSD_COLS, o_ssd + SSD_COLS + NA_COLS
    main_segs = [
        (o_ssd + SSD_D_INNER, SSD_CONV_DIM),
        (o_ssd, SSD_D_INNER),
        (o_na, NA_COLS),
        (o_gdn, 4 * GDN_DIM),
        (o_mla + MLA_Q_LORA, MLA_KV_LORA),
        (o_mla, MLA_Q_LORA),
        (o_mla + MLA_Q_LORA + MLA_KV_LORA, MLA_ROPE),
    ]
    small_segs = [(o_ssd + SSD_D_INNER + SSD_CONV_DIM, 2 * SSD_HEADS), (o_gdn + 4 * GDN_DIM, 4 * GDN_HEADS)]
    assert sum(n for _, n in main_segs) == P_MAIN
    main = jnp.concatenate([w[:, a:a + n] for a, n in main_segs], axis=1).astype(BF16)
    n_small = sum(n for _, n in small_segs)
    small = jnp.concatenate([w[:, a:a + n] for a, n in small_segs]
                            + [jnp.zeros((w.shape[0], P_SMALL - n_small), w.dtype)], axis=1).astype(BF16)
    return main, small


def _ada_kernel(c_ref, w_ref, b_ref, o_ref):
    s = _silu(c_ref[...]).astype(BF16)
    o_ref[...] = _dot(s, w_ref[...].astype(BF16)) + b_ref[...]


def _ada_call(cvec, w_ada, b_ada):
    depth, d, n = w_ada.shape
    tn = 1024
    return pl.pallas_call(
        _ada_kernel,
        grid=(depth, n // tn),
        in_specs=[pl.BlockSpec((8, d), lambda l, j: (0, 0)),
                  pl.BlockSpec((None, d, tn), lambda l, j: (l, 0, j)),
                  pl.BlockSpec((None, 1, tn), lambda l, j: (l, 0, j))],
        out_specs=pl.BlockSpec((None, 8, tn), lambda l, j: (l, 0, j)),
        out_shape=jax.ShapeDtypeStruct((depth, 8, n), F32),
        compiler_params=_cp(("parallel", "parallel"), VMEM_STREAM_MB),
        name="adaln",
    )(cvec, w_ada, b_ada.reshape(depth, 1, n))


def _inproj_kernel(h_ref, mod_ref, g_ref, w_ref, ws_ref, o_ref, os_ref, u_scr):
    @pl.when(pl.program_id(1) == 0)
    def _():
        x = h_ref[...]
        y = x * _rms_scale(x) * g_ref[...]
        u = (y * (1.0 + mod_ref[1:2, :]) + mod_ref[0:1, :]).astype(BF16)
        u_scr[...] = u
        os_ref[...] = _dot(u, ws_ref[...])

    o_ref[...] = _dot(u_scr[...], w_ref[...]).astype(BF16)


def _inproj_call(h, mod_l, g_pre, w_main, w_small, *, n_b, lx, tm):
    rows, d = h.shape
    tn = _pick_tile(P_MAIN, (1920, 640, 128))
    nxt, per_b = n_b * lx // tm, lx // tm

    def mod_idx(i, j):
        return (jnp.where(i < nxt, i // per_b, n_b), 0, 0)

    return pl.pallas_call(
        _inproj_kernel,
        grid=(rows // tm, P_MAIN // tn),
        in_specs=[pl.BlockSpec((tm, d), lambda i, j: (i, 0)),
                  pl.BlockSpec((None, 6, d), mod_idx),
                  pl.BlockSpec((1, d), lambda i, j: (0, 0)),
                  pl.BlockSpec((d, tn), lambda i, j: (0, j)),
                  pl.BlockSpec((d, P_SMALL), lambda i, j: (0, 0))],
        out_specs=[pl.BlockSpec((tm, tn), lambda i, j: (i, j)),
                   pl.BlockSpec((tm, P_SMALL), lambda i, j: (i, 0))],
        out_shape=[jax.ShapeDtypeStruct((rows, P_MAIN), BF16),
                   jax.ShapeDtypeStruct((rows, P_SMALL), F32)],
        scratch_shapes=[pltpu.VMEM((tm, d), BF16)],
        compiler_params=_cp(("parallel", "arbitrary"), VMEM_WEIGHTS_MB),
        name="in_proj",
    )(h, mod_l, g_pre.reshape(1, d), w_main, w_small)


HALO = 16


def _conv_kernel(prev_ref, cur_ref, next_ref, w_ref, b_ref, o_ref, ext_scr, *, tr, blocks_x, seq_x, seq_c):
    i = pl.program_id(0)
    in_x = i < blocks_x
    pos = jnp.where(in_x, i % seq_x, (i - blocks_x) % seq_c)
    last_pos = jnp.where(in_x, seq_x - 1, seq_c - 1)
    ext_scr[0:HALO, :] = jnp.where(pos == 0, 0.0, prev_ref[...].astype(F32))
    ext_scr[HALO:HALO + tr, :] = cur_ref[...].astype(F32)
    ext_scr[HALO + tr:2 * HALO + tr, :] = jnp.where(pos == last_pos, 0.0, next_ref[...].astype(F32))
    ext = ext_scr[...]
    n_ext = ext.shape[0]
    acc = b_ref[...]
    for k in range(CONV_W):
        sh = (CONV_W // 2 - k) % n_ext
        tap = ext if sh == 0 else pltpu.roll(ext, sh, 0)
        acc = acc + w_ref[k:k + 1, :] * tap[HALO:HALO + tr, :]
    o_ref[...] = _silu(acc).astype(BF16)


def _conv_call(proj, w, b, *, col_off, n_b, lx, lc, tr):
    rows = proj.shape[0]
    c = w.shape[1]
    cb = col_off // c
    assert cb * c == col_off
    hb = tr // HALO
    n_halo = rows // HALO
    kern = functools.partial(_conv_kernel, tr=tr, blocks_x=n_b * lx // tr, seq_x=lx // tr, seq_c=lc // tr)
    return pl.pallas_call(
        kern,
        grid=(rows // tr,),
        in_specs=[pl.BlockSpec((HALO, c), lambda i: (jnp.maximum(i * hb - 1, 0), cb)),
                  pl.BlockSpec((tr, c), lambda i: (i, cb)),
                  pl.BlockSpec((HALO, c), lambda i: (jnp.minimum((i + 1) * hb, n_halo - 1), cb)),
                  pl.BlockSpec((CONV_W, c), lambda i: (0, 0)),
                  pl.BlockSpec((1, c), lambda i: (0, 0))],
        out_specs=pl.BlockSpec((tr, c), lambda i: (i, 0)),
        out_shape=jax.ShapeDtypeStruct((rows, c), BF16),
        scratch_shapes=[pltpu.VMEM((tr + 2 * HALO, c), F32)],
        compiler_params=_cp(("parallel",), VMEM_STREAM_MB),
        name="dwconv_silu",
    )(proj, proj, proj, w, b.reshape(1, c))


def _mla_prep_kernel(p_ref, qn_ref, wq_ref, wqs_ref, kvn_ref, wkv_ref, cos_ref, sin_ref, perm_ref,
                     q_ref, k_ref, v_ref):
    p = p_ref[...].astype(F32)
    ckv = p[:, 0:MLA_KV_LORA]
    ce = p[:, MLA_KV_LORA:]
    lane = lax.broadcasted_iota(jnp.int32, ce.shape, 1)
    ssq = jnp.sum(jnp.where(lane < MLA_Q_LORA, ce * ce, 0.0), axis=-1, keepdims=True)
    cqn = (ce * lax.rsqrt(ssq / MLA_Q_LORA + EPS) * qn_ref[...]).astype(BF16)
    ckvn = (ckv * _rms_scale(ckv) * kvn_ref[...]).astype(BF16)
    q = _dot(cqn, wq_ref[...])
    qs = _dot(cqn, wqs_ref[...])
    kv = _dot(ckvn, wkv_ref[...])
    cos, sin = cos_ref[...], sin_ref[...]
    kr = p_ref[:, MLA_KV_LORA + MLA_Q_LORA:]
    kr_rot = kr.astype(F32) * cos + _dot(kr, perm_ref[...]) * sin
    nr = MLA_HEADS * MLA_NOPE
    scale = (MLA_NOPE + MLA_ROPE) ** -0.5
    ones_col = jnp.where(lax.broadcasted_iota(jnp.int32, (p.shape[0], MLA_V), 1) == 0, 1.0, 0.0).astype(BF16)
    for h in range(MLA_HEADS):
        q_ref[h, :, 0:MLA_NOPE] = (q[:, h * MLA_NOPE:(h + 1) * MLA_NOPE] * scale).astype(BF16)
        qr = q[:, nr + h * MLA_ROPE:nr + (h + 1) * MLA_ROPE] * cos + qs[:, h * MLA_ROPE:(h + 1) * MLA_ROPE] * sin
        q_ref[h, :, MLA_NOPE:] = (qr * scale).astype(BF16)
        hv = h * (MLA_NOPE + MLA_V)
        k_ref[h, :, 0:MLA_NOPE] = kv[:, hv:hv + MLA_NOPE].astype(BF16)
        k_ref[h, :, MLA_NOPE:] = kr_rot.astype(BF16)
        v_ref[h, :, 0:MLA_V] = kv[:, hv + MLA_NOPE:hv + MLA_NOPE + MLA_V].astype(BF16)
        v_ref[h, :, MLA_V:] = ones_col


def _mla_prep_call(proj, qn_ext, wq, wqs, kvn, wkv, cos_t, sin_t, perm, *, n_b, lx, tm):
    rows = proj.shape[0]
    nxt, per_b = n_b * lx // tm, lx // tm
    dk = MLA_NOPE + MLA_ROPE
    rope_idx = lambda i: (jnp.where(i < nxt, i % per_b, per_b), 0)
    full = lambda a: pl.BlockSpec(a.shape, lambda i: (0,) * a.ndim)
    return pl.pallas_call(
        _mla_prep_kernel,
        grid=(rows // tm,),
        in_specs=[pl.BlockSpec((tm, MLA_COLS), lambda i: (i, P_MLA // MLA_COLS)),
                  full(qn_ext), full(wq), full(wqs), full(kvn), full(wkv),
                  pl.BlockSpec((tm, MLA_ROPE), rope_idx), pl.BlockSpec((tm, MLA_ROPE), rope_idx),
                  full(perm)],
        out_specs=[pl.BlockSpec((MLA_HEADS, tm, dk), lambda i: (0, i, 0)),
                   pl.BlockSpec((MLA_HEADS, tm, dk), lambda i: (0, i, 0)),
                   pl.BlockSpec((MLA_HEADS, tm, 2 * MLA_V), lambda i: (0, i, 0))],
        out_shape=[jax.ShapeDtypeStruct((MLA_HEADS, rows, dk), BF16),
                   jax.ShapeDtypeStruct((MLA_HEADS, rows, dk), BF16),
                   jax.ShapeDtypeStruct((MLA_HEADS, rows, 2 * MLA_V), BF16)],
        compiler_params=_cp(("parallel",), VMEM_STREAM_MB),
        name="mla_prep",
    )(proj, qn_ext, wq, wqs, kvn, wkv, cos_t, sin_t, perm)


def _softmax_pv(scores, values):
    m = functools.reduce(jnp.maximum, [jnp.max(s, axis=-1, keepdims=True) for s in scores])
    ps = [jnp.exp(s - m) for s in scores]
    den = functools.reduce(lambda a, b: a + b, [jnp.sum(p, axis=-1, keepdims=True) for p in ps])
    num = functools.reduce(lambda a, b: a + b, [_dot(p.astype(BF16), v) for p, v in zip(ps, values)])
    return num / den


def _softmax_pv_aug(scores, values_aug):
    m = functools.reduce(jnp.maximum, [jnp.max(s, axis=-1, keepdims=True) for s in scores])
    acc = functools.reduce(lambda a, b: a + b,
                           [_dot(jnp.exp((s - m).astype(BF16)), v) for s, v in zip(scores, values_aug)])
    return acc[:, 0:MLA_V] / acc[:, MLA_V:MLA_V + 1]


MLA_HEADS_PER_STEP = 4


def _mla_attn_kernel(q_ref, kx_ref, vx_ref, kc_ref, vc_ref, o_ref, *, nqx):
    qi = pl.program_id(2)
    heads = range(q_ref.shape[0])

    @pl.when(qi < nqx)
    def _():
        outs = [_softmax_pv_aug([_dot_nt(q_ref[h], kx_ref[h]), _dot_nt(q_ref[h], kc_ref[h])], [vx_ref[h], vc_ref[h]])
                for h in heads]
        o_ref[...] = jnp.concatenate(outs, axis=1).astype(BF16)

    @pl.when(qi >= nqx)
    def _():
        outs = [_softmax_pv_aug([_dot_nt(q_ref[h], kc_ref[h])], [vc_ref[h]]) for h in heads]
        o_ref[...] = jnp.concatenate(outs, axis=1).astype(BF16)


def _mla_attn_call(q, k, v, *, n_b, lx, lc, tq, need_ctx):
    rows = q.shape[1]
    dk = q.shape[2]
    nqx, nqc = lx // tq, lc // tq
    nq = nqx + (nqc if need_ctx else 0)
    nbx = n_b * lx // lc

    def q_row(b, qi):
        return jnp.where(qi < nqx, b * nqx + qi, n_b * nqx + b * nqc + (qi - nqx))

    kern = functools.partial(_mla_attn_kernel, nqx=nqx)
    out_rows = rows if need_ctx else n_b * lx
    hp = MLA_HEADS_PER_STEP
    return pl.pallas_call(
        kern,
        grid=(n_b, MLA_HEADS // hp, nq),
        in_specs=[pl.BlockSpec((hp, tq, dk), lambda b, h, qi: (h, q_row(b, qi), 0)),
                  pl.BlockSpec((hp, lx, dk), lambda b, h, qi: (h, b, 0)),
                  pl.BlockSpec((hp, lx, 2 * MLA_V), lambda b, h, qi: (h, b, 0)),
                  pl.BlockSpec((hp, lc, dk), lambda b, h, qi: (h, nbx + b, 0)),
                  pl.BlockSpec((hp, lc, 2 * MLA_V), lambda b, h, qi: (h, nbx + b, 0))],
        out_specs=pl.BlockSpec((tq, hp * MLA_V), lambda b, h, qi: (q_row(b, qi), h)),
        out_shape=jax.ShapeDtypeStruct((out_rows, MLA_HEADS * MLA_V), BF16),
        compiler_params=_cp(("parallel", "parallel", "arbitrary"), VMEM_ATTN_MB),
        name="mla_attn",
    )(q, k, v, k, v)


def _na_plan(g_rows, lc):
    wr = min(NA_WIN_ROWS, g_rows)
    rg = next(r for r in (4, 2, 1) if g_rows % r == 0 and lc % (r * GRID_W) == 0)
    wk = min(rg + wr - 1, g_rows)
    n_groups = g_rows // rg
    ks = np.clip(np.arange(n_groups) * rg - wr // 2, 0, g_rows - wk)
    r = np.arange(g_rows)
    rs = np.clip(r - wr // 2, 0, g_rows - wr)
    q_off = (r - np.repeat(ks, rg)).reshape(n_groups, rg)
    rel = (rs - np.repeat(ks, rg)).reshape(n_groups, rg)
    assert (rel >= 0).all() and (rel + wr <= wk).all()
    pats = [tuple(q_off[g]) + tuple(rel[g]) for g in range(n_groups)]
    uniq = sorted(set(pats))
    var = np.array([uniq.index(p) for p in pats], np.int32)
    q_off_v = np.array([p[:rg] for p in uniq])
    rel_v = np.array([p[rg:] for p in uniq])
    return wr, rg, wk, ks.astype(np.int32), var, q_off_v, rel_v


def _na_bias_table(rpb, g_rows, lc):
    wr, rg, wk, _, _, q_off_v, rel_v = _na_plan(g_rows, lc)
    col_start = np.clip(np.arange(GRID_W) - NA_WIN_COLS // 2, 0, GRID_W - NA_WIN_COLS)
    cc = np.arange(GRID_W)
    col_ok = (cc[None, :] >= col_start[:, None]) & (cc[None, :] < col_start[:, None] + NA_WIN_COLS)
    dc = np.clip(cc[None, :] - cc[:, None] + NA_WIN_COLS - 1, 0, 2 * NA_WIN_COLS - 2)
    n_dc = 2 * NA_WIN_COLS - 1
    nh = rpb.shape[0]
    onehot = (dc.reshape(-1)[:, None] == np.arange(n_dc)[None, :]).astype(np.float32)
    g = jnp.einsum('hab,yb->hay', rpb.astype(F32), jnp.asarray(onehot), precision=HIGHEST)
    g = jnp.where(col_ok[None, None], g.reshape(nh, -1, GRID_W, GRID_W), NEG)
    g = jnp.pad(g, ((0, 0), (wk, wk), (0, 0), (0, 0)), constant_values=NEG)
    w = np.arange(wk)
    tabs = []
    for v in range(q_off_v.shape[0]):
        rows_v = []
        for j in range(rg):
            a0 = wk - q_off_v[v, j] + NA_WIN_ROWS - 1
            own = (w >= rel_v[v, j]) & (w < rel_v[v, j] + wr)
            blk = jnp.where(jnp.asarray(own)[None, :, None, None], g[:, a0:a0 + wk], NEG)
            rows_v.append(jnp.transpose(blk, (0, 2, 1, 3)).reshape(nh, GRID_W, wk * GRID_W))
        tabs.append(jnp.concatenate(rows_v, axis=-2))
    return jnp.stack(tabs, axis=0)


def _na_kernel(var_ref, ks_ref, q_ref, kx_ref, vx_ref, kc_ref, vc_ref, bias_ref, o_ref, *, n_groups, wk, scale):
    g = pl.program_id(1)
    hd = NA_HEAD_DIM
    cols = [slice(h * hd, (h + 1) * hd) for h in range(NA_HEADS)]

    @pl.when(g < n_groups)
    def _():
        start = pl.multiple_of(ks_ref[g] * GRID_W, GRID_W)
        rows = pl.ds(start, wk * GRID_W)
        outs = []
        for h, c in enumerate(cols):
            q = q_ref[:, c]
            sl = _dot_nt(q, kx_ref[rows, c]) * scale + bias_ref[h]
            sc = _dot_nt(q, kc_ref[:, c]) * scale
            outs.append(_softmax_pv([sl, sc], [vx_ref[rows, c], vc_ref[:, c]]))
        o_ref[...] = jnp.concatenate(outs, axis=1).astype(BF16)

    @pl.when(g >= n_groups)
    def _():
        outs = [_softmax_pv([_dot_nt(q_ref[:, c], kc_ref[:, c]) * scale], [vc_ref[:, c]]) for c in cols]
        o_ref[...] = jnp.concatenate(outs, axis=1).astype(BF16)


def _na_call(proj, bias_tab, *, layer, n_b, lx, lc, need_ctx):
    rows = proj.shape[0]
    g_rows = lx // GRID_W
    _, rg, wk, ks, var, _, _ = _na_plan(g_rows, lc)
    n_groups = g_rows // rg
    tq = rg * GRID_W
    nqc = lc // tq
    nq = n_groups + (nqc if need_ctx else 0)
    nbx = n_b * lx // lc
    hd, nd = NA_HEAD_DIM, NA_DIM
    cq, ck, cv = P_NAQ // nd, P_NAK // nd, P_NAV // nd

    def q_row(b, g):
        return jnp.where(g < n_groups, b * n_groups + g, n_b * n_groups + b * nqc + (g - n_groups))

    kern = functools.partial(_na_kernel, n_groups=n_groups, wk=wk, scale=hd ** -0.5)
    grid_spec = pltpu.PrefetchScalarGridSpec(
        num_scalar_prefetch=2,
        grid=(n_b, nq),
        in_specs=[pl.BlockSpec((tq, nd), lambda b, g, var_r, ks_r: (q_row(b, g), cq)),
                  pl.BlockSpec((lx, nd), lambda b, g, var_r, ks_r: (b, ck)),
                  pl.BlockSpec((lx, nd), lambda b, g, var_r, ks_r: (b, cv)),
                  pl.BlockSpec((lc, nd), lambda b, g, var_r, ks_r: (nbx + b, ck)),
                  pl.BlockSpec((lc, nd), lambda b, g, var_r, ks_r: (nbx + b, cv)),
                  pl.BlockSpec((None, NA_HEADS, tq, wk * GRID_W),
                               lambda b, g, var_r, ks_r: (var_r[jnp.minimum(g, n_groups - 1)], layer, 0, 0))],
        out_specs=pl.BlockSpec((tq, nd), lambda b, g, var_r, ks_r: (q_row(b, g), 0)),
    )
    return pl.pallas_call(
        kern,
        grid_spec=grid_spec,
        out_shape=jax.ShapeDtypeStruct((rows if need_ctx else n_b * lx, NA_DIM), BF16),
        compiler_params=_cp(("parallel", "arbitrary"), VMEM_STREAM_MB),
        name="na_attn",
    )(jnp.asarray(var), jnp.asarray(ks), proj, proj, proj, proj, proj, bias_tab)


def _chunk_block(n_b, nxc, ncc):
    def f(b, d, c):
        cc = jnp.where(d == 0, c, ncc - 1 - c)
        cx = jnp.where(d == 0, c - ncc, nxc - 1 - (c - ncc))
        return jnp.where(c < ncc, n_b * nxc + b * ncc + cc, b * nxc + cx)
    return f


def _lane_vec(vals, offset):
    flat = vals.reshape(-1).astype(F32)
    return jnp.pad(flat, (offset, P_SMALL - offset - flat.shape[0])).reshape(1, P_SMALL)


def _ssd_prep_kernel(xbc_ref, sm_ref, alog_ref, dtb_ref, dsk_ref, yp_ref, xw_ref, ea_ref):
    nh, hp, ns = SSD_HEADS, SSD_HEAD_DIM, SSD_STATE
    gh = nh // SSD_GROUPS
    dt2 = _softplus(sm_ref[...] + dtb_ref[...])
    dta2 = dt2 * (-jnp.exp(alog_ref[...]))
    row = lax.broadcasted_iota(jnp.int32, (CHUNK, CHUNK), 0)
    col = lax.broadcasted_iota(jnp.int32, (CHUNK, CHUNK), 1)
    incl = [row >= col, row <= col]
    cs = [_select_sum(m, dta2) for m in incl]
    cs_t = [x.T for x in cs]
    tot2 = jnp.sum(dta2, axis=0, keepdims=True)
    dsk = dsk_ref[...]
    bo, co = SSD_D_INNER, SSD_D_INNER + SSD_GROUPS * ns
    scores = [_dot_nt(xbc_ref[:, co + g * ns:co + (g + 1) * ns], xbc_ref[:, bo + g * ns:bo + (g + 1) * ns])
              for g in range(SSD_GROUPS)]
    xs = [xbc_ref[:, h * hp:(h + 1) * hp].astype(F32) for h in range(nh)]
    yp, xw = [], []
    for d in range(2):
        for h in range(nh):
            ln = d * nh + h
            a_c, a_r = cs[d][:, ln:ln + 1], cs_t[d][ln:ln + 1, :]
            dec = jnp.exp(jnp.where(incl[d], a_c - a_r, NEG))
            m = (scores[h // gh] * dec).astype(BF16)
            y = _dot(m, (xs[h] * dt2[:, ln:ln + 1]).astype(BF16))
            yp.append(y + dsk[:, h:h + 1] * xs[h] if d == 0 else y)
            xw.append((xs[h] * (jnp.exp(tot2[:, ln:ln + 1] - a_c) * dt2[:, ln:ln + 1])).astype(BF16))
    yp_ref[...] = jnp.concatenate(yp, axis=1)
    xw_ref[...] = jnp.concatenate(xw, axis=1)
    ea_ref[...] = jnp.concatenate([jnp.exp(cs[0]), jnp.exp(cs[1])], axis=1)


def _ssd_scan_kernel(*refs):
    (bc0, yp0, xw0, ea0, sm0, bc1, yp1, xw1, ea1, sm1, alog_ref, dtb_ref, y0_ref, y1_ref, s_scr) = refs
    nh, hp, ns = SSD_HEADS, SSD_HEAD_DIM, SSD_STATE
    gh = nh // SSD_GROUPS

    @pl.when(pl.program_id(1) == 0)
    def _():
        s_scr[...] = jnp.zeros_like(s_scr)

    neg_a = -jnp.exp(alog_ref[...])
    for d, (bc, yp, xw, ea, sm, y_ref) in enumerate([(bc0, yp0, xw0, ea0, sm0, y0_ref),
                                                      (bc1, yp1, xw1, ea1, sm1, y1_ref)]):
        c_dec = jnp.exp(jnp.sum(_softplus(sm[...] + dtb_ref[...]) * neg_a, axis=0, keepdims=True))
        e_acum = ea[...]
        ys, states = [], []
        for g in range(SSD_GROUPS):
            bg = bc[:, g * ns:(g + 1) * ns]
            cg = bc[:, SSD_GROUPS * ns + g * ns:SSD_GROUPS * ns + (g + 1) * ns]
            s_g = s_scr[d, :, g * gh * hp:(g + 1) * gh * hp]
            y_int = _dot(cg, s_g.astype(BF16))
            upd = _dot_tn(bg, xw[:, g * gh * hp:(g + 1) * gh * hp])
            for hh in range(gh):
                h = g * gh + hh
                ln = d * nh + h
                sl = slice(hh * hp, (hh + 1) * hp)
                ys.append(yp[:, h * hp:(h + 1) * hp] + y_int[:, sl] * e_acum[:, ln:ln + 1])
                states.append(s_g[:, sl] * c_dec[:, ln:ln + 1] + upd[:, sl])
        y_ref[...] = jnp.concatenate(ys, axis=1)
        s_scr[d] = jnp.concatenate(states, axis=1)


def _ssd_call(xbc, small, a_log, dt_bias, d_skip, *, n_b, lx, lc):
    rows = xbc.shape[0]
    nxc, ncc = lx // CHUNK, lc // CHUNK
    blk = _chunk_block(n_b, nxc, ncc)
    nh, di = SSD_HEADS, SSD_D_INNER
    alog_v, dtb_v = _lane_vec(a_log, 0), _lane_vec(dt_bias, 0)
    ypart, xw, ea = pl.pallas_call(
        _ssd_prep_kernel,
        grid=(rows // CHUNK,),
        in_specs=[pl.BlockSpec((CHUNK, SSD_CONV_DIM), lambda i: (i, 0)),
                  pl.BlockSpec((CHUNK, P_SMALL), lambda i: (i, 0)),
                  pl.BlockSpec((1, P_SMALL), lambda i: (0, 0)),
                  pl.BlockSpec((1, P_SMALL), lambda i: (0, 0)),
                  pl.BlockSpec((1, nh), lambda i: (0, 0))],
        out_specs=[pl.BlockSpec((CHUNK, 2 * di), lambda i: (i, 0)),
                   pl.BlockSpec((CHUNK, 2 * di), lambda i: (i, 0)),
                   pl.BlockSpec((CHUNK, 2 * P_SMALL), lambda i: (i, 0))],
        out_shape=[jax.ShapeDtypeStruct((rows, 2 * di), F32),
                   jax.ShapeDtypeStruct((rows, 2 * di), BF16),
                   jax.ShapeDtypeStruct((rows, 2 * P_SMALL), F32)],
        compiler_params=_cp(("parallel",), VMEM_STREAM_MB),
        name="ssd_prep",
    )(xbc, small, alog_v, dtb_v, d_skip.reshape(1, nh))

    def dir_specs(d):
        at = lambda b, c: blk(b, d, c)
        return [pl.BlockSpec((CHUNK, SSD_CONV_DIM - di), lambda b, c: (at(b, c), 1)),
                pl.BlockSpec((CHUNK, di), lambda b, c: (at(b, c), d)),
                pl.BlockSpec((CHUNK, di), lambda b, c: (at(b, c), d)),
                pl.BlockSpec((CHUNK, P_SMALL), lambda b, c: (at(b, c), d)),
                pl.BlockSpec((CHUNK, P_SMALL), lambda b, c: (at(b, c), 0))]

    const = pl.BlockSpec((1, P_SMALL), lambda b, c: (0, 0))
    return pl.pallas_call(
        _ssd_scan_kernel,
        grid=(n_b, ncc + nxc),
        in_specs=dir_specs(0) + dir_specs(1) + [const, const],
        out_specs=[pl.BlockSpec((CHUNK, di), lambda b, c: (blk(b, 0, c), 0)),
                   pl.BlockSpec((CHUNK, di), lambda b, c: (blk(b, 1, c), 0))],
        out_shape=[jax.ShapeDtypeStruct((rows, di), F32), jax.ShapeDtypeStruct((rows, di), F32)],
        scratch_shapes=[pltpu.VMEM((2, SSD_STATE, di), F32)],
        compiler_params=_cp(("parallel", "arbitrary"), VMEM_STREAM_MB),
        name="ssd_scan",
    )(xbc, ypart, xw, ea, small, xbc, ypart, xw, ea, small, alog_v, dtb_v)


SOLVE_BLOCK = 16


def _unit_tri_solve_many(n_mats, rhss):
    ln = n_mats[0].shape[0]
    row = lax.broadcasted_iota(jnp.int32, (ln, ln), 0)
    col = lax.broadcasted_iota(jnp.int32, (ln, ln), 1)
    same_block = lambda s: (row // s) == (col // s)
    eye = jnp.where(row == col, 1.0, 0.0)
    mm = lambda a, b: _dot(a.astype(BF16), b.astype(BF16))
    ms = [jnp.where(same_block(SOLVE_BLOCK), -n, 0.0) for n in n_mats]
    xs = [eye + m for m in ms]
    mps = ms
    k = 1
    while 2 * k < SOLVE_BLOCK:
        mps = [mm(m, m) for m in mps]
        xs = [x + mm(x, m) for x, m in zip(xs, mps)]
        k *= 2
    s = SOLVE_BLOCK
    while s < ln:
        joins = [jnp.where(same_block(2 * s), jnp.where(same_block(s), 0.0, n), 0.0) for n in n_mats]
        xs = [x - mm(mm(x, e), x) for x, e in zip(xs, joins)]
        s *= 2
    return [mm(x, r) for x, r in zip(xs, rhss)]


GDN_PACK = 5 * GDN_DIM
GDN_G_LANE = 2 * SSD_HEADS
GDN_B_LANE = 2 * SSD_HEADS + 2 * GDN_HEADS


def _gdn_prep_kernel(qkv_ref, sm_ref, alog_ref, dtb_ref, o_ref):
    nh, hd = GDN_HEADS, GDN_HEAD_DIM
    sm = sm_ref[...]
    g2 = -jnp.exp(alog_ref[...]) * _softplus(sm + dtb_ref[...])
    beta2 = _sigmoid(sm)
    row = lax.broadcasted_iota(jnp.int32, (CHUNK, CHUNK), 0)
    col = lax.broadcasted_iota(jnp.int32, (CHUNK, CHUNK), 1)
    incl = [row >= col, row <= col]
    strict = [row > col, row < col]
    cs = [_select_sum(m, g2) for m in incl]
    cs_t = [x.T for x in cs]
    gtot2 = jnp.sum(g2, axis=0, keepdims=True)
    qn, kn, kn_b, vv, qk_raw = [], [], [], [], []
    for h in range(nh):
        qh = qkv_ref[:, h * hd:(h + 1) * hd].astype(F32)
        kh = qkv_ref[:, GDN_DIM + h * hd:GDN_DIM + (h + 1) * hd].astype(F32)
        vv.append(qkv_ref[:, 2 * GDN_DIM + h * hd:2 * GDN_DIM + (h + 1) * hd].astype(F32))
        qn.append(qh * (lax.rsqrt(jnp.sum(qh * qh, axis=-1, keepdims=True) + EPS) * hd ** -0.5))
        kn.append(kh * lax.rsqrt(jnp.sum(kh * kh, axis=-1, keepdims=True) + EPS))
        kn_b.append(kn[h].astype(BF16))
        qk_raw.append(_dot_nt(qn[h].astype(BF16), kn_b[h]))
    n_mats, rhss, qks, qds, kds = [], [], [], [], []
    for d in range(2):
        for h in range(nh):
            lg, lb = GDN_G_LANE + d * nh + h, GDN_B_LANE + d * nh + h
            gcc, gcr = cs[d][:, lg:lg + 1], cs_t[d][lg:lg + 1, :]
            beta, gtot = beta2[:, lb:lb + 1], gtot2[:, lg:lg + 1]
            dec = jnp.exp(jnp.where(incl[d], gcc - gcr, NEG))
            kb = kn[h] * beta
            n_mats.append(jnp.where(strict[d], _dot_nt(kb.astype(BF16), kn_b[h]) * dec, 0.0))
            e_gc = jnp.exp(gcc)
            rhss.append(jnp.concatenate([vv[h] * beta, kb * e_gc], axis=1))
            qks.append(qk_raw[h] * dec)
            qds.append(qn[h] * e_gc)
            kds.append(kn[h] * jnp.exp(gtot - gcc))
    sols = _unit_tri_solve_many(n_mats, rhss)
    pieces = []
    for d in range(2):
        js = range(d * nh, (d + 1) * nh)
        pieces += [sols[j][:, 0:hd] for j in js] + [sols[j][:, hd:2 * hd] for j in js]
        pieces += [qks[j] for j in js] + [qds[j] for j in js] + [kds[j] for j in js]
    o_ref[...] = jnp.concatenate([p.astype(BF16) for p in pieces], axis=1)


def _gdn_scan_kernel(pk0, sm0, pk1, sm1, alog_ref, dtb_ref, o0_ref, o1_ref, s_scr):
    nh, hd = GDN_HEADS, GDN_HEAD_DIM

    @pl.when(pl.program_id(1) == 0)
    def _():
        s_scr[...] = jnp.zeros_like(s_scr)

    neg_a = -jnp.exp(alog_ref[...])
    for d, (pk_ref, sm_ref, o_ref) in enumerate([(pk0, sm0, o0_ref), (pk1, sm1, o1_ref)]):
        g_end2 = jnp.exp(jnp.sum(neg_a * _softplus(sm_ref[...] + dtb_ref[...]), axis=0, keepdims=True))
        outs, states = [], []
        for h in range(nh):
            lg = GDN_G_LANE + d * nh + h
            part = lambda j: pk_ref[:, (j * nh + h) * hd:(j * nh + h + 1) * hd]
            u, w, qk, qd, kd = part(0), part(1), part(2), part(3), part(4)
            s_h = s_scr[d, :, h * hd:(h + 1) * hd]
            s_b = s_h.astype(BF16)
            v_new = (u.astype(F32) - _dot(w, s_b)).astype(BF16)
            outs.append(_dot(qd, s_b) + _dot(qk, v_new))
            states.append(s_h * g_end2[:, lg:lg + 1] + _dot_tn(kd, v_new))
        o_ref[...] = jnp.concatenate(outs, axis=1)
        s_scr[d] = jnp.concatenate(states, axis=1)


def _gdn_call(qkv, small, a_log, dt_bias, *, n_b, lx, lc):
    rows = qkv.shape[0]
    nxc, ncc = lx // CHUNK, lc // CHUNK
    blk = _chunk_block(n_b, nxc, ncc)
    alog_v, dtb_v = _lane_vec(a_log, GDN_G_LANE), _lane_vec(dt_bias, GDN_G_LANE)
    packed = pl.pallas_call(
        _gdn_prep_kernel,
        grid=(rows // CHUNK,),
        in_specs=[pl.BlockSpec((CHUNK, 3 * GDN_DIM), lambda i: (i, 0)),
                  pl.BlockSpec((CHUNK, P_SMALL), lambda i: (i, 0)),
                  pl.BlockSpec((1, P_SMALL), lambda i: (0, 0)),
                  pl.BlockSpec((1, P_SMALL), lambda i: (0, 0))],
        out_specs=pl.BlockSpec((CHUNK, 2 * GDN_PACK), lambda i: (i, 0)),
        out_shape=jax.ShapeDtypeStruct((rows, 2 * GDN_PACK), BF16),
        compiler_params=_cp(("parallel",), VMEM_STREAM_MB),
        name="gdn_prep",
    )(qkv, small, alog_v, dtb_v)
    def dir_specs(d):
        return [pl.BlockSpec((CHUNK, GDN_PACK), lambda b, c: (blk(b, d, c), d)),
                pl.BlockSpec((CHUNK, P_SMALL), lambda b, c: (blk(b, d, c), 0))]

    const = pl.BlockSpec((1, P_SMALL), lambda b, c: (0, 0))
    return pl.pallas_call(
        _gdn_scan_kernel,
        grid=(n_b, ncc + nxc),
        in_specs=dir_specs(0) + dir_specs(1) + [const, const],
        out_specs=[pl.BlockSpec((CHUNK, GDN_DIM), lambda b, c: (blk(b, 0, c), 0)),
                   pl.BlockSpec((CHUNK, GDN_DIM), lambda b, c: (blk(b, 1, c), 0))],
        out_shape=[jax.ShapeDtypeStruct((rows, GDN_DIM), F32), jax.ShapeDtypeStruct((rows, GDN_DIM), F32)],
        scratch_shapes=[pltpu.VMEM((2, GDN_HEAD_DIM, GDN_DIM), F32)],
        compiler_params=_cp(("parallel", "arbitrary"), VMEM_STREAM_MB),
        name="gdn_scan",
    )(packed, small, packed, small, alog_v, dtb_v)


MIXOUT_SUB_ROWS = 256


def _mixout_kernel(h_ref, mod_ref, gpost_ref, ya_ref, ys0_ref, ys1_ref, zs_ref, sn_ref, yn_ref, og0_ref, og1_ref,
                   zg_ref, gn_ref, w_ref, o_ref):
    hd = GDN_HEAD_DIM
    tm = h_ref.shape[0]
    sub = min(tm, MIXOUT_SUB_ROWS)
    outs = []
    for s in range(0, tm, sub):
        r = slice(s, s + sub)
        ssd = (ys0_ref[r, :] + ys1_ref[r, :]) * _silu(zs_ref[r, :].astype(F32))
        yb = (ssd * _rms_scale(ssd) * sn_ref[...]).astype(BF16)
        gd = og0_ref[r, :] + og1_ref[r, :]
        zg = _silu(zg_ref[r, :].astype(F32))
        yd = []
        for h in range(GDN_HEADS):
            oh = gd[:, h * hd:(h + 1) * hd]
            yd.append((oh * _rms_scale(oh) * gn_ref[...] * zg[:, h * hd:(h + 1) * hd]).astype(BF16))
        parts = [ya_ref[r, :], yb, yn_ref[r, :]] + yd
        widths = [512, 512, 512] + [hd] * GDN_HEADS
        y = None
        off = 0
        for part, wd in zip(parts, widths):
            t = _dot(part, w_ref[off:off + wd, :])
            y = t if y is None else y + t
            off += wd
        outs.append(h_ref[r, :] + mod_ref[2:3, :] * (y * _rms_scale(y) * gpost_ref[...]))
    o_ref[...] = jnp.concatenate(outs, axis=0)


def _mixout_call(h, mod_l, g_post, ya, ys, proj, ssd_norm, yn, og, gdn_norm, w_out, *, n_b, lx, n_rows, tm):
    d = h.shape[1]
    nxt, per_b = n_b * lx // tm, lx // tm
    mod_idx = lambda i: (jnp.where(i < nxt, i // per_b, n_b), 0, 0)
    row = lambda i: (i, 0)
    const = lambda i: (0, 0)
    return pl.pallas_call(
        _mixout_kernel,
        grid=(n_rows // tm,),
        in_specs=[pl.BlockSpec((tm, d), row),
                  pl.BlockSpec((None, 6, d), mod_idx),
                  pl.BlockSpec((1, d), const),
                  pl.BlockSpec((tm, 512), row),
                  pl.BlockSpec((tm, 512), row),
                  pl.BlockSpec((tm, 512), row),
                  pl.BlockSpec((tm, 512), lambda i: (i, P_ZSSD // 512)),
                  pl.BlockSpec((1, 512), const),
                  pl.BlockSpec((tm, 512), row),
                  pl.BlockSpec((tm, 512), row),
                  pl.BlockSpec((tm, 512), row),
                  pl.BlockSpec((tm, 512), lambda i: (i, P_GZ // 512)),
                  pl.BlockSpec((1, GDN_HEAD_DIM), const),
                  pl.BlockSpec((d, d), const)],
        out_specs=pl.BlockSpec((tm, d), row),
        out_shape=jax.ShapeDtypeStruct((n_rows, d), F32),
        compiler_params=_cp(("parallel",), VMEM_WEIGHTS_MB),
        name="mix_out",
    )(h, mod_l, g_post.reshape(1, d), ya, ys[0], ys[1], proj, ssd_norm.reshape(1, 512), yn, og[0], og[1], proj,
      gdn_norm.reshape(1, GDN_HEAD_DIM), w_out)


def _ffn_kernel(h_ref, mod_ref, gpre_ref, gpost_ref, wg_ref, wu_ref, wd_ref, o_ref, u_scr, acc_scr):
    j = pl.program_id(1)

    @pl.when(j == 0)
    def _():
        x = h_ref[...]
        y = x * _rms_scale(x) * gpre_ref[...]
        u_scr[...] = (y * (1.0 + mod_ref[4:5, :]) + mod_ref[3:4, :]).astype(BF16)
        acc_scr[...] = jnp.zeros_like(acc_scr)

    u = u_scr[...]
    mid = (_silu(_dot(u, wg_ref[...])) * _dot(u, wu_ref[...])).astype(BF16)
    acc_scr[...] += _dot(mid, wd_ref[...])

    @pl.when(j == pl.num_programs(1) - 1)
    def _():
        y = acc_scr[...]
        o_ref[...] = h_ref[...] + mod_ref[5:6, :] * (y * _rms_scale(y) * gpost_ref[...])


def _ffn_call(h, mod_l, g_pre, g_post, wg, wu, wd, *, n_b, lx, tm, tf):
    rows, d = h.shape
    ff = wg.shape[1]
    nxt, per_b = n_b * lx // tm, lx // tm
    mod_idx = lambda i, j: (jnp.where(i < nxt, i // per_b, n_b), 0, 0)
    return pl.pallas_call(
        _ffn_kernel,
        grid=(rows // tm, ff // tf),
        in_specs=[pl.BlockSpec((tm, d), lambda i, j: (i, 0)),
                  pl.BlockSpec((None, 6, d), mod_idx),
                  pl.BlockSpec((1, d), lambda i, j: (0, 0)),
                  pl.BlockSpec((1, d), lambda i, j: (0, 0)),
                  pl.BlockSpec((d, tf), lambda i, j: (0, j)),
                  pl.BlockSpec((d, tf), lambda i, j: (0, j)),
                  pl.BlockSpec((tf, d), lambda i, j: (j, 0))],
        out_specs=pl.BlockSpec((tm, d), lambda i, j: (i, 0)),
        out_shape=jax.ShapeDtypeStruct((rows, d), F32),
        scratch_shapes=[pltpu.VMEM((tm, d), BF16), pltpu.VMEM((tm, d), F32)],
        compiler_params=_cp(("parallel", "arbitrary"), VMEM_WEIGHTS_MB),
        name="ffn_swiglu",
    )(h, mod_l, g_pre.reshape(1, d), g_post.reshape(1, d), wg, wu, wd)


MOE_ALIGN = 16
MOE_GROUP_TILE = 512


def _moe_local_rows(ts):
    return -(-(TOP_K * ts + N_EXPERTS * MOE_ALIGN) // 128) * 128


def _moe_route_kernel(h_ref, mod_ref, gpre_ref, wr_ref, u_ref, dest_ref, gate_ref, cnt_ref):
    ts, lanes = dest_ref.shape
    x = h_ref[...]
    y = x * _rms_scale(x) * gpre_ref[...]
    u = y * (1.0 + mod_ref[4:5, :]) + mod_ref[3:4, :]
    u_hi = u.astype(BF16)
    u_ref[...] = u_hi
    u_lo = (u - u_hi.astype(F32)).astype(BF16)
    logits = _dot(u_hi, wr_ref[0]) + (_dot(u_hi, wr_ref[1]) + _dot(u_lo, wr_ref[0]))
    lane = lax.broadcasted_iota(jnp.int32, logits.shape, 1).astype(F32)
    lg = jnp.where(lane < N_EXPERTS, logits, NEG)
    m1 = jnp.max(lg, axis=-1, keepdims=True)
    i1 = jnp.min(jnp.where(lg == m1, lane, float(lanes)), axis=-1, keepdims=True)
    lg2 = jnp.where(lane == i1, NEG, lg)
    m2 = jnp.max(lg2, axis=-1, keepdims=True)
    i2 = jnp.min(jnp.where(lg2 == m2, lane, float(lanes)), axis=-1, keepdims=True)
    e2 = jnp.exp(m2 - m1)
    gate_ref[...] = jnp.where(lane == 0.0, 1.0 / (1.0 + e2), jnp.where(lane == 1.0, e2 / (1.0 + e2), 0.0))
    sel = jnp.where(lane == i1, 1.0, 0.0) + jnp.where(lane == i2, 1.0, 0.0)
    cnt = jnp.sum(sel, axis=0, keepdims=True)
    cnt_al = jnp.floor((cnt + (MOE_ALIGN - 1)) / MOE_ALIGN) * MOE_ALIGN
    cnt_ref[...] = jnp.broadcast_to(cnt_al, cnt_ref.shape)
    before = lax.broadcasted_iota(jnp.int32, (lanes, lanes), 0) < lax.broadcasted_iota(jnp.int32, (lanes, lanes), 1)
    seg_lo = _select_sum(before, jnp.broadcast_to(cnt_al, (8, lanes)), mask_on_right=True)[0:1]
    earlier = lax.broadcasted_iota(jnp.int32, (ts, ts), 1) < lax.broadcasted_iota(jnp.int32, (ts, ts), 0)
    rank = _dot(earlier.astype(BF16), sel.astype(BF16))
    slot = seg_lo + rank
    slot_1 = jnp.sum(jnp.where(lane == i1, slot, 0.0), axis=-1, keepdims=True)
    slot_2 = jnp.sum(jnp.where(lane == i2, slot, 0.0), axis=-1, keepdims=True)
    dest_ref[...] = jnp.where(lane == 0.0, slot_1, jnp.where(lane == 1.0, slot_2, -1.0))


MOE_BIG = 4


def _moe_copy(hbm_ref, loc_ref, sems, far_row, loc_row, size_idx, *, to_hbm):
    rows = (MOE_BIG * MOE_ALIGN, MOE_ALIGN)[size_idx]
    loc = loc_ref.at[pl.ds(pl.multiple_of(loc_row, MOE_ALIGN), rows), :]
    far = hbm_ref.at[pl.ds(pl.multiple_of(far_row, MOE_ALIGN), rows), :]
    src, dst = (loc, far) if to_hbm else (far, loc)
    return pltpu.make_async_copy(src, dst, sems.at[size_idx])


def _moe_segment_copies(i, off_ref, n_ref, hbm_ref, loc_ref, sems, *, to_hbm):
    lo = jnp.int32(0)
    n_big_all, n_small_all = jnp.int32(0), jnp.int32(0)
    for e in range(N_EXPERTS):
        n_chunks = n_ref[i * N_EXPERTS + e]
        go = off_ref[i * N_EXPERTS + e]
        n_big = n_chunks // MOE_BIG
        n_small = n_chunks - n_big * MOE_BIG
        done = n_big * (MOE_BIG * MOE_ALIGN)

        def big(k, carry, lo=lo, go=go):
            step = k * (MOE_BIG * MOE_ALIGN)
            _moe_copy(hbm_ref, loc_ref, sems, go + step, lo + step, 0, to_hbm=to_hbm).start()
            return carry

        def small(k, carry, lo=lo, go=go, done=done):
            step = done + k * MOE_ALIGN
            _moe_copy(hbm_ref, loc_ref, sems, go + step, lo + step, 1, to_hbm=to_hbm).start()
            return carry

        lax.fori_loop(0, n_big, big, 0)
        lax.fori_loop(0, n_small, small, 0)
        lo = lo + n_chunks * MOE_ALIGN
        n_big_all, n_small_all = n_big_all + n_big, n_small_all + n_small
    return lo, n_big_all, n_small_all


def _moe_wait_copies(n_big, n_small, hbm_ref, loc_ref, sems, *, to_hbm):
    for size_idx, n in ((0, n_big), (1, n_small)):
        def body(k, carry, size_idx=size_idx):
            _moe_copy(hbm_ref, loc_ref, sems, 0, 0, size_idx, to_hbm=to_hbm).wait()
            return carry

        lax.fori_loop(0, n, body, 0)


def _moe_segment_counts(i, n_ref):
    n_chunks = [n_ref[i * N_EXPERTS + e] for e in range(N_EXPERTS)]
    n_big = sum(n // MOE_BIG for n in n_chunks)
    return sum(n_chunks) * MOE_ALIGN, n_big, sum(n_chunks) - n_big * MOE_BIG


def _moe_sort_kernel(off_ref, n_ref, u_ref, dest_ref, init_ref, us_ref, loc_scr, sems):
    del init_ref
    i = pl.program_id(0)
    cur = lax.rem(i, 2)
    dest = dest_ref[...]
    slot = lax.broadcasted_iota(jnp.int32, (dest.shape[0], loc_scr.shape[1]), 1).astype(F32)
    onehot = jnp.where(slot == dest[:, 0:1], 1.0, jnp.where(slot == dest[:, 1:2], 1.0, 0.0))
    loc_scr[cur] = _dot_tn(onehot.astype(BF16), u_ref[...]).astype(BF16)
    _, n_big, n_small = _moe_segment_copies(i, off_ref, n_ref, us_ref, loc_scr.at[cur], sems.at[cur], to_hbm=True)

    @pl.when(i > 0)
    def _():
        _, p_big, p_small = _moe_segment_counts(i - 1, n_ref)
        _moe_wait_copies(p_big, p_small, us_ref, loc_scr.at[1 - cur], sems.at[1 - cur], to_hbm=True)

    @pl.when(i == pl.num_programs(0) - 1)
    def _():
        _moe_wait_copies(n_big, n_small, us_ref, loc_scr.at[cur], sems.at[cur], to_hbm=True)


def _moe_expert_kernel(te_ref, tv_ref, u_ref, wg_ref, wu_ref, wd_ref, y_ref, acc_scr):
    k, j = pl.program_id(0), pl.program_id(1)
    last = pl.num_programs(1) - 1
    valid = tv_ref[k] > 0

    @pl.when(valid)
    def _():
        @pl.when(j == 0)
        def _():
            acc_scr[...] = jnp.zeros_like(acc_scr)

        u = u_ref[...]
        mid = (_silu(_dot(u, wg_ref[...])) * _dot(u, wu_ref[...])).astype(BF16)
        acc_scr[...] += _dot(mid, wd_ref[...])

        @pl.when(j == last)
        def _():
            y_ref[...] = acc_scr[...].astype(BF16)

    @pl.when(jnp.logical_not(valid) & (j == last))
    def _():
        y_ref[...] = jnp.zeros_like(y_ref)


def _moe_combine_kernel(off_ref, n_ref, h_ref, mod_ref, gpost_ref, dest_ref, gate_ref, ys_ref, o_ref, loc_scr, sems):
    i = pl.program_id(0)
    cur = lax.rem(i, 2)

    @pl.when(i == 0)
    def _():
        _moe_segment_copies(i, off_ref, n_ref, ys_ref, loc_scr.at[cur], sems.at[cur], to_hbm=False)

    @pl.when(i < pl.num_programs(0) - 1)
    def _():
        _moe_segment_copies(i + 1, off_ref, n_ref, ys_ref, loc_scr.at[1 - cur], sems.at[1 - cur], to_hbm=False)

    dest, gate = dest_ref[...], gate_ref[...]
    slot = lax.broadcasted_iota(jnp.int32, (dest.shape[0], loc_scr.shape[1]), 1).astype(F32)
    w = jnp.where(slot == dest[:, 0:1], gate[:, 0:1], jnp.where(slot == dest[:, 1:2], gate[:, 1:2], 0.0))
    w_hi = w.astype(BF16)
    w_lo = (w - w_hi.astype(F32)).astype(BF16)
    n_rows, n_big, n_small = _moe_segment_counts(i, n_ref)
    _moe_wait_copies(n_big, n_small, ys_ref, loc_scr.at[cur], sems.at[cur], to_hbm=False)
    filled = lax.broadcasted_iota(jnp.int32, (loc_scr.shape[1], 1), 0) < n_rows
    y_cur = loc_scr[cur]
    y_loc = jnp.where(filled, y_cur, jnp.zeros_like(y_cur))
    y = _dot(w_hi, y_loc) + _dot(w_lo, y_loc)
    o_ref[...] = h_ref[...] + mod_ref[5:6, :] * (y * _rms_scale(y) * gpost_ref[...])


def _moe_call(h, mod_l, g_pre, g_post, router_pad, wg, wu, wd, *, n_b, lx, tm, tf):
    rows, d = h.shape
    ne, _, fe = wg.shape
    ts, tg = tm, MOE_GROUP_TILE
    n_tiles = rows // ts
    lrows = _moe_local_rows(ts)
    nt_max = -(-(TOP_K * rows + n_tiles * ne * (MOE_ALIGN - 1)) // tg) + ne
    n_pad = nt_max * tg
    nxt, per_b = n_b * lx // ts, lx // ts
    mod_row = lambda i: jnp.where(i < nxt, i // per_b, n_b)

    u, dest, gate, cnt = pl.pallas_call(
        _moe_route_kernel,
        grid=(n_tiles,),
        in_specs=[pl.BlockSpec((ts, d), lambda i: (i, 0)),
                  pl.BlockSpec((None, 6, d), lambda i: (mod_row(i), 0, 0)),
                  pl.BlockSpec((1, d), lambda i: (0, 0)),
                  pl.BlockSpec((2, d, 128), lambda i: (0, 0, 0))],
        out_specs=[pl.BlockSpec((ts, d), lambda i: (i, 0)),
                   pl.BlockSpec((ts, 128), lambda i: (i, 0)),
                   pl.BlockSpec((ts, 128), lambda i: (i, 0)),
                   pl.BlockSpec((None, 8, 128), lambda i: (i, 0, 0))],
        out_shape=[jax.ShapeDtypeStruct((rows, d), BF16),
                   jax.ShapeDtypeStruct((rows, 128), F32),
                   jax.ShapeDtypeStruct((rows, 128), F32),
                   jax.ShapeDtypeStruct((n_tiles, 8, 128), F32)],
        compiler_params=_cp(("parallel",), VMEM_STREAM_MB),
        name="moe_route",
    )(h, mod_l, g_pre.reshape(1, d), router_pad)

    cnt_al = cnt[:, 0, :ne].astype(jnp.int32)
    group = -(-jnp.sum(cnt_al, axis=0) // tg) * tg
    group_end = jnp.cumsum(group)
    seg_off = (group_end - group)[None, :] + jnp.cumsum(cnt_al, axis=0) - cnt_al
    n_used = group_end[-1] // tg
    tile_id = jnp.arange(nt_max, dtype=jnp.int32)
    tile_valid = (tile_id < n_used).astype(jnp.int32)
    tile_expert = jnp.searchsorted(group_end // tg, jnp.minimum(tile_id, n_used - 1), side='right')
    tile_expert = jnp.minimum(tile_expert, ne - 1).astype(jnp.int32)
    seg_off = seg_off.reshape(-1).astype(jnp.int32)
    seg_chunks = (cnt_al // MOE_ALIGN).reshape(-1)

    u_sorted = pl.pallas_call(
        _moe_sort_kernel,
        grid_spec=pltpu.PrefetchScalarGridSpec(
            num_scalar_prefetch=2,
            grid=(n_tiles,),
            in_specs=[pl.BlockSpec((ts, d), lambda i, o_r, n_r: (i, 0)),
                      pl.BlockSpec((ts, 128), lambda i, o_r, n_r: (i, 0)),
                      pl.BlockSpec(memory_space=pl.ANY)],
            out_specs=pl.BlockSpec(memory_space=pl.ANY),
            scratch_shapes=[pltpu.VMEM((2, lrows, d), BF16), pltpu.SemaphoreType.DMA((2, 2))]),
        out_shape=jax.ShapeDtypeStruct((n_pad, d), BF16),
        input_output_aliases={4: 0},
        compiler_params=_cp(("arbitrary",), VMEM_STREAM_MB),
        name="moe_sort",
    )(seg_off, seg_chunks, u, dest, jnp.zeros((n_pad, d), BF16))

    nj = fe // tf
    w_col = lambda k, j, te_r, tv_r: jnp.where(tv_r[k] > 0, j, nj - 1)
    y_sorted = pl.pallas_call(
        _moe_expert_kernel,
        grid_spec=pltpu.PrefetchScalarGridSpec(
            num_scalar_prefetch=2,
            grid=(nt_max, nj),
            in_specs=[pl.BlockSpec((tg, d), lambda k, j, te_r, tv_r: (k, 0)),
                      pl.BlockSpec((None, d, tf), lambda k, j, te_r, tv_r: (te_r[k], 0, w_col(k, j, te_r, tv_r))),
                      pl.BlockSpec((None, d, tf), lambda k, j, te_r, tv_r: (te_r[k], 0, w_col(k, j, te_r, tv_r))),
                      pl.BlockSpec((None, tf, d), lambda k, j, te_r, tv_r: (te_r[k], w_col(k, j, te_r, tv_r), 0))],
            out_specs=pl.BlockSpec((tg, d), lambda k, j, te_r, tv_r: (k, 0)),
            scratch_shapes=[pltpu.VMEM((tg, d), F32)]),
        out_shape=jax.ShapeDtypeStruct((n_pad, d), BF16),
        compiler_params=_cp(("parallel", "arbitrary"), VMEM_ATTN_MB),
        name="moe_experts",
    )(tile_expert, tile_valid, u_sorted, wg, wu, wd)

    return pl.pallas_call(
        _moe_combine_kernel,
        grid_spec=pltpu.PrefetchScalarGridSpec(
            num_scalar_prefetch=2,
            grid=(n_tiles,),
            in_specs=[pl.BlockSpec((ts, d), lambda i, o_r, n_r: (i, 0)),
                      pl.BlockSpec((None, 6, d), lambda i, o_r, n_r: (mod_row(i), 0, 0)),
                      pl.BlockSpec((1, d), lambda i, o_r, n_r: (0, 0)),
                      pl.BlockSpec((ts, 128), lambda i, o_r, n_r: (i, 0)),
                      pl.BlockSpec((ts, 128), lambda i, o_r, n_r: (i, 0)),
                      pl.BlockSpec(memory_space=pl.ANY)],
            out_specs=pl.BlockSpec((ts, d), lambda i, o_r, n_r: (i, 0)),
            scratch_shapes=[pltpu.VMEM((2, lrows, d), BF16), pltpu.SemaphoreType.DMA((2, 2))]),
        out_shape=jax.ShapeDtypeStruct((rows, d), F32),
        compiler_params=_cp(("arbitrary",), VMEM_ATTN_MB),
        name="moe_combine",
    )(seg_off, seg_chunks, h, mod_l, g_post.reshape(1, d), dest, gate, y_sorted)


def _rope_tables(lx, tm):
    half = MLA_ROPE // 2
    n_axis = half // 2
    inv_freq = ROPE_THETA ** (-jnp.arange(n_axis, dtype=F32) / n_axis)
    pos = jnp.arange(lx)
    rows = (pos // GRID_W).astype(F32)
    cols = (pos % GRID_W).astype(F32)
    ang = jnp.concatenate([rows[:, None] * inv_freq, cols[:, None] * inv_freq], axis=-1)
    cos, sin = jnp.cos(ang), jnp.sin(ang)
    cos_t = jnp.concatenate([cos, cos], axis=-1)
    sin_t = jnp.concatenate([-sin, sin], axis=-1)
    cos_t = jnp.concatenate([cos_t, jnp.ones((tm, MLA_ROPE), F32)], axis=0)
    sin_t = jnp.concatenate([sin_t, jnp.zeros((tm, MLA_ROPE), F32)], axis=0)
    return cos_t, sin_t


def _mla_weights(q_norm, w_uq, kv_norm, w_ukv):
    dq = MLA_NOPE + MLA_ROPE
    half = MLA_ROPE // 2
    cols = lambda a, b: w_uq[:, a:b]
    nope = [cols(h * dq, h * dq + MLA_NOPE) for h in range(MLA_HEADS)]
    rope = [cols(h * dq + MLA_NOPE, (h + 1) * dq) for h in range(MLA_HEADS)]
    rope_sw = [cols(h * dq + MLA_NOPE + s * half, h * dq + MLA_NOPE + (s + 1) * half)
               for h in range(MLA_HEADS) for s in (1, 0)]
    pad = ((0, MLA_ROPE), (0, 0))
    wq = jnp.pad(jnp.concatenate(nope + rope, axis=1), pad).astype(BF16)
    wqs = jnp.pad(jnp.concatenate(rope_sw, axis=1), pad).astype(BF16)
    qn_ext = jnp.pad(q_norm, (0, MLA_ROPE)).reshape(1, -1)
    perm = np.zeros((MLA_ROPE, MLA_ROPE), np.float32)
    perm[(np.arange(MLA_ROPE) + half) % MLA_ROPE, np.arange(MLA_ROPE)] = 1.0
    return qn_ext, wq, wqs, kv_norm.reshape(1, -1), w_ukv.astype(BF16), jnp.asarray(perm, BF16)


def _pick_tile(n, cands):
    for t in cands:
        if n % t == 0:
            return t
    raise ValueError(f"no tile for {n}")


def _tile_plan(n_b, lx, lc, d_ff, d_expert):
    seq = math.gcd(lx, n_b * lc)
    return dict(
        tm=_pick_tile(seq, (512, 256, 128)),
        tm_in=_pick_tile(seq, (1024, 512, 256, 128)),
        tq=_pick_tile(lc, (256, 128)),
        tr=_pick_tile(lc, (256, 128)),
        tf_ffn=_pick_tile(d_ff, (512, 256, 128)),
        tf_moe=_pick_tile(d_expert, (256, 128)),
    )


def kernel(x, c, ctx, c_ctx, w_ada, b_ada, g_pre_mix, g_post_mix, g_pre_ffn, g_post_ffn, w_in, w_out, mla_q_norm, mla_w_uq, mla_kv_norm, mla_w_ukv, ssd_conv_w, ssd_conv_b, ssd_a_log, ssd_dt_bias, ssd_d, ssd_norm, na_rpb, gdn_conv_w, gdn_a_log, gdn_dt_bias, gdn_norm, ffn_w_gate, ffn_w_up, ffn_w_down, moe_router, moe_w_gate, moe_w_up, moe_w_down):
    n_b, lx, d = x.shape
    lc = ctx.shape[1]
    depth = w_ada.shape[0]
    rows_x, rows_c = n_b * lx, n_b * lc
    assert n_b + 1 <= 8 and lx % GRID_W == 0 and lx % lc == 0 and lc % CHUNK == 0
    plan = _tile_plan(n_b, lx, lc, ffn_w_gate.shape[-1], moe_w_gate.shape[-1])
    tm, tm_in, tq, tr = plan["tm"], plan["tm_in"], plan["tq"], plan["tr"]

    cvec = jnp.concatenate([c, c_ctx[None, :], jnp.zeros((8 - n_b - 1, d), F32)], axis=0)
    mod = _ada_call(cvec, w_ada, b_ada).reshape(depth, 8, 6, d)
    cos_t, sin_t = _rope_tables(lx, tm)
    na_bias = _na_bias_table(na_rpb.reshape((-1,) + na_rpb.shape[2:]), lx // GRID_W, lc)
    h_all = jnp.concatenate([x.reshape(rows_x, d), ctx.reshape(rows_c, d)], axis=0)

    for i in range(depth):
        need_ctx = i < depth - 1
        w_main, w_small = _regroup_w_in(w_in[i])
        proj, small = _inproj_call(h_all, mod[i], g_pre_mix[i], w_main, w_small, n_b=n_b, lx=lx, tm=tm_in)

        mla_w = _mla_weights(mla_q_norm[i], mla_w_uq[i], mla_kv_norm[i], mla_w_ukv[i])
        q_a, k_a, v_a = _mla_prep_call(proj, *mla_w[:5], cos_t, sin_t, mla_w[5], n_b=n_b, lx=lx, tm=tm)
        ya = _mla_attn_call(q_a, k_a, v_a, n_b=n_b, lx=lx, lc=lc, tq=tq, need_ctx=need_ctx)

        xbc = _conv_call(proj, ssd_conv_w[i], ssd_conv_b[i], col_off=P_XBC, n_b=n_b, lx=lx, lc=lc, tr=tr)
        ys = _ssd_call(xbc, small, ssd_a_log[i], ssd_dt_bias[i], ssd_d[i], n_b=n_b, lx=lx, lc=lc)

        yn = _na_call(proj, na_bias, layer=i, n_b=n_b, lx=lx, lc=lc, need_ctx=need_ctx)

        qkv = _conv_call(proj, gdn_conv_w[i], jnp.zeros((3 * GDN_DIM,), F32), col_off=P_GQKV,
                         n_b=n_b, lx=lx, lc=lc, tr=tr)
        og = _gdn_call(qkv, small, gdn_a_log[i], gdn_dt_bias[i], n_b=n_b, lx=lx, lc=lc)

        n_rows = rows_x + rows_c if need_ctx else rows_x
        h_mid = _mixout_call(h_all, mod[i], g_post_mix[i], ya, ys, proj, ssd_norm[i], yn, og, gdn_norm[i],
                             w_out[i].astype(BF16), n_b=n_b, lx=lx, n_rows=n_rows, tm=tm)
        j = i // 2
        if i % 2 == 0:
            h_all = _ffn_call(h_mid, mod[i], g_pre_ffn[i], g_post_ffn[i], ffn_w_gate[j].astype(BF16),
                              ffn_w_up[j].astype(BF16), ffn_w_down[j].astype(BF16), n_b=n_b, lx=lx, tm=tm,
                              tf=plan["tf_ffn"])
        else:
            router_f32 = jnp.pad(moe_router[j], ((0, 0), (0, 128 - N_EXPERTS)))
            router_hi = router_f32.astype(BF16)
            router_pad = jnp.stack([router_hi, (router_f32 - router_hi.astype(F32)).astype(BF16)])
            h_all = _moe_call(h_mid, mod[i], g_pre_ffn[i], g_post_ffn[i], router_pad, moe_w_gate[j].astype(BF16),
                              moe_w_up[j].astype(BF16), moe_w_down[j].astype(BF16), n_b=n_b, lx=lx, tm=tm,
                              tf=plan["tf_moe"])
    return h_all[:rows_x].reshape(n_b, lx, d)
```

```python
import functools
import math

import numpy as np
import jax
import jax.numpy as jnp
from jax import lax
from jax.experimental import pallas as pl
from jax.experimental.pallas import tpu as pltpu

F32 = jnp.float32
BF16 = jnp.bfloat16
HIGHEST = lax.Precision.HIGHEST

GRID_W = 64
EPS = 1e-6
ROPE_THETA = 10000.0
CHUNK = 128
CONV_W = 5
MLA_HEADS, MLA_NOPE, MLA_ROPE, MLA_V = 4, 128, 64, 128
MLA_Q_LORA, MLA_KV_LORA = 448, 128
SSD_HEADS, SSD_HEAD_DIM, SSD_STATE, SSD_GROUPS = 8, 64, 128, 2
SSD_D_INNER = SSD_HEADS * SSD_HEAD_DIM
SSD_CONV_DIM = SSD_D_INNER + 2 * SSD_GROUPS * SSD_STATE
NA_HEADS, NA_HEAD_DIM = 4, 128
NA_DIM = NA_HEADS * NA_HEAD_DIM
NA_WIN_ROWS, NA_WIN_COLS = 8, 16
GDN_HEADS, GDN_HEAD_DIM = 4, 128
GDN_DIM = GDN_HEADS * GDN_HEAD_DIM
N_EXPERTS, TOP_K = 8, 2
MLA_COLS = MLA_Q_LORA + MLA_KV_LORA + MLA_ROPE
SSD_COLS = SSD_D_INNER + SSD_CONV_DIM + 2 * SSD_HEADS
NA_COLS = 3 * NA_DIM
GDN_COLS = 4 * GDN_DIM + 4 * GDN_HEADS

P_XBC, P_ZSSD, P_NAQ, P_NAK, P_NAV = 0, 1024, 1536, 2048, 2560
P_GQKV, P_GZ, P_MLA = 3072, 4608, 5120
P_MAIN = 5760
P_SMALL = 128
NEG = -1e30
VMEM_MB = 1024 * 1024
VMEM_STREAM_MB, VMEM_ATTN_MB, VMEM_WEIGHTS_MB = 40, 48, 56


def _cp(sem, mb):
    return pltpu.CompilerParams(dimension_semantics=sem, vmem_limit_bytes=mb * VMEM_MB)


def _dot(a, b):
    return jnp.dot(a, b, preferred_element_type=F32)


def _dot_nt(a, b, precision=None):
    return lax.dot_general(a, b, (((1,), (1,)), ((), ())), preferred_element_type=F32, precision=precision)


def _dot_tn(a, b):
    return lax.dot_general(a, b, (((0,), (0,)), ((), ())), preferred_element_type=F32)


def _sigmoid(x):
    return 1.0 / (1.0 + jnp.exp(-x))


def _silu(x):
    return x * _sigmoid(x)


def _softplus(x):
    return jnp.maximum(x, 0.0) + jnp.log(1.0 + jnp.exp(-jnp.abs(x)))


def _select_sum(mask, x, mask_on_right=False):
    sel = jnp.where(mask, 1.0, 0.0).astype(BF16)
    x1 = x.astype(BF16)
    r1 = x - x1.astype(F32)
    x2 = r1.astype(BF16)
    x3 = (r1 - x2.astype(F32)).astype(BF16)
    mm = (lambda p: _dot(p, sel)) if mask_on_right else (lambda p: _dot(sel, p))
    return mm(x1) + (mm(x2) + mm(x3))


def _rms_scale(x):
    return lax.rsqrt(jnp.mean(x * x, axis=-1, keepdims=True) + EPS)


def _regroup_w_in(w):
    o_mla, o_ssd = 0, MLA_COLS
    o_na, o_gdn = o_ssd + SSD_COLS, o_ssd + SSD_COLS + NA_COLS
    main_segs = [
        (o_ssd + SSD_D_INNER, SSD_CONV_DIM),
        (o_ssd, SSD_D_INNER),
        (o_na, NA_COLS),
        (o_gdn, 4 * GDN_DIM),
        (o_mla + MLA_Q_LORA, MLA_KV_LORA),
        (o_mla, MLA_Q_LORA),
        (o_mla + MLA_Q_LORA + MLA_KV_LORA, MLA_ROPE),
    ]
    small_segs = [(o_ssd + SSD_D_INNER + SSD_CONV_DIM, 2 * SSD_HEADS), (o_gdn + 4 * GDN_DIM, 4 * GDN_HEADS)]
    assert sum(n for _, n in main_segs) == P_MAIN
    main = jnp.concatenate([w[:, a:a + n] for a, n in main_segs], axis=1).astype(BF16)
    n_small = sum(n for _, n in small_segs)
    small = jnp.concatenate([w[:, a:a + n] for a, n in small_segs]
                            + [jnp.zeros((w.shape[0], P_SMALL - n_small), w.dtype)], axis=1).astype(BF16)
    return main, small


def _ada_kernel(c_ref, w_ref, b_ref, o_ref):
    s = _silu(c_ref[...]).astype(BF16)
    o_ref[...] = _dot(s, w_ref[...].astype(BF16)) + b_ref[...]


def _ada_call(cvec, w_ada, b_ada):
    depth, d, n = w_ada.shape
    tn = 1024
    return pl.pallas_call(
        _ada_kernel,
        grid=(depth, n // tn),
        in_specs=[pl.BlockSpec((8, d), lambda l, j: (0, 0)),
                  pl.BlockSpec((None, d, tn), lambda l, j: (l, 0, j)),
                  pl.BlockSpec((None, 1, tn), lambda l, j: (l, 0, j))],
        out_specs=pl.BlockSpec((None, 8, tn), lambda l, j: (l, 0, j)),
        out_shape=jax.ShapeDtypeStruct((depth, 8, n), F32),
        compiler_params=_cp(("parallel", "parallel"), VMEM_STREAM_MB),
        name="adaln",
    )(cvec, w_ada, b_ada.reshape(depth, 1, n))


def _inproj_kernel(h_ref, mod_ref, g_ref, w_ref, ws_ref, o_ref, os_ref, u_scr):
    @pl.when(pl.program_id(1) == 0)
    def _():
        x = h_ref[...]
        y = x * _rms_scale(x) * g_ref[...]
        u = (y * (1.0 + mod_ref[1:2, :]) + mod_ref[0:1, :]).astype(BF16)
        u_scr[...] = u
        os_ref[...] = _dot(u, ws_ref[...])

    o_ref[...] = _dot(u_scr[...], w_ref[...]).astype(BF16)


def _inproj_call(h, mod_l, g_pre, w_main, w_small, *, n_b, lx, tm):
    rows, d = h.shape
    tn = _pick_tile(P_MAIN, (1920, 640, 128))
    nxt, per_b = n_b * lx // tm, lx // tm

    def mod_idx(i, j):
        return (jnp.where(i < nxt, i // per_b, n_b), 0, 0)

    return pl.pallas_call(
        _inproj_kernel,
        grid=(rows // tm, P_MAIN // tn),
        in_specs=[pl.BlockSpec((tm, d), lambda i, j: (i, 0)),
                  pl.BlockSpec((None, 6, d), mod_idx),
                  pl.BlockSpec((1, d), lambda i, j: (0, 0)),
                  pl.BlockSpec((d, tn), lambda i, j: (0, j)),
                  pl.BlockSpec((d, P_SMALL), lambda i, j: (0, 0))],
        out_specs=[pl.BlockSpec((tm, tn), lambda i, j: (i, j)),
                   pl.BlockSpec((tm, P_SMALL), lambda i, j: (i, 0))],
        out_shape=[jax.ShapeDtypeStruct((rows, P_MAIN), BF16),
                   jax.ShapeDtypeStruct((rows, P_SMALL), F32)],
        scratch_shapes=[pltpu.VMEM((tm, d), BF16)],
        compiler_params=_cp(("parallel", "arbitrary"), VMEM_WEIGHTS_MB),
        name="in_proj",
    )(h, mod_l, g_pre.reshape(1, d), w_main, w_small)


HALO = 16


def _conv_kernel(prev_ref, cur_ref, next_ref, w_ref, b_ref, o_ref, ext_scr, *, tr, blocks_x, seq_x, seq_c):
    i = pl.program_id(0)
    in_x = i < blocks_x
    pos = jnp.where(in_x, i % seq_x, (i - blocks_x) % seq_c)
    last_pos = jnp.where(in_x, seq_x - 1, seq_c - 1)
    ext_scr[0:HALO, :] = jnp.where(pos == 0, 0.0, prev_ref[...].astype(F32))
    ext_scr[HALO:HALO + tr, :] = cur_ref[...].astype(F32)
    ext_scr[HALO + tr:2 * HALO + tr, :] = jnp.where(pos == last_pos, 0.0, next_ref[...].astype(F32))
    ext = ext_scr[...]
    n_ext = ext.shape[0]
    acc = b_ref[...]
    for k in range(CONV_W):
        sh = (CONV_W // 2 - k) % n_ext
        tap = ext if sh == 0 else pltpu.roll(ext, sh, 0)
        acc = acc + w_ref[k:k + 1, :] * tap[HALO:HALO + tr, :]
    o_ref[...] = _silu(acc).astype(BF16)


def _conv_call(proj, w, b, *, col_off, n_b, lx, lc, tr):
    rows = proj.shape[0]
    c = w.shape[1]
    cb = col_off // c
    assert cb * c == col_off
    hb = tr // HALO
    n_halo = rows // HALO
    kern = functools.partial(_conv_kernel, tr=tr, blocks_x=n_b * lx // tr, seq_x=lx // tr, seq_c=lc // tr)
    return pl.pallas_call(
        kern,
        grid=(rows // tr,),
        in_specs=[pl.BlockSpec((HALO, c), lambda i: (jnp.maximum(i * hb - 1, 0), cb)),
                  pl.BlockSpec((tr, c), lambda i: (i, cb)),
                  pl.BlockSpec((HALO, c), lambda i: (jnp.minimum((i + 1) * hb, n_halo - 1), cb)),
                  pl.BlockSpec((CONV_W, c), lambda i: (0, 0)),
                  pl.BlockSpec((1, c), lambda i: (0, 0))],
        out_specs=pl.BlockSpec((tr, c), lambda i: (i, 0)),
        out_shape=jax.ShapeDtypeStruct((rows, c), BF16),
        scratch_shapes=[pltpu.VMEM((tr + 2 * HALO, c), F32)],
        compiler_params=_cp(("parallel",), VMEM_STREAM_MB),
        name="dwconv_silu",
    )(proj, proj, proj, w, b.reshape(1, c))


def _mla_prep_kernel(p_ref, qn_ref, wq_ref, wqs_ref, kvn_ref, wkv_ref, cos_ref, sin_ref, perm_ref,
                     q_ref, k_ref, v_ref):
    p = p_ref[...].astype(F32)
    ckv = p[:, 0:MLA_KV_LORA]
    ce = p[:, MLA_KV_LORA:]
    lane = lax.broadcasted_iota(jnp.int32, ce.shape, 1)
    ssq = jnp.sum(jnp.where(lane < MLA_Q_LORA, ce * ce, 0.0), axis=-1, keepdims=True)
    cqn = (ce * lax.rsqrt(ssq / MLA_Q_LORA + EPS) * qn_ref[...]).astype(BF16)
    ckvn = (ckv * _rms_scale(ckv) * kvn_ref[...]).astype(BF16)
    q = _dot(cqn, wq_ref[...])
    qs = _dot(cqn, wqs_ref[...])
    kv = _dot(ckvn, wkv_ref[...])
    cos, sin = cos_ref[...], sin_ref[...]
    kr = p_ref[:, MLA_KV_LORA + MLA_Q_LORA:]
    kr_rot = kr.astype(F32) * cos + _dot(kr, perm_ref[...]) * sin
    nr = MLA_HEADS * MLA_NOPE
    scale = (MLA_NOPE + MLA_ROPE) ** -0.5
    ones_col = jnp.where(lax.broadcasted_iota(jnp.int32, (p.shape[0], MLA_V), 1) == 0, 1.0, 0.0).astype(BF16)
    for h in range(MLA_HEADS):
        q_ref[h, :, 0:MLA_NOPE] = (q[:, h * MLA_NOPE:(h + 1) * MLA_NOPE] * scale).astype(BF16)
        qr = q[:, nr + h * MLA_ROPE:nr + (h + 1) * MLA_ROPE] * cos + qs[:, h * MLA_ROPE:(h + 1) * MLA_ROPE] * sin
        q_ref[h, :, MLA_NOPE:] = (qr * scale).astype(BF16)
        hv = h * (MLA_NOPE + MLA_V)
        k_ref[h, :, 0:MLA_NOPE] = kv[:, hv:hv + MLA_NOPE].astype(BF16)
        k_ref[h, :, MLA_NOPE:] = kr_rot.astype(BF16)
        v_ref[h, :, 0:MLA_V] = kv[:, hv + MLA_NOPE:hv + MLA_NOPE + MLA_V].astype(BF16)
        v_ref[h, :, MLA_V:] = ones_col


def _mla_prep_call(proj, qn_ext, wq, wqs, kvn, wkv, cos_t, sin_t, perm, *, n_b, lx, tm):
    rows = proj.shape[0]
    nxt, per_b = n_b * lx // tm, lx // tm
    dk = MLA_NOPE + MLA_ROPE
    rope_idx = lambda i: (jnp.where(i < nxt, i % per_b, per_b), 0)
    full = lambda a: pl.BlockSpec(a.shape, lambda i: (0,) * a.ndim)
    return pl.pallas_call(
        _mla_prep_kernel,
        grid=(rows // tm,),
        in_specs=[pl.BlockSpec((tm, MLA_COLS), lambda i: (i, P_MLA // MLA_COLS)),
                  full(qn_ext), full(wq), full(wqs), full(kvn), full(wkv),
                  pl.BlockSpec((tm, MLA_ROPE), rope_idx), pl.BlockSpec((tm, MLA_ROPE), rope_idx),
                  full(perm)],
        out_specs=[pl.BlockSpec((MLA_HEADS, tm, dk), lambda i: (0, i, 0)),
                   pl.BlockSpec((MLA_HEADS, tm, dk), lambda i: (0, i, 0)),
                   pl.BlockSpec((MLA_HEADS, tm, 2 * MLA_V), lambda i: (0, i, 0))],
        out_shape=[jax.ShapeDtypeStruct((MLA_HEADS, rows, dk), BF16),
                   jax.ShapeDtypeStruct((MLA_HEADS, rows, dk), BF16),
                   jax.ShapeDtypeStruct((MLA_HEADS, rows, 2 * MLA_V), BF16)],
        compiler_params=_cp(("parallel",), VMEM_STREAM_MB),
        name="mla_prep",
    )(proj, qn_ext, wq, wqs, kvn, wkv, cos_t, sin_t, perm)


def _softmax_pv(scores, values):
    m = functools.reduce(jnp.maximum, [jnp.max(s, axis=-1, keepdims=True) for s in scores])
    ps = [jnp.exp(s - m) for s in scores]
    den = functools.reduce(lambda a, b: a + b, [jnp.sum(p, axis=-1, keepdims=True) for p in ps])
    num = functools.reduce(lambda a, b: a + b, [_dot(p.astype(BF16), v) for p, v in zip(ps, values)])
    return num / den


def _softmax_pv_aug(scores, values_aug):
    m = functools.reduce(jnp.maximum, [jnp.max(s, axis=-1, keepdims=True) for s in scores])
    acc = functools.reduce(lambda a, b: a + b,
                           [_dot(jnp.exp((s - m).astype(BF16)), v) for s, v in zip(scores, values_aug)])
    return acc[:, 0:MLA_V] / acc[:, MLA_V:MLA_V + 1]


MLA_HEADS_PER_STEP = 4


def _mla_attn_kernel(q_ref, kx_ref, vx_ref, kc_ref, vc_ref, o_ref, *, nqx):
    qi = pl.program_id(2)
    heads = range(q_ref.shape[0])

    @pl.when(qi < nqx)
    def _():
        outs = [_softmax_pv_aug([_dot_nt(q_ref[h], kx_ref[h]), _dot_nt(q_ref[h], kc_ref[h])], [vx_ref[h], vc_ref[h]])
                for h in heads]
        o_ref[...] = jnp.concatenate(outs, axis=1).astype(BF16)

    @pl.when(qi >= nqx)
    def _():
        outs = [_softmax_pv_aug([_dot_nt(q_ref[h], kc_ref[h])], [vc_ref[h]]) for h in heads]
        o_ref[...] = jnp.concatenate(outs, axis=1).astype(BF16)


def _mla_attn_call(q, k, v, *, n_b, lx, lc, tq, need_ctx):
    rows = q.shape[1]
    dk = q.shape[2]
    nqx, nqc = lx // tq, lc // tq
    nq = nqx + (nqc if need_ctx else 0)
    nbx = n_b * lx // lc

    def q_row(b, qi):
        return jnp.where(qi < nqx, b * nqx + qi, n_b * nqx + b * nqc + (qi - nqx))

    kern = functools.partial(_mla_attn_kernel, nqx=nqx)
    out_rows = rows if need_ctx else n_b * lx
    hp = MLA_HEADS_PER_STEP
    return pl.pallas_call(
        kern,
        grid=(n_b, MLA_HEADS // hp, nq),
        in_specs=[pl.BlockSpec((hp, tq, dk), lambda b, h, qi: (h, q_row(b, qi), 0)),
                  pl.BlockSpec((hp, lx, dk), lambda b, h, qi: (h, b, 0)),
                  pl.BlockSpec((hp, lx, 2 * MLA_V), lambda b, h, qi: (h, b, 0)),
                  pl.BlockSpec((hp, lc, dk), lambda b, h, qi: (h, nbx + b, 0)),
                  pl.BlockSpec((hp, lc, 2 * MLA_V), lambda b, h, qi: (h, nbx + b, 0))],
        out_specs=pl.BlockSpec((tq, hp * MLA_V), lambda b, h, qi: (q_row(b, qi), h)),
        out_shape=jax.ShapeDtypeStruct((out_rows, MLA_HEADS * MLA_V), BF16),
        compiler_params=_cp(("parallel", "parallel", "arbitrary"), VMEM_ATTN_MB),
        name="mla_attn",
    )(q, k, v, k, v)


def _na_plan(g_rows, lc):
    wr = min(NA_WIN_ROWS, g_rows)
    rg = next(r for r in (4, 2, 1) if g_rows % r == 0 and lc % (r * GRID_W) == 0)
    wk = min(rg + wr - 1, g_rows)
    n_groups = g_rows // rg
    ks = np.clip(np.arange(n_groups) * rg - wr // 2, 0, g_rows - wk)
    r = np.arange(g_rows)
    rs = np.clip(r - wr // 2, 0, g_rows - wr)
    q_off = (r - np.repeat(ks, rg)).reshape(n_groups, rg)
    rel = (rs - np.repeat(ks, rg)).reshape(n_groups, rg)
    assert (rel >= 0).all() and (rel + wr <= wk).all()
    pats = [tuple(q_off[g]) + tuple(rel[g]) for g in range(n_groups)]
    uniq = sorted(set(pats))
    var = np.array([uniq.index(p) for p in pats], np.int32)
    q_off_v = np.array([p[:rg] for p in uniq])
    rel_v = np.array([p[rg:] for p in uniq])
    return wr, rg, wk, ks.astype(np.int32), var, q_off_v, rel_v


def _na_bias_table(rpb, g_rows, lc):
    wr, rg, wk, _, _, q_off_v, rel_v = _na_plan(g_rows, lc)
    col_start = np.clip(np.arange(GRID_W) - NA_WIN_COLS // 2, 0, GRID_W - NA_WIN_COLS)
    cc = np.arange(GRID_W)
    col_ok = (cc[None, :] >= col_start[:, None]) & (cc[None, :] < col_start[:, None] + NA_WIN_COLS)
    dc = np.clip(cc[None, :] - cc[:, None] + NA_WIN_COLS - 1, 0, 2 * NA_WIN_COLS - 2)
    n_dc = 2 * NA_WIN_COLS - 1
    nh = rpb.shape[0]
    onehot = (dc.reshape(-1)[:, None] == np.arange(n_dc)[None, :]).astype(np.float32)
    g = jnp.einsum('hab,yb->hay', rpb.astype(F32), jnp.asarray(onehot), precision=HIGHEST)
    g = jnp.where(col_ok[None, None], g.reshape(nh, -1, GRID_W, GRID_W), NEG)
    g = jnp.pad(g, ((0, 0), (wk, wk), (0, 0), (0, 0)), constant_values=NEG)
    w = np.arange(wk)
    tabs = []
    for v in range(q_off_v.shape[0]):
        rows_v = []
        for j in range(rg):
            a0 = wk - q_off_v[v, j] + NA_WIN_ROWS - 1
            own = (w >= rel_v[v, j]) & (w < rel_v[v, j] + wr)
            blk = jnp.where(jnp.asarray(own)[None, :, None, None], g[:, a0:a0 + wk], NEG)
            rows_v.append(jnp.transpose(blk, (0, 2, 1, 3)).reshape(nh, GRID_W, wk * GRID_W))
        tabs.append(jnp.concatenate(rows_v, axis=-2))
    return jnp.stack(tabs, axis=0)


def _na_kernel(var_ref, ks_ref, q_ref, kx_ref, vx_ref, kc_ref, vc_ref, bias_ref, o_ref, *, n_groups, wk, scale):
    g = pl.program_id(1)
    hd = NA_HEAD_DIM
    cols = [slice(h * hd, (h + 1) * hd) for h in range(NA_HEADS)]

    @pl.when(g < n_groups)
    def _():
        start = pl.multiple_of(ks_ref[g] * GRID_W, GRID_W)
        rows = pl.ds(start, wk * GRID_W)
        outs = []
        for h, c in enumerate(cols):
            q = q_ref[:, c]
            sl = _dot_nt(q, kx_ref[rows, c]) * scale + bias_ref[h]
            sc = _dot_nt(q, kc_ref[:, c]) * scale
            outs.append(_softmax_pv([sl, sc], [vx_ref[rows, c], vc_ref[:, c]]))
        o_ref[...] = jnp.concatenate(outs, axis=1).astype(BF16)

    @pl.when(g >= n_groups)
    def _():
        outs = [_softmax_pv([_dot_nt(q_ref[:, c], kc_ref[:, c]) * scale], [vc_ref[:, c]]) for c in cols]
        o_ref[...] = jnp.concatenate(outs, axis=1).astype(BF16)


def _na_call(proj, bias_tab, *, layer, n_b, lx, lc, need_ctx):
    rows = proj.shape[0]
    g_rows = lx // GRID_W
    _, rg, wk, ks, var, _, _ = _na_plan(g_rows, lc)
    n_groups = g_rows // rg
    tq = rg * GRID_W
    nqc = lc // tq
    nq = n_groups + (nqc if need_ctx else 0)
    nbx = n_b * lx // lc
    hd, nd = NA_HEAD_DIM, NA_DIM
    cq, ck, cv = P_NAQ // nd, P_NAK // nd, P_NAV // nd

    def q_row(b, g):
        return jnp.where(g < n_groups, b * n_groups + g, n_b * n_groups + b * nqc + (g - n_groups))

    kern = functools.partial(_na_kernel, n_groups=n_groups, wk=wk, scale=hd ** -0.5)
    grid_spec = pltpu.PrefetchScalarGridSpec(
        num_scalar_prefetch=2,
        grid=(n_b, nq),
        in_specs=[pl.BlockSpec((tq, nd), lambda b, g, var_r, ks_r: (q_row(b, g), cq)),
                  pl.BlockSpec((lx, nd), lambda b, g, var_r, ks_r: (b, ck)),
                  pl.BlockSpec((lx, nd), lambda b, g, var_r, ks_r: (b, cv)),
                  pl.BlockSpec((lc, nd), lambda b, g, var_r, ks_r: (nbx + b, ck)),
                  pl.BlockSpec((lc, nd), lambda b, g, var_r, ks_r: (nbx + b, cv)),
                  pl.BlockSpec((None, NA_HEADS, tq, wk * GRID_W),
                               lambda b, g, var_r, ks_r: (var_r[jnp.minimum(g, n_groups - 1)], layer, 0, 0))],
        out_specs=pl.BlockSpec((tq, nd), lambda b, g, var_r, ks_r: (q_row(b, g), 0)),
    )
    return pl.pallas_call(
        kern,
        grid_spec=grid_spec,
        out_shape=jax.ShapeDtypeStruct((rows if need_ctx else n_b * lx, NA_DIM), BF16),
        compiler_params=_cp(("parallel", "arbitrary"), VMEM_STREAM_MB),
        name="na_attn",
    )(jnp.asarray(var), jnp.asarray(ks), proj, proj, proj, proj, proj, bias_tab)


def _chunk_block(n_b, nxc, ncc):
    def f(b, d, c):
        cc = jnp.where(d == 0, c, ncc - 1 - c)
        cx = jnp.where(d == 0, c - ncc, nxc - 1 - (c - ncc))
        return jnp.where(c < ncc, n_b * nxc + b * ncc + cc, b * nxc + cx)
    return f


def _lane_vec(vals, offset):
    flat = vals.reshape(-1).astype(F32)
    return jnp.pad(flat, (offset, P_SMALL - offset - flat.shape[0])).reshape(1, P_SMALL)


def _ssd_prep_kernel(xbc_ref, sm_ref, alog_ref, dtb_ref, dsk_ref, yp_ref, xw_ref, ea_ref):
    nh, hp, ns = SSD_HEADS, SSD_HEAD_DIM, SSD_STATE
    gh = nh // SSD_GROUPS
    dt2 = _softplus(sm_ref[...] + dtb_ref[...])
    dta2 = dt2 * (-jnp.exp(alog_ref[...]))
    row = lax.broadcasted_iota(jnp.int32, (CHUNK, CHUNK), 0)
    col = lax.broadcasted_iota(jnp.int32, (CHUNK, CHUNK), 1)
    incl = [row >= col, row <= col]
    cs = [jnp.dot(m.astype(F32), dta2, preferred_element_type=F32, precision=HIGHEST) for m in incl]
    cs_t = [x.T for x in cs]
    tot2 = jnp.sum(dta2, axis=0, keepdims=True)
    dsk = dsk_ref[...]
    bo, co = SSD_D_INNER, SSD_D_INNER + SSD_GROUPS * ns
    scores = [_dot_nt(xbc_ref[:, co + g * ns:co + (g + 1) * ns], xbc_ref[:, bo + g * ns:bo + (g + 1) * ns])
              for g in range(SSD_GROUPS)]
    xs = [xbc_ref[:, h * hp:(h + 1) * hp].astype(F32) for h in range(nh)]
    yp, xw = [], []
    for d in range(2):
        for h in range(nh):
            ln = d * nh + h
            a_c, a_r = cs[d][:, ln:ln + 1], cs_t[d][ln:ln + 1, :]
            dec = jnp.exp(jnp.where(incl[d], a_c - a_r, NEG))
            m = (scores[h // gh] * dec).astype(BF16)
            y = _dot(m, (xs[h] * dt2[:, ln:ln + 1]).astype(BF16))
            yp.append(y + dsk[:, h:h + 1] * xs[h] if d == 0 else y)
            xw.append((xs[h] * (jnp.exp(tot2[:, ln:ln + 1] - a_c) * dt2[:, ln:ln + 1])).astype(BF16))
    yp_ref[...] = jnp.concatenate(yp, axis=1)
    xw_ref[...] = jnp.concatenate(xw, axis=1)
    ea_ref[...] = jnp.concatenate([jnp.exp(cs[0]), jnp.exp(cs[1])], axis=1)


def _ssd_scan_kernel(*refs):
    (bc0, yp0, xw0, ea0, sm0, bc1, yp1, xw1, ea1, sm1, alog_ref, dtb_ref, y0_ref, y1_ref, s_scr) = refs
    nh, hp, ns = SSD_HEADS, SSD_HEAD_DIM, SSD_STATE
    gh = nh // SSD_GROUPS

    @pl.when(pl.program_id(1) == 0)
    def _():
        s_scr[...] = jnp.zeros_like(s_scr)

    neg_a = -jnp.exp(alog_ref[...])
    for d, (bc, yp, xw, ea, sm, y_ref) in enumerate([(bc0, yp0, xw0, ea0, sm0, y0_ref),
                                                      (bc1, yp1, xw1, ea1, sm1, y1_ref)]):
        c_dec = jnp.exp(jnp.sum(_softplus(sm[...] + dtb_ref[...]) * neg_a, axis=0, keepdims=True))
        e_acum = ea[...]
        ys, states = [], []
        for g in range(SSD_GROUPS):
            bg = bc[:, g * ns:(g + 1) * ns]
            cg = bc[:, SSD_GROUPS * ns + g * ns:SSD_GROUPS * ns + (g + 1) * ns]
            s_g = s_scr[d, :, g * gh * hp:(g + 1) * gh * hp]
            y_int = _dot(cg, s_g.astype(BF16))
            upd = _dot_tn(bg, xw[:, g * gh * hp:(g + 1) * gh * hp])
            for hh in range(gh):
                h = g * gh + hh
                ln = d * nh + h
                sl = slice(hh * hp, (hh + 1) * hp)
                ys.append(yp[:, h * hp:(h + 1) * hp] + y_int[:, sl] * e_acum[:, ln:ln + 1])
                states.append(s_g[:, sl] * c_dec[:, ln:ln + 1] + upd[:, sl])
        y_ref[...] = jnp.concatenate(ys, axis=1)
        s_scr[d] = jnp.concatenate(states, axis=1)


def _ssd_call(xbc, small, a_log, dt_bias, d_skip, *, n_b, lx, lc):
    rows = xbc.shape[0]
    nxc, ncc = lx // CHUNK, lc // CHUNK
    blk = _chunk_block(n_b, nxc, ncc)
    nh, di = SSD_HEADS, SSD_D_INNER
    alog_v, dtb_v = _lane_vec(a_log, 0), _lane_vec(dt_bias, 0)
    ypart, xw, ea = pl.pallas_call(
        _ssd_prep_kernel,
        grid=(rows // CHUNK,),
        in_specs=[pl.BlockSpec((CHUNK, SSD_CONV_DIM), lambda i: (i, 0)),
                  pl.BlockSpec((CHUNK, P_SMALL), lambda i: (i, 0)),
                  pl.BlockSpec((1, P_SMALL), lambda i: (0, 0)),
                  pl.BlockSpec((1, P_SMALL), lambda i: (0, 0)),
                  pl.BlockSpec((1, nh), lambda i: (0, 0))],
        out_specs=[pl.BlockSpec((CHUNK, 2 * di), lambda i: (i, 0)),
                   pl.BlockSpec((CHUNK, 2 * di), lambda i: (i, 0)),
                   pl.BlockSpec((CHUNK, 2 * P_SMALL), lambda i: (i, 0))],
        out_shape=[jax.ShapeDtypeStruct((rows, 2 * di), F32),
                   jax.ShapeDtypeStruct((rows, 2 * di), BF16),
                   jax.ShapeDtypeStruct((rows, 2 * P_SMALL), F32)],
        compiler_params=_cp(("parallel",), VMEM_STREAM_MB),
        name="ssd_prep",
    )(xbc, small, alog_v, dtb_v, d_skip.reshape(1, nh))

    def dir_specs(d):
        at = lambda b, c: blk(b, d, c)
        return [pl.BlockSpec((CHUNK, SSD_CONV_DIM - di), lambda b, c: (at(b, c), 1)),
                pl.BlockSpec((CHUNK, di), lambda b, c: (at(b, c), d)),
                pl.BlockSpec((CHUNK, di), lambda b, c: (at(b, c), d)),
                pl.BlockSpec((CHUNK, P_SMALL), lambda b, c: (at(b, c), d)),
                pl.BlockSpec((CHUNK, P_SMALL), lambda b, c: (at(b, c), 0))]

    const = pl.BlockSpec((1, P_SMALL), lambda b, c: (0, 0))
    return pl.pallas_call(
        _ssd_scan_kernel,
        grid=(n_b, ncc + nxc),
        in_specs=dir_specs(0) + dir_specs(1) + [const, const],
        out_specs=[pl.BlockSpec((CHUNK, di), lambda b, c: (blk(b, 0, c), 0)),
                   pl.BlockSpec((CHUNK, di), lambda b, c: (blk(b, 1, c), 0))],
        out_shape=[jax.ShapeDtypeStruct((rows, di), F32), jax.ShapeDtypeStruct((rows, di), F32)],
        scratch_shapes=[pltpu.VMEM((2, SSD_STATE, di), F32)],
        compiler_params=_cp(("parallel", "arbitrary"), VMEM_STREAM_MB),
        name="ssd_scan",
    )(xbc, ypart, xw, ea, small, xbc, ypart, xw, ea, small, alog_v, dtb_v)


SOLVE_BLOCK = 16


def _unit_tri_solve_many(n_mats, rhss):
    ln = n_mats[0].shape[0]
    row = lax.broadcasted_iota(jnp.int32, (ln, ln), 0)
    col = lax.broadcasted_iota(jnp.int32, (ln, ln), 1)
    same_block = lambda s: (row // s) == (col // s)
    eye = jnp.where(row == col, 1.0, 0.0)
    mm = lambda a, b: _dot(a.astype(BF16), b.astype(BF16))
    ms = [jnp.where(same_block(SOLVE_BLOCK), -n, 0.0) for n in n_mats]
    xs = [eye + m for m in ms]
    mps = ms
    k = 1
    while 2 * k < SOLVE_BLOCK:
        mps = [mm(m, m) for m in mps]
        xs = [x + mm(x, m) for x, m in zip(xs, mps)]
        k *= 2
    s = SOLVE_BLOCK
    while s < ln:
        joins = [jnp.where(same_block(2 * s), jnp.where(same_block(s), 0.0, n), 0.0) for n in n_mats]
        xs = [x - mm(mm(x, e), x) for x, e in zip(xs, joins)]
        s *= 2
    return [mm(x, r) for x, r in zip(xs, rhss)]


GDN_PACK = 5 * GDN_DIM
GDN_G_LANE = 2 * SSD_HEADS
GDN_B_LANE = 2 * SSD_HEADS + 2 * GDN_HEADS


def _gdn_prep_kernel(qkv_ref, sm_ref, alog_ref, dtb_ref, o_ref):
    nh, hd = GDN_HEADS, GDN_HEAD_DIM
    sm = sm_ref[...]
    g2 = -jnp.exp(alog_ref[...]) * _softplus(sm + dtb_ref[...])
    beta2 = _sigmoid(sm)
    row = lax.broadcasted_iota(jnp.int32, (CHUNK, CHUNK), 0)
    col = lax.broadcasted_iota(jnp.int32, (CHUNK, CHUNK), 1)
    incl = [row >= col, row <= col]
    strict = [row > col, row < col]
    cs = [_select_sum(m, g2) for m in incl]
    cs_t = [x.T for x in cs]
    gtot2 = jnp.sum(g2, axis=0, keepdims=True)
    qn, kn, kn_b, vv, qk_raw = [], [], [], [], []
    for h in range(nh):
        qh = qkv_ref[:, h * hd:(h + 1) * hd].astype(F32)
        kh = qkv_ref[:, GDN_DIM + h * hd:GDN_DIM + (h + 1) * hd].astype(F32)
        vv.append(qkv_ref[:, 2 * GDN_DIM + h * hd:2 * GDN_DIM + (h + 1) * hd].astype(F32))
        qn.append(qh * (lax.rsqrt(jnp.sum(qh * qh, axis=-1, keepdims=True) + EPS) * hd ** -0.5))
        kn.append(kh * lax.rsqrt(jnp.sum(kh * kh, axis=-1, keepdims=True) + EPS))
        kn_b.append(kn[h].astype(BF16))
        qk_raw.append(_dot_nt(qn[h].astype(BF16), kn_b[h]))
    n_mats, rhss, qks, qds, kds = [], [], [], [], []
    for d in range(2):
        for h in range(nh):
            lg, lb = GDN_G_LANE + d * nh + h, GDN_B_LANE + d * nh + h
            gcc, gcr = cs[d][:, lg:lg + 1], cs_t[d][lg:lg + 1, :]
            beta, gtot = beta2[:, lb:lb + 1], gtot2[:, lg:lg + 1]
            dec = jnp.exp(jnp.where(incl[d], gcc - gcr, NEG))
            kb = kn[h] * beta
            n_mats.append(jnp.where(strict[d], _dot_nt(kb.astype(BF16), kn_b[h]) * dec, 0.0))
            e_gc = jnp.exp(gcc)
            rhss.append(jnp.concatenate([vv[h] * beta, kb * e_gc], axis=1))
            qks.append(qk_raw[h] * dec)
            qds.append(qn[h] * e_gc)
            kds.append(kn[h] * jnp.exp(gtot - gcc))
    sols = _unit_tri_solve_many(n_mats, rhss)
    pieces = []
    for d in range(2):
        js = range(d * nh, (d + 1) * nh)
        pieces += [sols[j][:, 0:hd] for j in js] + [sols[j][:, hd:2 * hd] for j in js]
        pieces += [qks[j] for j in js] + [qds[j] for j in js] + [kds[j] for j in js]
    o_ref[...] = jnp.concatenate([p.astype(BF16) for p in pieces], axis=1)


def _gdn_scan_kernel(pk0, sm0, pk1, sm1, alog_ref, dtb_ref, o0_ref, o1_ref, s_scr):
    nh, hd = GDN_HEADS, GDN_HEAD_DIM

    @pl.when(pl.program_id(1) == 0)
    def _():
        s_scr[...] = jnp.zeros_like(s_scr)

    neg_a = -jnp.exp(alog_ref[...])
    for d, (pk_ref, sm_ref, o_ref) in enumerate([(pk0, sm0, o0_ref), (pk1, sm1, o1_ref)]):
        g_end2 = jnp.exp(jnp.sum(neg_a * _softplus(sm_ref[...] + dtb_ref[...]), axis=0, keepdims=True))
        outs, states = [], []
        for h in range(nh):
            lg = GDN_G_LANE + d * nh + h
            part = lambda j: pk_ref[:, (j * nh + h) * hd:(j * nh + h + 1) * hd]
            u, w, qk, qd, kd = part(0), part(1), part(2), part(3), part(4)
            s_h = s_scr[d, :, h * hd:(h + 1) * hd]
            s_b = s_h.astype(BF16)
            v_new = (u.astype(F32) - _dot(w, s_b)).astype(BF16)
            outs.append(_dot(qd, s_b) + _dot(qk, v_new))
            states.append(s_h * g_end2[:, lg:lg + 1] + _dot_tn(kd, v_new))
        o_ref[...] = jnp.concatenate(outs, axis=1)
        s_scr[d] = jnp.concatenate(states, axis=1)


def _gdn_call(qkv, small, a_log, dt_bias, *, n_b, lx, lc):
    rows = qkv.shape[0]
    nxc, ncc = lx // CHUNK, lc // CHUNK
    blk = _chunk_block(n_b, nxc, ncc)
    alog_v, dtb_v = _lane_vec(a_log, GDN_G_LANE), _lane_vec(dt_bias, GDN_G_LANE)
    packed = pl.pallas_call(
        _gdn_prep_kernel,
        grid=(rows // CHUNK,),
        in_specs=[pl.BlockSpec((CHUNK, 3 * GDN_DIM), lambda i: (i, 0)),
                  pl.BlockSpec((CHUNK, P_SMALL), lambda i: (i, 0)),
                  pl.BlockSpec((1, P_SMALL), lambda i: (0, 0)),
                  pl.BlockSpec((1, P_SMALL), lambda i: (0, 0))],
        out_specs=pl.BlockSpec((CHUNK, 2 * GDN_PACK), lambda i: (i, 0)),
        out_shape=jax.ShapeDtypeStruct((rows, 2 * GDN_PACK), BF16),
        compiler_params=_cp(("parallel",), VMEM_STREAM_MB),
        name="gdn_prep",
    )(qkv, small, alog_v, dtb_v)
    def dir_specs(d):
        return [pl.BlockSpec((CHUNK, GDN_PACK), lambda b, c: (blk(b, d, c), d)),
                pl.BlockSpec((CHUNK, P_SMALL), lambda b, c: (blk(b, d, c), 0))]

    const = pl.BlockSpec((1, P_SMALL), lambda b, c: (0, 0))
    return pl.pallas_call(
        _gdn_scan_kernel,
        grid=(n_b, ncc + nxc),
        in_specs=dir_specs(0) + dir_specs(1) + [const, const],
        out_specs=[pl.BlockSpec((CHUNK, GDN_DIM), lambda b, c: (blk(b, 0, c), 0)),
                   pl.BlockSpec((CHUNK, GDN_DIM), lambda b, c: (blk(b, 1, c), 0))],
        out_shape=[jax.ShapeDtypeStruct((rows, GDN_DIM), F32), jax.ShapeDtypeStruct((rows, GDN_DIM), F32)],
        scratch_shapes=[pltpu.VMEM((2, GDN_HEAD_DIM, GDN_DIM), F32)],
        compiler_params=_cp(("parallel", "arbitrary"), VMEM_STREAM_MB),
        name="gdn_scan",
    )(packed, small, packed, small, alog_v, dtb_v)


MIXOUT_SUB_ROWS = 256


def _mixout_kernel(h_ref, mod_ref, gpost_ref, ya_ref, ys0_ref, ys1_ref, zs_ref, sn_ref, yn_ref, og0_ref, og1_ref,
                   zg_ref, gn_ref, w_ref, o_ref):
    hd = GDN_HEAD_DIM
    tm = h_ref.shape[0]
    sub = min(tm, MIXOUT_SUB_ROWS)
    outs = []
    for s in range(0, tm, sub):
        r = slice(s, s + sub)
        ssd = (ys0_ref[r, :] + ys1_ref[r, :]) * _silu(zs_ref[r, :].astype(F32))
        yb = (ssd * _rms_scale(ssd) * sn_ref[...]).astype(BF16)
        gd = og0_ref[r, :] + og1_ref[r, :]
        zg = _silu(zg_ref[r, :].astype(F32))
        yd = []
        for h in range(GDN_HEADS):
            oh = gd[:, h * hd:(h + 1) * hd]
            yd.append((oh * _rms_scale(oh) * gn_ref[...] * zg[:, h * hd:(h + 1) * hd]).astype(BF16))
        parts = [ya_ref[r, :], yb, yn_ref[r, :]] + yd
        widths = [512, 512, 512] + [hd] * GDN_HEADS
        y = None
        off = 0
        for part, wd in zip(parts, widths):
            t = _dot(part, w_ref[off:off + wd, :])
            y = t if y is None else y + t
            off += wd
        outs.append(h_ref[r, :] + mod_ref[2:3, :] * (y * _rms_scale(y) * gpost_ref[...]))
    o_ref[...] = jnp.concatenate(outs, axis=0)


def _mixout_call(h, mod_l, g_post, ya, ys, proj, ssd_norm, yn, og, gdn_norm, w_out, *, n_b, lx, n_rows, tm):
    d = h.shape[1]
    nxt, per_b = n_b * lx // tm, lx // tm
    mod_idx = lambda i: (jnp.where(i < nxt, i // per_b, n_b), 0, 0)
    row = lambda i: (i, 0)
    const = lambda i: (0, 0)
    return pl.pallas_call(
        _mixout_kernel,
        grid=(n_rows // tm,),
        in_specs=[pl.BlockSpec((tm, d), row),
                  pl.BlockSpec((None, 6, d), mod_idx),
                  pl.BlockSpec((1, d), const),
                  pl.BlockSpec((tm, 512), row),
                  pl.BlockSpec((tm, 512), row),
                  pl.BlockSpec((tm, 512), row),
                  pl.BlockSpec((tm, 512), lambda i: (i, P_ZSSD // 512)),
                  pl.BlockSpec((1, 512), const),
                  pl.BlockSpec((tm, 512), row),
                  pl.BlockSpec((tm, 512), row),
                  pl.BlockSpec((tm, 512), row),
                  pl.BlockSpec((tm, 512), lambda i: (i, P_GZ // 512)),
                  pl.BlockSpec((1, GDN_HEAD_DIM), const),
                  pl.BlockSpec((d, d), const)],
        out_specs=pl.BlockSpec((tm, d), row),
        out_shape=jax.ShapeDtypeStruct((n_rows, d), F32),
        compiler_params=_cp(("parallel",), VMEM_WEIGHTS_MB),
        name="mix_out",
    )(h, mod_l, g_post.reshape(1, d), ya, ys[0], ys[1], proj, ssd_norm.reshape(1, 512), yn, og[0], og[1], proj,
      gdn_norm.reshape(1, GDN_HEAD_DIM), w_out)


def _ffn_kernel(h_ref, mod_ref, gpre_ref, gpost_ref, wg_ref, wu_ref, wd_ref, o_ref, u_scr, acc_scr):
    j = pl.program_id(1)

    @pl.when(j == 0)
    def _():
        x = h_ref[...]
        y = x * _rms_scale(x) * gpre_ref[...]
        u_scr[...] = (y * (1.0 + mod_ref[4:5, :]) + mod_ref[3:4, :]).astype(BF16)
        acc_scr[...] = jnp.zeros_like(acc_scr)

    u = u_scr[...]
    mid = (_silu(_dot(u, wg_ref[...])) * _dot(u, wu_ref[...])).astype(BF16)
    acc_scr[...] += _dot(mid, wd_ref[...])

    @pl.when(j == pl.num_programs(1) - 1)
    def _():
        y = acc_scr[...]
        o_ref[...] = h_ref[...] + mod_ref[5:6, :] * (y * _rms_scale(y) * gpost_ref[...])


def _ffn_call(h, mod_l, g_pre, g_post, wg, wu, wd, *, n_b, lx, tm, tf):
    rows, d = h.shape
    ff = wg.shape[1]
    nxt, per_b = n_b * lx // tm, lx // tm
    mod_idx = lambda i, j: (jnp.where(i < nxt, i // per_b, n_b), 0, 0)
    return pl.pallas_call(
        _ffn_kernel,
        grid=(rows // tm, ff // tf),
        in_specs=[pl.BlockSpec((tm, d), lambda i, j: (i, 0)),
                  pl.BlockSpec((None, 6, d), mod_idx),
                  pl.BlockSpec((1, d), lambda i, j: (0, 0)),
                  pl.BlockSpec((1, d), lambda i, j: (0, 0)),
                  pl.BlockSpec((d, tf), lambda i, j: (0, j)),
                  pl.BlockSpec((d, tf), lambda i, j: (0, j)),
                  pl.BlockSpec((tf, d), lambda i, j: (j, 0))],
        out_specs=pl.BlockSpec((tm, d), lambda i, j: (i, 0)),
        out_shape=jax.ShapeDtypeStruct((rows, d), F32),
        scratch_shapes=[pltpu.VMEM((tm, d), BF16), pltpu.VMEM((tm, d), F32)],
        compiler_params=_cp(("parallel", "arbitrary"), VMEM_WEIGHTS_MB),
        name="ffn_swiglu",
    )(h, mod_l, g_pre.reshape(1, d), g_post.reshape(1, d), wg, wu, wd)


MOE_ALIGN = 16
MOE_GROUP_TILE = 512


def _moe_local_rows(ts):
    return -(-(TOP_K * ts + N_EXPERTS * MOE_ALIGN) // 128) * 128


def _moe_route_kernel(h_ref, mod_ref, gpre_ref, wr_ref, u_ref, dest_ref, gate_ref, cnt_ref):
    ts, lanes = dest_ref.shape
    x = h_ref[...]
    y = x * _rms_scale(x) * gpre_ref[...]
    u = y * (1.0 + mod_ref[4:5, :]) + mod_ref[3:4, :]
    u_hi = u.astype(BF16)
    u_ref[...] = u_hi
    u_lo = (u - u_hi.astype(F32)).astype(BF16)
    logits = _dot(u_hi, wr_ref[0]) + (_dot(u_hi, wr_ref[1]) + _dot(u_lo, wr_ref[0]))
    lane = lax.broadcasted_iota(jnp.int32, logits.shape, 1).astype(F32)
    lg = jnp.where(lane < N_EXPERTS, logits, NEG)
    m1 = jnp.max(lg, axis=-1, keepdims=True)
    i1 = jnp.min(jnp.where(lg == m1, lane, float(lanes)), axis=-1, keepdims=True)
    lg2 = jnp.where(lane == i1, NEG, lg)
    m2 = jnp.max(lg2, axis=-1, keepdims=True)
    i2 = jnp.min(jnp.where(lg2 == m2, lane, float(lanes)), axis=-1, keepdims=True)
    e2 = jnp.exp(m2 - m1)
    gate_ref[...] = jnp.where(lane == 0.0, 1.0 / (1.0 + e2), jnp.where(lane == 1.0, e2 / (1.0 + e2), 0.0))
    sel = jnp.where(lane == i1, 1.0, 0.0) + jnp.where(lane == i2, 1.0, 0.0)
    cnt = jnp.sum(sel, axis=0, keepdims=True)
    cnt_al = jnp.floor((cnt + (MOE_ALIGN - 1)) / MOE_ALIGN) * MOE_ALIGN
    cnt_ref[...] = jnp.broadcast_to(cnt_al, cnt_ref.shape)
    before = lax.broadcasted_iota(jnp.int32, (lanes, lanes), 0) < lax.broadcasted_iota(jnp.int32, (lanes, lanes), 1)
    seg_lo = _select_sum(before, jnp.broadcast_to(cnt_al, (8, lanes)), mask_on_right=True)[0:1]
    earlier = lax.broadcasted_iota(jnp.int32, (ts, ts), 1) < lax.broadcasted_iota(jnp.int32, (ts, ts), 0)
    rank = _dot(earlier.astype(BF16), sel.astype(BF16))
    slot = seg_lo + rank
    slot_1 = jnp.sum(jnp.where(lane == i1, slot, 0.0), axis=-1, keepdims=True)
    slot_2 = jnp.sum(jnp.where(lane == i2, slot, 0.0), axis=-1, keepdims=True)
    dest_ref[...] = jnp.where(lane == 0.0, slot_1, jnp.where(lane == 1.0, slot_2, -1.0))


MOE_BIG = 4


def _moe_copy(hbm_ref, loc_ref, sems, far_row, loc_row, size_idx, *, to_hbm):
    rows = (MOE_BIG * MOE_ALIGN, MOE_ALIGN)[size_idx]
    loc = loc_ref.at[pl.ds(pl.multiple_of(loc_row, MOE_ALIGN), rows), :]
    far = hbm_ref.at[pl.ds(pl.multiple_of(far_row, MOE_ALIGN), rows), :]
    src, dst = (loc, far) if to_hbm else (far, loc)
    return pltpu.make_async_copy(src, dst, sems.at[size_idx])


def _moe_segment_copies(i, off_ref, n_ref, hbm_ref, loc_ref, sems, *, to_hbm):
    lo = jnp.int32(0)
    n_big_all, n_small_all = jnp.int32(0), jnp.int32(0)
    for e in range(N_EXPERTS):
        n_chunks = n_ref[i * N_EXPERTS + e]
        go = off_ref[i * N_EXPERTS + e]
        n_big = n_chunks // MOE_BIG
        n_small = n_chunks - n_big * MOE_BIG
        done = n_big * (MOE_BIG * MOE_ALIGN)

        def big(k, carry, lo=lo, go=go):
            step = k * (MOE_BIG * MOE_ALIGN)
            _moe_copy(hbm_ref, loc_ref, sems, go + step, lo + step, 0, to_hbm=to_hbm).start()
            return carry

        def small(k, carry, lo=lo, go=go, done=done):
            step = done + k * MOE_ALIGN
            _moe_copy(hbm_ref, loc_ref, sems, go + step, lo + step, 1, to_hbm=to_hbm).start()
            return carry

        lax.fori_loop(0, n_big, big, 0)
        lax.fori_loop(0, n_small, small, 0)
        lo = lo + n_chunks * MOE_ALIGN
        n_big_all, n_small_all = n_big_all + n_big, n_small_all + n_small
    return lo, n_big_all, n_small_all


def _moe_wait_copies(n_big, n_small, hbm_ref, loc_ref, sems, *, to_hbm):
    for size_idx, n in ((0, n_big), (1, n_small)):
        def body(k, carry, size_idx=size_idx):
            _moe_copy(hbm_ref, loc_ref, sems, 0, 0, size_idx, to_hbm=to_hbm).wait()
            return carry

        lax.fori_loop(0, n, body, 0)


def _moe_segment_counts(i, n_ref):
    n_chunks = [n_ref[i * N_EXPERTS + e] for e in range(N_EXPERTS)]
    n_big = sum(n // MOE_BIG for n in n_chunks)
    return sum(n_chunks) * MOE_ALIGN, n_big, sum(n_chunks) - n_big * MOE_BIG


def _moe_sort_kernel(off_ref, n_ref, u_ref, dest_ref, init_ref, us_ref, loc_scr, sems):
    del init_ref
    i = pl.program_id(0)
    cur = lax.rem(i, 2)
    dest = dest_ref[...]
    slot = lax.broadcasted_iota(jnp.int32, (dest.shape[0], loc_scr.shape[1]), 1).astype(F32)
    onehot = jnp.where(slot == dest[:, 0:1], 1.0, jnp.where(slot == dest[:, 1:2], 1.0, 0.0))
    loc_scr[cur] = _dot_tn(onehot.astype(BF16), u_ref[...]).astype(BF16)
    _, n_big, n_small = _moe_segment_copies(i, off_ref, n_ref, us_ref, loc_scr.at[cur], sems.at[cur], to_hbm=True)

    @pl.when(i > 0)
    def _():
        _, p_big, p_small = _moe_segment_counts(i - 1, n_ref)
        _moe_wait_copies(p_big, p_small, us_ref, loc_scr.at[1 - cur], sems.at[1 - cur], to_hbm=True)

    @pl.when(i == pl.num_programs(0) - 1)
    def _():
        _moe_wait_copies(n_big, n_small, us_ref, loc_scr.at[cur], sems.at[cur], to_hbm=True)


def _moe_expert_kernel(te_ref, tv_ref, u_ref, wg_ref, wu_ref, wd_ref, y_ref, acc_scr):
    k, j = pl.program_id(0), pl.program_id(1)
    last = pl.num_programs(1) - 1
    valid = tv_ref[k] > 0

    @pl.when(valid)
    def _():
        @pl.when(j == 0)
        def _():
            acc_scr[...] = jnp.zeros_like(acc_scr)

        u = u_ref[...]
        mid = (_silu(_dot(u, wg_ref[...])) * _dot(u, wu_ref[...])).astype(BF16)
        acc_scr[...] += _dot(mid, wd_ref[...])

        @pl.when(j == last)
        def _():
            y_ref[...] = acc_scr[...].astype(BF16)

    @pl.when(jnp.logical_not(valid) & (j == last))
    def _():
        y_ref[...] = jnp.zeros_like(y_ref)


def _moe_combine_kernel(off_ref, n_ref, h_ref, mod_ref, gpost_ref, dest_ref, gate_ref, ys_ref, o_ref, loc_scr, sems):
    i = pl.program_id(0)
    cur = lax.rem(i, 2)

    @pl.when(i == 0)
    def _():
        _moe_segment_copies(i, off_ref, n_ref, ys_ref, loc_scr.at[cur], sems.at[cur], to_hbm=False)

    @pl.when(i < pl.num_programs(0) - 1)
    def _():
        _moe_segment_copies(i + 1, off_ref, n_ref, ys_ref, loc_scr.at[1 - cur], sems.at[1 - cur], to_hbm=False)

    dest, gate = dest_ref[...], gate_ref[...]
    slot = lax.broadcasted_iota(jnp.int32, (dest.shape[0], loc_scr.shape[1]), 1).astype(F32)
    w = jnp.where(slot == dest[:, 0:1], gate[:, 0:1], jnp.where(slot == dest[:, 1:2], gate[:, 1:2], 0.0))
    w_hi = w.astype(BF16)
    w_lo = (w - w_hi.astype(F32)).astype(BF16)
    n_rows, n_big, n_small = _moe_segment_counts(i, n_ref)
    _moe_wait_copies(n_big, n_small, ys_ref, loc_scr.at[cur], sems.at[cur], to_hbm=False)
    filled = lax.broadcasted_iota(jnp.int32, (loc_scr.shape[1], 1), 0) < n_rows
    y_cur = loc_scr[cur]
    y_loc = jnp.where(filled, y_cur, jnp.zeros_like(y_cur))
    y = _dot(w_hi, y_loc) + _dot(w_lo, y_loc)
    o_ref[...] = h_ref[...] + mod_ref[5:6, :] * (y * _rms_scale(y) * gpost_ref[...])


def _moe_call(h, mod_l, g_pre, g_post, router_pad, wg, wu, wd, *, n_b, lx, tm, tf):
    rows, d = h.shape
    ne, _, fe = wg.shape
    ts, tg = tm, MOE_GROUP_TILE
    n_tiles = rows // ts
    lrows = _moe_local_rows(ts)
    nt_max = -(-(TOP_K * rows + n_tiles * ne * (MOE_ALIGN - 1)) // tg) + ne
    n_pad = nt_max * tg
    nxt, per_b = n_b * lx // ts, lx // ts
    mod_row = lambda i: jnp.where(i < nxt, i // per_b, n_b)

    u, dest, gate, cnt = pl.pallas_call(
        _moe_route_kernel,
        grid=(n_tiles,),
        in_specs=[pl.BlockSpec((ts, d), lambda i: (i, 0)),
                  pl.BlockSpec((None, 6, d), lambda i: (mod_row(i), 0, 0)),
                  pl.BlockSpec((1, d), lambda i: (0, 0)),
                  pl.BlockSpec((2, d, 128), lambda i: (0, 0, 0))],
        out_specs=[pl.BlockSpec((ts, d), lambda i: (i, 0)),
                   pl.BlockSpec((ts, 128), lambda i: (i, 0)),
                   pl.BlockSpec((ts, 128), lambda i: (i, 0)),
                   pl.BlockSpec((None, 8, 128), lambda i: (i, 0, 0))],
        out_shape=[jax.ShapeDtypeStruct((rows, d), BF16),
                   jax.ShapeDtypeStruct((rows, 128), F32),
                   jax.ShapeDtypeStruct((rows, 128), F32),
                   jax.ShapeDtypeStruct((n_tiles, 8, 128), F32)],
        compiler_params=_cp(("parallel",), VMEM_STREAM_MB),
        name="moe_route",
    )(h, mod_l, g_pre.reshape(1, d), router_pad)

    cnt_al = cnt[:, 0, :ne].astype(jnp.int32)
    group = -(-jnp.sum(cnt_al, axis=0) // tg) * tg
    group_end = jnp.cumsum(group)
    seg_off = (group_end - group)[None, :] + jnp.cumsum(cnt_al, axis=0) - cnt_al
    n_used = group_end[-1] // tg
    tile_id = jnp.arange(nt_max, dtype=jnp.int32)
    tile_valid = (tile_id < n_used).astype(jnp.int32)
    tile_expert = jnp.searchsorted(group_end // tg, jnp.minimum(tile_id, n_used - 1), side='right')
    tile_expert = jnp.minimum(tile_expert, ne - 1).astype(jnp.int32)
    seg_off = seg_off.reshape(-1).astype(jnp.int32)
    seg_chunks = (cnt_al // MOE_ALIGN).reshape(-1)

    u_sorted = pl.pallas_call(
        _moe_sort_kernel,
        grid_spec=pltpu.PrefetchScalarGridSpec(
            num_scalar_prefetch=2,
            grid=(n_tiles,),
            in_specs=[pl.BlockSpec((ts, d), lambda i, o_r, n_r: (i, 0)),
                      pl.BlockSpec((ts, 128), lambda i, o_r, n_r: (i, 0)),
                      pl.BlockSpec(memory_space=pl.ANY)],
            out_specs=pl.BlockSpec(memory_space=pl.ANY),
            scratch_shapes=[pltpu.VMEM((2, lrows, d), BF16), pltpu.SemaphoreType.DMA((2, 2))]),
        out_shape=jax.ShapeDtypeStruct((n_pad, d), BF16),
        input_output_aliases={4: 0},
        compiler_params=_cp(("arbitrary",), VMEM_STREAM_MB),
        name="moe_sort",
    )(seg_off, seg_chunks, u, dest, jnp.zeros((n_pad, d), BF16))

    nj = fe // tf
    w_col = lambda k, j, te_r, tv_r: jnp.where(tv_r[k] > 0, j, nj - 1)
    y_sorted = pl.pallas_call(
        _moe_expert_kernel,
        grid_spec=pltpu.PrefetchScalarGridSpec(
            num_scalar_prefetch=2,
            grid=(nt_max, nj),
            in_specs=[pl.BlockSpec((tg, d), lambda k, j, te_r, tv_r: (k, 0)),
                      pl.BlockSpec((None, d, tf), lambda k, j, te_r, tv_r: (te_r[k], 0, w_col(k, j, te_r, tv_r))),
                      pl.BlockSpec((None, d, tf), lambda k, j, te_r, tv_r: (te_r[k], 0, w_col(k, j, te_r, tv_r))),
                      pl.BlockSpec((None, tf, d), lambda k, j, te_r, tv_r: (te_r[k], w_col(k, j, te_r, tv_r), 0))],
            out_specs=pl.BlockSpec((tg, d), lambda k, j, te_r, tv_r: (k, 0)),
            scratch_shapes=[pltpu.VMEM((tg, d), F32)]),
        out_shape=jax.ShapeDtypeStruct((n_pad, d), BF16),
        compiler_params=_cp(("parallel", "arbitrary"), VMEM_ATTN_MB),
        name="moe_experts",
    )(tile_expert, tile_valid, u_sorted, wg, wu, wd)

    return pl.pallas_call(
        _moe_combine_kernel,
        grid_spec=pltpu.PrefetchScalarGridSpec(
            num_scalar_prefetch=2,
            grid=(n_tiles,),
            in_specs=[pl.BlockSpec((ts, d), lambda i, o_r, n_r: (i, 0)),
                      pl.BlockSpec((None, 6, d), lambda i, o_r, n_r: (mod_row(i), 0, 0)),
                      pl.BlockSpec((1, d), lambda i, o_r, n_r: (0, 0)),
                      pl.BlockSpec((ts, 128), lambda i, o_r, n_r: (i, 0)),
                      pl.BlockSpec((ts, 128), lambda i, o_r, n_r: (i, 0)),
                      pl.BlockSpec(memory_space=pl.ANY)],
            out_specs=pl.BlockSpec((ts, d), lambda i, o_r, n_r: (i, 0)),
            scratch_shapes=[pltpu.VMEM((2, lrows, d), BF16), pltpu.SemaphoreType.DMA((2, 2))]),
        out_shape=jax.ShapeDtypeStruct((rows, d), F32),
        compiler_params=_cp(("arbitrary",), VMEM_ATTN_MB),
        name="moe_combine",
    )(seg_off, seg_chunks, h, mod_l, g_post.reshape(1, d), dest, gate, y_sorted)


def _rope_tables(lx, tm):
    half = MLA_ROPE // 2
    n_axis = half // 2
    inv_freq = ROPE_THETA ** (-jnp.arange(n_axis, dtype=F32) / n_axis)
    pos = jnp.arange(lx)
    rows = (pos // GRID_W).astype(F32)
    cols = (pos % GRID_W).astype(F32)
    ang = jnp.concatenate([rows[:, None] * inv_freq, cols[:, None] * inv_freq], axis=-1)
    cos, sin = jnp.cos(ang), jnp.sin(ang)
    cos_t = jnp.concatenate([cos, cos], axis=-1)
    sin_t = jnp.concatenate([-sin, sin], axis=-1)
    cos_t = jnp.concatenate([cos_t, jnp.ones((tm, MLA_ROPE), F32)], axis=0)
    sin_t = jnp.concatenate([sin_t, jnp.zeros((tm, MLA_ROPE), F32)], axis=0)
    return cos_t, sin_t


def _mla_weights(q_norm, w_uq, kv_norm, w_ukv):
    dq = MLA_NOPE + MLA_ROPE
    half = MLA_ROPE // 2
    cols = lambda a, b: w_uq[:, a:b]
    nope = [cols(h * dq, h * dq + MLA_NOPE) for h in range(MLA_HEADS)]
    rope = [cols(h * dq + MLA_NOPE, (h + 1) * dq) for h in range(MLA_HEADS)]
    rope_sw = [cols(h * dq + MLA_NOPE + s * half, h * dq + MLA_NOPE + (s + 1) * half)
               for h in range(MLA_HEADS) for s in (1, 0)]
    pad = ((0, MLA_ROPE), (0, 0))
    wq = jnp.pad(jnp.concatenate(nope + rope, axis=1), pad).astype(BF16)
    wqs = jnp.pad(jnp.concatenate(rope_sw, axis=1), pad).astype(BF16)
    qn_ext = jnp.pad(q_norm, (0, MLA_ROPE)).reshape(1, -1)
    perm = np.zeros((MLA_ROPE, MLA_ROPE), np.float32)
    perm[(np.arange(MLA_ROPE) + half) % MLA_ROPE, np.arange(MLA_ROPE)] = 1.0
    return qn_ext, wq, wqs, kv_norm.reshape(1, -1), w_ukv.astype(BF16), jnp.asarray(perm, BF16)


def _pick_tile(n, cands):
    for t in cands:
        if n % t == 0:
            return t
    raise ValueError(f"no tile for {n}")


def _tile_plan(n_b, lx, lc, d_ff, d_expert):
    seq = math.gcd(lx, n_b * lc)
    return dict(
        tm=_pick_tile(seq, (512, 256, 128)),
        tm_in=_pick_tile(seq, (1024, 512, 256, 128)),
        tq=_pick_tile(lc, (256, 128)),
        tr=_pick_tile(lc, (256, 128)),
        tf_ffn=_pick_tile(d_ff, (512, 256, 128)),
        tf_moe=_pick_tile(d_expert, (256, 128)),
    )


def kernel(x, c, ctx, c_ctx, w_ada, b_ada, g_pre_mix, g_post_mix, g_pre_ffn, g_post_ffn, w_in, w_out, mla_q_norm, mla_w_uq, mla_kv_norm, mla_w_ukv, ssd_conv_w, ssd_conv_b, ssd_a_log, ssd_dt_bias, ssd_d, ssd_norm, na_rpb, gdn_conv_w, gdn_a_log, gdn_dt_bias, gdn_norm, ffn_w_gate, ffn_w_up, ffn_w_down, moe_router, moe_w_gate, moe_w_up, moe_w_down):
    n_b, lx, d = x.shape
    lc = ctx.shape[1]
    depth = w_ada.shape[0]
    rows_x, rows_c = n_b * lx, n_b * lc
    assert n_b + 1 <= 8 and lx % GRID_W == 0 and lx % lc == 0 and lc % CHUNK == 0
    plan = _tile_plan(n_b, lx, lc, ffn_w_gate.shape[-1], moe_w_gate.shape[-1])
    tm, tm_in, tq, tr = plan["tm"], plan["tm_in"], plan["tq"], plan["tr"]

    cvec = jnp.concatenate([c, c_ctx[None, :], jnp.zeros((8 - n_b - 1, d), F32)], axis=0)
    mod = _ada_call(cvec, w_ada, b_ada).reshape(depth, 8, 6, d)
    cos_t, sin_t = _rope_tables(lx, tm)
    na_bias = _na_bias_table(na_rpb.reshape((-1,) + na_rpb.shape[2:]), lx // GRID_W, lc)
    h_all = jnp.concatenate([x.reshape(rows_x, d), ctx.reshape(rows_c, d)], axis=0)

    for i in range(depth):
        need_ctx = i < depth - 1
        w_main, w_small = _regroup_w_in(w_in[i])
        proj, small = _inproj_call(h_all, mod[i], g_pre_mix[i], w_main, w_small, n_b=n_b, lx=lx, tm=tm_in)

        mla_w = _mla_weights(mla_q_norm[i], mla_w_uq[i], mla_kv_norm[i], mla_w_ukv[i])
        q_a, k_a, v_a = _mla_prep_call(proj, *mla_w[:5], cos_t, sin_t, mla_w[5], n_b=n_b, lx=lx, tm=tm)
        ya = _mla_attn_call(q_a, k_a, v_a, n_b=n_b, lx=lx, lc=lc, tq=tq, need_ctx=need_ctx)

        xbc = _conv_call(proj, ssd_conv_w[i], ssd_conv_b[i], col_off=P_XBC, n_b=n_b, lx=lx, lc=lc, tr=tr)
        ys = _ssd_call(xbc, small, ssd_a_log[i], ssd_dt_bias[i], ssd_d[i], n_b=n_b, lx=lx, lc=lc)

        yn = _na_call(proj, na_bias, layer=i, n_b=n_b, lx=lx, lc=lc, need_ctx=need_ctx)

        qkv = _conv_call(proj, gdn_conv_w[i], jnp.zeros((3 * GDN_DIM,), F32), col_off=P_GQKV,
                         n_b=n_b, lx=lx, lc=lc, tr=tr)
        og = _gdn_call(qkv, small, gdn_a_log[i], gdn_dt_bias[i], n_b=n_b, lx=lx, lc=lc)

        n_rows = rows_x + rows_c if need_ctx else rows_x
        h_mid = _mixout_call(h_all, mod[i], g_post_mix[i], ya, ys, proj, ssd_norm[i], yn, og, gdn_norm[i],
                             w_out[i].astype(BF16), n_b=n_b, lx=lx, n_rows=n_rows, tm=tm)
        j = i // 2
        if i % 2 == 0:
            h_all = _ffn_call(h_mid, mod[i], g_pre_ffn[i], g_post_ffn[i], ffn_w_gate[j].astype(BF16),
                              ffn_w_up[j].astype(BF16), ffn_w_down[j].astype(BF16), n_b=n_b, lx=lx, tm=tm,
                              tf=plan["tf_ffn"])
        else:
            router_f32 = jnp.pad(moe_router[j], ((0, 0), (0, 128 - N_EXPERTS)))
            router_hi = router_f32.astype(BF16)
            router_pad = jnp.stack([router_hi, (router_f32 - router_hi.astype(F32)).astype(BF16)])
            h_all = _moe_call(h_mid, mod[i], g_pre_ffn[i], g_post_ffn[i], router_pad, moe_w_gate[j].astype(BF16),
                              moe_w_up[j].astype(BF16), moe_w_down[j].astype(BF16), n_b=n_b, lx=lx, tm=tm,
                              tf=plan["tf_moe"])
    return h_all[:rows_x].reshape(n_b, lx, d)
```
